```python
import math
import jax
import jax.numpy as jnp
from jax import lax
import numpy as np

D_MODEL = 2048
BATCH = 4
SEQ = 2048
DEPTH = 2
DEC_BATCH = 128
DEC_SEQ = 4
PAST_LEN = 16384
PAGE_SIZE = 128

N_EVEN = (DEPTH + 1) // 2
N_ODD = DEPTH // 2
CHUNK = 64
EPS = 1e-6
F32 = jnp.float32

SSD_HEADS = 32
SSD_HEAD_DIM = 64
D_SSD = SSD_HEADS * SSD_HEAD_DIM
SSD_GROUPS = 4
SSD_STATE = 128
SSD_CONV = 4
SSD_CONV_DIM = D_SSD + 2 * SSD_GROUPS * SSD_STATE
RET_HEADS = 8
RET_DK = 128
RET_DV = 256
RET_QK = RET_HEADS * RET_DK
D_RET = RET_HEADS * RET_DV
ROPE_BASE = 10000.0
HG_HEADS = 16
HG_DK = 128
HG_DV = 128
D_HG_K = HG_HEADS * HG_DK
D_HG = HG_HEADS * HG_DV
D_RNN = 2048
LRU_BLOCKS = 8
LRU_BW = D_RNN // LRU_BLOCKS
LRU_CONV = 4
LRU_C = 8.0
D_FF = 5632
FFN_CONV = 3

SIZES_EVEN = (D_SSD, SSD_CONV_DIM, SSD_HEADS, RET_QK, RET_QK, D_RET, D_RET)
SIZES_ODD = (D_HG_K, D_HG_K, D_HG, D_HG, D_RNN, D_RNN)
D_IN_EVEN = sum(SIZES_EVEN)
D_IN_ODD = sum(SIZES_ODD)
D_MIX_EVEN = D_SSD + D_RET
D_MIX_ODD = D_HG + D_RNN
STATE_KEYS = ('ssm', 'ssm_conv', 'ret', 'hgrn', 'lru', 'lru_conv', 'ffn_conv')

kernel_name = 'hybrid_ssd_retention_hgrn2_rglru_step'


def split_cols(t, sizes):
    idx = [int(i) for i in np.cumsum(sizes)[:-1]]
    return jnp.split(t, idx, axis=-1)


def rmsnorm(x, w):
    xf = x.astype(F32)
    y = xf * lax.rsqrt(jnp.mean(xf * xf, axis=-1, keepdims=True) + EPS)
    return (y * w.astype(F32)).astype(x.dtype)


def _normalize(x):
    return x * lax.rsqrt(jnp.mean(x * x, axis=-1, keepdims=True) + EPS)


def causal_dwconv(x, prev, w, b):
    width = w.shape[0]
    length = x.shape[1]
    xp = jnp.concatenate([prev.astype(x.dtype), x], axis=1)
    out = b.astype(x.dtype) + sum(xp[:, j:j + length] * w[j].astype(x.dtype) for j in range(width))
    return out, xp[:, length:]


def rotary(x, pos):
    half = x.shape[-1] // 2
    inv = ROPE_BASE ** (-jnp.arange(half, dtype=F32) / half)
    ang = pos.astype(F32)[:, None] * inv[None, :]
    cos = jnp.cos(ang)[None, :, None, :]
    sin = jnp.sin(ang)[None, :, None, :]
    x1, x2 = x[..., :half], x[..., half:]
    return jnp.concatenate([x1 * cos - x2 * sin, x1 * sin + x2 * cos], axis=-1)


def chunked_linear_recurrence(q, k, v, log_a, s0):
    bsz, length, heads, _ = q.shape
    c = math.gcd(length, CHUNK)
    n_chunks = length // c
    scalar = log_a.shape[-1] == 1

    def to_chunks(t):
        return jnp.moveaxis(t.reshape((bsz, n_chunks, c) + t.shape[2:]), 1, 0)

    causal = jnp.tril(jnp.ones((c, c), dtype=bool))[None, :, :, None, None]

    def step(s, inp):
        qc, kc, vc, la = inp
        cum = jnp.cumsum(la, axis=1)
        last = cum[:, -1]
        diff = cum[:, :, None] - cum[:, None, :]
        dec = jnp.exp(jnp.where(causal, diff, -jnp.inf))
        if scalar:
            scores = jnp.einsum('bihk,bjhk->bijh', qc, kc) * dec[..., 0]
        else:
            scores = jnp.einsum('bihk,bjhk,bijhk->bijh', qc, kc, dec)
        o = (jnp.einsum('bijh,bjhv->bihv', scores, vc)
             + jnp.einsum('bihk,bhkv->bihv', qc * jnp.exp(cum), s))
        s_new = (jnp.exp(last)[..., None] * s
                 + jnp.einsum('bjhk,bjhv->bhkv', kc * jnp.exp(last[:, None] - cum), vc))
        return s_new, o

    s_fin, o = lax.scan(step, s0.astype(F32),
                        (to_chunks(q), to_chunks(k), to_chunks(v), to_chunks(log_a)))
    o = jnp.moveaxis(o, 0, 1).reshape(bsz, length, heads, v.shape[-1])
    return o, s_fin


def even_mixer(h, s_ssm, s_conv, s_ret, pos, prm, j):
    bsz, length, _ = h.shape
    proj = h @ prm['even_w_in'][j]
    z, xbc, dt_raw, q, k, v, g = split_cols(proj, SIZES_EVEN)
    xbc, conv_new = causal_dwconv(xbc, s_conv, prm['ssd_conv_w'][j], prm['ssd_conv_b'][j])
    xbc = jax.nn.silu(xbc.astype(F32))
    xs, bm, cm = split_cols(xbc, (D_SSD, SSD_GROUPS * SSD_STATE, SSD_GROUPS * SSD_STATE))
    xs = xs.reshape(bsz, length, SSD_HEADS, SSD_HEAD_DIM)
    rep = SSD_HEADS // SSD_GROUPS
    bh = jnp.repeat(bm.reshape(bsz, length, SSD_GROUPS, SSD_STATE), rep, axis=2)
    ch = jnp.repeat(cm.reshape(bsz, length, SSD_GROUPS, SSD_STATE), rep, axis=2)
    dt = jax.nn.softplus(dt_raw.astype(F32) + prm['ssd_dt_bias'][j].astype(F32))
    a = -jnp.exp(prm['ssd_A_log'][j].astype(F32))
    y, ssm_new = chunked_linear_recurrence(ch, bh, xs * dt[..., None], (dt * a)[..., None], s_ssm)
    y = y + prm['ssd_D'][j].astype(F32)[:, None] * xs
    y = y.reshape(bsz, length, D_SSD) * jax.nn.silu(z.astype(F32))
    y = _normalize(y.reshape(bsz, length, SSD_GROUPS, D_SSD // SSD_GROUPS)).reshape(bsz, length, D_SSD)
    y = y * prm['ssd_norm_w'][j].astype(F32)
    qr = rotary(q.astype(F32).reshape(bsz, length, RET_HEADS, RET_DK), pos)
    kr = rotary(k.astype(F32).reshape(bsz, length, RET_HEADS, RET_DK), pos) * (RET_DK ** -0.5)
    vr = v.astype(F32).reshape(bsz, length, RET_HEADS, RET_DV)
    log_gamma = jnp.log1p(-jnp.exp(jnp.linspace(math.log(1.0 / 32.0), math.log(1.0 / 512.0), RET_HEADS, dtype=F32)))
    log_g = jnp.broadcast_to(log_gamma[None, None, :, None], (bsz, length, RET_HEADS, 1))
    o, ret_new = chunked_linear_recurrence(qr, kr, vr, log_g, s_ret)
    o = _normalize(o).reshape(bsz, length, D_RET) * jax.nn.silu(g.astype(F32))
    mix = jnp.concatenate([y, o], axis=-1).astype(h.dtype) @ prm['even_w_out'][j]
    return mix, ssm_new, conv_new, ret_new


def lru_combine(c1, c2):
    a1, b1 = c1
    a2, b2 = c2
    return a1 * a2, a2 * b1 + b2


def odd_mixer(h, s_hg, s_lru, s_lconv, pos, lb, prm, j):
    bsz, length, _ = h.shape
    proj = h @ prm['odd_w_in'][j]
    hq, hf, hi, hg, rx, rg = split_cols(proj, SIZES_ODD)
    q = jax.nn.silu(hq.astype(F32)).reshape(bsz, length, HG_HEADS, HG_DK)
    f = (lb + (1.0 - lb) * jax.nn.sigmoid(hf.astype(F32))).reshape(bsz, length, HG_HEADS, HG_DK)
    i = hi.astype(F32).reshape(bsz, length, HG_HEADS, HG_DV)
    og, hg_new = chunked_linear_recurrence(q, 1.0 - f, i, jnp.log(f), s_hg)
    og = (_normalize(og) * prm['hg_norm_w'][j].astype(F32)).reshape(bsz, length, D_HG)
    og = og * jax.nn.silu(hg.astype(F32))
    xr, lconv_new = causal_dwconv(rx, s_lconv, prm['lru_conv_w'][j], prm['lru_conv_b'][j])
    xr = xr.astype(F32)
    xb = xr.reshape(bsz, length, LRU_BLOCKS, LRU_BW)
    r = jax.nn.sigmoid(jnp.einsum('blnc,ncd->blnd', xb, prm['lru_w_r'][j].astype(F32)).reshape(bsz, length, D_RNN)
                       + prm['lru_b_r'][j].astype(F32))
    gi = jax.nn.sigmoid(jnp.einsum('blnc,ncd->blnd', xb, prm['lru_w_i'][j].astype(F32)).reshape(bsz, length, D_RNN)
                        + prm['lru_b_i'][j].astype(F32))
    log_a = -LRU_C * r * jax.nn.softplus(-prm['lru_a_param'][j].astype(F32))
    a = jnp.exp(log_a)
    mult = jnp.where((pos == 0)[None, :, None], 1.0, jnp.sqrt(-jnp.expm1(2.0 * log_a)))
    b = mult * gi * xr
    b = b.at[:, 0].add(a[:, 0] * s_lru.astype(F32))
    _, hs = lax.associative_scan(lru_combine, (a, b), axis=1)
    yl = hs * jax.nn.gelu(rg.astype(F32))
    mix = jnp.concatenate([og, yl], axis=-1).astype(h.dtype) @ prm['odd_w_out'][j]
    return mix, hg_new, hs[:, -1], lconv_new


def conv_ffn(h, prev, prm, l):
    up = h @ prm['ffn_w_up'][l]
    g, u = up[..., :D_FF], up[..., D_FF:]
    g, buf = causal_dwconv(g, prev, prm['ffn_conv_w'][l], prm['ffn_conv_b'][l])
    act = (jax.nn.silu(g.astype(F32)) * u.astype(F32)).astype(h.dtype)
    return act @ prm['ffn_w_down'][l], buf


def zero_states(bsz, dtype):
    return {
        'ssm': jnp.zeros((N_EVEN, bsz, SSD_HEADS, SSD_STATE, SSD_HEAD_DIM), dtype),
        'ssm_conv': jnp.zeros((N_EVEN, bsz, SSD_CONV - 1, SSD_CONV_DIM), dtype),
        'ret': jnp.zeros((N_EVEN, bsz, RET_HEADS, RET_DK, RET_DV), dtype),
        'hgrn': jnp.zeros((N_ODD, bsz, HG_HEADS, HG_DK, HG_DV), dtype),
        'lru': jnp.zeros((N_ODD, bsz, D_RNN), dtype),
        'lru_conv': jnp.zeros((N_ODD, bsz, LRU_CONV - 1, D_RNN), dtype),
        'ffn_conv': jnp.zeros((DEPTH, bsz, FFN_CONV - 1, D_FF), dtype),
    }


def trunk(x, st, pos0, prm):
    length = x.shape[1]
    pos = pos0 + jnp.arange(length, dtype=jnp.int32)
    lbs = jnp.cumsum(jax.nn.softmax(prm['hg_lower_bounds'].astype(F32), axis=0), axis=0)
    lbs = lbs - lbs[0]
    new = {name: [] for name in STATE_KEYS}
    for l in range(DEPTH):
        j = l // 2
        h = rmsnorm(x, prm['norm_mix'][l])
        if l % 2 == 0:
            mix, s1, s2, s3 = even_mixer(h, st['ssm'][j], st['ssm_conv'][j], st['ret'][j], pos, prm, j)
            new['ssm'].append(s1)
            new['ssm_conv'].append(s2)
            new['ret'].append(s3)
        else:
            mix, s1, s2, s3 = odd_mixer(h, st['hgrn'][j], st['lru'][j], st['lru_conv'][j], pos, lbs[l], prm, j)
            new['hgrn'].append(s1)
            new['lru'].append(s2)
            new['lru_conv'].append(s3)
        x = x + mix.astype(x.dtype)
        f, s4 = conv_ffn(rmsnorm(x, prm['norm_ffn'][l]), st['ffn_conv'][l], prm, l)
        new['ffn_conv'].append(s4)
        x = x + f.astype(x.dtype)
    y = rmsnorm(x, prm['norm_final'])
    return y, {name: jnp.stack(vals).astype(x.dtype) for name, vals in new.items()}


def setup_inputs(seed: int = 0) -> dict:
    key = jax.random.key(seed)
    keys = iter(jax.random.split(key, 48))

    def nrm(shape, scale):
        return scale * jax.random.normal(next(keys), shape, F32)

    def unif(shape, lo, hi):
        return jax.random.uniform(next(keys), shape, F32, lo, hi)

    dt0 = jnp.exp(unif((N_EVEN, SSD_HEADS), math.log(1e-3), math.log(1e-1)))
    a0 = unif((N_ODD, D_RNN), 0.9, 0.999)
    return {
        'x_prompt': nrm((BATCH, SEQ, D_MODEL), 1.0),
        'x_sample': nrm((DEC_BATCH, DEC_SEQ, D_MODEL), 1.0),
        'state_ssm': nrm((N_EVEN, DEC_BATCH, SSD_HEADS, SSD_STATE, SSD_HEAD_DIM), 0.5),
        'state_ssm_conv': nrm((N_EVEN, DEC_BATCH, SSD_CONV - 1, SSD_CONV_DIM), 1.0),
        'state_ret': nrm((N_EVEN, DEC_BATCH, RET_HEADS, RET_DK, RET_DV), 0.5),
        'state_hgrn': nrm((N_ODD, DEC_BATCH, HG_HEADS, HG_DK, HG_DV), 0.5),
        'state_lru': nrm((N_ODD, DEC_BATCH, D_RNN), 0.5),
        'state_lru_conv': nrm((N_ODD, DEC_BATCH, LRU_CONV - 1, D_RNN), 1.0),
        'state_ffn_conv': nrm((DEPTH, DEC_BATCH, FFN_CONV - 1, D_FF), 1.0),
        'norm_mix': 1.0 + nrm((DEPTH, D_MODEL), 0.02),
        'norm_ffn': 1.0 + nrm((DEPTH, D_MODEL), 0.02),
        'norm_final': 1.0 + nrm((D_MODEL,), 0.02),
        'even_w_in': nrm((N_EVEN, D_MODEL, D_IN_EVEN), D_MODEL ** -0.5),
        'ssd_conv_w': nrm((N_EVEN, SSD_CONV, SSD_CONV_DIM), SSD_CONV ** -0.5),
        'ssd_conv_b': nrm((N_EVEN, SSD_CONV_DIM), 0.01),
        'ssd_dt_bias': dt0 + jnp.log(-jnp.expm1(-dt0)),
        'ssd_A_log': jnp.log(unif((N_EVEN, SSD_HEADS), 1.0, 16.0)),
        'ssd_D': 1.0 + nrm((N_EVEN, SSD_HEADS), 0.02),
        'ssd_norm_w': 1.0 + nrm((N_EVEN, D_SSD), 0.02),
        'even_w_out': nrm((N_EVEN, D_MIX_EVEN, D_MODEL), D_MIX_EVEN ** -0.5),
        'odd_w_in': nrm((N_ODD, D_MODEL, D_IN_ODD), D_MODEL ** -0.5),
        'hg_lower_bounds': nrm((DEPTH, D_HG_K), 0.5),
        'hg_norm_w': 1.0 + nrm((N_ODD, HG_DV), 0.02),
        'lru_conv_w': nrm((N_ODD, LRU_CONV, D_RNN), LRU_CONV ** -0.5),
        'lru_conv_b': nrm((N_ODD, D_RNN), 0.01),
        'lru_w_r': nrm((N_ODD, LRU_BLOCKS, LRU_BW, LRU_BW), LRU_BW ** -0.5),
        'lru_b_r': nrm((N_ODD, D_RNN), 0.01),
        'lru_w_i': nrm((N_ODD, LRU_BLOCKS, LRU_BW, LRU_BW), LRU_BW ** -0.5),
        'lru_b_i': nrm((N_ODD, D_RNN), 0.01),
        'lru_a_param': jnp.log(a0) - jnp.log1p(-a0),
        'odd_w_out': nrm((N_ODD, D_MIX_ODD, D_MODEL), D_MIX_ODD ** -0.5),
        'ffn_w_up': nrm((DEPTH, D_MODEL, 2 * D_FF), D_MODEL ** -0.5),
        'ffn_conv_w': nrm((DEPTH, FFN_CONV, D_FF), FFN_CONV ** -0.5),
        'ffn_conv_b': nrm((DEPTH, D_FF), 0.01),
        'ffn_w_down': nrm((DEPTH, D_FF, D_MODEL), D_FF ** -0.5),
    }


def reference(x_prompt, x_sample, state_ssm, state_ssm_conv, state_ret, state_hgrn, state_lru,
              state_lru_conv, state_ffn_conv, norm_mix, norm_ffn, norm_final, even_w_in,
              ssd_conv_w, ssd_conv_b, ssd_dt_bias, ssd_A_log, ssd_D, ssd_norm_w, even_w_out,
              odd_w_in, hg_lower_bounds, hg_norm_w, lru_conv_w, lru_conv_b, lru_w_r, lru_b_r,
              lru_w_i, lru_b_i, lru_a_param, odd_w_out, ffn_w_up, ffn_conv_w, ffn_conv_b,
              ffn_w_down):
    prm = {
        'norm_mix': norm_mix, 'norm_ffn': norm_ffn, 'norm_final': norm_final,
        'even_w_in': even_w_in, 'ssd_conv_w': ssd_conv_w, 'ssd_conv_b': ssd_conv_b,
        'ssd_dt_bias': ssd_dt_bias, 'ssd_A_log': ssd_A_log, 'ssd_D': ssd_D,
        'ssd_norm_w': ssd_norm_w, 'even_w_out': even_w_out, 'odd_w_in': odd_w_in,
        'hg_lower_bounds': hg_lower_bounds, 'hg_norm_w': hg_norm_w,
        'lru_conv_w': lru_conv_w, 'lru_conv_b': lru_conv_b, 'lru_w_r': lru_w_r,
        'lru_b_r': lru_b_r, 'lru_w_i': lru_w_i, 'lru_b_i': lru_b_i,
        'lru_a_param': lru_a_param, 'odd_w_out': odd_w_out, 'ffn_w_up': ffn_w_up,
        'ffn_conv_w': ffn_conv_w, 'ffn_conv_b': ffn_conv_b, 'ffn_w_down': ffn_w_down,
    }
    st_sample = {
        'ssm': state_ssm, 'ssm_conv': state_ssm_conv, 'ret': state_ret, 'hgrn': state_hgrn,
        'lru': state_lru, 'lru_conv': state_lru_conv, 'ffn_conv': state_ffn_conv,
    }
    y_prompt, new_p = trunk(x_prompt, zero_states(x_prompt.shape[0], x_prompt.dtype), 0, prm)
    y_sample, new_s = trunk(x_sample, st_sample, PAST_LEN, prm)
    return (y_prompt, y_sample,
            new_p['ssm'], new_s['ssm'], new_p['ssm_conv'], new_s['ssm_conv'],
            new_p['ret'], new_s['ret'], new_p['hgrn'], new_s['hgrn'],
            new_p['lru'], new_s['lru'], new_p['lru_conv'], new_s['lru_conv'],
            new_p['ffn_conv'], new_s['ffn_conv'])
```

```python
import functools
import math

import numpy as np
import jax
import jax.numpy as jnp
from jax import lax
from jax.experimental import pallas as pl
from jax.experimental.pallas import tpu as pltpu

F32 = jnp.float32
BF16 = jnp.bfloat16
EPS = 1e-6

D_MODEL = 2048
PAST_LEN = 16384
SSD_HEADS = 32
SSD_HEAD_DIM = 64
D_SSD = SSD_HEADS * SSD_HEAD_DIM
SSD_GROUPS = 4
SSD_STATE = 128
SSD_CONV = 4
SSD_BC = 2 * SSD_GROUPS * SSD_STATE
RET_HEADS = 8
RET_DK = 128
RET_DV = 256
RET_QK = RET_HEADS * RET_DK
D_RET = RET_HEADS * RET_DV
ROPE_BASE = 10000.0
HG_HEADS = 16
HG_DK = 128
HG_DV = 128
D_HG = HG_HEADS * HG_DV
D_RNN = 2048
LRU_BLOCKS = 8
LRU_BW = D_RNN // LRU_BLOCKS
LRU_CONV = 4
LRU_C = 8.0
D_FF = 5632
FFN_CONV = 3
D_IN_EVEN = D_SSD + (D_SSD + SSD_BC) + SSD_HEADS + 2 * RET_QK + 2 * D_RET
EVEN_PACKED = 11776
DT_COL = 11264

V7X_VMEM_BYTES = 64 * 1024 * 1024
V7X_VMEM_CAP = 60 * 1024 * 1024
LANES = 128
SUBLANES = 8

RET_LOG_GAMMA = [float(v) for v in np.log1p(-np.exp(np.linspace(
    math.log(1.0 / 32.0), math.log(1.0 / 512.0), RET_HEADS, dtype=np.float32))).astype(np.float32)]

NT = (((1,), (1,)), ((), ()))
TN = (((0,), (0,)), ((), ()))


def _cparams(sem, vmem_bytes):
    return pltpu.CompilerParams(dimension_semantics=sem,
                                vmem_limit_bytes=int(min(V7X_VMEM_CAP, vmem_bytes)))


def _mib(n):
    return n * 1024 * 1024


def _silu(x):
    return x * jax.nn.sigmoid(x)


def _softplus(x):
    return jnp.maximum(x, 0.0) + jnp.log1p(jnp.exp(-jnp.abs(x)))


def _gelu_tanh(x):
    return 0.5 * x * (1.0 + jnp.tanh(math.sqrt(2.0 / math.pi) * (x + 0.044715 * (x * x * x))))


def _rms(x):
    return x * lax.rsqrt(jnp.mean(x * x, axis=-1, keepdims=True) + EPS)


def _dot(a, b):
    return jnp.dot(a, b, preferred_element_type=F32)


def _tril_ones(n):
    r = lax.broadcasted_iota(jnp.int32, (n, n), 0)
    c = lax.broadcasted_iota(jnp.int32, (n, n), 1)
    return (r >= c).astype(F32)


def _cumsum_rows(x):
    return jnp.dot(_tril_ones(x.shape[0]), x, precision=lax.Precision.HIGHEST,
                   preferred_element_type=F32)


def _causal_conv(buf_ref, x, w_ref, b_ref, shift, width):
    rows = x.shape[0]
    off = buf_ref.shape[0] - rows
    n = (width - 1) * shift
    buf_ref[off:off + rows, :] = x
    acc = None
    for j in range(width):
        start = off - (width - 1 - j) * shift
        term = buf_ref[start:start + rows, :] * w_ref[j:j + 1, :]
        acc = term if acc is None else acc + term
    out = b_ref[...] + acc
    tail = buf_ref[off + rows - n:off + rows, :]
    buf_ref[off - n:off, :] = tail
    return out, tail


def _conv_off(shift, width):
    n = (width - 1) * shift
    return -(-n // SUBLANES) * SUBLANES


def _mm_in_kernel(x_ref, nw_ref, w_ref, o_ref, xn_ref):
    @pl.when(pl.program_id(1) == 0)
    def _():
        xn_ref[...] = (_rms(x_ref[...]) * nw_ref[...]).astype(BF16)
    o_ref[...] = _dot(xn_ref[...], w_ref[...].astype(BF16))


def mm_in(x, nw, w, bm, bn):
    m, d = x.shape
    n = w.shape[1]
    wb = w.dtype.itemsize
    vmem = 2 * bm * d * 4 + bm * d * 2 + 2 * d * bn * wb + d * bn * 2 + 2 * bm * bn * 4 + bm * d * 4 + _mib(6)
    return pl.pallas_call(
        _mm_in_kernel,
        grid=(m // bm, n // bn),
        in_specs=[pl.BlockSpec((bm, d), lambda i, j: (i, 0)),
                  pl.BlockSpec((1, d), lambda i, j: (0, 0)),
                  pl.BlockSpec((d, bn), lambda i, j: (0, j))],
        out_specs=pl.BlockSpec((bm, bn), lambda i, j: (i, j)),
        out_shape=jax.ShapeDtypeStruct((m, n), F32),
        scratch_shapes=[pltpu.VMEM((bm, d), BF16)],
        compiler_params=_cparams(("parallel", "arbitrary"), vmem),
        name="mm_in",
    )(x, nw.reshape(1, d), w)


def _mm_out_kernel(*refs, nparts):
    a_refs = refs[:nparts]
    w_refs = refs[nparts:2 * nparts]
    r_ref = refs[2 * nparts]
    o_ref = refs[2 * nparts + 1]
    acc = r_ref[...]
    for a_ref, w_ref in zip(a_refs, w_refs):
        acc = acc + _dot(a_ref[...], w_ref[...].astype(BF16))
    o_ref[...] = acc


def mm_out(parts, w, resid, bm, bn):
    nparts = len(parts)
    m, kp = parts[0].shape
    n = w.shape[1]
    vmem = nparts * (2 * bm * kp * 2 + 2 * kp * bn * 4 + kp * bn * 2) + 4 * bm * bn * 4 + bm * bn * 4 + _mib(6)
    in_specs = [pl.BlockSpec((bm, kp), lambda i, j: (i, 0)) for _ in range(nparts)]
    in_specs += [pl.BlockSpec((kp, bn), functools.partial(lambda i, j, p: (p, j), p=p)) for p in range(nparts)]
    in_specs += [pl.BlockSpec((bm, bn), lambda i, j: (i, j))]
    return pl.pallas_call(
        functools.partial(_mm_out_kernel, nparts=nparts),
        grid=(m // bm, n // bn),
        in_specs=in_specs,
        out_specs=pl.BlockSpec((bm, bn), lambda i, j: (i, j)),
        out_shape=jax.ShapeDtypeStruct((m, n), F32),
        compiler_params=_cparams(("parallel", "parallel"), vmem),
        name="mm_out",
    )(*parts, *([w] * nparts), resid)


def _rmsnorm_kernel(x_ref, nw_ref, o_ref):
    o_ref[...] = _rms(x_ref[...]) * nw_ref[...]


def rmsnorm(x, nw, bm):
    m, d = x.shape
    return pl.pallas_call(
        _rmsnorm_kernel,
        grid=(m // bm,),
        in_specs=[pl.BlockSpec((bm, d), lambda i: (i, 0)), pl.BlockSpec((1, d), lambda i: (0, 0))],
        out_specs=pl.BlockSpec((bm, d), lambda i: (i, 0)),
        out_shape=jax.ShapeDtypeStruct((m, d), F32),
        compiler_params=_cparams(("parallel",), 6 * bm * d * 4 + _mib(4)),
        name="rmsnorm",
    )(x, nw.reshape(1, d))


def _ffn_up_kernel(x_ref, nw_ref, wg_ref, wu_ref, cw_ref, cb_ref, prev_ref,
                   act_ref, st_ref, xn_ref, gbuf_ref, carry_ref, *, shift, blocks_per_group):
    i = pl.program_id(0)
    j = pl.program_id(1)
    bm = x_ref.shape[0]
    n = (FFN_CONV - 1) * shift
    off = gbuf_ref.shape[0] - bm

    @pl.when(j == 0)
    def _():
        xn_ref[...] = (_rms(x_ref[...]) * nw_ref[...]).astype(BF16)

    xn = xn_ref[...]
    g = _dot(xn, wg_ref[...].astype(BF16))
    u = _dot(xn, wu_ref[...].astype(BF16))
    first = (i % blocks_per_group) == 0

    @pl.when(first)
    def _():
        gbuf_ref[off - n:off, :] = prev_ref[0]

    @pl.when(jnp.logical_not(first))
    def _():
        gbuf_ref[off - n:off, :] = carry_ref[j]

    gc, tail = _causal_conv(gbuf_ref, g, cw_ref, cb_ref, shift, FFN_CONV)
    act_ref[...] = (_silu(gc) * u).astype(BF16)
    carry_ref[j] = tail
    st_ref[0] = tail


def ffn_up(x, nw, w_up, cw, cb, prev, shift, group_rows, bm, bf):
    m, d = x.shape
    f = cw.shape[1]
    n = (FFN_CONV - 1) * shift
    off = _conv_off(shift, FFN_CONV)
    groups = m // group_rows
    bpg = group_rows // bm
    nf = f // bf
    vmem = (2 * bm * d * 4 + bm * d * 2 + 4 * d * bf * 4 + 2 * d * bf * 2 + 2 * bm * bf * 2
            + (off + bm) * bf * 4 + nf * max(n, SUBLANES) * bf * 4 + 4 * n * bf * 4 + 6 * bm * bf * 4 + _mib(6))
    return pl.pallas_call(
        functools.partial(_ffn_up_kernel, shift=shift, blocks_per_group=bpg),
        grid=(m // bm, nf),
        in_specs=[pl.BlockSpec((bm, d), lambda i, j: (i, 0)),
                  pl.BlockSpec((1, d), lambda i, j: (0, 0)),
                  pl.BlockSpec((d, bf), lambda i, j: (0, j)),
                  pl.BlockSpec((d, bf), lambda i, j: (0, j + nf)),
                  pl.BlockSpec((FFN_CONV, bf), lambda i, j: (0, j)),
                  pl.BlockSpec((1, bf), lambda i, j: (0, j)),
                  pl.BlockSpec((1, n, bf), lambda i, j: (i // bpg, 0, j))],
        out_specs=[pl.BlockSpec((bm, bf), lambda i, j: (i, j)),
                   pl.BlockSpec((1, n, bf), lambda i, j: (i // bpg, 0, j))],
        out_shape=[jax.ShapeDtypeStruct((m, f), BF16),
                   jax.ShapeDtypeStruct((groups, n, f), F32)],
        scratch_shapes=[pltpu.VMEM((bm, d), BF16),
                        pltpu.VMEM((off + bm, bf), F32),
                        pltpu.VMEM((nf, n, bf), F32)],
        compiler_params=_cparams(("arbitrary", "arbitrary"), vmem),
        name="ffn_up",
    )(x, nw.reshape(1, d), w_up, w_up, cw, cb.reshape(1, f), prev)


def _expand_heads64(x, nheads):
    r = x.shape[0]
    lo = lax.broadcasted_iota(jnp.int32, (r, LANES), 1) < SSD_HEAD_DIM
    tiles = []
    for p in range(nheads // 2):
        a0 = jnp.broadcast_to(x[:, 2 * p:2 * p + 1], (r, LANES))
        a1 = jnp.broadcast_to(x[:, 2 * p + 1:2 * p + 2], (r, LANES))
        tiles.append(jnp.where(lo, a0, a1))
    return jnp.concatenate(tiles, axis=1)


def _ssd_gates(xs_c, bc_c, dt_raw, dtb, alog):
    xs = _silu(xs_c)
    bc = _silu(bc_c)
    dt = _softplus(dt_raw + dtb)
    la = dt * (-jnp.exp(alog))
    return xs, bc, dt, la


def _ssd_finish(o, xs, z, dx, nw):
    y = (o + dx * xs) * _silu(z)
    gw = D_SSD // SSD_GROUPS
    y = jnp.concatenate([_rms(y[:, g * gw:(g + 1) * gw]) for g in range(SSD_GROUPS)], axis=1)
    return y * nw


def _ssd_prompt_kernel(z_ref, xs_ref, bc_ref, dt_ref, cwx_ref, cwb_ref, cbx_ref, cbb_ref,
                       dtb_ref, alog_ref, dx_ref, nw_ref,
                       y_ref, sfin_ref, cfin_ref, s_ref, bufx_ref, bufb_ref):
    c = pl.program_id(1)
    nc = pl.num_programs(1)
    rows = xs_ref.shape[0]
    gw = D_SSD // SSD_GROUPS
    hpg = SSD_HEADS // SSD_GROUPS

    @pl.when(c == 0)
    def _():
        s_ref[...] = jnp.zeros_like(s_ref)
        bufx_ref[0:SUBLANES, :] = jnp.zeros((SUBLANES, D_SSD), F32)
        bufb_ref[0:SUBLANES, :] = jnp.zeros((SUBLANES, SSD_BC), F32)

    xs_c, tailx = _causal_conv(bufx_ref, xs_ref[...], cwx_ref, cbx_ref, 1, SSD_CONV)
    bc_c, tailb = _causal_conv(bufb_ref, bc_ref[...], cwb_ref, cbb_ref, 1, SSD_CONV)
    xs, bc, dt, la = _ssd_gates(xs_c, bc_c, dt_ref[...], dtb_ref[...], alog_ref[...])
    cum = _cumsum_rows(la)
    cum_t = cum.T
    cumx = _expand_heads64(cum, SSD_HEADS)
    dtx = _expand_heads64(dt, SSD_HEADS)
    lastx = cumx[rows - 1:rows, :]
    ecum = jnp.exp(cumx)
    wx = jnp.exp(lastx - cumx)
    elast = jnp.exp(lastx)
    v_all = xs * dtx
    vw_all = v_all * wx

    ri = lax.broadcasted_iota(jnp.int32, (rows, rows), 0)
    ci = lax.broadcasted_iota(jnp.int32, (rows, rows), 1)
    causal = ri >= ci
    lo = lax.broadcasted_iota(jnp.int32, (rows, LANES), 1) < SSD_HEAD_DIM

    o_groups = []
    for g in range(SSD_GROUPS):
        kb = bc[:, g * SSD_STATE:(g + 1) * SSD_STATE].astype(BF16)
        qb = bc[:, (SSD_GROUPS + g) * SSD_STATE:(SSD_GROUPS + g + 1) * SSD_STATE].astype(BF16)
        qk = lax.dot_general(qb, kb, NT, preferred_element_type=F32)
        o_tiles = []
        for p in range(hpg // 2):
            acc = None
            for q in range(2):
                h = g * hpg + 2 * p + q
                diff = cum[:, h:h + 1] - cum_t[h:h + 1, :]
                dec = jnp.exp(jnp.where(causal, diff, -1e30))
                pm = (qk * dec).astype(BF16)
                col = g * gw + p * LANES
                vp = v_all[:, col:col + LANES]
                vh = jnp.where(lo, vp, 0.0) if q == 0 else jnp.where(lo, 0.0, vp)
                t = _dot(pm, vh.astype(BF16))
                acc = t if acc is None else acc + t
            o_tiles.append(acc)
        o_intra = jnp.concatenate(o_tiles, axis=1)
        sg = s_ref[g]
        sl = slice(g * gw, (g + 1) * gw)
        o_inter = _dot(qb, sg.astype(BF16)) * ecum[:, sl]
        upd = lax.dot_general(kb, vw_all[:, sl].astype(BF16), TN, preferred_element_type=F32)
        s_ref[g] = sg * elast[:, sl] + upd
        o_groups.append(o_intra + o_inter)
    o = jnp.concatenate(o_groups, axis=1)
    y_ref[...] = _ssd_finish(o, xs, z_ref[...], dx_ref[...], nw_ref[...]).astype(BF16)

    @pl.when(c == nc - 1)
    def _():
        for h in range(SSD_HEADS):
            g, hl = divmod(h, hpg)
            sfin_ref[0, h] = s_ref[g, :, hl * SSD_HEAD_DIM:(hl + 1) * SSD_HEAD_DIM]
        cfin_ref[0, :, 0:D_SSD] = tailx
        cfin_ref[0, :, D_SSD:D_SSD + SSD_BC] = tailb


def ssd_prompt(pe, prm, nseq, seq, chunk):
    m = nseq * seq
    nc = seq // chunk
    row = lambda b, c: b * nc + c
    full = lambda shp: pl.BlockSpec(shp, lambda b, c: (0,) * len(shp))
    in_specs = [
        pl.BlockSpec((chunk, D_SSD), lambda b, c: (row(b, c), 0)),
        pl.BlockSpec((chunk, D_SSD), lambda b, c: (row(b, c), 1)),
        pl.BlockSpec((chunk, SSD_BC), lambda b, c: (row(b, c), 10)),
        pl.BlockSpec((chunk, LANES), lambda b, c: (row(b, c), DT_COL // LANES)),
        pl.BlockSpec((SSD_CONV, D_SSD), lambda b, c: (0, 0)),
        pl.BlockSpec((SSD_CONV, SSD_BC), lambda b, c: (0, 2)),
        pl.BlockSpec((1, D_SSD), lambda b, c: (0, 0)),
        pl.BlockSpec((1, SSD_BC), lambda b, c: (0, 2)),
        full((1, LANES)), full((1, LANES)), full((1, D_SSD)), full((1, D_SSD)),
    ]
    vmem = _mib(48)
    return pl.pallas_call(
        _ssd_prompt_kernel,
        grid=(nseq, nc),
        in_specs=in_specs,
        out_specs=[pl.BlockSpec((chunk, D_SSD), lambda b, c: (row(b, c), 0)),
                   pl.BlockSpec((1, SSD_HEADS, SSD_STATE, SSD_HEAD_DIM), lambda b, c: (b, 0, 0, 0)),
                   pl.BlockSpec((1, SSD_CONV - 1, D_SSD + SSD_BC), lambda b, c: (b, 0, 0))],
        out_shape=[jax.ShapeDtypeStruct((m, D_SSD), BF16),
                   jax.ShapeDtypeStruct((nseq, SSD_HEADS, SSD_STATE, SSD_HEAD_DIM), F32),
                   jax.ShapeDtypeStruct((nseq, SSD_CONV - 1, D_SSD + SSD_BC), F32)],
        scratch_shapes=[pltpu.VMEM((SSD_GROUPS, SSD_STATE, D_SSD // SSD_GROUPS), F32),
                        pltpu.VMEM((SUBLANES + chunk, D_SSD), F32),
                        pltpu.VMEM((SUBLANES + chunk, SSD_BC), F32)],
        compiler_params=_cparams(("parallel", "arbitrary"), vmem),
        name="ssd_prompt",
    )(pe, pe, pe, pe, prm["ssd_cw"], prm["ssd_cw"], prm["ssd_cb"], prm["ssd_cb"],
      prm["ssd_dtb"], prm["ssd_alog"], prm["ssd_dx"], prm["ssd_nw"])


def _rotary(x, cos, sin_signed):
    return x * cos + pltpu.roll(x, RET_DK // 2, 1) * sin_signed


def _ret_prompt_kernel(q_ref, k_ref, v_ref, g_ref, cos_ref, sin_ref, o_ref, sfin_ref, s_ref):
    c = pl.program_id(1)
    nc = pl.num_programs(1)
    rows = q_ref.shape[0]

    @pl.when(c == 0)
    def _():
        s_ref[...] = jnp.zeros_like(s_ref)

    cos = cos_ref[...]
    sin = sin_ref[...]
    ri = lax.broadcasted_iota(jnp.int32, (rows, rows), 0)
    ci = lax.broadcasted_iota(jnp.int32, (rows, rows), 1)
    dij = jnp.where(ri >= ci, (ri - ci).astype(F32), 1e30)
    tk = lax.broadcasted_iota(jnp.int32, (rows, RET_DK), 0).astype(F32)
    tv = lax.broadcasted_iota(jnp.int32, (rows, RET_DV), 0).astype(F32)
    for h in range(RET_HEADS):
        lg = RET_LOG_GAMMA[h]
        qr = _rotary(q_ref[:, h * RET_DK:(h + 1) * RET_DK], cos, sin)
        kr = _rotary(k_ref[:, h * RET_DK:(h + 1) * RET_DK], cos, sin) * (RET_DK ** -0.5)
        qb = qr.astype(BF16)
        s = lax.dot_general(qb, kr.astype(BF16), NT, preferred_element_type=F32)
        pm = (s * jnp.exp(lg * dij)).astype(BF16)
        vb = v_ref[:, h * RET_DV:(h + 1) * RET_DV].astype(BF16)
        sh = s_ref[h]
        o = _dot(pm, vb) + _dot(qb, sh.astype(BF16)) * jnp.exp(lg * (tv + 1.0))
        kw = (kr * jnp.exp(lg * ((rows - 1.0) - tk))).astype(BF16)
        s_ref[h] = sh * math.exp(lg * rows) + lax.dot_general(kw, vb, TN, preferred_element_type=F32)
        gh = g_ref[:, h * RET_DV:(h + 1) * RET_DV]
        o_ref[:, h * RET_DV:(h + 1) * RET_DV] = (_rms(o) * _silu(gh)).astype(BF16)

    @pl.when(c == nc - 1)
    def _():
        sfin_ref[0] = s_ref[...]


def ret_prompt(pe, cos, sin, nseq, seq, chunk):
    m = nseq * seq
    nc = seq // chunk
    row = lambda b, c: b * nc + c
    return pl.pallas_call(
        _ret_prompt_kernel,
        grid=(nseq, nc),
        in_specs=[pl.BlockSpec((chunk, RET_QK), lambda b, c: (row(b, c), 8)),
                  pl.BlockSpec((chunk, RET_QK), lambda b, c: (row(b, c), 9)),
                  pl.BlockSpec((chunk, D_RET), lambda b, c: (row(b, c), 2)),
                  pl.BlockSpec((chunk, D_RET), lambda b, c: (row(b, c), 3)),
                  pl.BlockSpec((chunk, RET_DK), lambda b, c: (c, 0)),
                  pl.BlockSpec((chunk, RET_DK), lambda b, c: (c, 0))],
        out_specs=[pl.BlockSpec((chunk, D_RET), lambda b, c: (row(b, c), 0)),
                   pl.BlockSpec((1, RET_HEADS, RET_DK, RET_DV), lambda b, c: (b, 0, 0, 0))],
        out_shape=[jax.ShapeDtypeStruct((m, D_RET), BF16),
                   jax.ShapeDtypeStruct((nseq, RET_HEADS, RET_DK, RET_DV), F32)],
        scratch_shapes=[pltpu.VMEM((RET_HEADS, RET_DK, RET_DV), F32)],
        compiler_params=_cparams(("parallel", "arbitrary"), _mib(40)),
        name="ret_prompt",
    )(pe, pe, pe, pe, cos, sin)


HG_SUB = 16
HG_HB = 4


def _hg_gates(hq, hf, lb):
    q = _silu(hq)
    f = lb + (1.0 - lb) * jax.nn.sigmoid(hf)
    return q, f, 1.0 - f, jnp.log(f)


def _hg_prompt_kernel(hq_ref, hf_ref, hi_ref, hg_ref, lb_ref, nw_ref, og_ref, sfin_ref, st_ref):
    c = pl.program_id(2)
    nc = pl.num_programs(2)
    rows = hq_ref.shape[0]

    @pl.when(c == 0)
    def _():
        st_ref[...] = jnp.zeros_like(st_ref)

    q_all, _, kk_all, lg_all = _hg_gates(hq_ref[...], hf_ref[...], lb_ref[...])
    cum_all = _cumsum_rows(lg_all)
    rsub = lax.broadcasted_iota(jnp.int32, (HG_SUB, 1), 0)
    for h in range(HG_HB):
        sl = slice(h * HG_DK, (h + 1) * HG_DK)
        q = q_all[:, sl]
        kk = kk_all[:, sl]
        cum = cum_all[:, sl]
        v = hi_ref[:, sl]
        vb = v.astype(BF16)
        last = cum[rows - 1:rows, :]
        st = st_ref[h]
        o = lax.dot_general((q * jnp.exp(cum)).astype(BF16), st.astype(BF16), NT,
                            preferred_element_type=F32)
        kt = (kk * jnp.exp(last - cum)).astype(BF16)
        st_ref[h] = st * jnp.exp(last) + lax.dot_general(vb, kt, TN, preferred_element_type=F32)
        blocks = []
        for blk in range(rows // HG_SUB):
            r0 = blk * HG_SUB
            qi = q[r0:r0 + HG_SUB]
            ci = cum[r0:r0 + HG_SUB]
            oi = o[r0:r0 + HG_SUB]
            if blk > 0:
                ref = cum[r0 - 1:r0, :]
                qh = (qi * jnp.exp(ci - ref)).astype(BF16)
                kh = (kk[:r0] * jnp.exp(ref - cum[:r0])).astype(BF16)
                s = lax.dot_general(qh, kh, NT, preferred_element_type=F32)
                oi = oi + _dot(s.astype(BF16), vb[:r0])
            for j in range(HG_SUB):
                e = jnp.exp(jnp.minimum(ci - ci[j:j + 1, :], 0.0))
                sj = jnp.sum(qi * e * kk[r0 + j:r0 + j + 1, :], axis=1, keepdims=True)
                sj = jnp.where(rsub >= j, sj, 0.0)
                oi = oi + sj * v[r0 + j:r0 + j + 1, :]
            blocks.append(oi)
        o = jnp.concatenate(blocks, axis=0)
        og_ref[:, sl] = (_rms(o) * nw_ref[...] * _silu(hg_ref[:, sl])).astype(BF16)

    @pl.when(c == nc - 1)
    def _():
        for h in range(HG_HB):
            sfin_ref[0, h] = st_ref[h].T


def hg_prompt(po, lb, nw, nseq, seq, chunk):
    m = nseq * seq
    nc = seq // chunk
    nhb = HG_HEADS // HG_HB
    w = HG_HB * HG_DK
    row = lambda b, hb, c: b * nc + c
    return pl.pallas_call(
        _hg_prompt_kernel,
        grid=(nseq, nhb, nc),
        in_specs=[pl.BlockSpec((chunk, w), lambda b, hb, c: (row(b, hb, c), hb)),
                  pl.BlockSpec((chunk, w), lambda b, hb, c: (row(b, hb, c), nhb + hb)),
                  pl.BlockSpec((chunk, w), lambda b, hb, c: (row(b, hb, c), 2 * nhb + hb)),
                  pl.BlockSpec((chunk, w), lambda b, hb, c: (row(b, hb, c), 3 * nhb + hb)),
                  pl.BlockSpec((1, w), lambda b, hb, c: (0, hb)),
                  pl.BlockSpec((1, HG_DV), lambda b, hb, c: (0, 0))],
        out_specs=[pl.BlockSpec((chunk, w), lambda b, hb, c: (row(b, hb, c), hb)),
                   pl.BlockSpec((1, HG_HB, HG_DK, HG_DV), lambda b, hb, c: (b, hb, 0, 0))],
        out_shape=[jax.ShapeDtypeStruct((m, D_HG), BF16),
                   jax.ShapeDtypeStruct((nseq, HG_HEADS, HG_DK, HG_DV), F32)],
        scratch_shapes=[pltpu.VMEM((HG_HB, HG_DV, HG_DK), F32)],
        compiler_params=_cparams(("parallel", "parallel", "arbitrary"), _mib(32)),
        name="hg_prompt",
    )(po, po, po, po, lb, nw)


def _lin_scan(a, b, shift):
    rows = a.shape[0]
    ri = lax.broadcasted_iota(jnp.int32, a.shape, 0)
    d = shift
    while d < rows:
        keep = ri >= d
        a_s = jnp.where(keep, pltpu.roll(a, d, 0), 1.0)
        b_s = jnp.where(keep, pltpu.roll(b, d, 0), 0.0)
        b = a * b_s + b
        a = a * a_s
        d *= 2
    return a, b


def _lru_gates(xr, wr_ref, br, wi_ref, bi, ap):
    xb = xr.astype(BF16)
    r_parts, i_parts = [], []
    for n in range(LRU_BLOCKS):
        xn = xb[:, n * LRU_BW:(n + 1) * LRU_BW]
        r_parts.append(_dot(xn, wr_ref[n].astype(BF16)))
        i_parts.append(_dot(xn, wi_ref[n].astype(BF16)))
    r = jax.nn.sigmoid(jnp.concatenate(r_parts, axis=1) + br)
    gi = jax.nn.sigmoid(jnp.concatenate(i_parts, axis=1) + bi)
    la = -LRU_C * r * _softplus(-ap)
    a = jnp.exp(la)
    th = jnp.tanh(la)
    mult = jnp.sqrt(-2.0 * th / (1.0 - th))
    return a, mult, gi


def _lru_prompt_kernel(rx_ref, rg_ref, cw_ref, cb_ref, wr_ref, br_ref, wi_ref, bi_ref, ap_ref,
                       yl_ref, hfin_ref, cfin_ref, buf_ref, hc_ref):
    c = pl.program_id(1)
    nc = pl.num_programs(1)
    rows = rx_ref.shape[0]

    @pl.when(c == 0)
    def _():
        buf_ref[0:SUBLANES, :] = jnp.zeros((SUBLANES, D_RNN), F32)
        hc_ref[...] = jnp.zeros_like(hc_ref)

    xr, tail = _causal_conv(buf_ref, rx_ref[...], cw_ref, cb_ref, 1, LRU_CONV)
    a, mult, gi = _lru_gates(xr, wr_ref, br_ref[...], wi_ref, bi_ref[...], ap_ref[...])
    ri = lax.broadcasted_iota(jnp.int32, (rows, D_RNN), 0)
    mult = jnp.where(jnp.logical_and(c == 0, ri == 0), 1.0, mult)
    pa, hb = _lin_scan(a, mult * gi * xr, 1)
    hs = pa * hc_ref[0:1, :] + hb
    hc_ref[0:1, :] = hs[rows - 1:rows, :]
    yl_ref[...] = (hs * _gelu_tanh(rg_ref[...])).astype(BF16)

    @pl.when(c == nc - 1)
    def _():
        hfin_ref[0] = hs[rows - 1:rows, :]
        cfin_ref[0] = tail


def lru_prompt(po, prm, nseq, seq, chunk):
    m = nseq * seq
    nc = seq // chunk
    row = lambda b, c: b * nc + c
    full = lambda shp: pl.BlockSpec(shp, lambda b, c: (0,) * len(shp))
    return pl.pallas_call(
        _lru_prompt_kernel,
        grid=(nseq, nc),
        in_specs=[pl.BlockSpec((chunk, D_RNN), lambda b, c: (row(b, c), 4)),
                  pl.BlockSpec((chunk, D_RNN), lambda b, c: (row(b, c), 5)),
                  full((LRU_CONV, D_RNN)), full((1, D_RNN)),
                  full((LRU_BLOCKS, LRU_BW, LRU_BW)), full((1, D_RNN)),
                  full((LRU_BLOCKS, LRU_BW, LRU_BW)), full((1, D_RNN)), full((1, D_RNN))],
        out_specs=[pl.BlockSpec((chunk, D_RNN), lambda b, c: (row(b, c), 0)),
                   pl.BlockSpec((1, 1, D_RNN), lambda b, c: (b, 0, 0)),
                   pl.BlockSpec((1, LRU_CONV - 1, D_RNN), lambda b, c: (b, 0, 0))],
        out_shape=[jax.ShapeDtypeStruct((m, D_RNN), BF16),
                   jax.ShapeDtypeStruct((nseq, 1, D_RNN), F32),
                   jax.ShapeDtypeStruct((nseq, LRU_CONV - 1, D_RNN), F32)],
        scratch_shapes=[pltpu.VMEM((SUBLANES + chunk, D_RNN), F32),
                        pltpu.VMEM((SUBLANES, D_RNN), F32)],
        compiler_params=_cparams(("parallel", "arbitrary"), _mib(48)),
        name="lru_prompt",
    )(po, po, prm["lru_cw"], prm["lru_cb"], prm["lru_wr"], prm["lru_br"], prm["lru_wi"],
      prm["lru_bi"], prm["lru_ap"])


def _state_kernel(cols_ref, vrow_ref, dec_ref, s_ref, u_ref, snew_ref, *,
                  heads, dv, dvp, ntok, qcol, kcol, dcol):
    nb = s_ref.shape[0]

    def body(b, carry):
        cb = cols_ref[b]
        for h in range(heads):
            s0 = s_ref[b, h]
            lanes = slice(h * dvp, h * dvp + dv)
            if dcol is None:
                acc = s0 * dec_ref[b, 0:1, lanes]
            else:
                acc = s0 * cb[:, dcol(h):dcol(h) + 1]
            for t in range(ntok):
                qc = cb[:, qcol(h, t):qcol(h, t) + 1]
                kc = cb[:, kcol(h, t):kcol(h, t) + 1]
                u_ref[b, t:t + 1, lanes] = jnp.sum(s0 * qc, axis=0, keepdims=True)
                acc = acc + kc * vrow_ref[b, t:t + 1, lanes]
            snew_ref[b, h] = acc
        return carry

    lax.fori_loop(0, nb, body, 0)


def state_update(cols, vrow, dec, s, bb, qcol, kcol, dcol):
    nb, heads, dk, dv = s.shape
    dvp = max(dv, LANES)
    ntok = vrow.shape[1]
    lc = cols.shape[2]
    kern = functools.partial(_state_kernel, heads=heads, dv=dv, dvp=dvp, ntok=ntok,
                             qcol=qcol, kcol=kcol, dcol=dcol)
    sbytes = bb * heads * dk * dvp * 4
    vmem = 4 * sbytes + 2 * bb * dk * lc * 4 + 6 * bb * SUBLANES * heads * dvp * 4 + _mib(8)
    return pl.pallas_call(
        kern,
        grid=(nb // bb,),
        in_specs=[pl.BlockSpec((bb, dk, lc), lambda i: (i, 0, 0)),
                  pl.BlockSpec((bb, ntok, heads * dvp), lambda i: (i, 0, 0)),
                  pl.BlockSpec((bb, 1, heads * dvp), lambda i: (i, 0, 0)),
                  pl.BlockSpec((bb, heads, dk, dv), lambda i: (i, 0, 0, 0))],
        out_specs=[pl.BlockSpec((bb, ntok, heads * dvp), lambda i: (i, 0, 0)),
                   pl.BlockSpec((bb, heads, dk, dv), lambda i: (i, 0, 0, 0))],
        out_shape=[jax.ShapeDtypeStruct((nb, ntok, heads * dvp), F32),
                   jax.ShapeDtypeStruct((nb, heads, dk, dv), F32)],
        compiler_params=_cparams(("parallel",), vmem),
        name="state_update",
    )(cols, vrow, dec, s)


def _tok(x, t, nb):
    return x[t * nb:(t + 1) * nb]


def _head_sums(x, width):
    r, n = x.shape
    tiles = []
    for h in range(n // width):
        s = jnp.sum(x[:, h * width:(h + 1) * width], axis=1, keepdims=True)
        tiles.append(jnp.broadcast_to(s, (r, width)))
    return jnp.concatenate(tiles, axis=1)


def _ssd_sample_pre_kernel(xs_ref, bc_ref, dt_ref, prevx_ref, prevb_ref, cwx_ref, cwb_ref,
                           cbx_ref, cbb_ref, dtb_ref, alog_ref,
                           xs_out, bc_out, vw_out, oi_out, ecum_out, elast_out, tailx_out, tailb_out,
                           bufx_ref, bufb_ref, *, nb, ntok):
    n = (SSD_CONV - 1) * nb
    offx = bufx_ref.shape[0] - nb * ntok
    bufx_ref[offx - n:offx, :] = prevx_ref[...]
    bufb_ref[offx - n:offx, :] = prevb_ref[...]
    xs_c, tailx = _causal_conv(bufx_ref, xs_ref[...], cwx_ref, cbx_ref, nb, SSD_CONV)
    bc_c, tailb = _causal_conv(bufb_ref, bc_ref[...], cwb_ref, cbb_ref, nb, SSD_CONV)
    xs, bc, dt, la = _ssd_gates(xs_c, bc_c, dt_ref[...], dtb_ref[...], alog_ref[...])
    tailx_out[...] = tailx
    tailb_out[...] = tailb
    xs_out[...] = xs
    bc_out[...] = bc
    cums = []
    for t in range(ntok):
        lt = _tok(la, t, nb)
        cums.append(lt if t == 0 else cums[-1] + lt)
    cumx = [_expand_heads64(cm, SSD_HEADS) for cm in cums]
    v = xs * _expand_heads64(dt, SSD_HEADS)
    kw = SSD_GROUPS * SSD_STATE
    for t in range(ntok):
        ct = _tok(bc, t, nb)[:, kw:2 * kw]
        acc = None
        for t2 in range(t + 1):
            bt = _tok(bc, t2, nb)[:, 0:kw]
            sc = _head_sums(ct * bt, SSD_STATE)
            scx = jnp.concatenate(
                [jnp.concatenate([sc[:, g * SSD_STATE:(g + 1) * SSD_STATE]] * 4, axis=1)
                 for g in range(SSD_GROUPS)], axis=1)
            term = scx * jnp.exp(cumx[t] - cumx[t2]) * _tok(v, t2, nb)
            acc = term if acc is None else acc + term
        oi_out[t * nb:(t + 1) * nb, :] = acc
        ecum_out[t * nb:(t + 1) * nb, :] = jnp.exp(cumx[t])
        vw_out[t * nb:(t + 1) * nb, :] = _tok(v, t, nb) * jnp.exp(cumx[ntok - 1] - cumx[t])
    elast_out[...] = jnp.exp(cumx[ntok - 1])


def _ssd_sample_post_kernel(oi_ref, u_ref, ecum_ref, xs_ref, z_ref, dx_ref, nw_ref, y_ref):
    o = oi_ref[...] + ecum_ref[...] * u_ref[...]
    y_ref[...] = _ssd_finish(o, xs_ref[...], z_ref[...], dx_ref[...], nw_ref[...]).astype(BF16)


def _ret_sample_pre_kernel(q_ref, k_ref, v_ref, cos_ref, sin_ref, qd_out, kd_out, oi_out, *, nb, ntok):
    rows = nb * ntok
    cos = jnp.concatenate([jnp.broadcast_to(cos_ref[t:t + 1, :], (nb, RET_DK)) for t in range(ntok)], axis=0)
    sin = jnp.concatenate([jnp.broadcast_to(sin_ref[t:t + 1, :], (nb, RET_DK)) for t in range(ntok)], axis=0)
    qr, kr = [], []
    for h in range(RET_HEADS):
        sl = slice(h * RET_DK, (h + 1) * RET_DK)
        qr.append(_rotary(q_ref[:, sl], cos, sin))
        kr.append(_rotary(k_ref[:, sl], cos, sin) * (RET_DK ** -0.5))
    qr = jnp.concatenate(qr, axis=1)
    kr = jnp.concatenate(kr, axis=1)
    v = v_ref[...]
    for t in range(ntok):
        qt = _tok(qr, t, nb)
        acc = None
        for t2 in range(t + 1):
            sc = _head_sums(qt * _tok(kr, t2, nb), RET_DK)
            vt = _tok(v, t2, nb)
            tiles = []
            for h in range(RET_HEADS):
                dec = math.exp(RET_LOG_GAMMA[h] * (t - t2))
                s = sc[:, h * RET_DK:(h + 1) * RET_DK] * dec
                tiles.append(jnp.concatenate([s, s], axis=1) * vt[:, h * RET_DV:(h + 1) * RET_DV])
            term = jnp.concatenate(tiles, axis=1)
            acc = term if acc is None else acc + term
        oi_out[t * nb:(t + 1) * nb, :] = acc
        qd = jnp.concatenate([qt[:, h * RET_DK:(h + 1) * RET_DK] * math.exp(RET_LOG_GAMMA[h] * (t + 1))
                              for h in range(RET_HEADS)], axis=1)
        kt = _tok(kr, t, nb)
        kd = jnp.concatenate([kt[:, h * RET_DK:(h + 1) * RET_DK] * math.exp(RET_LOG_GAMMA[h] * (ntok - 1 - t))
                              for h in range(RET_HEADS)], axis=1)
        qd_out[t * nb:(t + 1) * nb, :] = qd
        kd_out[t * nb:(t + 1) * nb, :] = kd


def _ret_sample_post_kernel(oi_ref, u_ref, g_ref, o_ref):
    o = oi_ref[...] + u_ref[...]
    g = g_ref[...]
    for h in range(RET_HEADS):
        sl = slice(h * RET_DV, (h + 1) * RET_DV)
        o_ref[:, sl] = (_rms(o[:, sl]) * _silu(g[:, sl])).astype(BF16)


def _hg_sample_pre_kernel(hq_ref, hf_ref, hi_ref, lb_ref, qd_out, kd_out, dl_out, oi_out, *, nb, ntok):
    q, _, kk, lg = _hg_gates(hq_ref[...], hf_ref[...], lb_ref[...])
    v = hi_ref[...]
    cums = []
    for t in range(ntok):
        lt = _tok(lg, t, nb)
        cums.append(lt if t == 0 else cums[-1] + lt)
    for t in range(ntok):
        qt = _tok(q, t, nb)
        acc = None
        for t2 in range(t + 1):
            w = qt * _tok(kk, t2, nb)
            if t2 < t:
                w = w * jnp.exp(cums[t] - cums[t2])
            term = _head_sums(w, HG_DK) * _tok(v, t2, nb)
            acc = term if acc is None else acc + term
        oi_out[t * nb:(t + 1) * nb, :] = acc
        qd_out[t * nb:(t + 1) * nb, :] = qt * jnp.exp(cums[t])
        kd_out[t * nb:(t + 1) * nb, :] = _tok(kk, t, nb) * jnp.exp(cums[ntok - 1] - cums[t])
    dl_out[...] = jnp.exp(cums[ntok - 1])


def _hg_sample_post_kernel(oi_ref, u_ref, hg_ref, nw_ref, og_ref):
    o = oi_ref[...] + u_ref[...]
    hg = hg_ref[...]
    for h in range(HG_HEADS):
        sl = slice(h * HG_DV, (h + 1) * HG_DV)
        og_ref[:, sl] = (_rms(o[:, sl]) * nw_ref[...] * _silu(hg[:, sl])).astype(BF16)


def _lru_sample_kernel(rx_ref, rg_ref, prev_ref, h0_ref, cw_ref, cb_ref, wr_ref, br_ref, wi_ref,
                       bi_ref, ap_ref, yl_out, hfin_out, tail_out, buf_ref, *, nb, ntok):
    rows = nb * ntok
    n = (LRU_CONV - 1) * nb
    off = buf_ref.shape[0] - rows
    buf_ref[off - n:off, :] = prev_ref[...]
    xr, tail = _causal_conv(buf_ref, rx_ref[...], cw_ref, cb_ref, nb, LRU_CONV)
    tail_out[...] = tail
    a, mult, gi = _lru_gates(xr, wr_ref, br_ref[...], wi_ref, bi_ref[...], ap_ref[...])
    b = mult * gi * xr
    h = h0_ref[...]
    for t in range(ntok):
        h = _tok(a, t, nb) * h + _tok(b, t, nb)
        yl_out[t * nb:(t + 1) * nb, :] = (h * _gelu_tanh(rg_ref[t * nb:(t + 1) * nb, :])).astype(BF16)
    hfin_out[...] = h


def _whole(shape):
    return pl.BlockSpec(shape, lambda i: (0,) * len(shape))


def _colblock(rows, width, idx):
    return pl.BlockSpec((rows, width), functools.partial(lambda i, k: (0, k), k=idx))


def _call_whole(kern, in_arrays, in_specs, out_shapes, scratch, name, vmem):
    return pl.pallas_call(
        kern, grid=(1,), in_specs=in_specs,
        out_specs=[_whole(s.shape) for s in out_shapes],
        out_shape=out_shapes, scratch_shapes=scratch,
        compiler_params=_cparams(("arbitrary",), vmem), name=name,
    )(*in_arrays)


def _to_token_major(s):
    nb, w, c = s.shape
    return jnp.transpose(s, (1, 0, 2)).reshape(w * nb, c)


def _from_token_major(x, nb):
    w = x.shape[0] // nb
    return jnp.transpose(x.reshape(w, nb, x.shape[1]), (1, 0, 2))


def _rows_to_batch(x, nb, heads, dv):
    ntok = x.shape[0] // nb
    dvp = max(dv, LANES)
    x = jnp.transpose(x.reshape(ntok, nb, heads, dv), (1, 0, 2, 3))
    if dvp != dv:
        x = jnp.pad(x, ((0, 0), (0, 0), (0, 0), (0, dvp - dv)))
    return x.reshape(nb, ntok, heads * dvp)


def _batch_to_rows(u, nb, heads, dv):
    ntok = u.shape[1]
    dvp = max(dv, LANES)
    u = u.reshape(nb, ntok, heads, dvp)[..., :dv]
    return jnp.transpose(u, (1, 0, 2, 3)).reshape(ntok * nb, heads * dv)


def _cols(vecs, nb, units, dk, lanes):
    x = jnp.stack([v.reshape(nb, units, dk) for v in vecs], axis=-1)
    x = jnp.transpose(x, (0, 2, 1, 3)).reshape(nb, dk, units * len(vecs))
    return jnp.pad(x, ((0, 0), (0, 0), (0, lanes - x.shape[2])))


def ssd_sample(pe, z, s_ssm, s_conv, prm, nb, ntok):
    rows = nb * ntok
    n = (SSD_CONV - 1) * nb
    prev = _to_token_major(s_conv)
    off = _conv_off(nb, SSD_CONV)
    f = lambda *shape: jax.ShapeDtypeStruct(shape, F32)
    outs = _call_whole(
        functools.partial(_ssd_sample_pre_kernel, nb=nb, ntok=ntok),
        [pe, pe, pe, prev, prev, prm["ssd_cw"], prm["ssd_cw"], prm["ssd_cb"], prm["ssd_cb"],
         prm["ssd_dtb"], prm["ssd_alog"]],
        [_colblock(rows, D_SSD, 1), _colblock(rows, SSD_BC, 10), _colblock(rows, LANES, DT_COL // LANES),
         _colblock(n, D_SSD, 0), _colblock(n, SSD_BC, 2),
         _colblock(SSD_CONV, D_SSD, 0), _colblock(SSD_CONV, SSD_BC, 2),
         _colblock(1, D_SSD, 0), _colblock(1, SSD_BC, 2), _whole((1, LANES)), _whole((1, LANES))],
        [f(rows, D_SSD), f(rows, SSD_BC), f(rows, D_SSD), f(rows, D_SSD), f(rows, D_SSD),
         f(nb, D_SSD), f(n, D_SSD), f(n, SSD_BC)],
        [pltpu.VMEM((off + rows, D_SSD), F32), pltpu.VMEM((off + rows, SSD_BC), F32)],
        "ssd_sample_pre", _mib(56))
    xs, bc, vw, oi, ecum, elast, tailx, tailb = outs
    kw = SSD_GROUPS * SSD_STATE
    vecs = [bc[t * nb:(t + 1) * nb, kw:] for t in range(ntok)] + [bc[t * nb:(t + 1) * nb, :kw] for t in range(ntok)]
    cols = _cols(vecs, nb, SSD_GROUPS, SSD_STATE, LANES)
    nv = 2 * ntok
    hpg = SSD_HEADS // SSD_GROUPS
    u, s_new = state_update(
        cols, _rows_to_batch(vw, nb, SSD_HEADS, SSD_HEAD_DIM),
        _rows_to_batch(elast, nb, SSD_HEADS, SSD_HEAD_DIM), s_ssm, 2,
        qcol=lambda h, t: (h // hpg) * nv + t, kcol=lambda h, t: (h // hpg) * nv + ntok + t, dcol=None)
    u = _batch_to_rows(u, nb, SSD_HEADS, SSD_HEAD_DIM)
    (y,) = _call_whole(
        _ssd_sample_post_kernel, [oi, u, ecum, xs, z, prm["ssd_dx"], prm["ssd_nw"]],
        [_whole((rows, D_SSD))] * 4 + [_colblock(rows, D_SSD, 0), _whole((1, D_SSD)), _whole((1, D_SSD))],
        [jax.ShapeDtypeStruct((rows, D_SSD), BF16)], [], "ssd_sample_post", _mib(48))
    conv_new = _from_token_major(jnp.concatenate([tailx, tailb], axis=1), nb)
    return y, s_new, conv_new


def ret_sample(pe, s_ret, cos, sin, nb, ntok):
    rows = nb * ntok
    f = lambda *shape: jax.ShapeDtypeStruct(shape, F32)
    qd, kd, oi = _call_whole(
        functools.partial(_ret_sample_pre_kernel, nb=nb, ntok=ntok),
        [pe, pe, pe, cos, sin],
        [_colblock(rows, RET_QK, 8), _colblock(rows, RET_QK, 9), _colblock(rows, D_RET, 2),
         _whole(cos.shape), _whole(sin.shape)],
        [f(rows, RET_QK), f(rows, RET_QK), f(rows, D_RET)], [], "ret_sample_pre", _mib(48))
    vecs = [qd[t * nb:(t + 1) * nb] for t in range(ntok)] + [kd[t * nb:(t + 1) * nb] for t in range(ntok)]
    cols = _cols(vecs, nb, RET_HEADS, RET_DK, LANES)
    nv = 2 * ntok
    dec = jnp.broadcast_to(
        jnp.repeat(jnp.asarray([math.exp(lg * ntok) for lg in RET_LOG_GAMMA], F32), RET_DV)[None, None, :],
        (nb, 1, D_RET))
    v = _rows_to_batch(pe[:, 2 * D_RET:3 * D_RET], nb, RET_HEADS, RET_DV)
    u, s_new = state_update(cols, v, dec, s_ret, 2,
                            qcol=lambda h, t: h * nv + t, kcol=lambda h, t: h * nv + ntok + t, dcol=None)
    u = _batch_to_rows(u, nb, RET_HEADS, RET_DV)
    (o,) = _call_whole(
        _ret_sample_post_kernel, [oi, u, pe],
        [_whole((rows, D_RET)), _whole((rows, D_RET)), _colblock(rows, D_RET, 3)],
        [jax.ShapeDtypeStruct((rows, D_RET), BF16)], [], "ret_sample_post", _mib(40))
    return o, s_new


def hg_sample(po, s_hg, lb, nw, nb, ntok):
    rows = nb * ntok
    f = lambda *shape: jax.ShapeDtypeStruct(shape, F32)
    qd, kd, dl, oi = _call_whole(
        functools.partial(_hg_sample_pre_kernel, nb=nb, ntok=ntok),
        [po, po, po, lb],
        [_colblock(rows, D_HG, 0), _colblock(rows, D_HG, 1), _colblock(rows, D_HG, 2), _whole((1, D_HG))],
        [f(rows, D_HG), f(rows, D_HG), f(nb, D_HG), f(rows, D_HG)], [], "hg_sample_pre", _mib(48))
    ncol = 16
    vecs = ([qd[t * nb:(t + 1) * nb] for t in range(ntok)] + [kd[t * nb:(t + 1) * nb] for t in range(ntok)]
            + [dl] + [jnp.zeros_like(dl)] * (ncol - 2 * ntok - 1))
    cols = _cols(vecs, nb, HG_HEADS, HG_DK, HG_HEADS * ncol)
    v = _rows_to_batch(po[:, 2 * D_HG:3 * D_HG], nb, HG_HEADS, HG_DV)
    dec = jnp.zeros((nb, 1, D_HG), F32)
    u, s_new = state_update(cols, v, dec, s_hg, 2,
                            qcol=lambda h, t: h * ncol + t, kcol=lambda h, t: h * ncol + ntok + t,
                            dcol=lambda h: h * ncol + 2 * ntok)
    u = _batch_to_rows(u, nb, HG_HEADS, HG_DV)
    (og,) = _call_whole(
        _hg_sample_post_kernel, [oi, u, po, nw],
        [_whole((rows, D_HG)), _whole((rows, D_HG)), _colblock(rows, D_HG, 3), _whole((1, HG_DV))],
        [jax.ShapeDtypeStruct((rows, D_HG), BF16)], [], "hg_sample_post", _mib(40))
    return og, s_new


def lru_sample(po, s_lru, s_lconv, prm, nb, ntok):
    rows = nb * ntok
    n = (LRU_CONV - 1) * nb
    off = _conv_off(nb, LRU_CONV)
    prev = _to_token_major(s_lconv)
    yl, hfin, tail = _call_whole(
        functools.partial(_lru_sample_kernel, nb=nb, ntok=ntok),
        [po, po, prev, s_lru, prm["lru_cw"], prm["lru_cb"], prm["lru_wr"], prm["lru_br"], prm["lru_wi"],
         prm["lru_bi"], prm["lru_ap"]],
        [_colblock(rows, D_RNN, 4), _colblock(rows, D_RNN, 5), _whole((n, D_RNN)), _whole((nb, D_RNN)),
         _whole((LRU_CONV, D_RNN)), _whole((1, D_RNN)), _whole((LRU_BLOCKS, LRU_BW, LRU_BW)),
         _whole((1, D_RNN)), _whole((LRU_BLOCKS, LRU_BW, LRU_BW)), _whole((1, D_RNN)), _whole((1, D_RNN))],
        [jax.ShapeDtypeStruct((rows, D_RNN), BF16), jax.ShapeDtypeStruct((nb, D_RNN), F32),
         jax.ShapeDtypeStruct((n, D_RNN), F32)],
        [pltpu.VMEM((off + rows, D_RNN), F32)], "lru_sample", _mib(48))
    return yl, hfin, _from_token_major(tail, nb)


def _rope_tables(pos):
    half = RET_DK // 2
    inv = ROPE_BASE ** (-jnp.arange(half, dtype=F32) / half)
    ang = pos.astype(F32)[:, None] * inv[None, :]
    cos, sin = jnp.cos(ang), jnp.sin(ang)
    return jnp.concatenate([cos, cos], axis=1), jnp.concatenate([-sin, sin], axis=1)


def _pad_lanes(v):
    return jnp.pad(v.astype(F32), (0, LANES - v.shape[0])).reshape(1, LANES)


def _prepare(p):
    w = p["even_w_in"][0]
    o_z, o_xbc, o_dt = 0, D_SSD, 2 * D_SSD + SSD_BC
    o_q = o_dt + SSD_HEADS
    o_k, o_v = o_q + RET_QK, o_q + 2 * RET_QK
    o_g = o_v + D_RET
    packed = jnp.concatenate([
        w[:, o_z:o_z + D_SSD], w[:, o_xbc:o_xbc + D_SSD], w[:, o_v:o_v + D_RET], w[:, o_g:o_g + D_RET],
        w[:, o_q:o_q + RET_QK], w[:, o_k:o_k + RET_QK], w[:, o_xbc + D_SSD:o_xbc + D_SSD + SSD_BC],
        w[:, o_dt:o_dt + SSD_HEADS],
        jnp.zeros((D_MODEL, EVEN_PACKED - D_IN_EVEN), F32)], axis=1).astype(BF16)
    lbs = jnp.cumsum(jax.nn.softmax(p["hg_lower_bounds"].astype(F32), axis=0), axis=0)
    lbs = lbs - lbs[0]
    return {
        "even_w": packed,
        "ssd_cw": p["ssd_conv_w"][0], "ssd_cb": p["ssd_conv_b"][0].reshape(1, -1),
        "ssd_dtb": _pad_lanes(p["ssd_dt_bias"][0]), "ssd_alog": _pad_lanes(p["ssd_A_log"][0]),
        "ssd_dx": jnp.repeat(p["ssd_D"][0], SSD_HEAD_DIM).reshape(1, D_SSD),
        "ssd_nw": p["ssd_norm_w"][0].reshape(1, D_SSD),
        "hg_lb": lbs[1].reshape(1, D_HG), "hg_nw": p["hg_norm_w"][0].reshape(1, HG_DV),
        "lru_cw": p["lru_conv_w"][0], "lru_cb": p["lru_conv_b"][0].reshape(1, D_RNN),
        "lru_wr": p["lru_w_r"][0], "lru_br": p["lru_b_r"][0].reshape(1, D_RNN),
        "lru_wi": p["lru_w_i"][0], "lru_bi": p["lru_b_i"][0].reshape(1, D_RNN),
        "lru_ap": p["lru_a_param"][0].reshape(1, D_RNN),
    }


def _ffn(x, p, prm, l, prev, shift, group_rows, bm):
    act, st = ffn_up(x, p["norm_ffn"][l], p["ffn_w_up"][l], p["ffn_conv_w"][l], p["ffn_conv_b"][l],
                     prev, shift, group_rows, bm, 512)
    return mm_out([act], p["ffn_w_down"][l], x, bm, 256), st


def _trunk_prompt(x, p, prm, nseq, seq):
    m = nseq * seq
    bm = min(1024, seq)
    cos, sin = _rope_tables(jnp.arange(seq, dtype=jnp.int32))
    pe = mm_in(x, p["norm_mix"][0], prm["even_w"], bm, 512)
    y, ssm, ssm_conv = ssd_prompt(pe, prm, nseq, seq, 128)
    o, ret = ret_prompt(pe, cos, sin, nseq, seq, 128)
    x = mm_out([y, o], p["even_w_out"][0], x, bm, 512)
    zeros_ffn = jnp.zeros((nseq, FFN_CONV - 1, D_FF), F32)
    x, ffn0 = _ffn(x, p, prm, 0, zeros_ffn, 1, seq, bm)
    po = mm_in(x, p["norm_mix"][1], p["odd_w_in"][0], bm, 512)
    og, hgrn = hg_prompt(po, prm["hg_lb"], prm["hg_nw"], nseq, seq, 64)
    yl, lru, lru_conv = lru_prompt(po, prm, nseq, seq, 256)
    x = mm_out([og, yl], p["odd_w_out"][0], x, bm, 512)
    x, ffn1 = _ffn(x, p, prm, 1, zeros_ffn, 1, seq, bm)
    y_out = rmsnorm(x, p["norm_final"], min(512, m))
    return (y_out.reshape(nseq, seq, D_MODEL), ssm[None], ssm_conv[None], ret[None], hgrn[None],
            lru.reshape(1, nseq, D_RNN), lru_conv[None], jnp.stack([ffn0, ffn1]))


def _trunk_sample(x, st, p, prm, nb, ntok):
    rows = nb * ntok
    cos, sin = _rope_tables(PAST_LEN + jnp.arange(ntok, dtype=jnp.int32))
    pe = mm_in(x, p["norm_mix"][0], prm["even_w"], rows, 512)
    y, ssm, ssm_conv = ssd_sample(pe, pe, st["ssm"][0], st["ssm_conv"][0], prm, nb, ntok)
    o, ret = ret_sample(pe, st["ret"][0], cos, sin, nb, ntok)
    x = mm_out([y, o], p["even_w_out"][0], x, rows, 512)
    x, ffn0 = _ffn(x, p, prm, 0, _to_token_major(st["ffn_conv"][0])[None], nb, rows, rows)
    po = mm_in(x, p["norm_mix"][1], p["odd_w_in"][0], rows, 512)
    og, hgrn = hg_sample(po, st["hgrn"][0], prm["hg_lb"], prm["hg_nw"], nb, ntok)
    yl, lru, lru_conv = lru_sample(po, st["lru"][0], st["lru_conv"][0], prm, nb, ntok)
    x = mm_out([og, yl], p["odd_w_out"][0], x, rows, 512)
    x, ffn1 = _ffn(x, p, prm, 1, _to_token_major(st["ffn_conv"][1])[None], nb, rows, rows)
    y_out = rmsnorm(x, p["norm_final"], rows)
    y_out = jnp.transpose(y_out.reshape(ntok, nb, D_MODEL), (1, 0, 2))
    ffn = jnp.stack([_from_token_major(ffn0[0], nb), _from_token_major(ffn1[0], nb)])
    return (y_out, ssm[None], ssm_conv[None], ret[None], hgrn[None], lru[None], lru_conv[None], ffn)


def kernel(x_prompt, x_sample, state_ssm, state_ssm_conv, state_ret, state_hgrn, state_lru, state_lru_conv, state_ffn_conv, norm_mix, norm_ffn, norm_final, even_w_in, ssd_conv_w, ssd_conv_b, ssd_dt_bias, ssd_A_log, ssd_D, ssd_norm_w, even_w_out, odd_w_in, hg_lower_bounds, hg_norm_w, lru_conv_w, lru_conv_b, lru_w_r, lru_b_r, lru_w_i, lru_b_i, lru_a_param, odd_w_out, ffn_w_up, ffn_conv_w, ffn_conv_b, ffn_w_down):
    p = {
        "norm_mix": norm_mix, "norm_ffn": norm_ffn, "norm_final": norm_final,
        "even_w_in": even_w_in, "ssd_conv_w": ssd_conv_w, "ssd_conv_b": ssd_conv_b,
        "ssd_dt_bias": ssd_dt_bias, "ssd_A_log": ssd_A_log, "ssd_D": ssd_D,
        "ssd_norm_w": ssd_norm_w, "even_w_out": even_w_out, "odd_w_in": odd_w_in,
        "hg_lower_bounds": hg_lower_bounds, "hg_norm_w": hg_norm_w,
        "lru_conv_w": lru_conv_w, "lru_conv_b": lru_conv_b, "lru_w_r": lru_w_r,
        "lru_b_r": lru_b_r, "lru_w_i": lru_w_i, "lru_b_i": lru_b_i,
        "lru_a_param": lru_a_param, "odd_w_out": odd_w_out, "ffn_w_up": ffn_w_up,
        "ffn_conv_w": ffn_conv_w, "ffn_conv_b": ffn_conv_b, "ffn_w_down": ffn_w_down,
    }
    prm = _prepare(p)
    nseq, seq, _ = x_prompt.shape
    nb, ntok, _ = x_sample.shape
    st = {"ssm": state_ssm, "ssm_conv": state_ssm_conv, "ret": state_ret, "hgrn": state_hgrn,
          "lru": state_lru, "lru_conv": state_lru_conv, "ffn_conv": state_ffn_conv}
    yp = _trunk_prompt(x_prompt.reshape(nseq * seq, D_MODEL), p, prm, nseq, seq)
    xs_tm = jnp.transpose(x_sample, (1, 0, 2)).reshape(ntok * nb, D_MODEL)
    ys = _trunk_sample(xs_tm, st, p, prm, nb, ntok)
    return (yp[0], ys[0], yp[1], ys[1], yp[2], ys[2], yp[3], ys[3], yp[4], ys[4],
            yp[5], ys[5], yp[6], ys[6], yp[7], ys[7])
```

```python
import functools
import math

import numpy as np
import jax
import jax.numpy as jnp
from jax import lax
from jax.experimental import pallas as pl
from jax.experimental.pallas import tpu as pltpu

F32 = jnp.float32
BF16 = jnp.bfloat16
EPS = 1e-6

D_MODEL = 2048
PAST_LEN = 16384
SSD_HEADS = 32
SSD_HEAD_DIM = 64
D_SSD = SSD_HEADS * SSD_HEAD_DIM
SSD_GROUPS = 4
SSD_STATE = 128
SSD_CONV = 4
SSD_BC = 2 * SSD_GROUPS * SSD_STATE
RET_HEADS = 8
RET_DK = 128
RET_DV = 256
RET_QK = RET_HEADS * RET_DK
D_RET = RET_HEADS * RET_DV
ROPE_BASE = 10000.0
HG_HEADS = 16
HG_DK = 128
HG_DV = 128
D_HG = HG_HEADS * HG_DV
D_RNN = 2048
LRU_BLOCKS = 8
LRU_BW = D_RNN // LRU_BLOCKS
LRU_CONV = 4
LRU_C = 8.0
D_FF = 5632
FFN_CONV = 3
D_IN_EVEN = D_SSD + (D_SSD + SSD_BC) + SSD_HEADS + 2 * RET_QK + 2 * D_RET
EVEN_PACKED = 11776
DT_COL = 11264

V7X_VMEM_BYTES = 64 * 1024 * 1024
V7X_VMEM_CAP = 60 * 1024 * 1024
LANES = 128
SUBLANES = 8

RET_LOG_GAMMA = [float(v) for v in np.log1p(-np.exp(np.linspace(
    math.log(1.0 / 32.0), math.log(1.0 / 512.0), RET_HEADS, dtype=np.float32))).astype(np.float32)]

NT = (((1,), (1,)), ((), ()))
TN = (((0,), (0,)), ((), ()))


def _cparams(sem, vmem_bytes):
    return pltpu.CompilerParams(dimension_semantics=sem,
                                vmem_limit_bytes=int(min(V7X_VMEM_CAP, vmem_bytes)))


def _mib(n):
    return n * 1024 * 1024


def _silu(x):
    return x * jax.nn.sigmoid(x)


def _softplus(x):
    return jnp.maximum(x, 0.0) + jnp.log1p(jnp.exp(-jnp.abs(x)))


def _gelu_tanh(x):
    return 0.5 * x * (1.0 + jnp.tanh(math.sqrt(2.0 / math.pi) * (x + 0.044715 * (x * x * x))))


def _rms(x):
    return x * lax.rsqrt(jnp.mean(x * x, axis=-1, keepdims=True) + EPS)


def _dot(a, b):
    return jnp.dot(a, b, preferred_element_type=F32)


def _tril_ones(n):
    r = lax.broadcasted_iota(jnp.int32, (n, n), 0)
    c = lax.broadcasted_iota(jnp.int32, (n, n), 1)
    return (r >= c).astype(F32)


def _cumsum_rows(x):
    return jnp.dot(_tril_ones(x.shape[0]), x, precision=lax.Precision.HIGHEST,
                   preferred_element_type=F32)


def _causal_conv(buf_ref, x, w_ref, b_ref, shift, width):
    rows = x.shape[0]
    off = buf_ref.shape[0] - rows
    n = (width - 1) * shift
    buf_ref[off:off + rows, :] = x
    acc = None
    for j in range(width):
        start = off - (width - 1 - j) * shift
        term = buf_ref[start:start + rows, :] * w_ref[j:j + 1, :]
        acc = term if acc is None else acc + term
    out = b_ref[...] + acc
    tail = buf_ref[off + rows - n:off + rows, :]
    buf_ref[off - n:off, :] = tail
    return out, tail


def _conv_off(shift, width):
    n = (width - 1) * shift
    return -(-n // SUBLANES) * SUBLANES


def _mm_in_kernel(x_ref, nw_ref, w_ref, o_ref, xn_ref):
    @pl.when(pl.program_id(1) == 0)
    def _():
        xn_ref[...] = (_rms(x_ref[...]) * nw_ref[...]).astype(BF16)
    o_ref[...] = _dot(xn_ref[...], w_ref[...].astype(BF16))


def mm_in(x, nw, w, bm, bn):
    m, d = x.shape
    n = w.shape[1]
    wb = w.dtype.itemsize
    vmem = 2 * bm * d * 4 + bm * d * 2 + 2 * d * bn * wb + d * bn * 2 + 2 * bm * bn * 4 + bm * d * 4 + _mib(6)
    return pl.pallas_call(
        _mm_in_kernel,
        grid=(m // bm, n // bn),
        in_specs=[pl.BlockSpec((bm, d), lambda i, j: (i, 0)),
                  pl.BlockSpec((1, d), lambda i, j: (0, 0)),
                  pl.BlockSpec((d, bn), lambda i, j: (0, j))],
        out_specs=pl.BlockSpec((bm, bn), lambda i, j: (i, j)),
        out_shape=jax.ShapeDtypeStruct((m, n), F32),
        scratch_shapes=[pltpu.VMEM((bm, d), BF16)],
        compiler_params=_cparams(("parallel", "arbitrary"), vmem),
        name="mm_in",
    )(x, nw.reshape(1, d), w)


def _mm_in_t_kernel(offs_ref, x_ref, nw_ref, wt_ref, o_ref, xn_ref):
    del offs_ref
    @pl.when(pl.program_id(1) == 0)
    def _():
        xn_ref[...] = (_rms(x_ref[...]) * nw_ref[...]).astype(BF16)
    o_ref[...] = lax.dot_general(xn_ref[...], wt_ref[...].astype(BF16), NT, preferred_element_type=F32)


ROW_ALIGN = 32


def mm_in_t(x, nw, wt, row_offsets, bm, bn):
    m, d = x.shape
    nblk = len(row_offsets)
    assert all(o % ROW_ALIGN == 0 for o in row_offsets)
    vmem = 2 * bm * d * 4 + bm * d * 2 + 2 * d * bn * 4 + d * bn * 2 + 2 * bm * bn * 4 + bm * d * 4 + _mib(6)
    grid_spec = pltpu.PrefetchScalarGridSpec(
        num_scalar_prefetch=1,
        grid=(m // bm, nblk),
        in_specs=[pl.BlockSpec((bm, d), lambda i, j, offs: (i, 0)),
                  pl.BlockSpec((1, d), lambda i, j, offs: (0, 0)),
                  pl.BlockSpec((pl.Element(bn), pl.Element(d)),
                               lambda i, j, offs: (offs[j] * ROW_ALIGN, 0))],
        out_specs=pl.BlockSpec((bm, bn), lambda i, j, offs: (i, j)),
        scratch_shapes=[pltpu.VMEM((bm, d), BF16)])
    return pl.pallas_call(
        _mm_in_t_kernel,
        grid_spec=grid_spec,
        out_shape=jax.ShapeDtypeStruct((m, nblk * bn), F32),
        compiler_params=_cparams(("parallel", "arbitrary"), vmem),
        name="mm_in_t",
    )(jnp.asarray([o // ROW_ALIGN for o in row_offsets], jnp.int32), x, nw.reshape(1, d), wt)


def _mm_out_kernel(*refs, nparts):
    a_refs = refs[:nparts]
    w_refs = refs[nparts:2 * nparts]
    r_ref = refs[2 * nparts]
    o_ref = refs[2 * nparts + 1]
    acc = r_ref[...]
    for a_ref, w_ref in zip(a_refs, w_refs):
        acc = acc + _dot(a_ref[...], w_ref[...].astype(BF16))
    o_ref[...] = acc


def mm_out(parts, w, layer, resid, bm, bn):
    nparts = len(parts)
    m, kp = parts[0].shape
    n = w.shape[2]
    vmem = nparts * (2 * bm * kp * 2 + 2 * kp * bn * 4 + kp * bn * 2) + 4 * bm * bn * 4 + bm * bn * 4 + _mib(6)
    in_specs = [pl.BlockSpec((bm, kp), lambda i, j: (i, 0)) for _ in range(nparts)]
    in_specs += [pl.BlockSpec((None, kp, bn), functools.partial(lambda i, j, p: (layer, p, j), p=p))
                 for p in range(nparts)]
    in_specs += [pl.BlockSpec((bm, bn), lambda i, j: (i, j))]
    return pl.pallas_call(
        functools.partial(_mm_out_kernel, nparts=nparts),
        grid=(m // bm, n // bn),
        in_specs=in_specs,
        out_specs=pl.BlockSpec((bm, bn), lambda i, j: (i, j)),
        out_shape=jax.ShapeDtypeStruct((m, n), F32),
        compiler_params=_cparams(("parallel", "parallel"), vmem),
        name="mm_out",
    )(*parts, *([w] * nparts), resid)


def _rmsnorm_kernel(x_ref, nw_ref, o_ref):
    o_ref[...] = _rms(x_ref[...]) * nw_ref[...]


def rmsnorm(x, nw, bm):
    m, d = x.shape
    return pl.pallas_call(
        _rmsnorm_kernel,
        grid=(m // bm,),
        in_specs=[pl.BlockSpec((bm, d), lambda i: (i, 0)), pl.BlockSpec((1, d), lambda i: (0, 0))],
        out_specs=pl.BlockSpec((bm, d), lambda i: (i, 0)),
        out_shape=jax.ShapeDtypeStruct((m, d), F32),
        compiler_params=_cparams(("parallel",), 6 * bm * d * 4 + _mib(4)),
        name="rmsnorm",
    )(x, nw.reshape(1, d))


def _ffn_up_kernel(x_ref, nw_ref, wg_ref, wu_ref, cw_ref, cb_ref, prev_ref,
                   act_ref, st_ref, xn_ref, gbuf_ref, carry_ref, *, shift, blocks_per_group):
    i = pl.program_id(0)
    j = pl.program_id(1)
    bm = x_ref.shape[0]
    n = (FFN_CONV - 1) * shift
    off = gbuf_ref.shape[0] - bm

    @pl.when(j == 0)
    def _():
        xn_ref[...] = (_rms(x_ref[...]) * nw_ref[...]).astype(BF16)

    xn = xn_ref[...]
    g = _dot(xn, wg_ref[...].astype(BF16))
    u = _dot(xn, wu_ref[...].astype(BF16))
    first = (i % blocks_per_group) == 0

    @pl.when(first)
    def _():
        gbuf_ref[off - n:off, :] = prev_ref[0]

    @pl.when(jnp.logical_not(first))
    def _():
        gbuf_ref[off - n:off, :] = carry_ref[j]

    gc, tail = _causal_conv(gbuf_ref, g, cw_ref, cb_ref, shift, FFN_CONV)
    act_ref[...] = (_silu(gc) * u).astype(BF16)
    carry_ref[j] = tail
    st_ref[0] = tail


def ffn_up(x, nw, w_up, layer, cw, cb, prev, shift, group_rows, bm, bf):
    m, d = x.shape
    f = cw.shape[1]
    n = (FFN_CONV - 1) * shift
    off = _conv_off(shift, FFN_CONV)
    groups = m // group_rows
    bpg = group_rows // bm
    nf = f // bf
    vmem = (2 * bm * d * 4 + bm * d * 2 + 4 * d * bf * 4 + 2 * d * bf * 2 + 2 * bm * bf * 2
            + (off + bm) * bf * 4 + nf * max(n, SUBLANES) * bf * 4 + 4 * n * bf * 4 + 6 * bm * bf * 4 + _mib(6))
    act, st = pl.pallas_call(
        functools.partial(_ffn_up_kernel, shift=shift, blocks_per_group=bpg),
        grid=(m // bm, nf),
        in_specs=[pl.BlockSpec((bm, d), lambda i, j: (i, 0)),
                  pl.BlockSpec((1, d), lambda i, j: (0, 0)),
                  pl.BlockSpec((None, d, bf), lambda i, j: (layer, 0, j)),
                  pl.BlockSpec((None, d, bf), lambda i, j: (layer, 0, j + nf)),
                  pl.BlockSpec((FFN_CONV, bf), lambda i, j: (0, j)),
                  pl.BlockSpec((1, bf), lambda i, j: (0, j)),
                  pl.BlockSpec((1, n, bf), lambda i, j: (i // bpg, 0, j))],
        out_specs=[pl.BlockSpec((bm, bf), lambda i, j: (i, j)),
                   pl.BlockSpec((1, n, bf), lambda i, j: (i, 0, j))],
        out_shape=[jax.ShapeDtypeStruct((m, f), BF16),
                   jax.ShapeDtypeStruct((m // bm, n, f), F32)],
        scratch_shapes=[pltpu.VMEM((bm, d), BF16),
                        pltpu.VMEM((off + bm, bf), F32),
                        pltpu.VMEM((nf, n, bf), F32)],
        compiler_params=_cparams(("arbitrary", "arbitrary"), vmem),
        name="ffn_up",
    )(x, nw.reshape(1, d), w_up, w_up, cw, cb.reshape(1, f), prev)
    return act, st[bpg - 1::bpg]


def _expand_heads64(x, nheads):
    r = x.shape[0]
    lo = lax.broadcasted_iota(jnp.int32, (r, LANES), 1) < SSD_HEAD_DIM
    tiles = []
    for p in range(nheads // 2):
        a0 = jnp.broadcast_to(x[:, 2 * p:2 * p + 1], (r, LANES))
        a1 = jnp.broadcast_to(x[:, 2 * p + 1:2 * p + 2], (r, LANES))
        tiles.append(jnp.where(lo, a0, a1))
    return jnp.concatenate(tiles, axis=1)


def _ssd_gates(xs_c, bc_c, dt_raw, dtb, alog):
    xs = _silu(xs_c)
    bc = _silu(bc_c)
    dt = _softplus(dt_raw + dtb)
    la = dt * (-jnp.exp(alog))
    return xs, bc, dt, la


def _ssd_finish(o, xs, z, dx, nw):
    y = (o + dx * xs) * _silu(z)
    gw = D_SSD // SSD_GROUPS
    y = jnp.concatenate([_rms(y[:, g * gw:(g + 1) * gw]) for g in range(SSD_GROUPS)], axis=1)
    return y * nw


def _ssd_prompt_kernel(z_ref, xs_ref, bc_ref, dt_ref, cwx_ref, cwb_ref, cbx_ref, cbb_ref,
                       dtb_ref, alog_ref, dx_ref, nw_ref,
                       y_ref, sfin_ref, cfin_ref, s_ref, bufx_ref, bufb_ref):
    c = pl.program_id(1)
    nc = pl.num_programs(1)
    rows = xs_ref.shape[0]
    gw = D_SSD // SSD_GROUPS
    hpg = SSD_HEADS // SSD_GROUPS

    @pl.when(c == 0)
    def _():
        s_ref[...] = jnp.zeros_like(s_ref)
        bufx_ref[0:SUBLANES, :] = jnp.zeros((SUBLANES, D_SSD), F32)
        bufb_ref[0:SUBLANES, :] = jnp.zeros((SUBLANES, SSD_BC), F32)

    xs_c, tailx = _causal_conv(bufx_ref, xs_ref[...], cwx_ref, cbx_ref, 1, SSD_CONV)
    bc_c, tailb = _causal_conv(bufb_ref, bc_ref[...], cwb_ref, cbb_ref, 1, SSD_CONV)
    xs, bc, dt, la = _ssd_gates(xs_c, bc_c, dt_ref[...], dtb_ref[...], alog_ref[...])
    cum = _cumsum_rows(la)
    cum_t = cum.T
    cumx = _expand_heads64(cum, SSD_HEADS)
    dtx = _expand_heads64(dt, SSD_HEADS)
    lastx = cumx[rows - 1:rows, :]
    ecum = jnp.exp(cumx)
    wx = jnp.exp(lastx - cumx)
    elast = jnp.exp(lastx)
    v_all = xs * dtx
    vw_all = v_all * wx

    ri = lax.broadcasted_iota(jnp.int32, (rows, rows), 0)
    ci = lax.broadcasted_iota(jnp.int32, (rows, rows), 1)
    causal = ri >= ci
    lo = lax.broadcasted_iota(jnp.int32, (rows, LANES), 1) < SSD_HEAD_DIM

    o_groups = []
    for g in range(SSD_GROUPS):
        kb = bc[:, g * SSD_STATE:(g + 1) * SSD_STATE].astype(BF16)
        qb = bc[:, (SSD_GROUPS + g) * SSD_STATE:(SSD_GROUPS + g + 1) * SSD_STATE].astype(BF16)
        qk = lax.dot_general(qb, kb, NT, preferred_element_type=F32)
        o_tiles = []
        for p in range(hpg // 2):
            acc = None
            for q in range(2):
                h = g * hpg + 2 * p + q
                diff = cum[:, h:h + 1] - cum_t[h:h + 1, :]
                dec = jnp.exp(jnp.where(causal, diff, -1e30))
                pm = (qk * dec).astype(BF16)
                col = g * gw + p * LANES
                vp = v_all[:, col:col + LANES]
                vh = jnp.where(lo, vp, 0.0) if q == 0 else jnp.where(lo, 0.0, vp)
                t = _dot(pm, vh.astype(BF16))
                acc = t if acc is None else acc + t
            o_tiles.append(acc)
        o_intra = jnp.concatenate(o_tiles, axis=1)
        sg = s_ref[g]
        sl = slice(g * gw, (g + 1) * gw)
        o_inter = _dot(qb, sg.astype(BF16)) * ecum[:, sl]
        upd = lax.dot_general(kb, vw_all[:, sl].astype(BF16), TN, preferred_element_type=F32)
        s_ref[g] = sg * elast[:, sl] + upd
        o_groups.append(o_intra + o_inter)
    o = jnp.concatenate(o_groups, axis=1)
    y_ref[...] = _ssd_finish(o, xs, z_ref[...], dx_ref[...], nw_ref[...]).astype(BF16)

    @pl.when(c == nc - 1)
    def _():
        for h in range(SSD_HEADS):
            g, hl = divmod(h, hpg)
            sfin_ref[0, h] = s_ref[g, :, hl * SSD_HEAD_DIM:(hl + 1) * SSD_HEAD_DIM]
        cfin_ref[0, :, 0:D_SSD] = tailx
        cfin_ref[0, :, D_SSD:D_SSD + SSD_BC] = tailb


def ssd_prompt(pe, prm, nseq, seq, chunk):
    m = nseq * seq
    nc = seq // chunk
    row = lambda b, c: b * nc + c
    full = lambda shp: pl.BlockSpec(shp, lambda b, c: (0,) * len(shp))
    in_specs = [
        pl.BlockSpec((chunk, D_SSD), lambda b, c: (row(b, c), 0)),
        pl.BlockSpec((chunk, D_SSD), lambda b, c: (row(b, c), 1)),
        pl.BlockSpec((chunk, SSD_BC), lambda b, c: (row(b, c), 10)),
        pl.BlockSpec((chunk, LANES), lambda b, c: (row(b, c), DT_COL // LANES)),
        pl.BlockSpec((SSD_CONV, D_SSD), lambda b, c: (0, 0)),
        pl.BlockSpec((SSD_CONV, SSD_BC), lambda b, c: (0, 2)),
        pl.BlockSpec((1, D_SSD), lambda b, c: (0, 0)),
        pl.BlockSpec((1, SSD_BC), lambda b, c: (0, 2)),
        full((1, LANES)), full((1, LANES)), full((1, D_SSD)), full((1, D_SSD)),
    ]
    vmem = _mib(48)
    return pl.pallas_call(
        _ssd_prompt_kernel,
        grid=(nseq, nc),
        in_specs=in_specs,
        out_specs=[pl.BlockSpec((chunk, D_SSD), lambda b, c: (row(b, c), 0)),
                   pl.BlockSpec((1, SSD_HEADS, SSD_STATE, SSD_HEAD_DIM), lambda b, c: (b, 0, 0, 0)),
                   pl.BlockSpec((1, SSD_CONV - 1, D_SSD + SSD_BC), lambda b, c: (b, 0, 0))],
        out_shape=[jax.ShapeDtypeStruct((m, D_SSD), BF16),
                   jax.ShapeDtypeStruct((nseq, SSD_HEADS, SSD_STATE, SSD_HEAD_DIM), F32),
                   jax.ShapeDtypeStruct((nseq, SSD_CONV - 1, D_SSD + SSD_BC), F32)],
        scratch_shapes=[pltpu.VMEM((SSD_GROUPS, SSD_STATE, D_SSD // SSD_GROUPS), F32),
                        pltpu.VMEM((SUBLANES + chunk, D_SSD), F32),
                        pltpu.VMEM((SUBLANES + chunk, SSD_BC), F32)],
        compiler_params=_cparams(("parallel", "arbitrary"), vmem),
        name="ssd_prompt",
    )(pe, pe, pe, pe, prm["ssd_cw"], prm["ssd_cw"], prm["ssd_cb"], prm["ssd_cb"],
      prm["ssd_dtb"], prm["ssd_alog"], prm["ssd_dx"], prm["ssd_nw"])


def _rotary(x, cos, sin_signed):
    return x * cos + pltpu.roll(x, RET_DK // 2, 1) * sin_signed


def _ret_prompt_kernel(q_ref, k_ref, v_ref, g_ref, cos_ref, sin_ref, o_ref, sfin_ref, s_ref):
    c = pl.program_id(1)
    nc = pl.num_programs(1)
    rows = q_ref.shape[0]

    @pl.when(c == 0)
    def _():
        s_ref[...] = jnp.zeros_like(s_ref)

    cos = cos_ref[...]
    sin = sin_ref[...]
    ri = lax.broadcasted_iota(jnp.int32, (rows, rows), 0)
    ci = lax.broadcasted_iota(jnp.int32, (rows, rows), 1)
    dij = jnp.where(ri >= ci, (ri - ci).astype(F32), 1e30)
    tk = lax.broadcasted_iota(jnp.int32, (rows, RET_DK), 0).astype(F32)
    tv = lax.broadcasted_iota(jnp.int32, (rows, RET_DV), 0).astype(F32)
    for h in range(RET_HEADS):
        lg = RET_LOG_GAMMA[h]
        qr = _rotary(q_ref[:, h * RET_DK:(h + 1) * RET_DK], cos, sin)
        kr = _rotary(k_ref[:, h * RET_DK:(h + 1) * RET_DK], cos, sin) * (RET_DK ** -0.5)
        qb = qr.astype(BF16)
        s = lax.dot_general(qb, kr.astype(BF16), NT, preferred_element_type=F32)
        pm = (s * jnp.exp(lg * dij)).astype(BF16)
        vb = v_ref[:, h * RET_DV:(h + 1) * RET_DV].astype(BF16)
        sh = s_ref[h]
        o = _dot(pm, vb) + _dot(qb, sh.astype(BF16)) * jnp.exp(lg * (tv + 1.0))
        kw = (kr * jnp.exp(lg * ((rows - 1.0) - tk))).astype(BF16)
        s_ref[h] = sh * math.exp(lg * rows) + lax.dot_general(kw, vb, TN, preferred_element_type=F32)
        gh = g_ref[:, h * RET_DV:(h + 1) * RET_DV]
        o_ref[:, h * RET_DV:(h + 1) * RET_DV] = (_rms(o) * _silu(gh)).astype(BF16)

    @pl.when(c == nc - 1)
    def _():
        sfin_ref[0] = s_ref[...]


def ret_prompt(pe, cos, sin, nseq, seq, chunk):
    m = nseq * seq
    nc = seq // chunk
    row = lambda b, c: b * nc + c
    return pl.pallas_call(
        _ret_prompt_kernel,
        grid=(nseq, nc),
        in_specs=[pl.BlockSpec((chunk, RET_QK), lambda b, c: (row(b, c), 8)),
                  pl.BlockSpec((chunk, RET_QK), lambda b, c: (row(b, c), 9)),
                  pl.BlockSpec((chunk, D_RET), lambda b, c: (row(b, c), 2)),
                  pl.BlockSpec((chunk, D_RET), lambda b, c: (row(b, c), 3)),
                  pl.BlockSpec((chunk, RET_DK), lambda b, c: (c, 0)),
                  pl.BlockSpec((chunk, RET_DK), lambda b, c: (c, 0))],
        out_specs=[pl.BlockSpec((chunk, D_RET), lambda b, c: (row(b, c), 0)),
                   pl.BlockSpec((1, RET_HEADS, RET_DK, RET_DV), lambda b, c: (b, 0, 0, 0))],
        out_shape=[jax.ShapeDtypeStruct((m, D_RET), BF16),
                   jax.ShapeDtypeStruct((nseq, RET_HEADS, RET_DK, RET_DV), F32)],
        scratch_shapes=[pltpu.VMEM((RET_HEADS, RET_DK, RET_DV), F32)],
        compiler_params=_cparams(("parallel", "arbitrary"), _mib(40)),
        name="ret_prompt",
    )(pe, pe, pe, pe, cos, sin)


HG_HB = 4


def _hg_gates(hq, hf, lb):
    q = _silu(hq)
    f = lb + (1.0 - lb) * jax.nn.sigmoid(hf)
    return q, f, 1.0 - f, jnp.log(f)


def _hg_tables(rows):
    r = np.arange(rows)[:, None]
    t = np.arange(rows)[None, :]
    sums = [t <= r]
    masks = []
    s = rows // 2
    while s >= 1:
        blk, pos = r // (2 * s), r % (2 * s)
        ref = blk * 2 * s + s - 1
        upper = pos >= s
        sums.append(np.where(upper, (t > ref) & (t <= r), (t > r) & (t <= ref)))
        masks.append((blk == t // (2 * s)) & upper & (t % (2 * s) < s))
        s //= 2
    masks.append(r == t)
    return (np.concatenate(sums, axis=0).astype(np.float32),
            np.stack(masks).astype(np.float32))


def _hg_prompt_kernel(hq_ref, hf_ref, hi_ref, hg_ref, lb_ref, nw_ref, sums_ref, masks_ref,
                      og_ref, sfin_ref, st_ref):
    c = pl.program_id(2)
    nc = pl.num_programs(2)
    rows = hq_ref.shape[0]
    nlev = masks_ref.shape[0] - 1

    @pl.when(c == 0)
    def _():
        st_ref[...] = jnp.zeros_like(st_ref)

    q_all, _, kk_all, lg_all = _hg_gates(hq_ref[...], hf_ref[...], lb_ref[...])
    lg_hi = lg_all.astype(BF16)
    lg_lo = (lg_all - lg_hi.astype(F32)).astype(BF16)
    sums = sums_ref[...]
    dall = _dot(sums, lg_hi) + _dot(sums, lg_lo)
    cum_all = dall[0:rows]
    ecum_all = jnp.exp(cum_all)
    elev = [jnp.exp(dall[(l + 1) * rows:(l + 2) * rows]) for l in range(nlev)]
    for h in range(HG_HB):
        sl = slice(h * HG_DK, (h + 1) * HG_DK)
        q = q_all[:, sl]
        kk = kk_all[:, sl]
        cum = cum_all[:, sl]
        vb = hi_ref[:, sl].astype(BF16)
        last = cum[rows - 1:rows, :]
        st = st_ref[h]
        o = lax.dot_general((q * ecum_all[:, sl]).astype(BF16), st.astype(BF16), NT,
                            preferred_element_type=F32)
        kt = (kk * jnp.exp(last - cum)).astype(BF16)
        st_ref[h] = st * jnp.exp(last) + lax.dot_general(vb, kt, TN, preferred_element_type=F32)
        a = masks_ref[nlev] * lax.dot_general(q.astype(BF16), kk.astype(BF16), NT,
                                              preferred_element_type=F32)
        for l in range(nlev):
            e = elev[l][:, sl]
            a = a + masks_ref[l] * lax.dot_general((q * e).astype(BF16), (kk * e).astype(BF16), NT,
                                                   preferred_element_type=F32)
        o = o + _dot(a.astype(BF16), vb)
        og_ref[:, sl] = (_rms(o) * nw_ref[...] * _silu(hg_ref[:, sl])).astype(BF16)

    @pl.when(c == nc - 1)
    def _():
        for h in range(HG_HB):
            sfin_ref[0, h] = st_ref[h].T


def hg_prompt(po, lb, nw, nseq, seq, chunk):
    m = nseq * seq
    nc = seq // chunk
    nhb = HG_HEADS // HG_HB
    w = HG_HB * HG_DK
    row = lambda b, hb, c: b * nc + c
    sums_np, masks_np = _hg_tables(chunk)
    sums = jnp.asarray(sums_np, BF16)
    masks = jnp.asarray(masks_np, F32)
    return pl.pallas_call(
        _hg_prompt_kernel,
        grid=(nseq, nhb, nc),
        in_specs=[pl.BlockSpec((chunk, w), lambda b, hb, c: (row(b, hb, c), hb)),
                  pl.BlockSpec((chunk, w), lambda b, hb, c: (row(b, hb, c), nhb + hb)),
                  pl.BlockSpec((chunk, w), lambda b, hb, c: (row(b, hb, c), 2 * nhb + hb)),
                  pl.BlockSpec((chunk, w), lambda b, hb, c: (row(b, hb, c), 3 * nhb + hb)),
                  pl.BlockSpec((1, w), lambda b, hb, c: (0, hb)),
                  pl.BlockSpec((1, HG_DV), lambda b, hb, c: (0, 0)),
                  pl.BlockSpec(sums.shape, lambda b, hb, c: (0, 0)),
                  pl.BlockSpec(masks.shape, lambda b, hb, c: (0, 0, 0))],
        out_specs=[pl.BlockSpec((chunk, w), lambda b, hb, c: (row(b, hb, c), hb)),
                   pl.BlockSpec((1, HG_HB, HG_DK, HG_DV), lambda b, hb, c: (b, hb, 0, 0))],
        out_shape=[jax.ShapeDtypeStruct((m, D_HG), BF16),
                   jax.ShapeDtypeStruct((nseq, HG_HEADS, HG_DK, HG_DV), F32)],
        scratch_shapes=[pltpu.VMEM((HG_HB, HG_DV, HG_DK), F32)],
        compiler_params=_cparams(("parallel", "parallel", "arbitrary"), _mib(32)),
        name="hg_prompt",
    )(po, po, po, po, lb, nw, sums, masks)


def _lin_scan(a, b, shift):
    rows = a.shape[0]
    ri = lax.broadcasted_iota(jnp.int32, a.shape, 0)
    d = shift
    while d < rows:
        keep = ri >= d
        a_s = jnp.where(keep, pltpu.roll(a, d, 0), 1.0)
        b_s = jnp.where(keep, pltpu.roll(b, d, 0), 0.0)
        b = a * b_s + b
        a = a * a_s
        d *= 2
    return a, b


def _lru_gates(xr, wr_ref, br, wi_ref, bi, ap):
    xb = xr.astype(BF16)
    r_parts, i_parts = [], []
    for n in range(LRU_BLOCKS):
        xn = xb[:, n * LRU_BW:(n + 1) * LRU_BW]
        r_parts.append(_dot(xn, wr_ref[n].astype(BF16)))
        i_parts.append(_dot(xn, wi_ref[n].astype(BF16)))
    r = jax.nn.sigmoid(jnp.concatenate(r_parts, axis=1) + br)
    gi = jax.nn.sigmoid(jnp.concatenate(i_parts, axis=1) + bi)
    la = -LRU_C * r * _softplus(-ap)
    a = jnp.exp(la)
    th = jnp.tanh(la)
    mult = jnp.sqrt(-2.0 * th / (1.0 - th))
    return a, mult, gi


def _lru_prompt_kernel(rx_ref, rg_ref, cw_ref, cb_ref, wr_ref, br_ref, wi_ref, bi_ref, ap_ref,
                       yl_ref, hfin_ref, cfin_ref, buf_ref, hc_ref):
    c = pl.program_id(1)
    nc = pl.num_programs(1)
    rows = rx_ref.shape[0]

    @pl.when(c == 0)
    def _():
        buf_ref[0:SUBLANES, :] = jnp.zeros((SUBLANES, D_RNN), F32)
        hc_ref[...] = jnp.zeros_like(hc_ref)

    xr, tail = _causal_conv(buf_ref, rx_ref[...], cw_ref, cb_ref, 1, LRU_CONV)
    a, mult, gi = _lru_gates(xr, wr_ref, br_ref[...], wi_ref, bi_ref[...], ap_ref[...])
    ri = lax.broadcasted_iota(jnp.int32, (rows, D_RNN), 0)
    mult = jnp.where(jnp.logical_and(c == 0, ri == 0), 1.0, mult)
    pa, hb = _lin_scan(a, mult * gi * xr, 1)
    hs = pa * hc_ref[0:1, :] + hb
    hc_ref[0:1, :] = hs[rows - 1:rows, :]
    yl_ref[...] = (hs * _gelu_tanh(rg_ref[...])).astype(BF16)

    @pl.when(c == nc - 1)
    def _():
        hfin_ref[0] = hs[rows - 1:rows, :]
        cfin_ref[0] = tail


def lru_prompt(po, prm, nseq, seq, chunk):
    m = nseq * seq
    nc = seq // chunk
    row = lambda b, c: b * nc + c
    full = lambda shp: pl.BlockSpec(shp, lambda b, c: (0,) * len(shp))
    return pl.pallas_call(
        _lru_prompt_kernel,
        grid=(nseq, nc),
        in_specs=[pl.BlockSpec((chunk, D_RNN), lambda b, c: (row(b, c), 4)),
                  pl.BlockSpec((chunk, D_RNN), lambda b, c: (row(b, c), 5)),
                  full((LRU_CONV, D_RNN)), full((1, D_RNN)),
                  full((LRU_BLOCKS, LRU_BW, LRU_BW)), full((1, D_RNN)),
                  full((LRU_BLOCKS, LRU_BW, LRU_BW)), full((1, D_RNN)), full((1, D_RNN))],
        out_specs=[pl.BlockSpec((chunk, D_RNN), lambda b, c: (row(b, c), 0)),
                   pl.BlockSpec((1, 1, D_RNN), lambda b, c: (b, 0, 0)),
                   pl.BlockSpec((1, LRU_CONV - 1, D_RNN), lambda b, c: (b, 0, 0))],
        out_shape=[jax.ShapeDtypeStruct((m, D_RNN), BF16),
                   jax.ShapeDtypeStruct((nseq, 1, D_RNN), F32),
                   jax.ShapeDtypeStruct((nseq, LRU_CONV - 1, D_RNN), F32)],
        scratch_shapes=[pltpu.VMEM((SUBLANES + chunk, D_RNN), F32),
                        pltpu.VMEM((SUBLANES, D_RNN), F32)],
        compiler_params=_cparams(("parallel", "arbitrary"), _mib(48)),
        name="lru_prompt",
    )(po, po, prm["lru_cw"], prm["lru_cb"], prm["lru_wr"], prm["lru_br"], prm["lru_wi"],
      prm["lru_bi"], prm["lru_ap"])


TOKP = SUBLANES


def _ssd_state_kernel(q_ref, k_ref, vw_ref, dec_ref, st_ref, u_ref, snew_ref):
    hpg = SSD_HEADS // SSD_GROUPS
    gw = hpg * SSD_HEAD_DIM

    def body(b, carry):
        q = q_ref[b].astype(BF16)
        k = k_ref[b].astype(BF16)
        vw = vw_ref[b].astype(BF16)
        for g in range(SSD_GROUPS):
            ks = slice(g * SSD_STATE, (g + 1) * SSD_STATE)
            stg = st_ref[b, g * hpg:(g + 1) * hpg].reshape(gw, SSD_STATE)
            u_ref[b, :, g * gw:(g + 1) * gw] = lax.dot_general(
                q[:, ks], stg.astype(BF16), NT, preferred_element_type=F32)
            upd = lax.dot_general(vw[:, g * gw:(g + 1) * gw], k[:, ks], TN, preferred_element_type=F32)
            for hl in range(hpg):
                h = g * hpg + hl
                snew_ref[b, h] = (st_ref[b, h] * dec_ref[b, h:h + 1, :]
                                  + upd[hl * SSD_HEAD_DIM:(hl + 1) * SSD_HEAD_DIM, :])
        return carry

    lax.fori_loop(0, st_ref.shape[0], body, 0)


def _ret_state_kernel(q_ref, k_ref, v_ref, s_ref, u_ref, snew_ref, *, ntok):
    def body(b, carry):
        q = q_ref[b].astype(BF16)
        k = k_ref[b].astype(BF16)
        v = v_ref[b].astype(BF16)
        for h in range(RET_HEADS):
            ks = slice(h * RET_DK, (h + 1) * RET_DK)
            vs = slice(h * RET_DV, (h + 1) * RET_DV)
            s0 = s_ref[b, h]
            u_ref[b, :, vs] = _dot(q[:, ks], s0.astype(BF16))
            upd = lax.dot_general(k[:, ks], v[:, vs], TN, preferred_element_type=F32)
            snew_ref[b, h] = s0 * math.exp(RET_LOG_GAMMA[h] * ntok) + upd
        return carry

    lax.fori_loop(0, s_ref.shape[0], body, 0)


def _hg_state_kernel(q_ref, k_ref, v_ref, dcol_ref, s_ref, u_ref, snew_ref):
    def body(b, carry):
        q = q_ref[b].astype(BF16)
        k = k_ref[b].astype(BF16)
        v = v_ref[b].astype(BF16)
        dc = dcol_ref[b]
        for h in range(HG_HEADS):
            ks = slice(h * HG_DK, (h + 1) * HG_DK)
            s0 = s_ref[b, h]
            u_ref[b, :, ks] = _dot(q[:, ks], s0.astype(BF16))
            upd = lax.dot_general(k[:, ks], v[:, ks], TN, preferred_element_type=F32)
            snew_ref[b, h] = s0 * dc[:, h:h + 1] + upd
        return carry

    lax.fori_loop(0, s_ref.shape[0], body, 0)


def _state_call(kern, rows_in, extra, s, u_cols, bb, name):
    nb = s.shape[0]
    blk = lambda a: pl.BlockSpec((bb,) + a.shape[1:], lambda i: (i,) + (0,) * (a.ndim - 1))
    ins = list(rows_in) + ([extra] if extra is not None else []) + [s]
    sbytes = bb * int(np.prod(s.shape[1:])) * 4
    rbytes = sum(bb * int(np.prod(a.shape[1:])) * 4 for a in ins[:-1]) + bb * TOKP * u_cols * 4
    return pl.pallas_call(
        kern,
        grid=(nb // bb,),
        in_specs=[blk(a) for a in ins],
        out_specs=[pl.BlockSpec((bb, TOKP, u_cols), lambda i: (i, 0, 0)), blk(s)],
        out_shape=[jax.ShapeDtypeStruct((nb, TOKP, u_cols), F32),
                   jax.ShapeDtypeStruct(s.shape, F32)],
        compiler_params=_cparams(("parallel",), 5 * sbytes + 3 * rbytes + _mib(8)),
        name=name,
    )(*ins)


def _tok(x, t, nb):
    return x[t * nb:(t + 1) * nb]


def _head_sums(x, width):
    r, n = x.shape
    tiles = []
    for h in range(n // width):
        s = jnp.sum(x[:, h * width:(h + 1) * width], axis=1, keepdims=True)
        tiles.append(jnp.broadcast_to(s, (r, width)))
    return jnp.concatenate(tiles, axis=1)


def _ssd_sample_pre_kernel(xs_ref, bc_ref, dt_ref, prevx_ref, prevb_ref, cwx_ref, cwb_ref,
                           cbx_ref, cbb_ref, dtb_ref, alog_ref,
                           xs_out, bc_out, vw_out, oi_out, ecum_out, elast_out, tailx_out, tailb_out,
                           bufx_ref, bufb_ref, *, nb, ntok):
    n = (SSD_CONV - 1) * nb
    offx = bufx_ref.shape[0] - nb * ntok
    bufx_ref[offx - n:offx, :] = prevx_ref[...]
    bufb_ref[offx - n:offx, :] = prevb_ref[...]
    xs_c, tailx = _causal_conv(bufx_ref, xs_ref[...], cwx_ref, cbx_ref, nb, SSD_CONV)
    bc_c, tailb = _causal_conv(bufb_ref, bc_ref[...], cwb_ref, cbb_ref, nb, SSD_CONV)
    xs, bc, dt, la = _ssd_gates(xs_c, bc_c, dt_ref[...], dtb_ref[...], alog_ref[...])
    tailx_out[...] = tailx
    tailb_out[...] = tailb
    xs_out[...] = xs
    bc_out[...] = bc
    cums = []
    for t in range(ntok):
        lt = _tok(la, t, nb)
        cums.append(lt if t == 0 else cums[-1] + lt)
    cumx = [_expand_heads64(cm, SSD_HEADS) for cm in cums]
    v = xs * _expand_heads64(dt, SSD_HEADS)
    kw = SSD_GROUPS * SSD_STATE
    for t in range(ntok):
        ct = _tok(bc, t, nb)[:, kw:2 * kw]
        acc = None
        for t2 in range(t + 1):
            bt = _tok(bc, t2, nb)[:, 0:kw]
            sc = _head_sums(ct * bt, SSD_STATE)
            scx = jnp.concatenate(
                [jnp.concatenate([sc[:, g * SSD_STATE:(g + 1) * SSD_STATE]] * 4, axis=1)
                 for g in range(SSD_GROUPS)], axis=1)
            term = scx * jnp.exp(cumx[t] - cumx[t2]) * _tok(v, t2, nb)
            acc = term if acc is None else acc + term
        oi_out[t * nb:(t + 1) * nb, :] = acc
        ecum_out[t * nb:(t + 1) * nb, :] = jnp.exp(cumx[t])
        vw_out[t * nb:(t + 1) * nb, :] = _tok(v, t, nb) * jnp.exp(cumx[ntok - 1] - cumx[t])
    elast_out[...] = jnp.exp(cums[ntok - 1])


def _ssd_sample_post_kernel(oi_ref, u_ref, ecum_ref, xs_ref, z_ref, dx_ref, nw_ref, y_ref):
    o = oi_ref[...] + ecum_ref[...] * u_ref[...]
    y_ref[...] = _ssd_finish(o, xs_ref[...], z_ref[...], dx_ref[...], nw_ref[...]).astype(BF16)


def _ret_sample_pre_kernel(q_ref, k_ref, v_ref, cos_ref, sin_ref, qd_out, kd_out, oi_out, *, nb, ntok):
    rows = nb * ntok
    cos = jnp.concatenate([jnp.broadcast_to(cos_ref[t:t + 1, :], (nb, RET_DK)) for t in range(ntok)], axis=0)
    sin = jnp.concatenate([jnp.broadcast_to(sin_ref[t:t + 1, :], (nb, RET_DK)) for t in range(ntok)], axis=0)
    qr, kr = [], []
    for h in range(RET_HEADS):
        sl = slice(h * RET_DK, (h + 1) * RET_DK)
        qr.append(_rotary(q_ref[:, sl], cos, sin))
        kr.append(_rotary(k_ref[:, sl], cos, sin) * (RET_DK ** -0.5))
    qr = jnp.concatenate(qr, axis=1)
    kr = jnp.concatenate(kr, axis=1)
    v = v_ref[...]
    for t in range(ntok):
        qt = _tok(qr, t, nb)
        acc = None
        for t2 in range(t + 1):
            sc = _head_sums(qt * _tok(kr, t2, nb), RET_DK)
            vt = _tok(v, t2, nb)
            tiles = []
            for h in range(RET_HEADS):
                dec = math.exp(RET_LOG_GAMMA[h] * (t - t2))
                s = sc[:, h * RET_DK:(h + 1) * RET_DK] * dec
                tiles.append(jnp.concatenate([s, s], axis=1) * vt[:, h * RET_DV:(h + 1) * RET_DV])
            term = jnp.concatenate(tiles, axis=1)
            acc = term if acc is None else acc + term
        oi_out[t * nb:(t + 1) * nb, :] = acc
        qd = jnp.concatenate([qt[:, h * RET_DK:(h + 1) * RET_DK] * math.exp(RET_LOG_GAMMA[h] * (t + 1))
                              for h in range(RET_HEADS)], axis=1)
        kt = _tok(kr, t, nb)
        kd = jnp.concatenate([kt[:, h * RET_DK:(h + 1) * RET_DK] * math.exp(RET_LOG_GAMMA[h] * (ntok - 1 - t))
                              for h in range(RET_HEADS)], axis=1)
        qd_out[t * nb:(t + 1) * nb, :] = qd
        kd_out[t * nb:(t + 1) * nb, :] = kd


def _ret_sample_post_kernel(oi_ref, u_ref, g_ref, o_ref):
    o = oi_ref[...] + u_ref[...]
    g = g_ref[...]
    for h in range(RET_HEADS):
        sl = slice(h * RET_DV, (h + 1) * RET_DV)
        o_ref[:, sl] = (_rms(o[:, sl]) * _silu(g[:, sl])).astype(BF16)


def _hg_sample_pre_kernel(hq_ref, hf_ref, hi_ref, lb_ref, qd_out, kd_out, dl_out, oi_out, *, nb, ntok):
    q, _, kk, lg = _hg_gates(hq_ref[...], hf_ref[...], lb_ref[...])
    v = hi_ref[...]
    cums = []
    for t in range(ntok):
        lt = _tok(lg, t, nb)
        cums.append(lt if t == 0 else cums[-1] + lt)
    for t in range(ntok):
        qt = _tok(q, t, nb)
        acc = None
        for t2 in range(t + 1):
            w = qt * _tok(kk, t2, nb)
            if t2 < t:
                w = w * jnp.exp(cums[t] - cums[t2])
            term = _head_sums(w, HG_DK) * _tok(v, t2, nb)
            acc = term if acc is None else acc + term
        oi_out[t * nb:(t + 1) * nb, :] = acc
        qd_out[t * nb:(t + 1) * nb, :] = qt * jnp.exp(cums[t])
        kd_out[t * nb:(t + 1) * nb, :] = _tok(kk, t, nb) * jnp.exp(cums[ntok - 1] - cums[t])
    dl_out[...] = jnp.exp(cums[ntok - 1])


def _hg_sample_post_kernel(oi_ref, u_ref, hg_ref, nw_ref, og_ref):
    o = oi_ref[...] + u_ref[...]
    hg = hg_ref[...]
    for h in range(HG_HEADS):
        sl = slice(h * HG_DV, (h + 1) * HG_DV)
        og_ref[:, sl] = (_rms(o[:, sl]) * nw_ref[...] * _silu(hg[:, sl])).astype(BF16)


def _lru_sample_kernel(rx_ref, rg_ref, prev_ref, h0_ref, cw_ref, cb_ref, wr_ref, br_ref, wi_ref,
                       bi_ref, ap_ref, yl_out, hfin_out, tail_out, buf_ref, *, nb, ntok):
    rows = nb * ntok
    n = (LRU_CONV - 1) * nb
    off = buf_ref.shape[0] - rows
    buf_ref[off - n:off, :] = prev_ref[...]
    xr, tail = _causal_conv(buf_ref, rx_ref[...], cw_ref, cb_ref, nb, LRU_CONV)
    tail_out[...] = tail
    a, mult, gi = _lru_gates(xr, wr_ref, br_ref[...], wi_ref, bi_ref[...], ap_ref[...])
    b = mult * gi * xr
    h = h0_ref[...]
    for t in range(ntok):
        h = _tok(a, t, nb) * h + _tok(b, t, nb)
        yl_out[t * nb:(t + 1) * nb, :] = (h * _gelu_tanh(rg_ref[t * nb:(t + 1) * nb, :])).astype(BF16)
    hfin_out[...] = h


def _whole(shape):
    return pl.BlockSpec(shape, lambda i: (0,) * len(shape))


def _colblock(rows, width, idx):
    return pl.BlockSpec((rows, width), functools.partial(lambda i, k: (0, k), k=idx))


def _call_whole(kern, in_arrays, in_specs, out_shapes, scratch, name, vmem):
    return pl.pallas_call(
        kern, grid=(1,), in_specs=in_specs,
        out_specs=[_whole(s.shape) for s in out_shapes],
        out_shape=out_shapes, scratch_shapes=scratch,
        compiler_params=_cparams(("arbitrary",), vmem), name=name,
    )(*in_arrays)


def _to_token_major(s):
    nb, w, c = s.shape
    return jnp.transpose(s, (1, 0, 2)).reshape(w * nb, c)


def _from_token_major(x, nb):
    w = x.shape[0] // nb
    return jnp.transpose(x.reshape(w, nb, x.shape[1]), (1, 0, 2))


def _rows_to_batch(x, nb):
    ntok = x.shape[0] // nb
    x = jnp.transpose(x.reshape(ntok, nb, x.shape[1]), (1, 0, 2))
    return jnp.pad(x, ((0, 0), (0, TOKP - ntok), (0, 0)))


def _batch_to_rows(u, ntok):
    nb = u.shape[0]
    return jnp.transpose(u[:, :ntok], (1, 0, 2)).reshape(ntok * nb, u.shape[2])


def ssd_sample(pe, z, s_ssm, s_conv, prm, nb, ntok):
    rows = nb * ntok
    n = (SSD_CONV - 1) * nb
    prev = _to_token_major(s_conv)
    off = _conv_off(nb, SSD_CONV)
    f = lambda *shape: jax.ShapeDtypeStruct(shape, F32)
    outs = _call_whole(
        functools.partial(_ssd_sample_pre_kernel, nb=nb, ntok=ntok),
        [pe, pe, pe, prev, prev, prm["ssd_cw"], prm["ssd_cw"], prm["ssd_cb"], prm["ssd_cb"],
         prm["ssd_dtb"], prm["ssd_alog"]],
        [_colblock(rows, D_SSD, 1), _colblock(rows, SSD_BC, 10), _colblock(rows, LANES, DT_COL // LANES),
         _colblock(n, D_SSD, 0), _colblock(n, SSD_BC, 2),
         _colblock(SSD_CONV, D_SSD, 0), _colblock(SSD_CONV, SSD_BC, 2),
         _colblock(1, D_SSD, 0), _colblock(1, SSD_BC, 2), _whole((1, LANES)), _whole((1, LANES))],
        [f(rows, D_SSD), f(rows, SSD_BC), f(rows, D_SSD), f(rows, D_SSD), f(rows, D_SSD),
         f(nb, LANES), f(n, D_SSD), f(n, SSD_BC)],
        [pltpu.VMEM((off + rows, D_SSD), F32), pltpu.VMEM((off + rows, SSD_BC), F32)],
        "ssd_sample_pre", _mib(56))
    xs, bc, vw, oi, ecum, elast, tailx, tailb = outs
    kw = SSD_GROUPS * SSD_STATE
    dec = jnp.broadcast_to(elast[:, :SSD_HEADS, None], (nb, SSD_HEADS, SSD_STATE))
    st = jnp.swapaxes(s_ssm, -1, -2)
    u, st_new = _state_call(_ssd_state_kernel,
                            [_rows_to_batch(bc[:, kw:], nb), _rows_to_batch(bc[:, :kw], nb),
                             _rows_to_batch(vw, nb)], dec, st, D_SSD, 4, "ssd_state")
    s_new = jnp.swapaxes(st_new, -1, -2)
    u = _batch_to_rows(u, ntok)
    (y,) = _call_whole(
        _ssd_sample_post_kernel, [oi, u, ecum, xs, z, prm["ssd_dx"], prm["ssd_nw"]],
        [_whole((rows, D_SSD))] * 4 + [_colblock(rows, D_SSD, 0), _whole((1, D_SSD)), _whole((1, D_SSD))],
        [jax.ShapeDtypeStruct((rows, D_SSD), BF16)], [], "ssd_sample_post", _mib(48))
    conv_new = _from_token_major(jnp.concatenate([tailx, tailb], axis=1), nb)
    return y, s_new, conv_new


def ret_sample(pe, s_ret, cos, sin, nb, ntok):
    rows = nb * ntok
    f = lambda *shape: jax.ShapeDtypeStruct(shape, F32)
    qd, kd, oi = _call_whole(
        functools.partial(_ret_sample_pre_kernel, nb=nb, ntok=ntok),
        [pe, pe, pe, cos, sin],
        [_colblock(rows, RET_QK, 8), _colblock(rows, RET_QK, 9), _colblock(rows, D_RET, 2),
         _whole(cos.shape), _whole(sin.shape)],
        [f(rows, RET_QK), f(rows, RET_QK), f(rows, D_RET)], [], "ret_sample_pre", _mib(48))
    v = _rows_to_batch(pe[:, 2 * D_RET:3 * D_RET], nb)
    u, s_new = _state_call(functools.partial(_ret_state_kernel, ntok=ntok),
                           [_rows_to_batch(qd, nb), _rows_to_batch(kd, nb), v], None, s_ret,
                           D_RET, 4, "ret_state")
    u = _batch_to_rows(u, ntok)
    (o,) = _call_whole(
        _ret_sample_post_kernel, [oi, u, pe],
        [_whole((rows, D_RET)), _whole((rows, D_RET)), _colblock(rows, D_RET, 3)],
        [jax.ShapeDtypeStruct((rows, D_RET), BF16)], [], "ret_sample_post", _mib(40))
    return o, s_new


def hg_sample(po, s_hg, lb, nw, nb, ntok):
    rows = nb * ntok
    f = lambda *shape: jax.ShapeDtypeStruct(shape, F32)
    qd, kd, dl, oi = _call_whole(
        functools.partial(_hg_sample_pre_kernel, nb=nb, ntok=ntok),
        [po, po, po, lb],
        [_colblock(rows, D_HG, 0), _colblock(rows, D_HG, 1), _colblock(rows, D_HG, 2), _whole((1, D_HG))],
        [f(rows, D_HG), f(rows, D_HG), f(nb, D_HG), f(rows, D_HG)], [], "hg_sample_pre", _mib(48))
    dcol = jnp.pad(jnp.transpose(dl.reshape(nb, HG_HEADS, HG_DK), (0, 2, 1)),
                   ((0, 0), (0, 0), (0, LANES - HG_HEADS)))
    v = _rows_to_batch(po[:, 2 * D_HG:3 * D_HG], nb)
    u, s_new = _state_call(_hg_state_kernel, [_rows_to_batch(qd, nb), _rows_to_batch(kd, nb), v],
                           dcol, s_hg, D_HG, 4, "hg_state")
    u = _batch_to_rows(u, ntok)
    (og,) = _call_whole(
        _hg_sample_post_kernel, [oi, u, po, nw],
        [_whole((rows, D_HG)), _whole((rows, D_HG)), _colblock(rows, D_HG, 3), _whole((1, HG_DV))],
        [jax.ShapeDtypeStruct((rows, D_HG), BF16)], [], "hg_sample_post", _mib(40))
    return og, s_new


def lru_sample(po, s_lru, s_lconv, prm, nb, ntok):
    rows = nb * ntok
    n = (LRU_CONV - 1) * nb
    off = _conv_off(nb, LRU_CONV)
    prev = _to_token_major(s_lconv)
    yl, hfin, tail = _call_whole(
        functools.partial(_lru_sample_kernel, nb=nb, ntok=ntok),
        [po, po, prev, s_lru, prm["lru_cw"], prm["lru_cb"], prm["lru_wr"], prm["lru_br"], prm["lru_wi"],
         prm["lru_bi"], prm["lru_ap"]],
        [_colblock(rows, D_RNN, 4), _colblock(rows, D_RNN, 5), _whole((n, D_RNN)), _whole((nb, D_RNN)),
         _whole((LRU_CONV, D_RNN)), _whole((1, D_RNN)), _whole((LRU_BLOCKS, LRU_BW, LRU_BW)),
         _whole((1, D_RNN)), _whole((LRU_BLOCKS, LRU_BW, LRU_BW)), _whole((1, D_RNN)), _whole((1, D_RNN))],
        [jax.ShapeDtypeStruct((rows, D_RNN), BF16), jax.ShapeDtypeStruct((nb, D_RNN), F32),
         jax.ShapeDtypeStruct((n, D_RNN), F32)],
        [pltpu.VMEM((off + rows, D_RNN), F32)], "lru_sample", _mib(48))
    return yl, hfin, _from_token_major(tail, nb)


def _rope_tables(pos):
    half = RET_DK // 2
    inv = ROPE_BASE ** (-jnp.arange(half, dtype=F32) / half)
    ang = pos.astype(F32)[:, None] * inv[None, :]
    cos, sin = jnp.cos(ang), jnp.sin(ang)
    return jnp.concatenate([cos, cos], axis=1), jnp.concatenate([-sin, sin], axis=1)


def _pad_lanes(v):
    return jnp.pad(v.astype(F32), (0, LANES - v.shape[0])).reshape(1, LANES)


def _prepare(p):
    lbs = jnp.cumsum(jax.nn.softmax(p["hg_lower_bounds"].astype(F32), axis=0), axis=0)
    lbs = lbs - lbs[0]
    return {
        "even_wt": jnp.swapaxes(p["even_w_in"][0], 0, 1),
        "ssd_cw": p["ssd_conv_w"][0], "ssd_cb": p["ssd_conv_b"][0].reshape(1, -1),
        "ssd_dtb": _pad_lanes(p["ssd_dt_bias"][0]), "ssd_alog": _pad_lanes(p["ssd_A_log"][0]),
        "ssd_dx": jnp.repeat(p["ssd_D"][0], SSD_HEAD_DIM).reshape(1, D_SSD),
        "ssd_nw": p["ssd_norm_w"][0].reshape(1, D_SSD),
        "hg_lb": lbs[1].reshape(1, D_HG), "hg_nw": p["hg_norm_w"][0].reshape(1, HG_DV),
        "lru_cw": p["lru_conv_w"][0], "lru_cb": p["lru_conv_b"][0].reshape(1, D_RNN),
        "lru_wr": p["lru_w_r"][0], "lru_br": p["lru_b_r"][0].reshape(1, D_RNN),
        "lru_wi": p["lru_w_i"][0], "lru_bi": p["lru_b_i"][0].reshape(1, D_RNN),
        "lru_ap": p["lru_a_param"][0].reshape(1, D_RNN),
    }


def _even_row_offsets(bn):
    o_xbc, o_dt = D_SSD, 2 * D_SSD + SSD_BC
    o_q = o_dt + SSD_HEADS
    o_k, o_v = o_q + RET_QK, o_q + 2 * RET_QK
    o_g = o_v + D_RET
    segs = [(0, D_SSD), (o_xbc, D_SSD), (o_v, D_RET), (o_g, D_RET), (o_q, RET_QK), (o_k, RET_QK),
            (o_xbc + D_SSD, SSD_BC), (o_dt, bn)]
    offs = [start + i for start, width in segs for i in range(0, width, bn)]
    assert len(offs) * bn == EVEN_PACKED and offs[-1] + bn <= D_IN_EVEN
    return offs


def _even_proj(x, p, prm, bm):
    return mm_in_t(x, p["norm_mix"][0], prm["even_wt"], _even_row_offsets(512), bm, 512)


def _ffn(x, p, prm, l, prev, shift, group_rows, bm):
    act, st = ffn_up(x, p["norm_ffn"][l], p["ffn_w_up"], l, p["ffn_conv_w"][l], p["ffn_conv_b"][l],
                     prev, shift, group_rows, bm, 512)
    return mm_out([act], p["ffn_w_down"], l, x, bm, 256), st


def _trunk_prompt(x, p, prm, nseq, seq):
    m = nseq * seq
    bm = min(1024, seq)
    cos, sin = _rope_tables(jnp.arange(seq, dtype=jnp.int32))
    pe = _even_proj(x, p, prm, bm)
    y, ssm, ssm_conv = ssd_prompt(pe, prm, nseq, seq, 128)
    o, ret = ret_prompt(pe, cos, sin, nseq, seq, 128)
    x = mm_out([y, o], p["even_w_out"], 0, x, bm, 512)
    zeros_ffn = jnp.zeros((nseq, FFN_CONV - 1, D_FF), F32)
    x, ffn0 = _ffn(x, p, prm, 0, zeros_ffn, 1, seq, bm)
    po = mm_in(x, p["norm_mix"][1], p["odd_w_in"][0], bm, 512)
    og, hgrn = hg_prompt(po, prm["hg_lb"], prm["hg_nw"], nseq, seq, 128)
    yl, lru, lru_conv = lru_prompt(po, prm, nseq, seq, 256)
    x = mm_out([og, yl], p["odd_w_out"], 0, x, bm, 512)
    x, ffn1 = _ffn(x, p, prm, 1, zeros_ffn, 1, seq, bm)
    y_out = rmsnorm(x, p["norm_final"], min(512, m))
    return (y_out.reshape(nseq, seq, D_MODEL), ssm[None], ssm_conv[None], ret[None], hgrn[None],
            lru.reshape(1, nseq, D_RNN), lru_conv[None], jnp.stack([ffn0, ffn1]))


def _trunk_sample(x, st, p, prm, nb, ntok):
    rows = nb * ntok
    cos, sin = _rope_tables(PAST_LEN + jnp.arange(ntok, dtype=jnp.int32))
    pe = _even_proj(x, p, prm, rows)
    y, ssm, ssm_conv = ssd_sample(pe, pe, st["ssm"][0], st["ssm_conv"][0], prm, nb, ntok)
    o, ret = ret_sample(pe, st["ret"][0], cos, sin, nb, ntok)
    x = mm_out([y, o], p["even_w_out"], 0, x, rows, 512)
    x, ffn0 = _ffn(x, p, prm, 0, _to_token_major(st["ffn_conv"][0])[None], nb, rows, rows)
    po = mm_in(x, p["norm_mix"][1], p["odd_w_in"][0], rows, 512)
    og, hgrn = hg_sample(po, st["hgrn"][0], prm["hg_lb"], prm["hg_nw"], nb, ntok)
    yl, lru, lru_conv = lru_sample(po, st["lru"][0], st["lru_conv"][0], prm, nb, ntok)
    x = mm_out([og, yl], p["odd_w_out"], 0, x, rows, 512)
    x, ffn1 = _ffn(x, p, prm, 1, _to_token_major(st["ffn_conv"][1])[None], nb, rows, rows)
    y_out = rmsnorm(x, p["norm_final"], rows)
    y_out = jnp.transpose(y_out.reshape(ntok, nb, D_MODEL), (1, 0, 2))
    ffn = jnp.stack([_from_token_major(ffn0[0], nb), _from_token_major(ffn1[0], nb)])
    return (y_out, ssm[None], ssm_conv[None], ret[None], hgrn[None], lru[None], lru_conv[None], ffn)


def kernel(x_prompt, x_sample, state_ssm, state_ssm_conv, state_ret, state_hgrn, state_lru, state_lru_conv, state_ffn_conv, norm_mix, norm_ffn, norm_final, even_w_in, ssd_conv_w, ssd_conv_b, ssd_dt_bias, ssd_A_log, ssd_D, ssd_norm_w, even_w_out, odd_w_in, hg_lower_bounds, hg_norm_w, lru_conv_w, lru_conv_b, lru_w_r, lru_b_r, lru_w_i, lru_b_i, lru_a_param, odd_w_out, ffn_w_up, ffn_conv_w, ffn_conv_b, ffn_w_down):
    p = {
        "norm_mix": norm_mix, "norm_ffn": norm_ffn, "norm_final": norm_final,
        "even_w_in": even_w_in, "ssd_conv_w": ssd_conv_w, "ssd_conv_b": ssd_conv_b,
        "ssd_dt_bias": ssd_dt_bias, "ssd_A_log": ssd_A_log, "ssd_D": ssd_D,
        "ssd_norm_w": ssd_norm_w, "even_w_out": even_w_out, "odd_w_in": odd_w_in,
        "hg_lower_bounds": hg_lower_bounds, "hg_norm_w": hg_norm_w,
        "lru_conv_w": lru_conv_w, "lru_conv_b": lru_conv_b, "lru_w_r": lru_w_r,
        "lru_b_r": lru_b_r, "lru_w_i": lru_w_i, "lru_b_i": lru_b_i,
        "lru_a_param": lru_a_param, "odd_w_out": odd_w_out, "ffn_w_up": ffn_w_up,
        "ffn_conv_w": ffn_conv_w, "ffn_conv_b": ffn_conv_b, "ffn_w_down": ffn_w_down,
    }
    prm = _prepare(p)
    nseq, seq, _ = x_prompt.shape
    nb, ntok, _ = x_sample.shape
    st = {"ssm": state_ssm, "ssm_conv": state_ssm_conv, "ret": state_ret, "hgrn": state_hgrn,
          "lru": state_lru, "lru_conv": state_lru_conv, "ffn_conv": state_ffn_conv}
    yp = _trunk_prompt(x_prompt.reshape(nseq * seq, D_MODEL), p, prm, nseq, seq)
    xs_tm = jnp.transpose(x_sample, (1, 0, 2)).reshape(ntok * nb, D_MODEL)
    ys = _trunk_sample(xs_tm, st, p, prm, nb, ntok)
    return (yp[0], ys[0], yp[1], ys[1], yp[2], ys[2], yp[3], ys[3], yp[4], ys[4],
            yp[5], ys[5], yp[6], ys[6], yp[7], ys[7])
```

```python
import functools
import math

import numpy as np
import jax
import jax.numpy as jnp
from jax import lax
from jax.experimental import pallas as pl
from jax.experimental.pallas import tpu as pltpu

F32 = jnp.float32
BF16 = jnp.bfloat16
EPS = 1e-6

D_MODEL = 2048
PAST_LEN = 16384
SSD_HEADS = 32
SSD_HEAD_DIM = 64
D_SSD = SSD_HEADS * SSD_HEAD_DIM
SSD_GROUPS = 4
SSD_STATE = 128
SSD_CONV = 4
SSD_BC = 2 * SSD_GROUPS * SSD_STATE
RET_HEADS = 8
RET_DK = 128
RET_DV = 256
RET_QK = RET_HEADS * RET_DK
D_RET = RET_HEADS * RET_DV
ROPE_BASE = 10000.0
HG_HEADS = 16
HG_DK = 128
HG_DV = 128
D_HG = HG_HEADS * HG_DV
D_RNN = 2048
LRU_BLOCKS = 8
LRU_BW = D_RNN // LRU_BLOCKS
LRU_CONV = 4
LRU_C = 8.0
D_FF = 5632
FFN_CONV = 3
D_IN_EVEN = D_SSD + (D_SSD + SSD_BC) + SSD_HEADS + 2 * RET_QK + 2 * D_RET
EVEN_PACKED = 11776
DT_COL = 11264

V7X_VMEM_BYTES = 64 * 1024 * 1024
V7X_VMEM_CAP = 60 * 1024 * 1024
LANES = 128
SUBLANES = 8

RET_LOG_GAMMA = [float(v) for v in np.log1p(-np.exp(np.linspace(
    math.log(1.0 / 32.0), math.log(1.0 / 512.0), RET_HEADS, dtype=np.float32))).astype(np.float32)]

NT = (((1,), (1,)), ((), ()))
TN = (((0,), (0,)), ((), ()))


def _cparams(sem, vmem_bytes):
    return pltpu.CompilerParams(dimension_semantics=sem,
                                vmem_limit_bytes=int(min(V7X_VMEM_CAP, vmem_bytes)))


def _mib(n):
    return n * 1024 * 1024


def _silu(x):
    return x * jax.nn.sigmoid(x)


def _softplus(x):
    return jnp.maximum(x, 0.0) + jnp.log1p(jnp.exp(-jnp.abs(x)))


def _gelu_tanh(x):
    return 0.5 * x * (1.0 + jnp.tanh(math.sqrt(2.0 / math.pi) * (x + 0.044715 * (x * x * x))))


def _rms(x):
    return x * lax.rsqrt(jnp.mean(x * x, axis=-1, keepdims=True) + EPS)


def _dot(a, b):
    return jnp.dot(a, b, preferred_element_type=F32)


def _tril_ones(n):
    r = lax.broadcasted_iota(jnp.int32, (n, n), 0)
    c = lax.broadcasted_iota(jnp.int32, (n, n), 1)
    return (r >= c).astype(F32)


def _cumsum_rows(x):
    return jnp.dot(_tril_ones(x.shape[0]), x, precision=lax.Precision.HIGHEST,
                   preferred_element_type=F32)


def _causal_conv(buf_ref, x, w_ref, b_ref, shift, width):
    rows = x.shape[0]
    off = buf_ref.shape[0] - rows
    n = (width - 1) * shift
    buf_ref[off:off + rows, :] = x
    acc = None
    for j in range(width):
        start = off - (width - 1 - j) * shift
        term = buf_ref[start:start + rows, :] * w_ref[j:j + 1, :]
        acc = term if acc is None else acc + term
    out = b_ref[...] + acc
    tail = buf_ref[off + rows - n:off + rows, :]
    buf_ref[off - n:off, :] = tail
    return out, tail


def _conv_off(shift, width):
    n = (width - 1) * shift
    return -(-n // SUBLANES) * SUBLANES


NORM_ROWS = 256


def _norm_rows_to(x_ref, nw_ref, xn_ref):
    rows = x_ref.shape[0]
    step = min(NORM_ROWS, rows)

    def body(i, carry):
        r = pl.multiple_of(i * step, step)
        xn_ref[pl.ds(r, step), :] = (_rms(x_ref[pl.ds(r, step), :]) * nw_ref[...]).astype(BF16)
        return carry

    lax.fori_loop(0, rows // step, body, 0)


def _mm_in_kernel(x_ref, nw_ref, w_ref, o_ref, xn_ref):
    @pl.when(pl.program_id(1) == 0)
    def _():
        _norm_rows_to(x_ref, nw_ref, xn_ref)
    o_ref[...] = _dot(xn_ref[...], w_ref[...].astype(BF16))


def _mm_in_vmem(bm, d, bn):
    return (bm * d * 4 + bm * d * 2 + 2 * d * bn * 4 + d * bn * 2 + 2 * bm * bn * 4
            + 4 * NORM_ROWS * d * 4 + _mib(6))


def mm_in(x, nw, w, bm, bn):
    m, d = x.shape
    n = w.shape[1]
    vmem = _mm_in_vmem(bm, d, bn)
    return pl.pallas_call(
        _mm_in_kernel,
        grid=(m // bm, n // bn),
        in_specs=[pl.BlockSpec((bm, d), lambda i, j: (i, 0), pipeline_mode=pl.Buffered(1)),
                  pl.BlockSpec((1, d), lambda i, j: (0, 0)),
                  pl.BlockSpec((d, bn), lambda i, j: (0, j))],
        out_specs=pl.BlockSpec((bm, bn), lambda i, j: (i, j)),
        out_shape=jax.ShapeDtypeStruct((m, n), F32),
        scratch_shapes=[pltpu.VMEM((bm, d), BF16)],
        compiler_params=_cparams(("parallel", "arbitrary"), vmem),
        name="mm_in",
    )(x, nw.reshape(1, d), w)


def _mm_in_t_kernel(offs_ref, x_ref, nw_ref, wt_ref, o_ref, xn_ref):
    del offs_ref
    @pl.when(pl.program_id(1) == 0)
    def _():
        _norm_rows_to(x_ref, nw_ref, xn_ref)
    o_ref[...] = lax.dot_general(xn_ref[...], wt_ref[...].astype(BF16), NT, preferred_element_type=F32)


ROW_ALIGN = 32


def mm_in_t(x, nw, wt, row_offsets, bm, bn):
    m, d = x.shape
    nblk = len(row_offsets)
    assert all(o % ROW_ALIGN == 0 for o in row_offsets)
    vmem = _mm_in_vmem(bm, d, bn)
    grid_spec = pltpu.PrefetchScalarGridSpec(
        num_scalar_prefetch=1,
        grid=(m // bm, nblk),
        in_specs=[pl.BlockSpec((bm, d), lambda i, j, offs: (i, 0), pipeline_mode=pl.Buffered(1)),
                  pl.BlockSpec((1, d), lambda i, j, offs: (0, 0)),
                  pl.BlockSpec((pl.Element(bn), pl.Element(d)),
                               lambda i, j, offs: (offs[j] * ROW_ALIGN, 0))],
        out_specs=pl.BlockSpec((bm, bn), lambda i, j, offs: (i, j)),
        scratch_shapes=[pltpu.VMEM((bm, d), BF16)])
    return pl.pallas_call(
        _mm_in_t_kernel,
        grid_spec=grid_spec,
        out_shape=jax.ShapeDtypeStruct((m, nblk * bn), F32),
        compiler_params=_cparams(("parallel", "arbitrary"), vmem),
        name="mm_in_t",
    )(jnp.asarray([o // ROW_ALIGN for o in row_offsets], jnp.int32), x, nw.reshape(1, d), wt)


def _mm_out_kernel(*refs, nparts):
    a_refs = refs[:nparts]
    w_refs = refs[nparts:2 * nparts]
    r_ref = refs[2 * nparts]
    o_ref = refs[2 * nparts + 1]
    acc = r_ref[...]
    for a_ref, w_ref in zip(a_refs, w_refs):
        acc = acc + _dot(a_ref[...], w_ref[...].astype(BF16))
    o_ref[...] = acc


def mm_out(parts, w, layer, resid, bm, bn):
    nparts = len(parts)
    m, kp = parts[0].shape
    n = w.shape[2]
    vmem = nparts * (bm * kp * 2 + 2 * kp * bn * 4 + kp * bn * 2) + 4 * bm * bn * 4 + 2 * bm * bn * 4 + _mib(6)
    in_specs = [pl.BlockSpec((bm, kp), lambda i, j: (i, 0), pipeline_mode=pl.Buffered(1))
                for _ in range(nparts)]
    in_specs += [pl.BlockSpec((None, kp, bn), functools.partial(lambda i, j, p: (layer, p, j), p=p))
                 for p in range(nparts)]
    in_specs += [pl.BlockSpec((bm, bn), lambda i, j: (i, j))]
    return pl.pallas_call(
        functools.partial(_mm_out_kernel, nparts=nparts),
        grid=(m // bm, n // bn),
        in_specs=in_specs,
        out_specs=pl.BlockSpec((bm, bn), lambda i, j: (i, j)),
        out_shape=jax.ShapeDtypeStruct((m, n), F32),
        compiler_params=_cparams(("parallel", "parallel"), vmem),
        name="mm_out",
    )(*parts, *([w] * nparts), resid)


def _rmsnorm_kernel(x_ref, nw_ref, o_ref):
    o_ref[...] = _rms(x_ref[...]) * nw_ref[...]


def rmsnorm(x, nw, bm):
    m, d = x.shape
    return pl.pallas_call(
        _rmsnorm_kernel,
        grid=(m // bm,),
        in_specs=[pl.BlockSpec((bm, d), lambda i: (i, 0)), pl.BlockSpec((1, d), lambda i: (0, 0))],
        out_specs=pl.BlockSpec((bm, d), lambda i: (i, 0)),
        out_shape=jax.ShapeDtypeStruct((m, d), F32),
        compiler_params=_cparams(("parallel",), 6 * bm * d * 4 + _mib(4)),
        name="rmsnorm",
    )(x, nw.reshape(1, d))


def _ffn_up_kernel(x_ref, nw_ref, wg_ref, wu_ref, cw_ref, cb_ref, prev_ref,
                   act_ref, st_ref, xn_ref, gbuf_ref, carry_ref, *, shift, blocks_per_group):
    i = pl.program_id(0)
    j = pl.program_id(1)
    bm = x_ref.shape[0]
    n = (FFN_CONV - 1) * shift
    off = gbuf_ref.shape[0] - bm

    @pl.when(j == 0)
    def _():
        _norm_rows_to(x_ref, nw_ref, xn_ref)

    xn = xn_ref[...]
    g = _dot(xn, wg_ref[...].astype(BF16))
    u = _dot(xn, wu_ref[...].astype(BF16))
    first = (i % blocks_per_group) == 0

    @pl.when(first)
    def _():
        gbuf_ref[off - n:off, :] = prev_ref[0]

    @pl.when(jnp.logical_not(first))
    def _():
        gbuf_ref[off - n:off, :] = carry_ref[j]

    gc, tail = _causal_conv(gbuf_ref, g, cw_ref, cb_ref, shift, FFN_CONV)
    act_ref[...] = (_silu(gc) * u).astype(BF16)
    carry_ref[j] = tail
    st_ref[0] = tail


def ffn_up(x, nw, w_up, layer, cw, cb, prev, shift, group_rows, bm, bf):
    m, d = x.shape
    f = cw.shape[1]
    n = (FFN_CONV - 1) * shift
    off = _conv_off(shift, FFN_CONV)
    bpg = group_rows // bm
    nf = f // bf
    vmem = (bm * d * 4 + bm * d * 2 + 4 * d * bf * 4 + 2 * d * bf * 2 + 2 * bm * bf * 2
            + (off + bm) * bf * 4 + nf * max(n, SUBLANES) * bf * 4 + 4 * n * bf * 4 + 5 * bm * bf * 4
            + 4 * NORM_ROWS * d * 4 + _mib(6))
    act, st = pl.pallas_call(
        functools.partial(_ffn_up_kernel, shift=shift, blocks_per_group=bpg),
        grid=(m // bm, nf),
        in_specs=[pl.BlockSpec((bm, d), lambda i, j: (i, 0), pipeline_mode=pl.Buffered(1)),
                  pl.BlockSpec((1, d), lambda i, j: (0, 0)),
                  pl.BlockSpec((None, d, bf), lambda i, j: (layer, 0, j)),
                  pl.BlockSpec((None, d, bf), lambda i, j: (layer, 0, j + nf)),
                  pl.BlockSpec((FFN_CONV, bf), lambda i, j: (0, j)),
                  pl.BlockSpec((1, bf), lambda i, j: (0, j)),
                  pl.BlockSpec((1, n, bf), lambda i, j: (i // bpg, 0, j))],
        out_specs=[pl.BlockSpec((bm, bf), lambda i, j: (i, j)),
                   pl.BlockSpec((1, n, bf), lambda i, j: (i, 0, j))],
        out_shape=[jax.ShapeDtypeStruct((m, f), BF16),
                   jax.ShapeDtypeStruct((m // bm, n, f), F32)],
        scratch_shapes=[pltpu.VMEM((bm, d), BF16),
                        pltpu.VMEM((off + bm, bf), F32),
                        pltpu.VMEM((nf, n, bf), F32)],
        compiler_params=_cparams(("arbitrary", "arbitrary"), vmem),
        name="ffn_up",
    )(x, nw.reshape(1, d), w_up, w_up, cw, cb.reshape(1, f), prev)
    return act, st[bpg - 1::bpg]


def _expand_heads64(x, nheads):
    r = x.shape[0]
    lo = lax.broadcasted_iota(jnp.int32, (r, LANES), 1) < SSD_HEAD_DIM
    tiles = []
    for p in range(nheads // 2):
        a0 = jnp.broadcast_to(x[:, 2 * p:2 * p + 1], (r, LANES))
        a1 = jnp.broadcast_to(x[:, 2 * p + 1:2 * p + 2], (r, LANES))
        tiles.append(jnp.where(lo, a0, a1))
    return jnp.concatenate(tiles, axis=1)


def _ssd_gates(xs_c, bc_c, dt_raw, dtb, alog):
    xs = _silu(xs_c)
    bc = _silu(bc_c)
    dt = _softplus(dt_raw + dtb)
    la = dt * (-jnp.exp(alog))
    return xs, bc, dt, la


def _ssd_finish(o, xs, z, dx, nw):
    y = (o + dx * xs) * _silu(z)
    gw = D_SSD // SSD_GROUPS
    y = jnp.concatenate([_rms(y[:, g * gw:(g + 1) * gw]) for g in range(SSD_GROUPS)], axis=1)
    return y * nw


def _ssd_prompt_kernel(z_ref, xs_ref, bc_ref, dt_ref, cwx_ref, cwb_ref, cbx_ref, cbb_ref,
                       dtb_ref, alog_ref, dx_ref, nw_ref,
                       y_ref, sfin_ref, cfin_ref, s_ref, bufx_ref, bufb_ref):
    c = pl.program_id(1)
    nc = pl.num_programs(1)
    rows = xs_ref.shape[0]
    gw = D_SSD // SSD_GROUPS
    hpg = SSD_HEADS // SSD_GROUPS

    @pl.when(c == 0)
    def _():
        s_ref[...] = jnp.zeros_like(s_ref)
        bufx_ref[0:SUBLANES, :] = jnp.zeros((SUBLANES, D_SSD), F32)
        bufb_ref[0:SUBLANES, :] = jnp.zeros((SUBLANES, SSD_BC), F32)

    xs_c, tailx = _causal_conv(bufx_ref, xs_ref[...], cwx_ref, cbx_ref, 1, SSD_CONV)
    bc_c, tailb = _causal_conv(bufb_ref, bc_ref[...], cwb_ref, cbb_ref, 1, SSD_CONV)
    xs, bc, dt, la = _ssd_gates(xs_c, bc_c, dt_ref[...], dtb_ref[...], alog_ref[...])
    cum = _cumsum_rows(la)
    cum_t = cum.T
    cumx = _expand_heads64(cum, SSD_HEADS)
    dtx = _expand_heads64(dt, SSD_HEADS)
    lastx = cumx[rows - 1:rows, :]
    ecum = jnp.exp(cumx)
    wx = jnp.exp(lastx - cumx)
    elast = jnp.exp(lastx)
    v_all = xs * dtx
    vw_all = v_all * wx

    ri = lax.broadcasted_iota(jnp.int32, (rows, rows), 0)
    ci = lax.broadcasted_iota(jnp.int32, (rows, rows), 1)
    causal = ri >= ci
    lo = lax.broadcasted_iota(jnp.int32, (rows, LANES), 1) < SSD_HEAD_DIM

    o_groups = []
    for g in range(SSD_GROUPS):
        kb = bc[:, g * SSD_STATE:(g + 1) * SSD_STATE].astype(BF16)
        qb = bc[:, (SSD_GROUPS + g) * SSD_STATE:(SSD_GROUPS + g + 1) * SSD_STATE].astype(BF16)
        qk = lax.dot_general(qb, kb, NT, preferred_element_type=F32)
        o_tiles = []
        for p in range(hpg // 2):
            acc = None
            for q in range(2):
                h = g * hpg + 2 * p + q
                diff = cum[:, h:h + 1] - cum_t[h:h + 1, :]
                dec = jnp.exp(jnp.where(causal, diff, -1e30))
                pm = (qk * dec).astype(BF16)
                col = g * gw + p * LANES
                vp = v_all[:, col:col + LANES]
                vh = jnp.where(lo, vp, 0.0) if q == 0 else jnp.where(lo, 0.0, vp)
                t = _dot(pm, vh.astype(BF16))
                acc = t if acc is None else acc + t
            o_tiles.append(acc)
        o_intra = jnp.concatenate(o_tiles, axis=1)
        sg = s_ref[g]
        sl = slice(g * gw, (g + 1) * gw)
        o_inter = _dot(qb, sg.astype(BF16)) * ecum[:, sl]
        upd = lax.dot_general(kb, vw_all[:, sl].astype(BF16), TN, preferred_element_type=F32)
        s_ref[g] = sg * elast[:, sl] + upd
        o_groups.append(o_intra + o_inter)
    o = jnp.concatenate(o_groups, axis=1)
    y_ref[...] = _ssd_finish(o, xs, z_ref[...], dx_ref[...], nw_ref[...]).astype(BF16)

    @pl.when(c == nc - 1)
    def _():
        for h in range(SSD_HEADS):
            g, hl = divmod(h, hpg)
            sfin_ref[0, h] = s_ref[g, :, hl * SSD_HEAD_DIM:(hl + 1) * SSD_HEAD_DIM]
        cfin_ref[0, :, 0:D_SSD] = tailx
        cfin_ref[0, :, D_SSD:D_SSD + SSD_BC] = tailb


def ssd_prompt(pe, prm, nseq, seq, chunk):
    m = nseq * seq
    nc = seq // chunk
    row = lambda b, c: b * nc + c
    full = lambda shp: pl.BlockSpec(shp, lambda b, c: (0,) * len(shp))
    in_specs = [
        pl.BlockSpec((chunk, D_SSD), lambda b, c: (row(b, c), 0)),
        pl.BlockSpec((chunk, D_SSD), lambda b, c: (row(b, c), 1)),
        pl.BlockSpec((chunk, SSD_BC), lambda b, c: (row(b, c), 10)),
        pl.BlockSpec((chunk, LANES), lambda b, c: (row(b, c), DT_COL // LANES)),
        pl.BlockSpec((SSD_CONV, D_SSD), lambda b, c: (0, 0)),
        pl.BlockSpec((SSD_CONV, SSD_BC), lambda b, c: (0, 2)),
        pl.BlockSpec((1, D_SSD), lambda b, c: (0, 0)),
        pl.BlockSpec((1, SSD_BC), lambda b, c: (0, 2)),
        full((1, LANES)), full((1, LANES)), full((1, D_SSD)), full((1, D_SSD)),
    ]
    vmem = _mib(48)
    return pl.pallas_call(
        _ssd_prompt_kernel,
        grid=(nseq, nc),
        in_specs=in_specs,
        out_specs=[pl.BlockSpec((chunk, D_SSD), lambda b, c: (row(b, c), 0)),
                   pl.BlockSpec((1, SSD_HEADS, SSD_STATE, SSD_HEAD_DIM), lambda b, c: (b, 0, 0, 0)),
                   pl.BlockSpec((1, SSD_CONV - 1, D_SSD + SSD_BC), lambda b, c: (b, 0, 0))],
        out_shape=[jax.ShapeDtypeStruct((m, D_SSD), BF16),
                   jax.ShapeDtypeStruct((nseq, SSD_HEADS, SSD_STATE, SSD_HEAD_DIM), F32),
                   jax.ShapeDtypeStruct((nseq, SSD_CONV - 1, D_SSD + SSD_BC), F32)],
        scratch_shapes=[pltpu.VMEM((SSD_GROUPS, SSD_STATE, D_SSD // SSD_GROUPS), F32),
                        pltpu.VMEM((SUBLANES + chunk, D_SSD), F32),
                        pltpu.VMEM((SUBLANES + chunk, SSD_BC), F32)],
        compiler_params=_cparams(("parallel", "arbitrary"), vmem),
        name="ssd_prompt",
    )(pe, pe, pe, pe, prm["ssd_cw"], prm["ssd_cw"], prm["ssd_cb"], prm["ssd_cb"],
      prm["ssd_dtb"], prm["ssd_alog"], prm["ssd_dx"], prm["ssd_nw"])


def _rotary(x, cos, sin_signed):
    return x * cos + pltpu.roll(x, RET_DK // 2, 1) * sin_signed


def _ret_prompt_kernel(q_ref, k_ref, v_ref, g_ref, cos_ref, sin_ref, o_ref, sfin_ref, s_ref):
    c = pl.program_id(1)
    nc = pl.num_programs(1)
    rows = q_ref.shape[0]

    @pl.when(c == 0)
    def _():
        s_ref[...] = jnp.zeros_like(s_ref)

    cos = cos_ref[...]
    sin = sin_ref[...]
    ri = lax.broadcasted_iota(jnp.int32, (rows, rows), 0)
    ci = lax.broadcasted_iota(jnp.int32, (rows, rows), 1)
    dij = jnp.where(ri >= ci, (ri - ci).astype(F32), 1e30)
    tk = lax.broadcasted_iota(jnp.int32, (rows, RET_DK), 0).astype(F32)
    tv = lax.broadcasted_iota(jnp.int32, (rows, RET_DV), 0).astype(F32)
    for h in range(RET_HEADS):
        lg = RET_LOG_GAMMA[h]
        qr = _rotary(q_ref[:, h * RET_DK:(h + 1) * RET_DK], cos, sin)
        kr = _rotary(k_ref[:, h * RET_DK:(h + 1) * RET_DK], cos, sin) * (RET_DK ** -0.5)
        qb = qr.astype(BF16)
        s = lax.dot_general(qb, kr.astype(BF16), NT, preferred_element_type=F32)
        pm = (s * jnp.exp(lg * dij)).astype(BF16)
        vb = v_ref[:, h * RET_DV:(h + 1) * RET_DV].astype(BF16)
        sh = s_ref[h]
        o = _dot(pm, vb) + _dot(qb, sh.astype(BF16)) * jnp.exp(lg * (tv + 1.0))
        kw = (kr * jnp.exp(lg * ((rows - 1.0) - tk))).astype(BF16)
        s_ref[h] = sh * math.exp(lg * rows) + lax.dot_general(kw, vb, TN, preferred_element_type=F32)
        gh = g_ref[:, h * RET_DV:(h + 1) * RET_DV]
        o_ref[:, h * RET_DV:(h + 1) * RET_DV] = (_rms(o) * _silu(gh)).astype(BF16)

    @pl.when(c == nc - 1)
    def _():
        sfin_ref[0] = s_ref[...]


def ret_prompt(pe, cos, sin, nseq, seq, chunk):
    m = nseq * seq
    nc = seq // chunk
    row = lambda b, c: b * nc + c
    return pl.pallas_call(
        _ret_prompt_kernel,
        grid=(nseq, nc),
        in_specs=[pl.BlockSpec((chunk, RET_QK), lambda b, c: (row(b, c), 8)),
                  pl.BlockSpec((chunk, RET_QK), lambda b, c: (row(b, c), 9)),
                  pl.BlockSpec((chunk, D_RET), lambda b, c: (row(b, c), 2)),
                  pl.BlockSpec((chunk, D_RET), lambda b, c: (row(b, c), 3)),
                  pl.BlockSpec((chunk, RET_DK), lambda b, c: (c, 0)),
                  pl.BlockSpec((chunk, RET_DK), lambda b, c: (c, 0))],
        out_specs=[pl.BlockSpec((chunk, D_RET), lambda b, c: (row(b, c), 0)),
                   pl.BlockSpec((1, RET_HEADS, RET_DK, RET_DV), lambda b, c: (b, 0, 0, 0))],
        out_shape=[jax.ShapeDtypeStruct((m, D_RET), BF16),
                   jax.ShapeDtypeStruct((nseq, RET_HEADS, RET_DK, RET_DV), F32)],
        scratch_shapes=[pltpu.VMEM((RET_HEADS, RET_DK, RET_DV), F32)],
        compiler_params=_cparams(("parallel", "arbitrary"), _mib(40)),
        name="ret_prompt",
    )(pe, pe, pe, pe, cos, sin)


HG_HB = 4


def _hg_gates(hq, hf, lb):
    q = _silu(hq)
    f = lb + (1.0 - lb) * jax.nn.sigmoid(hf)
    return q, f, 1.0 - f, jnp.log(f)


def _hg_tables(rows):
    r = np.arange(rows)[:, None]
    t = np.arange(rows)[None, :]
    sums = [t <= r]
    masks = []
    s = rows // 2
    while s >= 1:
        blk, pos = r // (2 * s), r % (2 * s)
        ref = blk * 2 * s + s - 1
        upper = pos >= s
        sums.append(np.where(upper, (t > ref) & (t <= r), (t > r) & (t <= ref)))
        masks.append((blk == t // (2 * s)) & upper & (t % (2 * s) < s))
        s //= 2
    masks.append(r == t)
    return (np.concatenate(sums, axis=0).astype(np.float32),
            np.stack(masks).astype(np.float32))


def _hg_prompt_kernel(hq_ref, hf_ref, hi_ref, hg_ref, lb_ref, nw_ref, sums_ref, masks_ref,
                      og_ref, sfin_ref, st_ref):
    c = pl.program_id(2)
    nc = pl.num_programs(2)
    rows = hq_ref.shape[0]
    nlev = masks_ref.shape[0] - 1

    @pl.when(c == 0)
    def _():
        st_ref[...] = jnp.zeros_like(st_ref)

    q_all, _, kk_all, lg_all = _hg_gates(hq_ref[...], hf_ref[...], lb_ref[...])
    lg_hi = lg_all.astype(BF16)
    lg_lo = (lg_all - lg_hi.astype(F32)).astype(BF16)
    sums = sums_ref[...]
    dall = _dot(sums, lg_hi) + _dot(sums, lg_lo)
    cum_all = dall[0:rows]
    ecum_all = jnp.exp(cum_all)
    elev = [jnp.exp(dall[(l + 1) * rows:(l + 2) * rows]) for l in range(nlev)]
    for h in range(HG_HB):
        sl = slice(h * HG_DK, (h + 1) * HG_DK)
        q = q_all[:, sl]
        kk = kk_all[:, sl]
        cum = cum_all[:, sl]
        vb = hi_ref[:, sl].astype(BF16)
        last = cum[rows - 1:rows, :]
        st = st_ref[h]
        o = lax.dot_general((q * ecum_all[:, sl]).astype(BF16), st.astype(BF16), NT,
                            preferred_element_type=F32)
        kt = (kk * jnp.exp(last - cum)).astype(BF16)
        st_ref[h] = st * jnp.exp(last) + lax.dot_general(vb, kt, TN, preferred_element_type=F32)
        a = masks_ref[nlev] * lax.dot_general(q.astype(BF16), kk.astype(BF16), NT,
                                              preferred_element_type=F32)
        for l in range(nlev):
            e = elev[l][:, sl]
            a = a + masks_ref[l] * lax.dot_general((q * e).astype(BF16), (kk * e).astype(BF16), NT,
                                                   preferred_element_type=F32)
        o = o + _dot(a.astype(BF16), vb)
        og_ref[:, sl] = (_rms(o) * nw_ref[...] * _silu(hg_ref[:, sl])).astype(BF16)

    @pl.when(c == nc - 1)
    def _():
        for h in range(HG_HB):
            sfin_ref[0, h] = st_ref[h].T


def hg_prompt(po, lb, nw, nseq, seq, chunk):
    m = nseq * seq
    nc = seq // chunk
    nhb = HG_HEADS // HG_HB
    w = HG_HB * HG_DK
    row = lambda b, hb, c: b * nc + c
    sums_np, masks_np = _hg_tables(chunk)
    sums = jnp.asarray(sums_np, BF16)
    masks = jnp.asarray(masks_np, F32)
    return pl.pallas_call(
        _hg_prompt_kernel,
        grid=(nseq, nhb, nc),
        in_specs=[pl.BlockSpec((chunk, w), lambda b, hb, c: (row(b, hb, c), hb)),
                  pl.BlockSpec((chunk, w), lambda b, hb, c: (row(b, hb, c), nhb + hb)),
                  pl.BlockSpec((chunk, w), lambda b, hb, c: (row(b, hb, c), 2 * nhb + hb)),
                  pl.BlockSpec((chunk, w), lambda b, hb, c: (row(b, hb, c), 3 * nhb + hb)),
                  pl.BlockSpec((1, w), lambda b, hb, c: (0, hb)),
                  pl.BlockSpec((1, HG_DV), lambda b, hb, c: (0, 0)),
                  pl.BlockSpec(sums.shape, lambda b, hb, c: (0, 0)),
                  pl.BlockSpec(masks.shape, lambda b, hb, c: (0, 0, 0))],
        out_specs=[pl.BlockSpec((chunk, w), lambda b, hb, c: (row(b, hb, c), hb)),
                   pl.BlockSpec((1, HG_HB, HG_DK, HG_DV), lambda b, hb, c: (b, hb, 0, 0))],
        out_shape=[jax.ShapeDtypeStruct((m, D_HG), BF16),
                   jax.ShapeDtypeStruct((nseq, HG_HEADS, HG_DK, HG_DV), F32)],
        scratch_shapes=[pltpu.VMEM((HG_HB, HG_DV, HG_DK), F32)],
        compiler_params=_cparams(("parallel", "parallel", "arbitrary"), _mib(32)),
        name="hg_prompt",
    )(po, po, po, po, lb, nw, sums, masks)


def _lin_scan(a, b, shift):
    rows = a.shape[0]
    ri = lax.broadcasted_iota(jnp.int32, a.shape, 0)
    d = shift
    while d < rows:
        keep = ri >= d
        a_s = jnp.where(keep, pltpu.roll(a, d, 0), 1.0)
        b_s = jnp.where(keep, pltpu.roll(b, d, 0), 0.0)
        b = a * b_s + b
        a = a * a_s
        d *= 2
    return a, b


def _lru_gates(xr, wr_ref, br, wi_ref, bi, ap):
    xb = xr.astype(BF16)
    r_parts, i_parts = [], []
    for n in range(LRU_BLOCKS):
        xn = xb[:, n * LRU_BW:(n + 1) * LRU_BW]
        r_parts.append(_dot(xn, wr_ref[n].astype(BF16)))
        i_parts.append(_dot(xn, wi_ref[n].astype(BF16)))
    r = jax.nn.sigmoid(jnp.concatenate(r_parts, axis=1) + br)
    gi = jax.nn.sigmoid(jnp.concatenate(i_parts, axis=1) + bi)
    la = -LRU_C * r * _softplus(-ap)
    a = jnp.exp(la)
    th = jnp.tanh(la)
    mult = jnp.sqrt(-2.0 * th / (1.0 - th))
    return a, mult, gi


def _lru_prompt_kernel(rx_ref, rg_ref, cw_ref, cb_ref, wr_ref, br_ref, wi_ref, bi_ref, ap_ref,
                       yl_ref, hfin_ref, cfin_ref, buf_ref, hc_ref):
    c = pl.program_id(1)
    nc = pl.num_programs(1)
    rows = rx_ref.shape[0]

    @pl.when(c == 0)
    def _():
        buf_ref[0:SUBLANES, :] = jnp.zeros((SUBLANES, D_RNN), F32)
        hc_ref[...] = jnp.zeros_like(hc_ref)

    xr, tail = _causal_conv(buf_ref, rx_ref[...], cw_ref, cb_ref, 1, LRU_CONV)
    a, mult, gi = _lru_gates(xr, wr_ref, br_ref[...], wi_ref, bi_ref[...], ap_ref[...])
    ri = lax.broadcasted_iota(jnp.int32, (rows, D_RNN), 0)
    mult = jnp.where(jnp.logical_and(c == 0, ri == 0), 1.0, mult)
    pa, hb = _lin_scan(a, mult * gi * xr, 1)
    hs = pa * hc_ref[0:1, :] + hb
    hc_ref[0:1, :] = hs[rows - 1:rows, :]
    yl_ref[...] = (hs * _gelu_tanh(rg_ref[...])).astype(BF16)

    @pl.when(c == nc - 1)
    def _():
        hfin_ref[0] = hs[rows - 1:rows, :]
        cfin_ref[0] = tail


def lru_prompt(po, prm, nseq, seq, chunk):
    m = nseq * seq
    nc = seq // chunk
    row = lambda b, c: b * nc + c
    full = lambda shp: pl.BlockSpec(shp, lambda b, c: (0,) * len(shp))
    return pl.pallas_call(
        _lru_prompt_kernel,
        grid=(nseq, nc),
        in_specs=[pl.BlockSpec((chunk, D_RNN), lambda b, c: (row(b, c), 4)),
                  pl.BlockSpec((chunk, D_RNN), lambda b, c: (row(b, c), 5)),
                  full((LRU_CONV, D_RNN)), full((1, D_RNN)),
                  full((LRU_BLOCKS, LRU_BW, LRU_BW)), full((1, D_RNN)),
                  full((LRU_BLOCKS, LRU_BW, LRU_BW)), full((1, D_RNN)), full((1, D_RNN))],
        out_specs=[pl.BlockSpec((chunk, D_RNN), lambda b, c: (row(b, c), 0)),
                   pl.BlockSpec((1, 1, D_RNN), lambda b, c: (b, 0, 0)),
                   pl.BlockSpec((1, LRU_CONV - 1, D_RNN), lambda b, c: (b, 0, 0))],
        out_shape=[jax.ShapeDtypeStruct((m, D_RNN), BF16),
                   jax.ShapeDtypeStruct((nseq, 1, D_RNN), F32),
                   jax.ShapeDtypeStruct((nseq, LRU_CONV - 1, D_RNN), F32)],
        scratch_shapes=[pltpu.VMEM((SUBLANES + chunk, D_RNN), F32),
                        pltpu.VMEM((SUBLANES, D_RNN), F32)],
        compiler_params=_cparams(("parallel", "arbitrary"), _mib(48)),
        name="lru_prompt",
    )(po, po, prm["lru_cw"], prm["lru_cb"], prm["lru_wr"], prm["lru_br"], prm["lru_wi"],
      prm["lru_bi"], prm["lru_ap"])


TOKP = SUBLANES


def _ssd_state_kernel(q_ref, k_ref, vw_ref, dec_ref, st_ref, u_ref, snew_ref):
    hpg = SSD_HEADS // SSD_GROUPS
    gw = hpg * SSD_HEAD_DIM

    def body(b, carry):
        q = q_ref[b].astype(BF16)
        k = k_ref[b].astype(BF16)
        vw = vw_ref[b].astype(BF16)
        for g in range(SSD_GROUPS):
            ks = slice(g * SSD_STATE, (g + 1) * SSD_STATE)
            stg = st_ref[b, g * hpg:(g + 1) * hpg].reshape(gw, SSD_STATE)
            u_ref[b, :, g * gw:(g + 1) * gw] = lax.dot_general(
                q[:, ks], stg.astype(BF16), NT, preferred_element_type=F32)
            upd = lax.dot_general(vw[:, g * gw:(g + 1) * gw], k[:, ks], TN, preferred_element_type=F32)
            for hl in range(hpg):
                h = g * hpg + hl
                snew_ref[b, h] = (st_ref[b, h] * dec_ref[b, h:h + 1, :]
                                  + upd[hl * SSD_HEAD_DIM:(hl + 1) * SSD_HEAD_DIM, :])
        return carry

    lax.fori_loop(0, st_ref.shape[0], body, 0)


def _ret_state_kernel(q_ref, k_ref, v_ref, s_ref, u_ref, snew_ref, *, ntok):
    def body(b, carry):
        q = q_ref[b].astype(BF16)
        k = k_ref[b].astype(BF16)
        v = v_ref[b].astype(BF16)
        for h in range(RET_HEADS):
            ks = slice(h * RET_DK, (h + 1) * RET_DK)
            vs = slice(h * RET_DV, (h + 1) * RET_DV)
            s0 = s_ref[b, h]
            u_ref[b, :, vs] = _dot(q[:, ks], s0.astype(BF16))
            upd = lax.dot_general(k[:, ks], v[:, vs], TN, preferred_element_type=F32)
            snew_ref[b, h] = s0 * math.exp(RET_LOG_GAMMA[h] * ntok) + upd
        return carry

    lax.fori_loop(0, s_ref.shape[0], body, 0)


def _hg_state_kernel(q_ref, k_ref, v_ref, dcol_ref, s_ref, u_ref, snew_ref):
    def body(b, carry):
        q = q_ref[b].astype(BF16)
        k = k_ref[b].astype(BF16)
        v = v_ref[b].astype(BF16)
        dc = dcol_ref[b]
        for h in range(HG_HEADS):
            ks = slice(h * HG_DK, (h + 1) * HG_DK)
            s0 = s_ref[b, h]
            u_ref[b, :, ks] = _dot(q[:, ks], s0.astype(BF16))
            upd = lax.dot_general(k[:, ks], v[:, ks], TN, preferred_element_type=F32)
            snew_ref[b, h] = s0 * dc[:, h:h + 1] + upd
        return carry

    lax.fori_loop(0, s_ref.shape[0], body, 0)


def _state_call(kern, rows_in, extra, s, u_cols, bb, name):
    nb = s.shape[0]
    blk = lambda a: pl.BlockSpec((bb,) + a.shape[1:], lambda i: (i,) + (0,) * (a.ndim - 1))
    ins = list(rows_in) + ([extra] if extra is not None else []) + [s]
    sbytes = bb * int(np.prod(s.shape[1:])) * 4
    rbytes = sum(bb * int(np.prod(a.shape[1:])) * 4 for a in ins[:-1]) + bb * TOKP * u_cols * 4
    return pl.pallas_call(
        kern,
        grid=(nb // bb,),
        in_specs=[blk(a) for a in ins],
        out_specs=[pl.BlockSpec((bb, TOKP, u_cols), lambda i: (i, 0, 0)), blk(s)],
        out_shape=[jax.ShapeDtypeStruct((nb, TOKP, u_cols), F32),
                   jax.ShapeDtypeStruct(s.shape, F32)],
        compiler_params=_cparams(("parallel",), 5 * sbytes + 3 * rbytes + _mib(8)),
        name=name,
    )(*ins)


def _tok(x, t, nb):
    return x[t * nb:(t + 1) * nb]


def _head_sums(x, width):
    r, n = x.shape
    tiles = []
    for h in range(n // width):
        s = jnp.sum(x[:, h * width:(h + 1) * width], axis=1, keepdims=True)
        tiles.append(jnp.broadcast_to(s, (r, width)))
    return jnp.concatenate(tiles, axis=1)


def _ssd_sample_pre_kernel(xs_ref, bc_ref, dt_ref, prevx_ref, prevb_ref, cwx_ref, cwb_ref,
                           cbx_ref, cbb_ref, dtb_ref, alog_ref,
                           xs_out, bc_out, vw_out, oi_out, ecum_out, elast_out, tailx_out, tailb_out,
                           bufx_ref, bufb_ref, *, nb, ntok):
    n = (SSD_CONV - 1) * nb
    offx = bufx_ref.shape[0] - nb * ntok
    bufx_ref[offx - n:offx, :] = prevx_ref[...]
    bufb_ref[offx - n:offx, :] = prevb_ref[...]
    xs_c, tailx = _causal_conv(bufx_ref, xs_ref[...], cwx_ref, cbx_ref, nb, SSD_CONV)
    bc_c, tailb = _causal_conv(bufb_ref, bc_ref[...], cwb_ref, cbb_ref, nb, SSD_CONV)
    xs, bc, dt, la = _ssd_gates(xs_c, bc_c, dt_ref[...], dtb_ref[...], alog_ref[...])
    tailx_out[...] = tailx
    tailb_out[...] = tailb
    xs_out[...] = xs
    bc_out[...] = bc
    cums = []
    for t in range(ntok):
        lt = _tok(la, t, nb)
        cums.append(lt if t == 0 else cums[-1] + lt)
    cumx = [_expand_heads64(cm, SSD_HEADS) for cm in cums]
    v = xs * _expand_heads64(dt, SSD_HEADS)
    kw = SSD_GROUPS * SSD_STATE
    for t in range(ntok):
        ct = _tok(bc, t, nb)[:, kw:2 * kw]
        acc = None
        for t2 in range(t + 1):
            bt = _tok(bc, t2, nb)[:, 0:kw]
            sc = _head_sums(ct * bt, SSD_STATE)
            scx = jnp.concatenate(
                [jnp.concatenate([sc[:, g * SSD_STATE:(g + 1) * SSD_STATE]] * 4, axis=1)
                 for g in range(SSD_GROUPS)], axis=1)
            term = scx * jnp.exp(cumx[t] - cumx[t2]) * _tok(v, t2, nb)
            acc = term if acc is None else acc + term
        oi_out[t * nb:(t + 1) * nb, :] = acc
        ecum_out[t * nb:(t + 1) * nb, :] = jnp.exp(cumx[t])
        vw_out[t * nb:(t + 1) * nb, :] = _tok(v, t, nb) * jnp.exp(cumx[ntok - 1] - cumx[t])
    elast_out[...] = jnp.exp(cums[ntok - 1])


def _ssd_sample_post_kernel(oi_ref, u_ref, ecum_ref, xs_ref, z_ref, dx_ref, nw_ref, y_ref):
    o = oi_ref[...] + ecum_ref[...] * u_ref[...]
    y_ref[...] = _ssd_finish(o, xs_ref[...], z_ref[...], dx_ref[...], nw_ref[...]).astype(BF16)


def _ret_sample_pre_kernel(q_ref, k_ref, v_ref, cos_ref, sin_ref, qd_out, kd_out, oi_out, *, nb, ntok):
    rows = nb * ntok
    cos = jnp.concatenate([jnp.broadcast_to(cos_ref[t:t + 1, :], (nb, RET_DK)) for t in range(ntok)], axis=0)
    sin = jnp.concatenate([jnp.broadcast_to(sin_ref[t:t + 1, :], (nb, RET_DK)) for t in range(ntok)], axis=0)
    qr, kr = [], []
    for h in range(RET_HEADS):
        sl = slice(h * RET_DK, (h + 1) * RET_DK)
        qr.append(_rotary(q_ref[:, sl], cos, sin))
        kr.append(_rotary(k_ref[:, sl], cos, sin) * (RET_DK ** -0.5))
    qr = jnp.concatenate(qr, axis=1)
    kr = jnp.concatenate(kr, axis=1)
    v = v_ref[...]
    for t in range(ntok):
        qt = _tok(qr, t, nb)
        acc = None
        for t2 in range(t + 1):
            sc = _head_sums(qt * _tok(kr, t2, nb), RET_DK)
            vt = _tok(v, t2, nb)
            tiles = []
            for h in range(RET_HEADS):
                dec = math.exp(RET_LOG_GAMMA[h] * (t - t2))
                s = sc[:, h * RET_DK:(h + 1) * RET_DK] * dec
                tiles.append(jnp.concatenate([s, s], axis=1) * vt[:, h * RET_DV:(h + 1) * RET_DV])
            term = jnp.concatenate(tiles, axis=1)
            acc = term if acc is None else acc + term
        oi_out[t * nb:(t + 1) * nb, :] = acc
        qd = jnp.concatenate([qt[:, h * RET_DK:(h + 1) * RET_DK] * math.exp(RET_LOG_GAMMA[h] * (t + 1))
                              for h in range(RET_HEADS)], axis=1)
        kt = _tok(kr, t, nb)
        kd = jnp.concatenate([kt[:, h * RET_DK:(h + 1) * RET_DK] * math.exp(RET_LOG_GAMMA[h] * (ntok - 1 - t))
                              for h in range(RET_HEADS)], axis=1)
        qd_out[t * nb:(t + 1) * nb, :] = qd
        kd_out[t * nb:(t + 1) * nb, :] = kd


def _ret_sample_post_kernel(oi_ref, u_ref, g_ref, o_ref):
    o = oi_ref[...] + u_ref[...]
    g = g_ref[...]
    for h in range(RET_HEADS):
        sl = slice(h * RET_DV, (h + 1) * RET_DV)
        o_ref[:, sl] = (_rms(o[:, sl]) * _silu(g[:, sl])).astype(BF16)


def _hg_sample_pre_kernel(hq_ref, hf_ref, hi_ref, lb_ref, qd_out, kd_out, dl_out, oi_out, *, nb, ntok):
    q, _, kk, lg = _hg_gates(hq_ref[...], hf_ref[...], lb_ref[...])
    v = hi_ref[...]
    cums = []
    for t in range(ntok):
        lt = _tok(lg, t, nb)
        cums.append(lt if t == 0 else cums[-1] + lt)
    for t in range(ntok):
        qt = _tok(q, t, nb)
        acc = None
        for t2 in range(t + 1):
            w = qt * _tok(kk, t2, nb)
            if t2 < t:
                w = w * jnp.exp(cums[t] - cums[t2])
            term = _head_sums(w, HG_DK) * _tok(v, t2, nb)
            acc = term if acc is None else acc + term
        oi_out[t * nb:(t + 1) * nb, :] = acc
        qd_out[t * nb:(t + 1) * nb, :] = qt * jnp.exp(cums[t])
        kd_out[t * nb:(t + 1) * nb, :] = _tok(kk, t, nb) * jnp.exp(cums[ntok - 1] - cums[t])
    dl_out[...] = jnp.exp(cums[ntok - 1])


def _hg_sample_post_kernel(oi_ref, u_ref, hg_ref, nw_ref, og_ref):
    o = oi_ref[...] + u_ref[...]
    hg = hg_ref[...]
    for h in range(HG_HEADS):
        sl = slice(h * HG_DV, (h + 1) * HG_DV)
        og_ref[:, sl] = (_rms(o[:, sl]) * nw_ref[...] * _silu(hg[:, sl])).astype(BF16)


def _lru_sample_kernel(rx_ref, rg_ref, prev_ref, h0_ref, cw_ref, cb_ref, wr_ref, br_ref, wi_ref,
                       bi_ref, ap_ref, yl_out, hfin_out, tail_out, buf_ref, *, nb, ntok):
    rows = nb * ntok
    n = (LRU_CONV - 1) * nb
    off = buf_ref.shape[0] - rows
    buf_ref[off - n:off, :] = prev_ref[...]
    xr, tail = _causal_conv(buf_ref, rx_ref[...], cw_ref, cb_ref, nb, LRU_CONV)
    tail_out[...] = tail
    a, mult, gi = _lru_gates(xr, wr_ref, br_ref[...], wi_ref, bi_ref[...], ap_ref[...])
    b = mult * gi * xr
    h = h0_ref[...]
    for t in range(ntok):
        h = _tok(a, t, nb) * h + _tok(b, t, nb)
        yl_out[t * nb:(t + 1) * nb, :] = (h * _gelu_tanh(rg_ref[t * nb:(t + 1) * nb, :])).astype(BF16)
    hfin_out[...] = h


def _whole(shape):
    return pl.BlockSpec(shape, lambda i: (0,) * len(shape))


def _colblock(rows, width, idx):
    return pl.BlockSpec((rows, width), functools.partial(lambda i, k: (0, k), k=idx))


def _call_whole(kern, in_arrays, in_specs, out_shapes, scratch, name, vmem):
    return pl.pallas_call(
        kern, grid=(1,), in_specs=in_specs,
        out_specs=[_whole(s.shape) for s in out_shapes],
        out_shape=out_shapes, scratch_shapes=scratch,
        compiler_params=_cparams(("arbitrary",), vmem), name=name,
    )(*in_arrays)


def _to_token_major(s):
    nb, w, c = s.shape
    return jnp.transpose(s, (1, 0, 2)).reshape(w * nb, c)


def _from_token_major(x, nb):
    w = x.shape[0] // nb
    return jnp.transpose(x.reshape(w, nb, x.shape[1]), (1, 0, 2))


def _rows_to_batch(x, nb):
    ntok = x.shape[0] // nb
    x = jnp.transpose(x.reshape(ntok, nb, x.shape[1]), (1, 0, 2))
    return jnp.pad(x, ((0, 0), (0, TOKP - ntok), (0, 0)))


def _batch_to_rows(u, ntok):
    nb = u.shape[0]
    return jnp.transpose(u[:, :ntok], (1, 0, 2)).reshape(ntok * nb, u.shape[2])


def ssd_sample(pe, z, s_ssm, s_conv, prm, nb, ntok):
    rows = nb * ntok
    n = (SSD_CONV - 1) * nb
    prev = _to_token_major(s_conv)
    off = _conv_off(nb, SSD_CONV)
    f = lambda *shape: jax.ShapeDtypeStruct(shape, F32)
    outs = _call_whole(
        functools.partial(_ssd_sample_pre_kernel, nb=nb, ntok=ntok),
        [pe, pe, pe, prev, prev, prm["ssd_cw"], prm["ssd_cw"], prm["ssd_cb"], prm["ssd_cb"],
         prm["ssd_dtb"], prm["ssd_alog"]],
        [_colblock(rows, D_SSD, 1), _colblock(rows, SSD_BC, 10), _colblock(rows, LANES, DT_COL // LANES),
         _colblock(n, D_SSD, 0), _colblock(n, SSD_BC, 2),
         _colblock(SSD_CONV, D_SSD, 0), _colblock(SSD_CONV, SSD_BC, 2),
         _colblock(1, D_SSD, 0), _colblock(1, SSD_BC, 2), _whole((1, LANES)), _whole((1, LANES))],
        [f(rows, D_SSD), f(rows, SSD_BC), f(rows, D_SSD), f(rows, D_SSD), f(rows, D_SSD),
         f(nb, LANES), f(n, D_SSD), f(n, SSD_BC)],
        [pltpu.VMEM((off + rows, D_SSD), F32), pltpu.VMEM((off + rows, SSD_BC), F32)],
        "ssd_sample_pre", _mib(56))
    xs, bc, vw, oi, ecum, elast, tailx, tailb = outs
    kw = SSD_GROUPS * SSD_STATE
    dec = jnp.broadcast_to(elast[:, :SSD_HEADS, None], (nb, SSD_HEADS, SSD_STATE))
    st = jnp.swapaxes(s_ssm, -1, -2)
    u, st_new = _state_call(_ssd_state_kernel,
                            [_rows_to_batch(bc[:, kw:], nb), _rows_to_batch(bc[:, :kw], nb),
                             _rows_to_batch(vw, nb)], dec, st, D_SSD, 4, "ssd_state")
    s_new = jnp.swapaxes(st_new, -1, -2)
    u = _batch_to_rows(u, ntok)
    (y,) = _call_whole(
        _ssd_sample_post_kernel, [oi, u, ecum, xs, z, prm["ssd_dx"], prm["ssd_nw"]],
        [_whole((rows, D_SSD))] * 4 + [_colblock(rows, D_SSD, 0), _whole((1, D_SSD)), _whole((1, D_SSD))],
        [jax.ShapeDtypeStruct((rows, D_SSD), BF16)], [], "ssd_sample_post", _mib(48))
    conv_new = _from_token_major(jnp.concatenate([tailx, tailb], axis=1), nb)
    return y, s_new, conv_new


def ret_sample(pe, s_ret, cos, sin, nb, ntok):
    rows = nb * ntok
    f = lambda *shape: jax.ShapeDtypeStruct(shape, F32)
    qd, kd, oi = _call_whole(
        functools.partial(_ret_sample_pre_kernel, nb=nb, ntok=ntok),
        [pe, pe, pe, cos, sin],
        [_colblock(rows, RET_QK, 8), _colblock(rows, RET_QK, 9), _colblock(rows, D_RET, 2),
         _whole(cos.shape), _whole(sin.shape)],
        [f(rows, RET_QK), f(rows, RET_QK), f(rows, D_RET)], [], "ret_sample_pre", _mib(48))
    v = _rows_to_batch(pe[:, 2 * D_RET:3 * D_RET], nb)
    u, s_new = _state_call(functools.partial(_ret_state_kernel, ntok=ntok),
                           [_rows_to_batch(qd, nb), _rows_to_batch(kd, nb), v], None, s_ret,
                           D_RET, 4, "ret_state")
    u = _batch_to_rows(u, ntok)
    (o,) = _call_whole(
        _ret_sample_post_kernel, [oi, u, pe],
        [_whole((rows, D_RET)), _whole((rows, D_RET)), _colblock(rows, D_RET, 3)],
        [jax.ShapeDtypeStruct((rows, D_RET), BF16)], [], "ret_sample_post", _mib(40))
    return o, s_new


def hg_sample(po, s_hg, lb, nw, nb, ntok):
    rows = nb * ntok
    f = lambda *shape: jax.ShapeDtypeStruct(shape, F32)
    qd, kd, dl, oi = _call_whole(
        functools.partial(_hg_sample_pre_kernel, nb=nb, ntok=ntok),
        [po, po, po, lb],
        [_colblock(rows, D_HG, 0), _colblock(rows, D_HG, 1), _colblock(rows, D_HG, 2), _whole((1, D_HG))],
        [f(rows, D_HG), f(rows, D_HG), f(nb, D_HG), f(rows, D_HG)], [], "hg_sample_pre", _mib(48))
    dcol = jnp.pad(jnp.transpose(dl.reshape(nb, HG_HEADS, HG_DK), (0, 2, 1)),
                   ((0, 0), (0, 0), (0, LANES - HG_HEADS)))
    v = _rows_to_batch(po[:, 2 * D_HG:3 * D_HG], nb)
    u, s_new = _state_call(_hg_state_kernel, [_rows_to_batch(qd, nb), _rows_to_batch(kd, nb), v],
                           dcol, s_hg, D_HG, 4, "hg_state")
    u = _batch_to_rows(u, ntok)
    (og,) = _call_whole(
        _hg_sample_post_kernel, [oi, u, po, nw],
        [_whole((rows, D_HG)), _whole((rows, D_HG)), _colblock(rows, D_HG, 3), _whole((1, HG_DV))],
        [jax.ShapeDtypeStruct((rows, D_HG), BF16)], [], "hg_sample_post", _mib(40))
    return og, s_new


def lru_sample(po, s_lru, s_lconv, prm, nb, ntok):
    rows = nb * ntok
    n = (LRU_CONV - 1) * nb
    off = _conv_off(nb, LRU_CONV)
    prev = _to_token_major(s_lconv)
    yl, hfin, tail = _call_whole(
        functools.partial(_lru_sample_kernel, nb=nb, ntok=ntok),
        [po, po, prev, s_lru, prm["lru_cw"], prm["lru_cb"], prm["lru_wr"], prm["lru_br"], prm["lru_wi"],
         prm["lru_bi"], prm["lru_ap"]],
        [_colblock(rows, D_RNN, 4), _colblock(rows, D_RNN, 5), _whole((n, D_RNN)), _whole((nb, D_RNN)),
         _whole((LRU_CONV, D_RNN)), _whole((1, D_RNN)), _whole((LRU_BLOCKS, LRU_BW, LRU_BW)),
         _whole((1, D_RNN)), _whole((LRU_BLOCKS, LRU_BW, LRU_BW)), _whole((1, D_RNN)), _whole((1, D_RNN))],
        [jax.ShapeDtypeStruct((rows, D_RNN), BF16), jax.ShapeDtypeStruct((nb, D_RNN), F32),
         jax.ShapeDtypeStruct((n, D_RNN), F32)],
        [pltpu.VMEM((off + rows, D_RNN), F32)], "lru_sample", _mib(48))
    return yl, hfin, _from_token_major(tail, nb)


def _rope_tables(pos):
    half = RET_DK // 2
    inv = ROPE_BASE ** (-jnp.arange(half, dtype=F32) / half)
    ang = pos.astype(F32)[:, None] * inv[None, :]
    cos, sin = jnp.cos(ang), jnp.sin(ang)
    return jnp.concatenate([cos, cos], axis=1), jnp.concatenate([-sin, sin], axis=1)


def _pad_lanes(v):
    return jnp.pad(v.astype(F32), (0, LANES - v.shape[0])).reshape(1, LANES)


def _prepare(p):
    lbs = jnp.cumsum(jax.nn.softmax(p["hg_lower_bounds"].astype(F32), axis=0), axis=0)
    lbs = lbs - lbs[0]
    return {
        "even_wt": jnp.swapaxes(p["even_w_in"][0], 0, 1),
        "ssd_cw": p["ssd_conv_w"][0], "ssd_cb": p["ssd_conv_b"][0].reshape(1, -1),
        "ssd_dtb": _pad_lanes(p["ssd_dt_bias"][0]), "ssd_alog": _pad_lanes(p["ssd_A_log"][0]),
        "ssd_dx": jnp.repeat(p["ssd_D"][0], SSD_HEAD_DIM).reshape(1, D_SSD),
        "ssd_nw": p["ssd_norm_w"][0].reshape(1, D_SSD),
        "hg_lb": lbs[1].reshape(1, D_HG), "hg_nw": p["hg_norm_w"][0].reshape(1, HG_DV),
        "lru_cw": p["lru_conv_w"][0], "lru_cb": p["lru_conv_b"][0].reshape(1, D_RNN),
        "lru_wr": p["lru_w_r"][0], "lru_br": p["lru_b_r"][0].reshape(1, D_RNN),
        "lru_wi": p["lru_w_i"][0], "lru_bi": p["lru_b_i"][0].reshape(1, D_RNN),
        "lru_ap": p["lru_a_param"][0].reshape(1, D_RNN),
    }


def _even_row_offsets(bn):
    o_xbc, o_dt = D_SSD, 2 * D_SSD + SSD_BC
    o_q = o_dt + SSD_HEADS
    o_k, o_v = o_q + RET_QK, o_q + 2 * RET_QK
    o_g = o_v + D_RET
    segs = [(0, D_SSD), (o_xbc, D_SSD), (o_v, D_RET), (o_g, D_RET), (o_q, RET_QK), (o_k, RET_QK),
            (o_xbc + D_SSD, SSD_BC), (o_dt, bn)]
    offs = [start + i for start, width in segs for i in range(0, width, bn)]
    assert len(offs) * bn == EVEN_PACKED and offs[-1] + bn <= D_IN_EVEN
    return offs


def _even_proj(x, p, prm, bm):
    return mm_in_t(x, p["norm_mix"][0], prm["even_wt"], _even_row_offsets(512), bm, 512)


def _ffn(x, p, prm, l, prev, shift, group_rows, bm, bf=512):
    act, st = ffn_up(x, p["norm_ffn"][l], p["ffn_w_up"], l, p["ffn_conv_w"][l], p["ffn_conv_b"][l],
                     prev, shift, group_rows, bm, bf)
    return mm_out([act], p["ffn_w_down"], l, x, min(bm, 1024), 256), st


def _trunk_prompt(x, p, prm, nseq, seq):
    m = nseq * seq
    bm = min(2048, seq)
    cos, sin = _rope_tables(jnp.arange(seq, dtype=jnp.int32))
    pe = _even_proj(x, p, prm, bm)
    y, ssm, ssm_conv = ssd_prompt(pe, prm, nseq, seq, 128)
    o, ret = ret_prompt(pe, cos, sin, nseq, seq, 128)
    x = mm_out([y, o], p["even_w_out"], 0, x, bm, 256)
    zeros_ffn = jnp.zeros((nseq, FFN_CONV - 1, D_FF), F32)
    x, ffn0 = _ffn(x, p, prm, 0, zeros_ffn, 1, seq, bm, 256)
    po = mm_in(x, p["norm_mix"][1], p["odd_w_in"][0], bm, 512)
    og, hgrn = hg_prompt(po, prm["hg_lb"], prm["hg_nw"], nseq, seq, 128)
    yl, lru, lru_conv = lru_prompt(po, prm, nseq, seq, 256)
    x = mm_out([og, yl], p["odd_w_out"], 0, x, bm, 256)
    x, ffn1 = _ffn(x, p, prm, 1, zeros_ffn, 1, seq, bm, 256)
    y_out = rmsnorm(x, p["norm_final"], min(512, m))
    return (y_out.reshape(nseq, seq, D_MODEL), ssm[None], ssm_conv[None], ret[None], hgrn[None],
            lru.reshape(1, nseq, D_RNN), lru_conv[None], jnp.stack([ffn0, ffn1]))


def _trunk_sample(x, st, p, prm, nb, ntok):
    rows = nb * ntok
    cos, sin = _rope_tables(PAST_LEN + jnp.arange(ntok, dtype=jnp.int32))
    pe = _even_proj(x, p, prm, rows)
    y, ssm, ssm_conv = ssd_sample(pe, pe, st["ssm"][0], st["ssm_conv"][0], prm, nb, ntok)
    o, ret = ret_sample(pe, st["ret"][0], cos, sin, nb, ntok)
    x = mm_out([y, o], p["even_w_out"], 0, x, rows, 512)
    x, ffn0 = _ffn(x, p, prm, 0, _to_token_major(st["ffn_conv"][0])[None], nb, rows, rows)
    po = mm_in(x, p["norm_mix"][1], p["odd_w_in"][0], rows, 512)
    og, hgrn = hg_sample(po, st["hgrn"][0], prm["hg_lb"], prm["hg_nw"], nb, ntok)
    yl, lru, lru_conv = lru_sample(po, st["lru"][0], st["lru_conv"][0], prm, nb, ntok)
    x = mm_out([og, yl], p["odd_w_out"], 0, x, rows, 512)
    x, ffn1 = _ffn(x, p, prm, 1, _to_token_major(st["ffn_conv"][1])[None], nb, rows, rows)
    y_out = rmsnorm(x, p["norm_final"], rows)
    y_out = jnp.transpose(y_out.reshape(ntok, nb, D_MODEL), (1, 0, 2))
    ffn = jnp.stack([_from_token_major(ffn0[0], nb), _from_token_major(ffn1[0], nb)])
    return (y_out, ssm[None], ssm_conv[None], ret[None], hgrn[None], lru[None], lru_conv[None], ffn)


def kernel(x_prompt, x_sample, state_ssm, state_ssm_conv, state_ret, state_hgrn, state_lru, state_lru_conv, state_ffn_conv, norm_mix, norm_ffn, norm_final, even_w_in, ssd_conv_w, ssd_conv_b, ssd_dt_bias, ssd_A_log, ssd_D, ssd_norm_w, even_w_out, odd_w_in, hg_lower_bounds, hg_norm_w, lru_conv_w, lru_conv_b, lru_w_r, lru_b_r, lru_w_i, lru_b_i, lru_a_param, odd_w_out, ffn_w_up, ffn_conv_w, ffn_conv_b, ffn_w_down):
    p = {
        "norm_mix": norm_mix, "norm_ffn": norm_ffn, "norm_final": norm_final,
        "even_w_in": even_w_in, "ssd_conv_w": ssd_conv_w, "ssd_conv_b": ssd_conv_b,
        "ssd_dt_bias": ssd_dt_bias, "ssd_A_log": ssd_A_log, "ssd_D": ssd_D,
        "ssd_norm_w": ssd_norm_w, "even_w_out": even_w_out, "odd_w_in": odd_w_in,
        "hg_lower_bounds": hg_lower_bounds, "hg_norm_w": hg_norm_w,
        "lru_conv_w": lru_conv_w, "lru_conv_b": lru_conv_b, "lru_w_r": lru_w_r,
        "lru_b_r": lru_b_r, "lru_w_i": lru_w_i, "lru_b_i": lru_b_i,
        "lru_a_param": lru_a_param, "odd_w_out": odd_w_out, "ffn_w_up": ffn_w_up,
        "ffn_conv_w": ffn_conv_w, "ffn_conv_b": ffn_conv_b, "ffn_w_down": ffn_w_down,
    }
    prm = _prepare(p)
    nseq, seq, _ = x_prompt.shape
    nb, ntok, _ = x_sample.shape
    st = {"ssm": state_ssm, "ssm_conv": state_ssm_conv, "ret": state_ret, "hgrn": state_hgrn,
          "lru": state_lru, "lru_conv": state_lru_conv, "ffn_conv": state_ffn_conv}
    yp = _trunk_prompt(x_prompt.reshape(nseq * seq, D_MODEL), p, prm, nseq, seq)
    xs_tm = jnp.transpose(x_sample, (1, 0, 2)).reshape(ntok * nb, D_MODEL)
    ys = _trunk_sample(xs_tm, st, p, prm, nb, ntok)
    return (yp[0], ys[0], yp[1], ys[1], yp[2], ys[2], yp[3], ys[3], yp[4], ys[4],
            yp[5], ys[5], yp[6], ys[6], yp[7], ys[7])
```

```python
import functools
import math

import numpy as np
import jax
import jax.numpy as jnp
from jax import lax
from jax.experimental import pallas as pl
from jax.experimental.pallas import tpu as pltpu

F32 = jnp.float32
BF16 = jnp.bfloat16
EPS = 1e-6

D_MODEL = 2048
PAST_LEN = 16384
SSD_HEADS = 32
SSD_HEAD_DIM = 64
D_SSD = SSD_HEADS * SSD_HEAD_DIM
SSD_GROUPS = 4
SSD_STATE = 128
SSD_CONV = 4
SSD_BC = 2 * SSD_GROUPS * SSD_STATE
RET_HEADS = 8
RET_DK = 128
RET_DV = 256
RET_QK = RET_HEADS * RET_DK
D_RET = RET_HEADS * RET_DV
ROPE_BASE = 10000.0
HG_HEADS = 16
HG_DK = 128
HG_DV = 128
D_HG = HG_HEADS * HG_DV
D_RNN = 2048
LRU_BLOCKS = 8
LRU_BW = D_RNN // LRU_BLOCKS
LRU_CONV = 4
LRU_C = 8.0
D_FF = 5632
FFN_CONV = 3
D_IN_EVEN = D_SSD + (D_SSD + SSD_BC) + SSD_HEADS + 2 * RET_QK + 2 * D_RET
EVEN_PACKED = 11776
DT_COL = 11264

V7X_VMEM_BYTES = 64 * 1024 * 1024
V7X_VMEM_CAP = 60 * 1024 * 1024
LANES = 128
SUBLANES = 8

RET_LOG_GAMMA = [float(v) for v in np.log1p(-np.exp(np.linspace(
    math.log(1.0 / 32.0), math.log(1.0 / 512.0), RET_HEADS, dtype=np.float32))).astype(np.float32)]

NT = (((1,), (1,)), ((), ()))
TN = (((0,), (0,)), ((), ()))


def _cparams(sem, vmem_bytes):
    return pltpu.CompilerParams(dimension_semantics=sem,
                                vmem_limit_bytes=int(min(V7X_VMEM_CAP, vmem_bytes)))


def _mib(n):
    return n * 1024 * 1024


def _silu(x):
    return x * jax.nn.sigmoid(x)


def _softplus(x):
    return jnp.maximum(x, 0.0) + jnp.log1p(jnp.exp(-jnp.abs(x)))


def _gelu_tanh(x):
    return 0.5 * x * (1.0 + jnp.tanh(math.sqrt(2.0 / math.pi) * (x + 0.044715 * (x * x * x))))


def _rms(x):
    return x * lax.rsqrt(jnp.mean(x * x, axis=-1, keepdims=True) + EPS)


def _dot(a, b):
    return jnp.dot(a, b, preferred_element_type=F32)


def _tril_ones(n):
    r = lax.broadcasted_iota(jnp.int32, (n, n), 0)
    c = lax.broadcasted_iota(jnp.int32, (n, n), 1)
    return (r >= c).astype(F32)


def _cumsum_rows(x):
    return jnp.dot(_tril_ones(x.shape[0]), x, precision=lax.Precision.HIGHEST,
                   preferred_element_type=F32)


def _causal_conv(buf_ref, x, w_ref, b_ref, shift, width):
    rows = x.shape[0]
    off = buf_ref.shape[0] - rows
    n = (width - 1) * shift
    buf_ref[off:off + rows, :] = x
    acc = None
    for j in range(width):
        start = off - (width - 1 - j) * shift
        term = buf_ref[start:start + rows, :] * w_ref[j:j + 1, :]
        acc = term if acc is None else acc + term
    out = b_ref[...] + acc
    tail = buf_ref[off + rows - n:off + rows, :]
    buf_ref[off - n:off, :] = tail
    return out, tail


def _conv_off(shift, width):
    n = (width - 1) * shift
    return -(-n // SUBLANES) * SUBLANES


NORM_ROWS = 256


def _norm_rows_to(x_ref, nw_ref, xn_ref):
    rows = x_ref.shape[0]
    step = min(NORM_ROWS, rows)

    def body(i, carry):
        r = pl.multiple_of(i * step, step)
        xn_ref[pl.ds(r, step), :] = (_rms(x_ref[pl.ds(r, step), :]) * nw_ref[...]).astype(BF16)
        return carry

    lax.fori_loop(0, rows // step, body, 0)


def _mm_in_kernel(x_ref, nw_ref, w_ref, o_ref, xn_ref):
    @pl.when(pl.program_id(1) == 0)
    def _():
        _norm_rows_to(x_ref, nw_ref, xn_ref)
    o_ref[...] = _dot(xn_ref[...], w_ref[...].astype(BF16))


def _mm_in_vmem(bm, d, bn):
    return (bm * d * 4 + bm * d * 2 + 2 * d * bn * 4 + d * bn * 2 + 2 * bm * bn * 4
            + 4 * NORM_ROWS * d * 4 + _mib(6))


def mm_in(x, nw, w, bm, bn):
    m, d = x.shape
    n = w.shape[1]
    vmem = _mm_in_vmem(bm, d, bn)
    return pl.pallas_call(
        _mm_in_kernel,
        grid=(m // bm, n // bn),
        in_specs=[pl.BlockSpec((bm, d), lambda i, j: (i, 0), pipeline_mode=pl.Buffered(1)),
                  pl.BlockSpec((1, d), lambda i, j: (0, 0)),
                  pl.BlockSpec((d, bn), lambda i, j: (0, j))],
        out_specs=pl.BlockSpec((bm, bn), lambda i, j: (i, j)),
        out_shape=jax.ShapeDtypeStruct((m, n), F32),
        scratch_shapes=[pltpu.VMEM((bm, d), BF16)],
        compiler_params=_cparams(("parallel", "arbitrary"), vmem),
        name="mm_in",
    )(x, nw.reshape(1, d), w)


def _mm_in_t_kernel(offs_ref, x_ref, nw_ref, wt_ref, o_ref, xn_ref):
    del offs_ref
    @pl.when(pl.program_id(1) == 0)
    def _():
        _norm_rows_to(x_ref, nw_ref, xn_ref)
    o_ref[...] = lax.dot_general(xn_ref[...], wt_ref[...].astype(BF16), NT, preferred_element_type=F32)


ROW_ALIGN = 32


def mm_in_t(x, nw, wt, row_offsets, bm, bn):
    m, d = x.shape
    nblk = len(row_offsets)
    assert all(o % ROW_ALIGN == 0 for o in row_offsets)
    vmem = _mm_in_vmem(bm, d, bn)
    grid_spec = pltpu.PrefetchScalarGridSpec(
        num_scalar_prefetch=1,
        grid=(m // bm, nblk),
        in_specs=[pl.BlockSpec((bm, d), lambda i, j, offs: (i, 0), pipeline_mode=pl.Buffered(1)),
                  pl.BlockSpec((1, d), lambda i, j, offs: (0, 0)),
                  pl.BlockSpec((pl.Element(bn), pl.Element(d)),
                               lambda i, j, offs: (offs[j] * ROW_ALIGN, 0))],
        out_specs=pl.BlockSpec((bm, bn), lambda i, j, offs: (i, j)),
        scratch_shapes=[pltpu.VMEM((bm, d), BF16)])
    return pl.pallas_call(
        _mm_in_t_kernel,
        grid_spec=grid_spec,
        out_shape=jax.ShapeDtypeStruct((m, nblk * bn), F32),
        compiler_params=_cparams(("parallel", "arbitrary"), vmem),
        name="mm_in_t",
    )(jnp.asarray([o // ROW_ALIGN for o in row_offsets], jnp.int32), x, nw.reshape(1, d), wt)


def _mm_out_kernel(*refs, nparts):
    a_refs = refs[:nparts]
    w_refs = refs[nparts:2 * nparts]
    r_ref = refs[2 * nparts]
    o_ref = refs[2 * nparts + 1]
    acc = r_ref[...]
    for a_ref, w_ref in zip(a_refs, w_refs):
        acc = acc + _dot(a_ref[...], w_ref[...].astype(BF16))
    o_ref[...] = acc


def mm_out(parts, w, layer, resid, bm, bn):
    nparts = len(parts)
    m, kp = parts[0].shape
    n = w.shape[2]
    vmem = nparts * (2 * bm * kp * 2 + 2 * kp * bn * 4 + kp * bn * 2) + 4 * bm * bn * 4 + 2 * bm * bn * 4 + _mib(6)
    in_specs = [pl.BlockSpec((bm, kp), lambda i, j: (i, 0)) for _ in range(nparts)]
    in_specs += [pl.BlockSpec((None, kp, bn), functools.partial(lambda i, j, p: (layer, p, j), p=p))
                 for p in range(nparts)]
    in_specs += [pl.BlockSpec((bm, bn), lambda i, j: (i, j))]
    return pl.pallas_call(
        functools.partial(_mm_out_kernel, nparts=nparts),
        grid=(m // bm, n // bn),
        in_specs=in_specs,
        out_specs=pl.BlockSpec((bm, bn), lambda i, j: (i, j)),
        out_shape=jax.ShapeDtypeStruct((m, n), F32),
        compiler_params=_cparams(("parallel", "parallel"), vmem),
        name="mm_out",
    )(*parts, *([w] * nparts), resid)


def _rmsnorm_kernel(x_ref, nw_ref, o_ref):
    o_ref[...] = _rms(x_ref[...]) * nw_ref[...]


def rmsnorm(x, nw, bm):
    m, d = x.shape
    return pl.pallas_call(
        _rmsnorm_kernel,
        grid=(m // bm,),
        in_specs=[pl.BlockSpec((bm, d), lambda i: (i, 0)), pl.BlockSpec((1, d), lambda i: (0, 0))],
        out_specs=pl.BlockSpec((bm, d), lambda i: (i, 0)),
        out_shape=jax.ShapeDtypeStruct((m, d), F32),
        compiler_params=_cparams(("parallel",), 6 * bm * d * 4 + _mib(4)),
        name="rmsnorm",
    )(x, nw.reshape(1, d))


FFN_SLAB = 128


def _ffn_up_kernel(x_ref, nw_ref, wg_ref, wu_ref, cw_ref, cb_ref, prev_ref,
                   act_ref, st_ref, xn_ref, gbuf_ref, carry_ref, *, shift, blocks_per_group):
    i = pl.program_id(0)
    j = pl.program_id(1)
    bm = x_ref.shape[0]
    n = (FFN_CONV - 1) * shift
    off = gbuf_ref.shape[0] - bm

    @pl.when(j == 0)
    def _():
        _norm_rows_to(x_ref, nw_ref, xn_ref)

    first = (i % blocks_per_group) == 0

    @pl.when(first)
    def _():
        gbuf_ref[off - n:off, :] = prev_ref[0]

    @pl.when(jnp.logical_not(first))
    def _():
        gbuf_ref[off - n:off, :] = carry_ref[j]

    wg = wg_ref[...].astype(BF16)
    wu = wu_ref[...].astype(BF16)
    slab = min(FFN_SLAB, bm)
    for s in range(bm // slab):
        r0 = s * slab
        xs = xn_ref[r0:r0 + slab, :]
        gbuf_ref[off + r0:off + r0 + slab, :] = _dot(xs, wg)
        u = _dot(xs, wu)
        gc = cb_ref[...]
        for t in range(FFN_CONV):
            start = off + r0 - (FFN_CONV - 1 - t) * shift
            gc = gc + gbuf_ref[start:start + slab, :] * cw_ref[t:t + 1, :]
        act_ref[r0:r0 + slab, :] = (_silu(gc) * u).astype(BF16)
    tail = gbuf_ref[off + bm - n:off + bm, :]
    carry_ref[j] = tail
    st_ref[0] = tail


def ffn_up(x, nw, w_up, layer, cw, cb, prev, shift, group_rows, bm, bf):
    m, d = x.shape
    f = cw.shape[1]
    n = (FFN_CONV - 1) * shift
    off = _conv_off(shift, FFN_CONV)
    bpg = group_rows // bm
    nf = f // bf
    vmem = (bm * d * 4 + bm * d * 2 + 4 * d * bf * 4 + 2 * d * bf * 2 + 2 * bm * bf * 2
            + (off + bm) * bf * 4 + nf * max(n, SUBLANES) * bf * 4 + 4 * n * bf * 4 + 5 * bm * bf * 4
            + 4 * NORM_ROWS * d * 4 + _mib(6))
    act, st = pl.pallas_call(
        functools.partial(_ffn_up_kernel, shift=shift, blocks_per_group=bpg),
        grid=(m // bm, nf),
        in_specs=[pl.BlockSpec((bm, d), lambda i, j: (i, 0), pipeline_mode=pl.Buffered(1)),
                  pl.BlockSpec((1, d), lambda i, j: (0, 0)),
                  pl.BlockSpec((None, d, bf), lambda i, j: (layer, 0, j)),
                  pl.BlockSpec((None, d, bf), lambda i, j: (layer, 0, j + nf)),
                  pl.BlockSpec((FFN_CONV, bf), lambda i, j: (0, j)),
                  pl.BlockSpec((1, bf), lambda i, j: (0, j)),
                  pl.BlockSpec((1, n, bf), lambda i, j: (i // bpg, 0, j))],
        out_specs=[pl.BlockSpec((bm, bf), lambda i, j: (i, j)),
                   pl.BlockSpec((1, n, bf), lambda i, j: (i, 0, j))],
        out_shape=[jax.ShapeDtypeStruct((m, f), BF16),
                   jax.ShapeDtypeStruct((m // bm, n, f), F32)],
        scratch_shapes=[pltpu.VMEM((bm, d), BF16),
                        pltpu.VMEM((off + bm, bf), F32),
                        pltpu.VMEM((nf, n, bf), F32)],
        compiler_params=_cparams(("arbitrary", "arbitrary"), vmem),
        name="ffn_up",
    )(x, nw.reshape(1, d), w_up, w_up, cw, cb.reshape(1, f), prev)
    return act, st[bpg - 1::bpg]


def _expand_heads64(x, nheads):
    r = x.shape[0]
    lo = lax.broadcasted_iota(jnp.int32, (r, LANES), 1) < SSD_HEAD_DIM
    tiles = []
    for p in range(nheads // 2):
        a0 = jnp.broadcast_to(x[:, 2 * p:2 * p + 1], (r, LANES))
        a1 = jnp.broadcast_to(x[:, 2 * p + 1:2 * p + 2], (r, LANES))
        tiles.append(jnp.where(lo, a0, a1))
    return jnp.concatenate(tiles, axis=1)


def _ssd_gates(xs_c, bc_c, dt_raw, dtb, alog):
    xs = _silu(xs_c)
    bc = _silu(bc_c)
    dt = _softplus(dt_raw + dtb)
    la = dt * (-jnp.exp(alog))
    return xs, bc, dt, la


def _ssd_finish(o, xs, z, dx, nw):
    y = (o + dx * xs) * _silu(z)
    gw = D_SSD // SSD_GROUPS
    y = jnp.concatenate([_rms(y[:, g * gw:(g + 1) * gw]) for g in range(SSD_GROUPS)], axis=1)
    return y * nw


def _ssd_prompt_kernel(z_ref, xs_ref, bc_ref, dt_ref, cwx_ref, cwb_ref, cbx_ref, cbb_ref,
                       dtb_ref, alog_ref, dx_ref, nw_ref,
                       y_ref, sfin_ref, cfin_ref, s_ref, bufx_ref, bufb_ref):
    c = pl.program_id(1)
    nc = pl.num_programs(1)
    rows = xs_ref.shape[0]
    gw = D_SSD // SSD_GROUPS
    hpg = SSD_HEADS // SSD_GROUPS

    @pl.when(c == 0)
    def _():
        s_ref[...] = jnp.zeros_like(s_ref)
        bufx_ref[0:SUBLANES, :] = jnp.zeros((SUBLANES, D_SSD), F32)
        bufb_ref[0:SUBLANES, :] = jnp.zeros((SUBLANES, SSD_BC), F32)

    xs_c, tailx = _causal_conv(bufx_ref, xs_ref[...], cwx_ref, cbx_ref, 1, SSD_CONV)
    bc_c, tailb = _causal_conv(bufb_ref, bc_ref[...], cwb_ref, cbb_ref, 1, SSD_CONV)
    xs, bc, dt, la = _ssd_gates(xs_c, bc_c, dt_ref[...], dtb_ref[...], alog_ref[...])
    cum = _cumsum_rows(la)
    cum_t = cum.T
    cumx = _expand_heads64(cum, SSD_HEADS)
    dtx = _expand_heads64(dt, SSD_HEADS)
    lastx = cumx[rows - 1:rows, :]
    ecum = jnp.exp(cumx)
    wx = jnp.exp(lastx - cumx)
    elast = jnp.exp(lastx)
    v_all = xs * dtx
    vw_all = v_all * wx

    ri = lax.broadcasted_iota(jnp.int32, (rows, rows), 0)
    ci = lax.broadcasted_iota(jnp.int32, (rows, rows), 1)
    causal = ri >= ci
    lo = lax.broadcasted_iota(jnp.int32, (rows, LANES), 1) < SSD_HEAD_DIM

    o_groups = []
    for g in range(SSD_GROUPS):
        kb = bc[:, g * SSD_STATE:(g + 1) * SSD_STATE].astype(BF16)
        qb = bc[:, (SSD_GROUPS + g) * SSD_STATE:(SSD_GROUPS + g + 1) * SSD_STATE].astype(BF16)
        qk = lax.dot_general(qb, kb, NT, preferred_element_type=F32)
        o_tiles = []
        for p in range(hpg // 2):
            acc = None
            for q in range(2):
                h = g * hpg + 2 * p + q
                diff = cum[:, h:h + 1] - cum_t[h:h + 1, :]
                dec = jnp.exp(jnp.where(causal, diff, -1e30))
                pm = (qk * dec).astype(BF16)
                col = g * gw + p * LANES
                vp = v_all[:, col:col + LANES]
                vh = jnp.where(lo, vp, 0.0) if q == 0 else jnp.where(lo, 0.0, vp)
                t = _dot(pm, vh.astype(BF16))
                acc = t if acc is None else acc + t
            o_tiles.append(acc)
        o_intra = jnp.concatenate(o_tiles, axis=1)
        sg = s_ref[g]
        sl = slice(g * gw, (g + 1) * gw)
        o_inter = _dot(qb, sg.astype(BF16)) * ecum[:, sl]
        upd = lax.dot_general(kb, vw_all[:, sl].astype(BF16), TN, preferred_element_type=F32)
        s_ref[g] = sg * elast[:, sl] + upd
        o_groups.append(o_intra + o_inter)
    o = jnp.concatenate(o_groups, axis=1)
    y_ref[...] = _ssd_finish(o, xs, z_ref[...], dx_ref[...], nw_ref[...]).astype(BF16)

    @pl.when(c == nc - 1)
    def _():
        for h in range(SSD_HEADS):
            g, hl = divmod(h, hpg)
            sfin_ref[0, h] = s_ref[g, :, hl * SSD_HEAD_DIM:(hl + 1) * SSD_HEAD_DIM]
        cfin_ref[0, :, 0:D_SSD] = tailx
        cfin_ref[0, :, D_SSD:D_SSD + SSD_BC] = tailb


def ssd_prompt(pe, prm, nseq, seq, chunk):
    m = nseq * seq
    nc = seq // chunk
    row = lambda b, c: b * nc + c
    full = lambda shp: pl.BlockSpec(shp, lambda b, c: (0,) * len(shp))
    in_specs = [
        pl.BlockSpec((chunk, D_SSD), lambda b, c: (row(b, c), 0)),
        pl.BlockSpec((chunk, D_SSD), lambda b, c: (row(b, c), 1)),
        pl.BlockSpec((chunk, SSD_BC), lambda b, c: (row(b, c), 10)),
        pl.BlockSpec((chunk, LANES), lambda b, c: (row(b, c), DT_COL // LANES)),
        pl.BlockSpec((SSD_CONV, D_SSD), lambda b, c: (0, 0)),
        pl.BlockSpec((SSD_CONV, SSD_BC), lambda b, c: (0, 2)),
        pl.BlockSpec((1, D_SSD), lambda b, c: (0, 0)),
        pl.BlockSpec((1, SSD_BC), lambda b, c: (0, 2)),
        full((1, LANES)), full((1, LANES)), full((1, D_SSD)), full((1, D_SSD)),
    ]
    vmem = _mib(48)
    return pl.pallas_call(
        _ssd_prompt_kernel,
        grid=(nseq, nc),
        in_specs=in_specs,
        out_specs=[pl.BlockSpec((chunk, D_SSD), lambda b, c: (row(b, c), 0)),
                   pl.BlockSpec((1, SSD_HEADS, SSD_STATE, SSD_HEAD_DIM), lambda b, c: (b, 0, 0, 0)),
                   pl.BlockSpec((1, SSD_CONV - 1, D_SSD + SSD_BC), lambda b, c: (b, 0, 0))],
        out_shape=[jax.ShapeDtypeStruct((m, D_SSD), BF16),
                   jax.ShapeDtypeStruct((nseq, SSD_HEADS, SSD_STATE, SSD_HEAD_DIM), F32),
                   jax.ShapeDtypeStruct((nseq, SSD_CONV - 1, D_SSD + SSD_BC), F32)],
        scratch_shapes=[pltpu.VMEM((SSD_GROUPS, SSD_STATE, D_SSD // SSD_GROUPS), F32),
                        pltpu.VMEM((SUBLANES + chunk, D_SSD), F32),
                        pltpu.VMEM((SUBLANES + chunk, SSD_BC), F32)],
        compiler_params=_cparams(("parallel", "arbitrary"), vmem),
        name="ssd_prompt",
    )(pe, pe, pe, pe, prm["ssd_cw"], prm["ssd_cw"], prm["ssd_cb"], prm["ssd_cb"],
      prm["ssd_dtb"], prm["ssd_alog"], prm["ssd_dx"], prm["ssd_nw"])


def _rotary(x, cos, sin_signed):
    return x * cos + pltpu.roll(x, RET_DK // 2, 1) * sin_signed


def _ret_prompt_kernel(q_ref, k_ref, v_ref, g_ref, cos_ref, sin_ref, o_ref, sfin_ref, s_ref):
    c = pl.program_id(1)
    nc = pl.num_programs(1)
    rows = q_ref.shape[0]

    @pl.when(c == 0)
    def _():
        s_ref[...] = jnp.zeros_like(s_ref)

    cos = cos_ref[...]
    sin = sin_ref[...]
    ri = lax.broadcasted_iota(jnp.int32, (rows, rows), 0)
    ci = lax.broadcasted_iota(jnp.int32, (rows, rows), 1)
    dij = jnp.where(ri >= ci, (ri - ci).astype(F32), 1e30)
    tk = lax.broadcasted_iota(jnp.int32, (rows, RET_DK), 0).astype(F32)
    tv = lax.broadcasted_iota(jnp.int32, (rows, RET_DV), 0).astype(F32)
    for h in range(RET_HEADS):
        lg = RET_LOG_GAMMA[h]
        qr = _rotary(q_ref[:, h * RET_DK:(h + 1) * RET_DK], cos, sin)
        kr = _rotary(k_ref[:, h * RET_DK:(h + 1) * RET_DK], cos, sin) * (RET_DK ** -0.5)
        qb = qr.astype(BF16)
        s = lax.dot_general(qb, kr.astype(BF16), NT, preferred_element_type=F32)
        pm = (s * jnp.exp(lg * dij)).astype(BF16)
        vb = v_ref[:, h * RET_DV:(h + 1) * RET_DV].astype(BF16)
        sh = s_ref[h]
        o = _dot(pm, vb) + _dot(qb, sh.astype(BF16)) * jnp.exp(lg * (tv + 1.0))
        kw = (kr * jnp.exp(lg * ((rows - 1.0) - tk))).astype(BF16)
        s_ref[h] = sh * math.exp(lg * rows) + lax.dot_general(kw, vb, TN, preferred_element_type=F32)
        gh = g_ref[:, h * RET_DV:(h + 1) * RET_DV]
        o_ref[:, h * RET_DV:(h + 1) * RET_DV] = (_rms(o) * _silu(gh)).astype(BF16)

    @pl.when(c == nc - 1)
    def _():
        sfin_ref[0] = s_ref[...]


def ret_prompt(pe, cos, sin, nseq, seq, chunk):
    m = nseq * seq
    nc = seq // chunk
    row = lambda b, c: b * nc + c
    return pl.pallas_call(
        _ret_prompt_kernel,
        grid=(nseq, nc),
        in_specs=[pl.BlockSpec((chunk, RET_QK), lambda b, c: (row(b, c), 8)),
                  pl.BlockSpec((chunk, RET_QK), lambda b, c: (row(b, c), 9)),
                  pl.BlockSpec((chunk, D_RET), lambda b, c: (row(b, c), 2)),
                  pl.BlockSpec((chunk, D_RET), lambda b, c: (row(b, c), 3)),
                  pl.BlockSpec((chunk, RET_DK), lambda b, c: (c, 0)),
                  pl.BlockSpec((chunk, RET_DK), lambda b, c: (c, 0))],
        out_specs=[pl.BlockSpec((chunk, D_RET), lambda b, c: (row(b, c), 0)),
                   pl.BlockSpec((1, RET_HEADS, RET_DK, RET_DV), lambda b, c: (b, 0, 0, 0))],
        out_shape=[jax.ShapeDtypeStruct((m, D_RET), BF16),
                   jax.ShapeDtypeStruct((nseq, RET_HEADS, RET_DK, RET_DV), F32)],
        scratch_shapes=[pltpu.VMEM((RET_HEADS, RET_DK, RET_DV), F32)],
        compiler_params=_cparams(("parallel", "arbitrary"), _mib(40)),
        name="ret_prompt",
    )(pe, pe, pe, pe, cos, sin)


HG_HB = 16


def _hg_gates(hq, hf, lb):
    q = _silu(hq)
    f = lb + (1.0 - lb) * jax.nn.sigmoid(hf)
    return q, f, 1.0 - f, jnp.log(f)


def _hg_tables(rows):
    r = np.arange(rows)[:, None]
    t = np.arange(rows)[None, :]
    sums = [t <= r]
    masks = []
    s = rows // 2
    while s >= 1:
        blk, pos = r // (2 * s), r % (2 * s)
        ref = blk * 2 * s + s - 1
        upper = pos >= s
        sums.append(np.where(upper, (t > ref) & (t <= r), (t > r) & (t <= ref)))
        masks.append((blk == t // (2 * s)) & upper & (t % (2 * s) < s))
        s //= 2
    masks.append(r == t)
    return (np.concatenate(sums, axis=0).astype(np.float32),
            np.stack(masks).astype(np.float32))


def _hg_prompt_kernel(hq_ref, hf_ref, hi_ref, hg_ref, lb_ref, nw_ref, sums_ref, masks_ref,
                      og_ref, sfin_ref, st_ref):
    c = pl.program_id(2)
    nc = pl.num_programs(2)
    rows = hq_ref.shape[0]
    nlev = masks_ref.shape[0] - 1

    @pl.when(c == 0)
    def _():
        st_ref[...] = jnp.zeros_like(st_ref)

    q_all, _, kk_all, lg_all = _hg_gates(hq_ref[...], hf_ref[...], lb_ref[...])
    lg_hi = lg_all.astype(BF16)
    lg_lo = (lg_all - lg_hi.astype(F32)).astype(BF16)
    sums = sums_ref[...]
    dall = _dot(sums, lg_hi) + _dot(sums, lg_lo)
    cum_all = dall[0:rows]
    ecum_all = jnp.exp(cum_all)
    elev = [jnp.exp(dall[(l + 1) * rows:(l + 2) * rows]) for l in range(nlev)]
    for h in range(HG_HB):
        sl = slice(h * HG_DK, (h + 1) * HG_DK)
        q = q_all[:, sl]
        kk = kk_all[:, sl]
        cum = cum_all[:, sl]
        vb = hi_ref[:, sl].astype(BF16)
        last = cum[rows - 1:rows, :]
        st = st_ref[h]
        o = lax.dot_general((q * ecum_all[:, sl]).astype(BF16), st.astype(BF16), NT,
                            preferred_element_type=F32)
        kt = (kk * jnp.exp(last - cum)).astype(BF16)
        st_ref[h] = st * jnp.exp(last) + lax.dot_general(vb, kt, TN, preferred_element_type=F32)
        a = masks_ref[nlev] * lax.dot_general(q.astype(BF16), kk.astype(BF16), NT,
                                              preferred_element_type=F32)
        for l in range(nlev):
            e = elev[l][:, sl]
            a = a + masks_ref[l] * lax.dot_general((q * e).astype(BF16), (kk * e).astype(BF16), NT,
                                                   preferred_element_type=F32)
        o = o + _dot(a.astype(BF16), vb)
        og_ref[:, sl] = (_rms(o) * nw_ref[...] * _silu(hg_ref[:, sl])).astype(BF16)

    @pl.when(c == nc - 1)
    def _():
        for h in range(HG_HB):
            sfin_ref[0, h] = st_ref[h].T


def hg_prompt(po, lb, nw, nseq, seq, chunk):
    m = nseq * seq
    nc = seq // chunk
    nhb = HG_HEADS // HG_HB
    w = HG_HB * HG_DK
    row = lambda b, hb, c: b * nc + c
    sums_np, masks_np = _hg_tables(chunk)
    sums = jnp.asarray(sums_np, BF16)
    masks = jnp.asarray(masks_np, F32)
    return pl.pallas_call(
        _hg_prompt_kernel,
        grid=(nseq, nhb, nc),
        in_specs=[pl.BlockSpec((chunk, w), lambda b, hb, c: (row(b, hb, c), hb)),
                  pl.BlockSpec((chunk, w), lambda b, hb, c: (row(b, hb, c), nhb + hb)),
                  pl.BlockSpec((chunk, w), lambda b, hb, c: (row(b, hb, c), 2 * nhb + hb)),
                  pl.BlockSpec((chunk, w), lambda b, hb, c: (row(b, hb, c), 3 * nhb + hb)),
                  pl.BlockSpec((1, w), lambda b, hb, c: (0, hb)),
                  pl.BlockSpec((1, HG_DV), lambda b, hb, c: (0, 0)),
                  pl.BlockSpec(sums.shape, lambda b, hb, c: (0, 0)),
                  pl.BlockSpec(masks.shape, lambda b, hb, c: (0, 0, 0))],
        out_specs=[pl.BlockSpec((chunk, w), lambda b, hb, c: (row(b, hb, c), hb)),
                   pl.BlockSpec((1, HG_HB, HG_DK, HG_DV), lambda b, hb, c: (b, hb, 0, 0))],
        out_shape=[jax.ShapeDtypeStruct((m, D_HG), BF16),
                   jax.ShapeDtypeStruct((nseq, HG_HEADS, HG_DK, HG_DV), F32)],
        scratch_shapes=[pltpu.VMEM((HG_HB, HG_DV, HG_DK), F32)],
        compiler_params=_cparams(("parallel", "parallel", "arbitrary"), _mib(32)),
        name="hg_prompt",
    )(po, po, po, po, lb, nw, sums, masks)


def _lin_scan(a, b, shift):
    rows = a.shape[0]
    ri = lax.broadcasted_iota(jnp.int32, a.shape, 0)
    d = shift
    while d < rows:
        keep = ri >= d
        a_s = jnp.where(keep, pltpu.roll(a, d, 0), 1.0)
        b_s = jnp.where(keep, pltpu.roll(b, d, 0), 0.0)
        b = a * b_s + b
        a = a * a_s
        d *= 2
    return a, b


def _lru_gates(xr, wr_ref, br, wi_ref, bi, ap):
    xb = xr.astype(BF16)
    r_parts, i_parts = [], []
    for n in range(LRU_BLOCKS):
        xn = xb[:, n * LRU_BW:(n + 1) * LRU_BW]
        r_parts.append(_dot(xn, wr_ref[n].astype(BF16)))
        i_parts.append(_dot(xn, wi_ref[n].astype(BF16)))
    r = jax.nn.sigmoid(jnp.concatenate(r_parts, axis=1) + br)
    gi = jax.nn.sigmoid(jnp.concatenate(i_parts, axis=1) + bi)
    la = -LRU_C * r * _softplus(-ap)
    a = jnp.exp(la)
    th = jnp.tanh(la)
    mult = jnp.sqrt(-2.0 * th / (1.0 - th))
    return a, mult, gi


def _lru_prompt_kernel(rx_ref, rg_ref, cw_ref, cb_ref, wr_ref, br_ref, wi_ref, bi_ref, ap_ref,
                       yl_ref, hfin_ref, cfin_ref, buf_ref, hc_ref):
    c = pl.program_id(1)
    nc = pl.num_programs(1)
    rows = rx_ref.shape[0]

    @pl.when(c == 0)
    def _():
        buf_ref[0:SUBLANES, :] = jnp.zeros((SUBLANES, D_RNN), F32)
        hc_ref[...] = jnp.zeros_like(hc_ref)

    xr, tail = _causal_conv(buf_ref, rx_ref[...], cw_ref, cb_ref, 1, LRU_CONV)
    a, mult, gi = _lru_gates(xr, wr_ref, br_ref[...], wi_ref, bi_ref[...], ap_ref[...])
    ri = lax.broadcasted_iota(jnp.int32, (rows, D_RNN), 0)
    mult = jnp.where(jnp.logical_and(c == 0, ri == 0), 1.0, mult)
    pa, hb = _lin_scan(a, mult * gi * xr, 1)
    hs = pa * hc_ref[0:1, :] + hb
    hc_ref[0:1, :] = hs[rows - 1:rows, :]
    yl_ref[...] = (hs * _gelu_tanh(rg_ref[...])).astype(BF16)

    @pl.when(c == nc - 1)
    def _():
        hfin_ref[0] = hs[rows - 1:rows, :]
        cfin_ref[0] = tail


def lru_prompt(po, prm, nseq, seq, chunk):
    m = nseq * seq
    nc = seq // chunk
    row = lambda b, c: b * nc + c
    full = lambda shp: pl.BlockSpec(shp, lambda b, c: (0,) * len(shp))
    return pl.pallas_call(
        _lru_prompt_kernel,
        grid=(nseq, nc),
        in_specs=[pl.BlockSpec((chunk, D_RNN), lambda b, c: (row(b, c), 4)),
                  pl.BlockSpec((chunk, D_RNN), lambda b, c: (row(b, c), 5)),
                  full((LRU_CONV, D_RNN)), full((1, D_RNN)),
                  full((LRU_BLOCKS, LRU_BW, LRU_BW)), full((1, D_RNN)),
                  full((LRU_BLOCKS, LRU_BW, LRU_BW)), full((1, D_RNN)), full((1, D_RNN))],
        out_specs=[pl.BlockSpec((chunk, D_RNN), lambda b, c: (row(b, c), 0)),
                   pl.BlockSpec((1, 1, D_RNN), lambda b, c: (b, 0, 0)),
                   pl.BlockSpec((1, LRU_CONV - 1, D_RNN), lambda b, c: (b, 0, 0))],
        out_shape=[jax.ShapeDtypeStruct((m, D_RNN), BF16),
                   jax.ShapeDtypeStruct((nseq, 1, D_RNN), F32),
                   jax.ShapeDtypeStruct((nseq, LRU_CONV - 1, D_RNN), F32)],
        scratch_shapes=[pltpu.VMEM((SUBLANES + chunk, D_RNN), F32),
                        pltpu.VMEM((SUBLANES, D_RNN), F32)],
        compiler_params=_cparams(("parallel", "arbitrary"), _mib(48)),
        name="lru_prompt",
    )(po, po, prm["lru_cw"], prm["lru_cb"], prm["lru_wr"], prm["lru_br"], prm["lru_wi"],
      prm["lru_bi"], prm["lru_ap"])


TOKP = SUBLANES


def _ssd_state_kernel(q_ref, k_ref, vw_ref, dec_ref, st_ref, u_ref, snew_ref):
    hpg = SSD_HEADS // SSD_GROUPS
    gw = hpg * SSD_HEAD_DIM

    def body(b, carry):
        q = q_ref[b].astype(BF16)
        k = k_ref[b].astype(BF16)
        vw = vw_ref[b].astype(BF16)
        for g in range(SSD_GROUPS):
            ks = slice(g * SSD_STATE, (g + 1) * SSD_STATE)
            stg = st_ref[b, g * hpg:(g + 1) * hpg].reshape(gw, SSD_STATE)
            u_ref[b, :, g * gw:(g + 1) * gw] = lax.dot_general(
                q[:, ks], stg.astype(BF16), NT, preferred_element_type=F32)
            upd = lax.dot_general(vw[:, g * gw:(g + 1) * gw], k[:, ks], TN, preferred_element_type=F32)
            for hl in range(hpg):
                h = g * hpg + hl
                snew_ref[b, h] = (st_ref[b, h] * dec_ref[b, h:h + 1, :]
                                  + upd[hl * SSD_HEAD_DIM:(hl + 1) * SSD_HEAD_DIM, :])
        return carry

    lax.fori_loop(0, st_ref.shape[0], body, 0)


def _ret_state_kernel(q_ref, k_ref, v_ref, s_ref, u_ref, snew_ref, *, ntok):
    def body(b, carry):
        q = q_ref[b].astype(BF16)
        k = k_ref[b].astype(BF16)
        v = v_ref[b].astype(BF16)
        for h in range(RET_HEADS):
            ks = slice(h * RET_DK, (h + 1) * RET_DK)
            vs = slice(h * RET_DV, (h + 1) * RET_DV)
            s0 = s_ref[b, h]
            u_ref[b, :, vs] = _dot(q[:, ks], s0.astype(BF16))
            upd = lax.dot_general(k[:, ks], v[:, vs], TN, preferred_element_type=F32)
            snew_ref[b, h] = s0 * math.exp(RET_LOG_GAMMA[h] * ntok) + upd
        return carry

    lax.fori_loop(0, s_ref.shape[0], body, 0)


def _hg_state_kernel(q_ref, k_ref, v_ref, dcol_ref, s_ref, u_ref, snew_ref):
    def body(b, carry):
        q = q_ref[b].astype(BF16)
        k = k_ref[b].astype(BF16)
        v = v_ref[b].astype(BF16)
        dc = dcol_ref[b]
        for h in range(HG_HEADS):
            ks = slice(h * HG_DK, (h + 1) * HG_DK)
            s0 = s_ref[b, h]
            u_ref[b, :, ks] = _dot(q[:, ks], s0.astype(BF16))
            upd = lax.dot_general(k[:, ks], v[:, ks], TN, preferred_element_type=F32)
            snew_ref[b, h] = s0 * dc[:, h:h + 1] + upd
        return carry

    lax.fori_loop(0, s_ref.shape[0], body, 0)


def _state_call(kern, rows_in, extra, s, u_cols, bb, name):
    nb = s.shape[0]
    blk = lambda a: pl.BlockSpec((bb,) + a.shape[1:], lambda i: (i,) + (0,) * (a.ndim - 1))
    ins = list(rows_in) + ([extra] if extra is not None else []) + [s]
    sbytes = bb * int(np.prod(s.shape[1:])) * 4
    rbytes = sum(bb * int(np.prod(a.shape[1:])) * 4 for a in ins[:-1]) + bb * TOKP * u_cols * 4
    return pl.pallas_call(
        kern,
        grid=(nb // bb,),
        in_specs=[blk(a) for a in ins],
        out_specs=[pl.BlockSpec((bb, TOKP, u_cols), lambda i: (i, 0, 0)), blk(s)],
        out_shape=[jax.ShapeDtypeStruct((nb, TOKP, u_cols), F32),
                   jax.ShapeDtypeStruct(s.shape, F32)],
        compiler_params=_cparams(("parallel",), 5 * sbytes + 3 * rbytes + _mib(8)),
        name=name,
    )(*ins)


def _tok(x, t, nb):
    return x[t * nb:(t + 1) * nb]


def _head_sums(x, width):
    r, n = x.shape
    tiles = []
    for h in range(n // width):
        s = jnp.sum(x[:, h * width:(h + 1) * width], axis=1, keepdims=True)
        tiles.append(jnp.broadcast_to(s, (r, width)))
    return jnp.concatenate(tiles, axis=1)


def _ssd_sample_pre_kernel(xs_ref, bc_ref, dt_ref, prevx_ref, prevb_ref, cwx_ref, cwb_ref,
                           cbx_ref, cbb_ref, dtb_ref, alog_ref,
                           xs_out, bc_out, vw_out, oi_out, ecum_out, elast_out, tailx_out, tailb_out,
                           bufx_ref, bufb_ref, *, nb, ntok):
    n = (SSD_CONV - 1) * nb
    offx = bufx_ref.shape[0] - nb * ntok
    bufx_ref[offx - n:offx, :] = prevx_ref[...]
    bufb_ref[offx - n:offx, :] = prevb_ref[...]
    xs_c, tailx = _causal_conv(bufx_ref, xs_ref[...], cwx_ref, cbx_ref, nb, SSD_CONV)
    bc_c, tailb = _causal_conv(bufb_ref, bc_ref[...], cwb_ref, cbb_ref, nb, SSD_CONV)
    xs, bc, dt, la = _ssd_gates(xs_c, bc_c, dt_ref[...], dtb_ref[...], alog_ref[...])
    tailx_out[...] = tailx
    tailb_out[...] = tailb
    xs_out[...] = xs
    bc_out[...] = bc
    cums = []
    for t in range(ntok):
        lt = _tok(la, t, nb)
        cums.append(lt if t == 0 else cums[-1] + lt)
    cumx = [_expand_heads64(cm, SSD_HEADS) for cm in cums]
    v = xs * _expand_heads64(dt, SSD_HEADS)
    kw = SSD_GROUPS * SSD_STATE
    for t in range(ntok):
        ct = _tok(bc, t, nb)[:, kw:2 * kw]
        acc = None
        for t2 in range(t + 1):
            bt = _tok(bc, t2, nb)[:, 0:kw]
            sc = _head_sums(ct * bt, SSD_STATE)
            scx = jnp.concatenate(
                [jnp.concatenate([sc[:, g * SSD_STATE:(g + 1) * SSD_STATE]] * 4, axis=1)
                 for g in range(SSD_GROUPS)], axis=1)
            term = scx * jnp.exp(cumx[t] - cumx[t2]) * _tok(v, t2, nb)
            acc = term if acc is None else acc + term
        oi_out[t * nb:(t + 1) * nb, :] = acc
        ecum_out[t * nb:(t + 1) * nb, :] = jnp.exp(cumx[t])
        vw_out[t * nb:(t + 1) * nb, :] = _tok(v, t, nb) * jnp.exp(cumx[ntok - 1] - cumx[t])
    elast_out[...] = jnp.exp(cums[ntok - 1])


def _ssd_sample_post_kernel(oi_ref, u_ref, ecum_ref, xs_ref, z_ref, dx_ref, nw_ref, y_ref):
    o = oi_ref[...] + ecum_ref[...] * u_ref[...]
    y_ref[...] = _ssd_finish(o, xs_ref[...], z_ref[...], dx_ref[...], nw_ref[...]).astype(BF16)


def _ret_sample_pre_kernel(q_ref, k_ref, v_ref, cos_ref, sin_ref, qd_out, kd_out, oi_out, *, nb, ntok):
    rows = nb * ntok
    cos = jnp.concatenate([jnp.broadcast_to(cos_ref[t:t + 1, :], (nb, RET_DK)) for t in range(ntok)], axis=0)
    sin = jnp.concatenate([jnp.broadcast_to(sin_ref[t:t + 1, :], (nb, RET_DK)) for t in range(ntok)], axis=0)
    qr, kr = [], []
    for h in range(RET_HEADS):
        sl = slice(h * RET_DK, (h + 1) * RET_DK)
        qr.append(_rotary(q_ref[:, sl], cos, sin))
        kr.append(_rotary(k_ref[:, sl], cos, sin) * (RET_DK ** -0.5))
    qr = jnp.concatenate(qr, axis=1)
    kr = jnp.concatenate(kr, axis=1)
    v = v_ref[...]
    for t in range(ntok):
        qt = _tok(qr, t, nb)
        acc = None
        for t2 in range(t + 1):
            sc = _head_sums(qt * _tok(kr, t2, nb), RET_DK)
            vt = _tok(v, t2, nb)
            tiles = []
            for h in range(RET_HEADS):
                dec = math.exp(RET_LOG_GAMMA[h] * (t - t2))
                s = sc[:, h * RET_DK:(h + 1) * RET_DK] * dec
                tiles.append(jnp.concatenate([s, s], axis=1) * vt[:, h * RET_DV:(h + 1) * RET_DV])
            term = jnp.concatenate(tiles, axis=1)
            acc = term if acc is None else acc + term
        oi_out[t * nb:(t + 1) * nb, :] = acc
        qd = jnp.concatenate([qt[:, h * RET_DK:(h + 1) * RET_DK] * math.exp(RET_LOG_GAMMA[h] * (t + 1))
                              for h in range(RET_HEADS)], axis=1)
        kt = _tok(kr, t, nb)
        kd = jnp.concatenate([kt[:, h * RET_DK:(h + 1) * RET_DK] * math.exp(RET_LOG_GAMMA[h] * (ntok - 1 - t))
                              for h in range(RET_HEADS)], axis=1)
        qd_out[t * nb:(t + 1) * nb, :] = qd
        kd_out[t * nb:(t + 1) * nb, :] = kd


def _ret_sample_post_kernel(oi_ref, u_ref, g_ref, o_ref):
    o = oi_ref[...] + u_ref[...]
    g = g_ref[...]
    for h in range(RET_HEADS):
        sl = slice(h * RET_DV, (h + 1) * RET_DV)
        o_ref[:, sl] = (_rms(o[:, sl]) * _silu(g[:, sl])).astype(BF16)


def _hg_sample_pre_kernel(hq_ref, hf_ref, hi_ref, lb_ref, qd_out, kd_out, dl_out, oi_out, *, nb, ntok):
    q, _, kk, lg = _hg_gates(hq_ref[...], hf_ref[...], lb_ref[...])
    v = hi_ref[...]
    cums = []
    for t in range(ntok):
        lt = _tok(lg, t, nb)
        cums.append(lt if t == 0 else cums[-1] + lt)
    for t in range(ntok):
        qt = _tok(q, t, nb)
        acc = None
        for t2 in range(t + 1):
            w = qt * _tok(kk, t2, nb)
            if t2 < t:
                w = w * jnp.exp(cums[t] - cums[t2])
            term = _head_sums(w, HG_DK) * _tok(v, t2, nb)
            acc = term if acc is None else acc + term
        oi_out[t * nb:(t + 1) * nb, :] = acc
        qd_out[t * nb:(t + 1) * nb, :] = qt * jnp.exp(cums[t])
        kd_out[t * nb:(t + 1) * nb, :] = _tok(kk, t, nb) * jnp.exp(cums[ntok - 1] - cums[t])
    dl_out[...] = jnp.exp(cums[ntok - 1])


def _hg_sample_post_kernel(oi_ref, u_ref, hg_ref, nw_ref, og_ref):
    o = oi_ref[...] + u_ref[...]
    hg = hg_ref[...]
    for h in range(HG_HEADS):
        sl = slice(h * HG_DV, (h + 1) * HG_DV)
        og_ref[:, sl] = (_rms(o[:, sl]) * nw_ref[...] * _silu(hg[:, sl])).astype(BF16)


def _lru_sample_kernel(rx_ref, rg_ref, prev_ref, h0_ref, cw_ref, cb_ref, wr_ref, br_ref, wi_ref,
                       bi_ref, ap_ref, yl_out, hfin_out, tail_out, buf_ref, *, nb, ntok):
    rows = nb * ntok
    n = (LRU_CONV - 1) * nb
    off = buf_ref.shape[0] - rows
    buf_ref[off - n:off, :] = prev_ref[...]
    xr, tail = _causal_conv(buf_ref, rx_ref[...], cw_ref, cb_ref, nb, LRU_CONV)
    tail_out[...] = tail
    a, mult, gi = _lru_gates(xr, wr_ref, br_ref[...], wi_ref, bi_ref[...], ap_ref[...])
    b = mult * gi * xr
    h = h0_ref[...]
    for t in range(ntok):
        h = _tok(a, t, nb) * h + _tok(b, t, nb)
        yl_out[t * nb:(t + 1) * nb, :] = (h * _gelu_tanh(rg_ref[t * nb:(t + 1) * nb, :])).astype(BF16)
    hfin_out[...] = h


def _whole(shape):
    return pl.BlockSpec(shape, lambda i: (0,) * len(shape))


def _colblock(rows, width, idx):
    return pl.BlockSpec((rows, width), functools.partial(lambda i, k: (0, k), k=idx))


def _call_whole(kern, in_arrays, in_specs, out_shapes, scratch, name, vmem):
    return pl.pallas_call(
        kern, grid=(1,), in_specs=in_specs,
        out_specs=[_whole(s.shape) for s in out_shapes],
        out_shape=out_shapes, scratch_shapes=scratch,
        compiler_params=_cparams(("arbitrary",), vmem), name=name,
    )(*in_arrays)


def _to_token_major(s):
    nb, w, c = s.shape
    return jnp.transpose(s, (1, 0, 2)).reshape(w * nb, c)


def _from_token_major(x, nb):
    w = x.shape[0] // nb
    return jnp.transpose(x.reshape(w, nb, x.shape[1]), (1, 0, 2))


def _rows_to_batch(x, nb):
    ntok = x.shape[0] // nb
    x = jnp.transpose(x.reshape(ntok, nb, x.shape[1]), (1, 0, 2))
    return jnp.pad(x, ((0, 0), (0, TOKP - ntok), (0, 0)))


def _batch_to_rows(u, ntok):
    nb = u.shape[0]
    return jnp.transpose(u[:, :ntok], (1, 0, 2)).reshape(ntok * nb, u.shape[2])


def ssd_sample(pe, z, s_ssm, s_conv, prm, nb, ntok):
    rows = nb * ntok
    n = (SSD_CONV - 1) * nb
    prev = _to_token_major(s_conv)
    off = _conv_off(nb, SSD_CONV)
    f = lambda *shape: jax.ShapeDtypeStruct(shape, F32)
    outs = _call_whole(
        functools.partial(_ssd_sample_pre_kernel, nb=nb, ntok=ntok),
        [pe, pe, pe, prev, prev, prm["ssd_cw"], prm["ssd_cw"], prm["ssd_cb"], prm["ssd_cb"],
         prm["ssd_dtb"], prm["ssd_alog"]],
        [_colblock(rows, D_SSD, 1), _colblock(rows, SSD_BC, 10), _colblock(rows, LANES, DT_COL // LANES),
         _colblock(n, D_SSD, 0), _colblock(n, SSD_BC, 2),
         _colblock(SSD_CONV, D_SSD, 0), _colblock(SSD_CONV, SSD_BC, 2),
         _colblock(1, D_SSD, 0), _colblock(1, SSD_BC, 2), _whole((1, LANES)), _whole((1, LANES))],
        [f(rows, D_SSD), f(rows, SSD_BC), f(rows, D_SSD), f(rows, D_SSD), f(rows, D_SSD),
         f(nb, LANES), f(n, D_SSD), f(n, SSD_BC)],
        [pltpu.VMEM((off + rows, D_SSD), F32), pltpu.VMEM((off + rows, SSD_BC), F32)],
        "ssd_sample_pre", _mib(56))
    xs, bc, vw, oi, ecum, elast, tailx, tailb = outs
    kw = SSD_GROUPS * SSD_STATE
    dec = jnp.broadcast_to(elast[:, :SSD_HEADS, None], (nb, SSD_HEADS, SSD_STATE))
    st = jnp.swapaxes(s_ssm, -1, -2)
    u, st_new = _state_call(_ssd_state_kernel,
                            [_rows_to_batch(bc[:, kw:], nb), _rows_to_batch(bc[:, :kw], nb),
                             _rows_to_batch(vw, nb)], dec, st, D_SSD, 4, "ssd_state")
    s_new = jnp.swapaxes(st_new, -1, -2)
    u = _batch_to_rows(u, ntok)
    (y,) = _call_whole(
        _ssd_sample_post_kernel, [oi, u, ecum, xs, z, prm["ssd_dx"], prm["ssd_nw"]],
        [_whole((rows, D_SSD))] * 4 + [_colblock(rows, D_SSD, 0), _whole((1, D_SSD)), _whole((1, D_SSD))],
        [jax.ShapeDtypeStruct((rows, D_SSD), BF16)], [], "ssd_sample_post", _mib(48))
    conv_new = _from_token_major(jnp.concatenate([tailx, tailb], axis=1), nb)
    return y, s_new, conv_new


def ret_sample(pe, s_ret, cos, sin, nb, ntok):
    rows = nb * ntok
    f = lambda *shape: jax.ShapeDtypeStruct(shape, F32)
    qd, kd, oi = _call_whole(
        functools.partial(_ret_sample_pre_kernel, nb=nb, ntok=ntok),
        [pe, pe, pe, cos, sin],
        [_colblock(rows, RET_QK, 8), _colblock(rows, RET_QK, 9), _colblock(rows, D_RET, 2),
         _whole(cos.shape), _whole(sin.shape)],
        [f(rows, RET_QK), f(rows, RET_QK), f(rows, D_RET)], [], "ret_sample_pre", _mib(48))
    v = _rows_to_batch(pe[:, 2 * D_RET:3 * D_RET], nb)
    u, s_new = _state_call(functools.partial(_ret_state_kernel, ntok=ntok),
                           [_rows_to_batch(qd, nb), _rows_to_batch(kd, nb), v], None, s_ret,
                           D_RET, 4, "ret_state")
    u = _batch_to_rows(u, ntok)
    (o,) = _call_whole(
        _ret_sample_post_kernel, [oi, u, pe],
        [_whole((rows, D_RET)), _whole((rows, D_RET)), _colblock(rows, D_RET, 3)],
        [jax.ShapeDtypeStruct((rows, D_RET), BF16)], [], "ret_sample_post", _mib(40))
    return o, s_new


def hg_sample(po, s_hg, lb, nw, nb, ntok):
    rows = nb * ntok
    f = lambda *shape: jax.ShapeDtypeStruct(shape, F32)
    qd, kd, dl, oi = _call_whole(
        functools.partial(_hg_sample_pre_kernel, nb=nb, ntok=ntok),
        [po, po, po, lb],
        [_colblock(rows, D_HG, 0), _colblock(rows, D_HG, 1), _colblock(rows, D_HG, 2), _whole((1, D_HG))],
        [f(rows, D_HG), f(rows, D_HG), f(nb, D_HG), f(rows, D_HG)], [], "hg_sample_pre", _mib(48))
    dcol = jnp.pad(jnp.transpose(dl.reshape(nb, HG_HEADS, HG_DK), (0, 2, 1)),
                   ((0, 0), (0, 0), (0, LANES - HG_HEADS)))
    v = _rows_to_batch(po[:, 2 * D_HG:3 * D_HG], nb)
    u, s_new = _state_call(_hg_state_kernel, [_rows_to_batch(qd, nb), _rows_to_batch(kd, nb), v],
                           dcol, s_hg, D_HG, 4, "hg_state")
    u = _batch_to_rows(u, ntok)
    (og,) = _call_whole(
        _hg_sample_post_kernel, [oi, u, po, nw],
        [_whole((rows, D_HG)), _whole((rows, D_HG)), _colblock(rows, D_HG, 3), _whole((1, HG_DV))],
        [jax.ShapeDtypeStruct((rows, D_HG), BF16)], [], "hg_sample_post", _mib(40))
    return og, s_new


def lru_sample(po, s_lru, s_lconv, prm, nb, ntok):
    rows = nb * ntok
    n = (LRU_CONV - 1) * nb
    off = _conv_off(nb, LRU_CONV)
    prev = _to_token_major(s_lconv)
    yl, hfin, tail = _call_whole(
        functools.partial(_lru_sample_kernel, nb=nb, ntok=ntok),
        [po, po, prev, s_lru, prm["lru_cw"], prm["lru_cb"], prm["lru_wr"], prm["lru_br"], prm["lru_wi"],
         prm["lru_bi"], prm["lru_ap"]],
        [_colblock(rows, D_RNN, 4), _colblock(rows, D_RNN, 5), _whole((n, D_RNN)), _whole((nb, D_RNN)),
         _whole((LRU_CONV, D_RNN)), _whole((1, D_RNN)), _whole((LRU_BLOCKS, LRU_BW, LRU_BW)),
         _whole((1, D_RNN)), _whole((LRU_BLOCKS, LRU_BW, LRU_BW)), _whole((1, D_RNN)), _whole((1, D_RNN))],
        [jax.ShapeDtypeStruct((rows, D_RNN), BF16), jax.ShapeDtypeStruct((nb, D_RNN), F32),
         jax.ShapeDtypeStruct((n, D_RNN), F32)],
        [pltpu.VMEM((off + rows, D_RNN), F32)], "lru_sample", _mib(48))
    return yl, hfin, _from_token_major(tail, nb)


def _rope_tables(pos):
    half = RET_DK // 2
    inv = ROPE_BASE ** (-jnp.arange(half, dtype=F32) / half)
    ang = pos.astype(F32)[:, None] * inv[None, :]
    cos, sin = jnp.cos(ang), jnp.sin(ang)
    return jnp.concatenate([cos, cos], axis=1), jnp.concatenate([-sin, sin], axis=1)


def _pad_lanes(v):
    return jnp.pad(v.astype(F32), (0, LANES - v.shape[0])).reshape(1, LANES)


def _prepare(p):
    lbs = jnp.cumsum(jax.nn.softmax(p["hg_lower_bounds"].astype(F32), axis=0), axis=0)
    lbs = lbs - lbs[0]
    return {
        "even_wt": jnp.swapaxes(p["even_w_in"][0], 0, 1),
        "ssd_cw": p["ssd_conv_w"][0], "ssd_cb": p["ssd_conv_b"][0].reshape(1, -1),
        "ssd_dtb": _pad_lanes(p["ssd_dt_bias"][0]), "ssd_alog": _pad_lanes(p["ssd_A_log"][0]),
        "ssd_dx": jnp.repeat(p["ssd_D"][0], SSD_HEAD_DIM).reshape(1, D_SSD),
        "ssd_nw": p["ssd_norm_w"][0].reshape(1, D_SSD),
        "hg_lb": lbs[1].reshape(1, D_HG), "hg_nw": p["hg_norm_w"][0].reshape(1, HG_DV),
        "lru_cw": p["lru_conv_w"][0], "lru_cb": p["lru_conv_b"][0].reshape(1, D_RNN),
        "lru_wr": p["lru_w_r"][0], "lru_br": p["lru_b_r"][0].reshape(1, D_RNN),
        "lru_wi": p["lru_w_i"][0], "lru_bi": p["lru_b_i"][0].reshape(1, D_RNN),
        "lru_ap": p["lru_a_param"][0].reshape(1, D_RNN),
    }


def _even_row_offsets(bn):
    o_xbc, o_dt = D_SSD, 2 * D_SSD + SSD_BC
    o_q = o_dt + SSD_HEADS
    o_k, o_v = o_q + RET_QK, o_q + 2 * RET_QK
    o_g = o_v + D_RET
    segs = [(0, D_SSD), (o_xbc, D_SSD), (o_v, D_RET), (o_g, D_RET), (o_q, RET_QK), (o_k, RET_QK),
            (o_xbc + D_SSD, SSD_BC), (o_dt, bn)]
    offs = [start + i for start, width in segs for i in range(0, width, bn)]
    assert len(offs) * bn == EVEN_PACKED and offs[-1] + bn <= D_IN_EVEN
    return offs


def _even_proj(x, p, prm, bm):
    return mm_in_t(x, p["norm_mix"][0], prm["even_wt"], _even_row_offsets(512), bm, 512)


def _ffn(x, p, prm, l, prev, shift, group_rows, bm, bf=512):
    act, st = ffn_up(x, p["norm_ffn"][l], p["ffn_w_up"], l, p["ffn_conv_w"][l], p["ffn_conv_b"][l],
                     prev, shift, group_rows, bm, bf)
    return mm_out([act], p["ffn_w_down"], l, x, min(bm, 1024), 256), st


def _trunk_prompt(x, p, prm, nseq, seq):
    m = nseq * seq
    bm = min(2048, seq)
    cos, sin = _rope_tables(jnp.arange(seq, dtype=jnp.int32))
    pe = _even_proj(x, p, prm, bm)
    y, ssm, ssm_conv = ssd_prompt(pe, prm, nseq, seq, 128)
    o, ret = ret_prompt(pe, cos, sin, nseq, seq, 128)
    x = mm_out([y, o], p["even_w_out"], 0, x, min(bm, 1024), 512)
    zeros_ffn = jnp.zeros((nseq, FFN_CONV - 1, D_FF), F32)
    x, ffn0 = _ffn(x, p, prm, 0, zeros_ffn, 1, seq, bm, 256)
    po = mm_in(x, p["norm_mix"][1], p["odd_w_in"][0], bm, 512)
    og, hgrn = hg_prompt(po, prm["hg_lb"], prm["hg_nw"], nseq, seq, 128)
    yl, lru, lru_conv = lru_prompt(po, prm, nseq, seq, 256)
    x = mm_out([og, yl], p["odd_w_out"], 0, x, min(bm, 1024), 512)
    x, ffn1 = _ffn(x, p, prm, 1, zeros_ffn, 1, seq, bm, 256)
    y_out = rmsnorm(x, p["norm_final"], min(512, m))
    return (y_out.reshape(nseq, seq, D_MODEL), ssm[None], ssm_conv[None], ret[None], hgrn[None],
            lru.reshape(1, nseq, D_RNN), lru_conv[None], jnp.stack([ffn0, ffn1]))


def _trunk_sample(x, st, p, prm, nb, ntok):
    rows = nb * ntok
    cos, sin = _rope_tables(PAST_LEN + jnp.arange(ntok, dtype=jnp.int32))
    pe = _even_proj(x, p, prm, rows)
    y, ssm, ssm_conv = ssd_sample(pe, pe, st["ssm"][0], st["ssm_conv"][0], prm, nb, ntok)
    o, ret = ret_sample(pe, st["ret"][0], cos, sin, nb, ntok)
    x = mm_out([y, o], p["even_w_out"], 0, x, rows, 512)
    x, ffn0 = _ffn(x, p, prm, 0, _to_token_major(st["ffn_conv"][0])[None], nb, rows, rows)
    po = mm_in(x, p["norm_mix"][1], p["odd_w_in"][0], rows, 512)
    og, hgrn = hg_sample(po, st["hgrn"][0], prm["hg_lb"], prm["hg_nw"], nb, ntok)
    yl, lru, lru_conv = lru_sample(po, st["lru"][0], st["lru_conv"][0], prm, nb, ntok)
    x = mm_out([og, yl], p["odd_w_out"], 0, x, rows, 512)
    x, ffn1 = _ffn(x, p, prm, 1, _to_token_major(st["ffn_conv"][1])[None], nb, rows, rows)
    y_out = rmsnorm(x, p["norm_final"], rows)
    y_out = jnp.transpose(y_out.reshape(ntok, nb, D_MODEL), (1, 0, 2))
    ffn = jnp.stack([_from_token_major(ffn0[0], nb), _from_token_major(ffn1[0], nb)])
    return (y_out, ssm[None], ssm_conv[None], ret[None], hgrn[None], lru[None], lru_conv[None], ffn)


def kernel(x_prompt, x_sample, state_ssm, state_ssm_conv, state_ret, state_hgrn, state_lru, state_lru_conv, state_ffn_conv, norm_mix, norm_ffn, norm_final, even_w_in, ssd_conv_w, ssd_conv_b, ssd_dt_bias, ssd_A_log, ssd_D, ssd_norm_w, even_w_out, odd_w_in, hg_lower_bounds, hg_norm_w, lru_conv_w, lru_conv_b, lru_w_r, lru_b_r, lru_w_i, lru_b_i, lru_a_param, odd_w_out, ffn_w_up, ffn_conv_w, ffn_conv_b, ffn_w_down):
    p = {
        "norm_mix": norm_mix, "norm_ffn": norm_ffn, "norm_final": norm_final,
        "even_w_in": even_w_in, "ssd_conv_w": ssd_conv_w, "ssd_conv_b": ssd_conv_b,
        "ssd_dt_bias": ssd_dt_bias, "ssd_A_log": ssd_A_log, "ssd_D": ssd_D,
        "ssd_norm_w": ssd_norm_w, "even_w_out": even_w_out, "odd_w_in": odd_w_in,
        "hg_lower_bounds": hg_lower_bounds, "hg_norm_w": hg_norm_w,
        "lru_conv_w": lru_conv_w, "lru_conv_b": lru_conv_b, "lru_w_r": lru_w_r,
        "lru_b_r": lru_b_r, "lru_w_i": lru_w_i, "lru_b_i": lru_b_i,
        "lru_a_param": lru_a_param, "odd_w_out": odd_w_out, "ffn_w_up": ffn_w_up,
        "ffn_conv_w": ffn_conv_w, "ffn_conv_b": ffn_conv_b, "ffn_w_down": ffn_w_down,
    }
    prm = _prepare(p)
    nseq, seq, _ = x_prompt.shape
    nb, ntok, _ = x_sample.shape
    st = {"ssm": state_ssm, "ssm_conv": state_ssm_conv, "ret": state_ret, "hgrn": state_hgrn,
          "lru": state_lru, "lru_conv": state_lru_conv, "ffn_conv": state_ffn_conv}
    yp = _trunk_prompt(x_prompt.reshape(nseq * seq, D_MODEL), p, prm, nseq, seq)
    xs_tm = jnp.transpose(x_sample, (1, 0, 2)).reshape(ntok * nb, D_MODEL)
    ys = _trunk_sample(xs_tm, st, p, prm, nb, ntok)
    return (yp[0], ys[0], yp[1], ys[1], yp[2], ys[2], yp[3], ys[3], yp[4], ys[4],
            yp[5], ys[5], yp[6], ys[6], yp[7], ys[7])
```

```python
import functools
import math

import numpy as np
import jax
import jax.numpy as jnp
from jax import lax
from jax.experimental import pallas as pl
from jax.experimental.pallas import tpu as pltpu

F32 = jnp.float32
BF16 = jnp.bfloat16
EPS = 1e-6

D_MODEL = 2048
PAST_LEN = 16384
SSD_HEADS = 32
SSD_HEAD_DIM = 64
D_SSD = SSD_HEADS * SSD_HEAD_DIM
SSD_GROUPS = 4
SSD_STATE = 128
SSD_CONV = 4
SSD_BC = 2 * SSD_GROUPS * SSD_STATE
RET_HEADS = 8
RET_DK = 128
RET_DV = 256
RET_QK = RET_HEADS * RET_DK
D_RET = RET_HEADS * RET_DV
ROPE_BASE = 10000.0
HG_HEADS = 16
HG_DK = 128
HG_DV = 128
D_HG = HG_HEADS * HG_DV
D_RNN = 2048
LRU_BLOCKS = 8
LRU_BW = D_RNN // LRU_BLOCKS
LRU_CONV = 4
LRU_C = 8.0
D_FF = 5632
FFN_CONV = 3
D_IN_EVEN = D_SSD + (D_SSD + SSD_BC) + SSD_HEADS + 2 * RET_QK + 2 * D_RET
EVEN_PACKED = 11776
DT_COL = 11264

V7X_VMEM_BYTES = 64 * 1024 * 1024
V7X_VMEM_CAP = 60 * 1024 * 1024
LANES = 128
SUBLANES = 8

RET_LOG_GAMMA = [float(v) for v in np.log1p(-np.exp(np.linspace(
    math.log(1.0 / 32.0), math.log(1.0 / 512.0), RET_HEADS, dtype=np.float32))).astype(np.float32)]

NT = (((1,), (1,)), ((), ()))
TN = (((0,), (0,)), ((), ()))


def _cparams(sem, vmem_bytes):
    return pltpu.CompilerParams(dimension_semantics=sem,
                                vmem_limit_bytes=int(min(V7X_VMEM_CAP, vmem_bytes)))


def _mib(n):
    return n * 1024 * 1024


def _silu(x):
    return x * jax.nn.sigmoid(x)


def _softplus(x):
    return jnp.maximum(x, 0.0) + jnp.log1p(jnp.exp(-jnp.abs(x)))


def _gelu_tanh(x):
    return 0.5 * x * (1.0 + jnp.tanh(math.sqrt(2.0 / math.pi) * (x + 0.044715 * (x * x * x))))


def _rms(x):
    return x * lax.rsqrt(jnp.mean(x * x, axis=-1, keepdims=True) + EPS)


def _dot(a, b):
    return jnp.dot(a, b, preferred_element_type=F32)


def _tril_ones(n):
    r = lax.broadcasted_iota(jnp.int32, (n, n), 0)
    c = lax.broadcasted_iota(jnp.int32, (n, n), 1)
    return (r >= c).astype(F32)


def _cumsum_rows(x):
    return jnp.dot(_tril_ones(x.shape[0]), x, precision=lax.Precision.HIGHEST,
                   preferred_element_type=F32)


def _causal_conv(buf_ref, x, w_ref, b_ref, shift, width):
    rows = x.shape[0]
    off = buf_ref.shape[0] - rows
    n = (width - 1) * shift
    buf_ref[off:off + rows, :] = x
    acc = None
    for j in range(width):
        start = off - (width - 1 - j) * shift
        term = buf_ref[start:start + rows, :] * w_ref[j:j + 1, :]
        acc = term if acc is None else acc + term
    out = b_ref[...] + acc
    tail = buf_ref[off + rows - n:off + rows, :]
    buf_ref[off - n:off, :] = tail
    return out, tail


def _conv_off(shift, width):
    n = (width - 1) * shift
    return -(-n // SUBLANES) * SUBLANES


NORM_ROWS = 256


def _norm_rows_to(x_ref, nw_ref, xn_ref):
    rows = x_ref.shape[0]
    step = min(NORM_ROWS, rows)

    def body(i, carry):
        r = pl.multiple_of(i * step, step)
        xn_ref[pl.ds(r, step), :] = (_rms(x_ref[pl.ds(r, step), :]) * nw_ref[...]).astype(BF16)
        return carry

    lax.fori_loop(0, rows // step, body, 0)


def _mm_in_kernel(x_ref, nw_ref, w_ref, o_ref, xn_ref):
    @pl.when(pl.program_id(1) == 0)
    def _():
        _norm_rows_to(x_ref, nw_ref, xn_ref)
    o_ref[...] = _dot(xn_ref[...], w_ref[...].astype(BF16))


def _mm_in_vmem(bm, d, bn):
    return (bm * d * 4 + bm * d * 2 + 2 * d * bn * 4 + d * bn * 2 + 2 * bm * bn * 4
            + 4 * NORM_ROWS * d * 4 + _mib(6))


def mm_in(x, nw, w, bm, bn):
    m, d = x.shape
    n = w.shape[1]
    vmem = _mm_in_vmem(bm, d, bn)
    return pl.pallas_call(
        _mm_in_kernel,
        grid=(m // bm, n // bn),
        in_specs=[pl.BlockSpec((bm, d), lambda i, j: (i, 0), pipeline_mode=pl.Buffered(1)),
                  pl.BlockSpec((1, d), lambda i, j: (0, 0)),
                  pl.BlockSpec((d, bn), lambda i, j: (0, j))],
        out_specs=pl.BlockSpec((bm, bn), lambda i, j: (i, j)),
        out_shape=jax.ShapeDtypeStruct((m, n), F32),
        scratch_shapes=[pltpu.VMEM((bm, d), BF16)],
        compiler_params=_cparams(("parallel", "arbitrary"), vmem),
        name="mm_in",
    )(x, nw.reshape(1, d), w)


def _mm_in_t_kernel(offs_ref, x_ref, nw_ref, wt_ref, o_ref, xn_ref):
    del offs_ref
    @pl.when(pl.program_id(1) == 0)
    def _():
        _norm_rows_to(x_ref, nw_ref, xn_ref)
    o_ref[...] = lax.dot_general(xn_ref[...], wt_ref[...].astype(BF16), NT, preferred_element_type=F32)


ROW_ALIGN = 32


def mm_in_t(x, nw, wt, row_offsets, bm, bn):
    m, d = x.shape
    nblk = len(row_offsets)
    assert all(o % ROW_ALIGN == 0 for o in row_offsets)
    vmem = _mm_in_vmem(bm, d, bn)
    grid_spec = pltpu.PrefetchScalarGridSpec(
        num_scalar_prefetch=1,
        grid=(m // bm, nblk),
        in_specs=[pl.BlockSpec((bm, d), lambda i, j, offs: (i, 0), pipeline_mode=pl.Buffered(1)),
                  pl.BlockSpec((1, d), lambda i, j, offs: (0, 0)),
                  pl.BlockSpec((pl.Element(bn), pl.Element(d)),
                               lambda i, j, offs: (offs[j] * ROW_ALIGN, 0))],
        out_specs=pl.BlockSpec((bm, bn), lambda i, j, offs: (i, j)),
        scratch_shapes=[pltpu.VMEM((bm, d), BF16)])
    return pl.pallas_call(
        _mm_in_t_kernel,
        grid_spec=grid_spec,
        out_shape=jax.ShapeDtypeStruct((m, nblk * bn), F32),
        compiler_params=_cparams(("parallel", "arbitrary"), vmem),
        name="mm_in_t",
    )(jnp.asarray([o // ROW_ALIGN for o in row_offsets], jnp.int32), x, nw.reshape(1, d), wt)


def _mm_out_kernel(*refs, nparts):
    a_refs = refs[:nparts]
    w_refs = refs[nparts:2 * nparts]
    r_ref = refs[2 * nparts]
    o_ref = refs[2 * nparts + 1]
    acc = r_ref[...]
    for a_ref, w_ref in zip(a_refs, w_refs):
        acc = acc + _dot(a_ref[...], w_ref[...].astype(BF16))
    o_ref[...] = acc


def mm_out(parts, w, layer, resid, bm, bn):
    nparts = len(parts)
    m, kp = parts[0].shape
    n = w.shape[2]
    vmem = nparts * (2 * bm * kp * 2 + 2 * kp * bn * 4 + kp * bn * 2) + 4 * bm * bn * 4 + 2 * bm * bn * 4 + _mib(6)
    in_specs = [pl.BlockSpec((bm, kp), lambda i, j: (i, 0)) for _ in range(nparts)]
    in_specs += [pl.BlockSpec((None, kp, bn), functools.partial(lambda i, j, p: (layer, p, j), p=p))
                 for p in range(nparts)]
    in_specs += [pl.BlockSpec((bm, bn), lambda i, j: (i, j))]
    return pl.pallas_call(
        functools.partial(_mm_out_kernel, nparts=nparts),
        grid=(m // bm, n // bn),
        in_specs=in_specs,
        out_specs=pl.BlockSpec((bm, bn), lambda i, j: (i, j)),
        out_shape=jax.ShapeDtypeStruct((m, n), F32),
        compiler_params=_cparams(("parallel", "parallel"), vmem),
        name="mm_out",
    )(*parts, *([w] * nparts), resid)


def _rmsnorm_kernel(x_ref, nw_ref, o_ref):
    o_ref[...] = _rms(x_ref[...]) * nw_ref[...]


def rmsnorm(x, nw, bm):
    m, d = x.shape
    return pl.pallas_call(
        _rmsnorm_kernel,
        grid=(m // bm,),
        in_specs=[pl.BlockSpec((bm, d), lambda i: (i, 0)), pl.BlockSpec((1, d), lambda i: (0, 0))],
        out_specs=pl.BlockSpec((bm, d), lambda i: (i, 0)),
        out_shape=jax.ShapeDtypeStruct((m, d), F32),
        compiler_params=_cparams(("parallel",), 6 * bm * d * 4 + _mib(4)),
        name="rmsnorm",
    )(x, nw.reshape(1, d))


FFN_SLAB = 512


def _ffn_up_kernel(x_ref, nw_ref, wg_ref, wu_ref, cw_ref, cb_ref, prev_ref,
                   act_ref, st_ref, xn_ref, gbuf_ref, carry_ref, *, shift, blocks_per_group):
    i = pl.program_id(0)
    j = pl.program_id(1)
    bm = x_ref.shape[0]
    n = (FFN_CONV - 1) * shift
    off = gbuf_ref.shape[0] - bm

    @pl.when(j == 0)
    def _():
        _norm_rows_to(x_ref, nw_ref, xn_ref)

    first = (i % blocks_per_group) == 0

    @pl.when(first)
    def _():
        gbuf_ref[off - n:off, :] = prev_ref[0]

    @pl.when(jnp.logical_not(first))
    def _():
        gbuf_ref[off - n:off, :] = carry_ref[j]

    wg = wg_ref[...].astype(BF16)
    wu = wu_ref[...].astype(BF16)
    slab = min(FFN_SLAB, bm)
    for s in range(bm // slab):
        r0 = s * slab
        xs = xn_ref[r0:r0 + slab, :]
        gbuf_ref[off + r0:off + r0 + slab, :] = _dot(xs, wg)
        u = _dot(xs, wu)
        gc = cb_ref[...]
        for t in range(FFN_CONV):
            start = off + r0 - (FFN_CONV - 1 - t) * shift
            gc = gc + gbuf_ref[start:start + slab, :] * cw_ref[t:t + 1, :]
        act_ref[r0:r0 + slab, :] = (_silu(gc) * u).astype(BF16)
    tail = gbuf_ref[off + bm - n:off + bm, :]
    carry_ref[j] = tail
    st_ref[0] = tail


def ffn_up(x, nw, w_up, layer, cw, cb, prev, shift, group_rows, bm, bf):
    m, d = x.shape
    f = cw.shape[1]
    n = (FFN_CONV - 1) * shift
    off = _conv_off(shift, FFN_CONV)
    bpg = group_rows // bm
    nf = f // bf
    vmem = (bm * d * 4 + bm * d * 2 + 4 * d * bf * 4 + 2 * d * bf * 2 + 2 * bm * bf * 2
            + (off + bm) * bf * 4 + nf * max(n, SUBLANES) * bf * 4 + 4 * n * bf * 4 + 5 * bm * bf * 4
            + 4 * NORM_ROWS * d * 4 + _mib(6))
    act, st = pl.pallas_call(
        functools.partial(_ffn_up_kernel, shift=shift, blocks_per_group=bpg),
        grid=(m // bm, nf),
        in_specs=[pl.BlockSpec((bm, d), lambda i, j: (i, 0), pipeline_mode=pl.Buffered(1)),
                  pl.BlockSpec((1, d), lambda i, j: (0, 0)),
                  pl.BlockSpec((None, d, bf), lambda i, j: (layer, 0, j)),
                  pl.BlockSpec((None, d, bf), lambda i, j: (layer, 0, j + nf)),
                  pl.BlockSpec((FFN_CONV, bf), lambda i, j: (0, j)),
                  pl.BlockSpec((1, bf), lambda i, j: (0, j)),
                  pl.BlockSpec((1, n, bf), lambda i, j: (i // bpg, 0, j))],
        out_specs=[pl.BlockSpec((bm, bf), lambda i, j: (i, j)),
                   pl.BlockSpec((1, n, bf), lambda i, j: (i, 0, j))],
        out_shape=[jax.ShapeDtypeStruct((m, f), BF16),
                   jax.ShapeDtypeStruct((m // bm, n, f), F32)],
        scratch_shapes=[pltpu.VMEM((bm, d), BF16),
                        pltpu.VMEM((off + bm, bf), F32),
                        pltpu.VMEM((nf, n, bf), F32)],
        compiler_params=_cparams(("arbitrary", "arbitrary"), vmem),
        name="ffn_up",
    )(x, nw.reshape(1, d), w_up, w_up, cw, cb.reshape(1, f), prev)
    return act, st[bpg - 1::bpg]


def _expand_heads64(x, nheads):
    r = x.shape[0]
    lo = lax.broadcasted_iota(jnp.int32, (r, LANES), 1) < SSD_HEAD_DIM
    tiles = []
    for p in range(nheads // 2):
        a0 = jnp.broadcast_to(x[:, 2 * p:2 * p + 1], (r, LANES))
        a1 = jnp.broadcast_to(x[:, 2 * p + 1:2 * p + 2], (r, LANES))
        tiles.append(jnp.where(lo, a0, a1))
    return jnp.concatenate(tiles, axis=1)


def _ssd_gates(xs_c, bc_c, dt_raw, dtb, alog):
    xs = _silu(xs_c)
    bc = _silu(bc_c)
    dt = _softplus(dt_raw + dtb)
    la = dt * (-jnp.exp(alog))
    return xs, bc, dt, la


def _ssd_finish(o, xs, z, dx, nw):
    y = (o + dx * xs) * _silu(z)
    gw = D_SSD // SSD_GROUPS
    y = jnp.concatenate([_rms(y[:, g * gw:(g + 1) * gw]) for g in range(SSD_GROUPS)], axis=1)
    return y * nw


def _ssd_prompt_kernel(z_ref, xs_ref, bc_ref, dt_ref, cwx_ref, cwb_ref, cbx_ref, cbb_ref,
                       dtb_ref, alog_ref, dx_ref, nw_ref,
                       y_ref, sfin_ref, cfin_ref, s_ref, bufx_ref, bufb_ref):
    c = pl.program_id(1)
    nc = pl.num_programs(1)
    rows = xs_ref.shape[0]
    gw = D_SSD // SSD_GROUPS
    hpg = SSD_HEADS // SSD_GROUPS

    @pl.when(c == 0)
    def _():
        s_ref[...] = jnp.zeros_like(s_ref)
        bufx_ref[0:SUBLANES, :] = jnp.zeros((SUBLANES, D_SSD), F32)
        bufb_ref[0:SUBLANES, :] = jnp.zeros((SUBLANES, SSD_BC), F32)

    yield
    xs_c, tailx = _causal_conv(bufx_ref, xs_ref[...], cwx_ref, cbx_ref, 1, SSD_CONV)
    bc_c, tailb = _causal_conv(bufb_ref, bc_ref[...], cwb_ref, cbb_ref, 1, SSD_CONV)
    xs, bc, dt, la = _ssd_gates(xs_c, bc_c, dt_ref[...], dtb_ref[...], alog_ref[...])
    cum = _cumsum_rows(la)
    cum_t = cum.T
    cumx = _expand_heads64(cum, SSD_HEADS)
    dtx = _expand_heads64(dt, SSD_HEADS)
    lastx = cumx[rows - 1:rows, :]
    ecum = jnp.exp(cumx)
    wx = jnp.exp(lastx - cumx)
    elast = jnp.exp(lastx)
    v_all = xs * dtx
    vw_all = v_all * wx

    ri = lax.broadcasted_iota(jnp.int32, (rows, rows), 0)
    ci = lax.broadcasted_iota(jnp.int32, (rows, rows), 1)
    causal = ri >= ci
    lo = lax.broadcasted_iota(jnp.int32, (rows, LANES), 1) < SSD_HEAD_DIM

    o_groups = []
    for g in range(SSD_GROUPS):
        kb = bc[:, g * SSD_STATE:(g + 1) * SSD_STATE].astype(BF16)
        qb = bc[:, (SSD_GROUPS + g) * SSD_STATE:(SSD_GROUPS + g + 1) * SSD_STATE].astype(BF16)
        qk = lax.dot_general(qb, kb, NT, preferred_element_type=F32)
        o_tiles = []
        for p in range(hpg // 2):
            acc = None
            for q in range(2):
                h = g * hpg + 2 * p + q
                diff = cum[:, h:h + 1] - cum_t[h:h + 1, :]
                dec = jnp.exp(jnp.where(causal, diff, -1e30))
                pm = (qk * dec).astype(BF16)
                col = g * gw + p * LANES
                vp = v_all[:, col:col + LANES]
                vh = jnp.where(lo, vp, 0.0) if q == 0 else jnp.where(lo, 0.0, vp)
                t = _dot(pm, vh.astype(BF16))
                acc = t if acc is None else acc + t
            o_tiles.append(acc)
        o_intra = jnp.concatenate(o_tiles, axis=1)
        sg = s_ref[g]
        sl = slice(g * gw, (g + 1) * gw)
        o_inter = _dot(qb, sg.astype(BF16)) * ecum[:, sl]
        upd = lax.dot_general(kb, vw_all[:, sl].astype(BF16), TN, preferred_element_type=F32)
        s_ref[g] = sg * elast[:, sl] + upd
        o_groups.append(o_intra + o_inter)
    o = jnp.concatenate(o_groups, axis=1)
    y_ref[...] = _ssd_finish(o, xs, z_ref[...], dx_ref[...], nw_ref[...]).astype(BF16)

    yield
    @pl.when(c == nc - 1)
    def _():
        for h in range(SSD_HEADS):
            g, hl = divmod(h, hpg)
            sfin_ref[0, h] = s_ref[g, :, hl * SSD_HEAD_DIM:(hl + 1) * SSD_HEAD_DIM]
        cfin_ref[0, :, 0:D_SSD] = tailx
        cfin_ref[0, :, D_SSD:D_SSD + SSD_BC] = tailb


def ssd_prompt(pe, prm, nseq, seq, chunk):
    m = nseq * seq
    nc = seq // chunk
    row = lambda b, c: b * nc + c
    full = lambda shp: pl.BlockSpec(shp, lambda b, c: (0,) * len(shp))
    in_specs = [
        pl.BlockSpec((chunk, D_SSD), lambda b, c: (row(b, c), 0)),
        pl.BlockSpec((chunk, D_SSD), lambda b, c: (row(b, c), 1)),
        pl.BlockSpec((chunk, SSD_BC), lambda b, c: (row(b, c), 10)),
        pl.BlockSpec((chunk, LANES), lambda b, c: (row(b, c), DT_COL // LANES)),
        pl.BlockSpec((SSD_CONV, D_SSD), lambda b, c: (0, 0)),
        pl.BlockSpec((SSD_CONV, SSD_BC), lambda b, c: (0, 2)),
        pl.BlockSpec((1, D_SSD), lambda b, c: (0, 0)),
        pl.BlockSpec((1, SSD_BC), lambda b, c: (0, 2)),
        full((1, LANES)), full((1, LANES)), full((1, D_SSD)), full((1, D_SSD)),
    ]
    return dict(
        kernel=_ssd_prompt_kernel,
        arrays=[pe, pe, pe, pe, prm["ssd_cw"], prm["ssd_cw"], prm["ssd_cb"], prm["ssd_cb"],
                prm["ssd_dtb"], prm["ssd_alog"], prm["ssd_dx"], prm["ssd_nw"]],
        in_specs=in_specs,
        out_specs=[pl.BlockSpec((chunk, D_SSD), lambda b, c: (row(b, c), 0)),
                   pl.BlockSpec((1, SSD_HEADS, SSD_STATE, SSD_HEAD_DIM), lambda b, c: (b, 0, 0, 0)),
                   pl.BlockSpec((1, SSD_CONV - 1, D_SSD + SSD_BC), lambda b, c: (b, 0, 0))],
        out_shape=[jax.ShapeDtypeStruct((m, D_SSD), BF16),
                   jax.ShapeDtypeStruct((nseq, SSD_HEADS, SSD_STATE, SSD_HEAD_DIM), F32),
                   jax.ShapeDtypeStruct((nseq, SSD_CONV - 1, D_SSD + SSD_BC), F32)],
        scratch=[pltpu.VMEM((SSD_GROUPS, SSD_STATE, D_SSD // SSD_GROUPS), F32),
                 pltpu.VMEM((SUBLANES + chunk, D_SSD), F32),
                 pltpu.VMEM((SUBLANES + chunk, SSD_BC), F32)])


MIX_CHUNK = 128


def _run_parts(parts, grid, name, vmem):
    counts = [(len(p["arrays"]), len(p["out_shape"]), len(p["scratch"])) for p in parts]
    n_in = sum(c[0] for c in counts)
    n_out = sum(c[1] for c in counts)

    def body(*refs):
        i, o, s = 0, n_in, n_in + n_out
        gens = []
        for p, (a, b, c) in zip(parts, counts):
            gens.append(p["kernel"](*refs[i:i + a], *refs[o:o + b], *refs[s:s + c]))
            i, o, s = i + a, o + b, s + c
        for _ in range(3):
            for g in gens:
                next(g, None)

    outs = pl.pallas_call(
        body,
        grid=grid,
        in_specs=[sp for p in parts for sp in p["in_specs"]],
        out_specs=[sp for p in parts for sp in p["out_specs"]],
        out_shape=[sh for p in parts for sh in p["out_shape"]],
        scratch_shapes=[sc for p in parts for sc in p["scratch"]],
        compiler_params=_cparams(("parallel", "arbitrary"), vmem),
        name=name,
    )(*[a for p in parts for a in p["arrays"]])
    res, k = [], 0
    for _, b, _ in counts:
        res.append(outs[k:k + b])
        k += b
    return res


def _rotary(x, cos, sin_signed):
    return x * cos + pltpu.roll(x, RET_DK // 2, 1) * sin_signed


def _ret_prompt_kernel(q_ref, k_ref, v_ref, g_ref, cos_ref, sin_ref, o_ref, sfin_ref, s_ref):
    c = pl.program_id(1)
    nc = pl.num_programs(1)
    rows = q_ref.shape[0]

    @pl.when(c == 0)
    def _():
        s_ref[...] = jnp.zeros_like(s_ref)

    yield
    cos = cos_ref[...]
    sin = sin_ref[...]
    ri = lax.broadcasted_iota(jnp.int32, (rows, rows), 0)
    ci = lax.broadcasted_iota(jnp.int32, (rows, rows), 1)
    dij = jnp.where(ri >= ci, (ri - ci).astype(F32), 1e30)
    tk = lax.broadcasted_iota(jnp.int32, (rows, RET_DK), 0).astype(F32)
    tv = lax.broadcasted_iota(jnp.int32, (rows, RET_DV), 0).astype(F32)
    for h in range(RET_HEADS):
        lg = RET_LOG_GAMMA[h]
        qr = _rotary(q_ref[:, h * RET_DK:(h + 1) * RET_DK], cos, sin)
        kr = _rotary(k_ref[:, h * RET_DK:(h + 1) * RET_DK], cos, sin) * (RET_DK ** -0.5)
        qb = qr.astype(BF16)
        s = lax.dot_general(qb, kr.astype(BF16), NT, preferred_element_type=F32)
        pm = (s * jnp.exp(lg * dij)).astype(BF16)
        vb = v_ref[:, h * RET_DV:(h + 1) * RET_DV].astype(BF16)
        sh = s_ref[h]
        o = _dot(pm, vb) + _dot(qb, sh.astype(BF16)) * jnp.exp(lg * (tv + 1.0))
        kw = (kr * jnp.exp(lg * ((rows - 1.0) - tk))).astype(BF16)
        s_ref[h] = sh * math.exp(lg * rows) + lax.dot_general(kw, vb, TN, preferred_element_type=F32)
        gh = g_ref[:, h * RET_DV:(h + 1) * RET_DV]
        o_ref[:, h * RET_DV:(h + 1) * RET_DV] = (_rms(o) * _silu(gh)).astype(BF16)

    yield
    @pl.when(c == nc - 1)
    def _():
        sfin_ref[0] = s_ref[...]


def ret_prompt(pe, cos, sin, nseq, seq, chunk):
    m = nseq * seq
    nc = seq // chunk
    row = lambda b, c: b * nc + c
    return dict(
        kernel=_ret_prompt_kernel,
        arrays=[pe, pe, pe, pe, cos, sin],
        in_specs=[pl.BlockSpec((chunk, RET_QK), lambda b, c: (row(b, c), 8)),
                  pl.BlockSpec((chunk, RET_QK), lambda b, c: (row(b, c), 9)),
                  pl.BlockSpec((chunk, D_RET), lambda b, c: (row(b, c), 2)),
                  pl.BlockSpec((chunk, D_RET), lambda b, c: (row(b, c), 3)),
                  pl.BlockSpec((chunk, RET_DK), lambda b, c: (c, 0)),
                  pl.BlockSpec((chunk, RET_DK), lambda b, c: (c, 0))],
        out_specs=[pl.BlockSpec((chunk, D_RET), lambda b, c: (row(b, c), 0)),
                   pl.BlockSpec((1, RET_HEADS, RET_DK, RET_DV), lambda b, c: (b, 0, 0, 0))],
        out_shape=[jax.ShapeDtypeStruct((m, D_RET), BF16),
                   jax.ShapeDtypeStruct((nseq, RET_HEADS, RET_DK, RET_DV), F32)],
        scratch=[pltpu.VMEM((RET_HEADS, RET_DK, RET_DV), F32)])


HG_HB = HG_HEADS


def _hg_gates(hq, hf, lb):
    q = _silu(hq)
    f = lb + (1.0 - lb) * jax.nn.sigmoid(hf)
    return q, f, 1.0 - f, jnp.log(f)


def _hg_tables(rows):
    r = np.arange(rows)[:, None]
    t = np.arange(rows)[None, :]
    sums = [t <= r]
    masks = []
    s = rows // 2
    while s >= 1:
        blk, pos = r // (2 * s), r % (2 * s)
        ref = blk * 2 * s + s - 1
        upper = pos >= s
        sums.append(np.where(upper, (t > ref) & (t <= r), (t > r) & (t <= ref)))
        masks.append((blk == t // (2 * s)) & upper & (t % (2 * s) < s))
        s //= 2
    masks.append(r == t)
    return (np.concatenate(sums, axis=0).astype(np.float32),
            np.stack(masks).astype(np.float32))


def _hg_prompt_kernel(hq_ref, hf_ref, hi_ref, hg_ref, lb_ref, nw_ref, sums_ref, masks_ref,
                      og_ref, sfin_ref, st_ref):
    c = pl.program_id(1)
    nc = pl.num_programs(1)
    rows = hq_ref.shape[0]
    nlev = masks_ref.shape[0] - 1

    @pl.when(c == 0)
    def _():
        st_ref[...] = jnp.zeros_like(st_ref)

    yield
    q_all, _, kk_all, lg_all = _hg_gates(hq_ref[...], hf_ref[...], lb_ref[...])
    lg_hi = lg_all.astype(BF16)
    lg_lo = (lg_all - lg_hi.astype(F32)).astype(BF16)
    sums = sums_ref[...]
    dall = _dot(sums, lg_hi) + _dot(sums, lg_lo)
    cum_all = dall[0:rows]
    ecum_all = jnp.exp(cum_all)
    elev = [jnp.exp(dall[(l + 1) * rows:(l + 2) * rows]) for l in range(nlev)]
    for h in range(HG_HB):
        sl = slice(h * HG_DK, (h + 1) * HG_DK)
        q = q_all[:, sl]
        kk = kk_all[:, sl]
        cum = cum_all[:, sl]
        vb = hi_ref[:, sl].astype(BF16)
        last = cum[rows - 1:rows, :]
        st = st_ref[h]
        o = lax.dot_general((q * ecum_all[:, sl]).astype(BF16), st.astype(BF16), NT,
                            preferred_element_type=F32)
        kt = (kk * jnp.exp(last - cum)).astype(BF16)
        st_ref[h] = st * jnp.exp(last) + lax.dot_general(vb, kt, TN, preferred_element_type=F32)
        a = masks_ref[nlev] * lax.dot_general(q.astype(BF16), kk.astype(BF16), NT,
                                              preferred_element_type=F32)
        for l in range(nlev):
            e = elev[l][:, sl]
            a = a + masks_ref[l] * lax.dot_general((q * e).astype(BF16), (kk * e).astype(BF16), NT,
                                                   preferred_element_type=F32)
        o = o + _dot(a.astype(BF16), vb)
        og_ref[:, sl] = (_rms(o) * nw_ref[...] * _silu(hg_ref[:, sl])).astype(BF16)

    yield
    @pl.when(c == nc - 1)
    def _():
        for h in range(HG_HB):
            sfin_ref[0, h] = st_ref[h].T


def hg_prompt(po, lb, nw, nseq, seq, chunk):
    m = nseq * seq
    nc = seq // chunk
    row = lambda b, c: b * nc + c
    sums_np, masks_np = _hg_tables(chunk)
    sums = jnp.asarray(sums_np, BF16)
    masks = jnp.asarray(masks_np, F32)
    return dict(
        kernel=_hg_prompt_kernel,
        arrays=[po, po, po, po, lb, nw, sums, masks],
        in_specs=[pl.BlockSpec((chunk, D_HG), lambda b, c: (row(b, c), 0)),
                  pl.BlockSpec((chunk, D_HG), lambda b, c: (row(b, c), 1)),
                  pl.BlockSpec((chunk, D_HG), lambda b, c: (row(b, c), 2)),
                  pl.BlockSpec((chunk, D_HG), lambda b, c: (row(b, c), 3)),
                  pl.BlockSpec((1, D_HG), lambda b, c: (0, 0)),
                  pl.BlockSpec((1, HG_DV), lambda b, c: (0, 0)),
                  pl.BlockSpec(sums.shape, lambda b, c: (0, 0)),
                  pl.BlockSpec(masks.shape, lambda b, c: (0, 0, 0))],
        out_specs=[pl.BlockSpec((chunk, D_HG), lambda b, c: (row(b, c), 0)),
                   pl.BlockSpec((1, HG_HEADS, HG_DK, HG_DV), lambda b, c: (b, 0, 0, 0))],
        out_shape=[jax.ShapeDtypeStruct((m, D_HG), BF16),
                   jax.ShapeDtypeStruct((nseq, HG_HEADS, HG_DK, HG_DV), F32)],
        scratch=[pltpu.VMEM((HG_HEADS, HG_DV, HG_DK), F32)])


def _lin_scan(a, b, shift):
    rows = a.shape[0]
    ri = lax.broadcasted_iota(jnp.int32, a.shape, 0)
    d = shift
    while d < rows:
        keep = ri >= d
        a_s = jnp.where(keep, pltpu.roll(a, d, 0), 1.0)
        b_s = jnp.where(keep, pltpu.roll(b, d, 0), 0.0)
        b = a * b_s + b
        a = a * a_s
        d *= 2
    return a, b


def _lru_gates(xr, wr_ref, br, wi_ref, bi, ap):
    xb = xr.astype(BF16)
    r_parts, i_parts = [], []
    for n in range(LRU_BLOCKS):
        xn = xb[:, n * LRU_BW:(n + 1) * LRU_BW]
        r_parts.append(_dot(xn, wr_ref[n].astype(BF16)))
        i_parts.append(_dot(xn, wi_ref[n].astype(BF16)))
    r = jax.nn.sigmoid(jnp.concatenate(r_parts, axis=1) + br)
    gi = jax.nn.sigmoid(jnp.concatenate(i_parts, axis=1) + bi)
    la = -LRU_C * r * _softplus(-ap)
    a = jnp.exp(la)
    th = jnp.tanh(la)
    mult = jnp.sqrt(-2.0 * th / (1.0 - th))
    return a, mult, gi


def _lru_prompt_kernel(rx_ref, rg_ref, cw_ref, cb_ref, wr_ref, br_ref, wi_ref, bi_ref, ap_ref,
                       yl_ref, hfin_ref, cfin_ref, buf_ref, hc_ref):
    c = pl.program_id(1)
    nc = pl.num_programs(1)
    rows = rx_ref.shape[0]

    @pl.when(c == 0)
    def _():
        buf_ref[0:SUBLANES, :] = jnp.zeros((SUBLANES, D_RNN), F32)
        hc_ref[...] = jnp.zeros_like(hc_ref)

    yield
    xr, tail = _causal_conv(buf_ref, rx_ref[...], cw_ref, cb_ref, 1, LRU_CONV)
    a, mult, gi = _lru_gates(xr, wr_ref, br_ref[...], wi_ref, bi_ref[...], ap_ref[...])
    ri = lax.broadcasted_iota(jnp.int32, (rows, D_RNN), 0)
    mult = jnp.where(jnp.logical_and(c == 0, ri == 0), 1.0, mult)
    pa, hb = _lin_scan(a, mult * gi * xr, 1)
    hs = pa * hc_ref[0:1, :] + hb
    hc_ref[0:1, :] = hs[rows - 1:rows, :]
    yl_ref[...] = (hs * _gelu_tanh(rg_ref[...])).astype(BF16)

    yield
    @pl.when(c == nc - 1)
    def _():
        hfin_ref[0] = hs[rows - 1:rows, :]
        cfin_ref[0] = tail


def lru_prompt(po, prm, nseq, seq, chunk):
    m = nseq * seq
    nc = seq // chunk
    row = lambda b, c: b * nc + c
    full = lambda shp: pl.BlockSpec(shp, lambda b, c: (0,) * len(shp))
    return dict(
        kernel=_lru_prompt_kernel,
        arrays=[po, po, prm["lru_cw"], prm["lru_cb"], prm["lru_wr"], prm["lru_br"], prm["lru_wi"],
                prm["lru_bi"], prm["lru_ap"]],
        in_specs=[pl.BlockSpec((chunk, D_RNN), lambda b, c: (row(b, c), 4)),
                  pl.BlockSpec((chunk, D_RNN), lambda b, c: (row(b, c), 5)),
                  full((LRU_CONV, D_RNN)), full((1, D_RNN)),
                  full((LRU_BLOCKS, LRU_BW, LRU_BW)), full((1, D_RNN)),
                  full((LRU_BLOCKS, LRU_BW, LRU_BW)), full((1, D_RNN)), full((1, D_RNN))],
        out_specs=[pl.BlockSpec((chunk, D_RNN), lambda b, c: (row(b, c), 0)),
                   pl.BlockSpec((1, 1, D_RNN), lambda b, c: (b, 0, 0)),
                   pl.BlockSpec((1, LRU_CONV - 1, D_RNN), lambda b, c: (b, 0, 0))],
        out_shape=[jax.ShapeDtypeStruct((m, D_RNN), BF16),
                   jax.ShapeDtypeStruct((nseq, 1, D_RNN), F32),
                   jax.ShapeDtypeStruct((nseq, LRU_CONV - 1, D_RNN), F32)],
        scratch=[pltpu.VMEM((SUBLANES + chunk, D_RNN), F32),
                 pltpu.VMEM((SUBLANES, D_RNN), F32)])


TOKP = SUBLANES


def _ssd_state_kernel(q_ref, k_ref, vw_ref, dec_ref, st_ref, u_ref, snew_ref):
    hpg = SSD_HEADS // SSD_GROUPS
    gw = hpg * SSD_HEAD_DIM

    def body(b, carry):
        q = q_ref[b].astype(BF16)
        k = k_ref[b].astype(BF16)
        vw = vw_ref[b].astype(BF16)
        for g in range(SSD_GROUPS):
            ks = slice(g * SSD_STATE, (g + 1) * SSD_STATE)
            stg = st_ref[b, g * hpg:(g + 1) * hpg].reshape(gw, SSD_STATE)
            u_ref[b, :, g * gw:(g + 1) * gw] = lax.dot_general(
                q[:, ks], stg.astype(BF16), NT, preferred_element_type=F32)
            upd = lax.dot_general(vw[:, g * gw:(g + 1) * gw], k[:, ks], TN, preferred_element_type=F32)
            for hl in range(hpg):
                h = g * hpg + hl
                snew_ref[b, h] = (st_ref[b, h] * dec_ref[b, h:h + 1, :]
                                  + upd[hl * SSD_HEAD_DIM:(hl + 1) * SSD_HEAD_DIM, :])
        return carry

    lax.fori_loop(0, st_ref.shape[0], body, 0)


def _ret_state_kernel(q_ref, k_ref, v_ref, s_ref, u_ref, snew_ref, *, ntok):
    def body(b, carry):
        q = q_ref[b].astype(BF16)
        k = k_ref[b].astype(BF16)
        v = v_ref[b].astype(BF16)
        for h in range(RET_HEADS):
            ks = slice(h * RET_DK, (h + 1) * RET_DK)
            vs = slice(h * RET_DV, (h + 1) * RET_DV)
            s0 = s_ref[b, h]
            u_ref[b, :, vs] = _dot(q[:, ks], s0.astype(BF16))
            upd = lax.dot_general(k[:, ks], v[:, vs], TN, preferred_element_type=F32)
            snew_ref[b, h] = s0 * math.exp(RET_LOG_GAMMA[h] * ntok) + upd
        return carry

    lax.fori_loop(0, s_ref.shape[0], body, 0)


def _hg_state_kernel(q_ref, k_ref, v_ref, dcol_ref, s_ref, u_ref, snew_ref):
    def body(b, carry):
        q = q_ref[b].astype(BF16)
        k = k_ref[b].astype(BF16)
        v = v_ref[b].astype(BF16)
        dc = dcol_ref[b]
        for h in range(HG_HEADS):
            ks = slice(h * HG_DK, (h + 1) * HG_DK)
            s0 = s_ref[b, h]
            u_ref[b, :, ks] = _dot(q[:, ks], s0.astype(BF16))
            upd = lax.dot_general(k[:, ks], v[:, ks], TN, preferred_element_type=F32)
            snew_ref[b, h] = s0 * dc[:, h:h + 1] + upd
        return carry

    lax.fori_loop(0, s_ref.shape[0], body, 0)


def _state_call(kern, rows_in, extra, s, u_cols, bb, name):
    nb = s.shape[0]
    blk = lambda a: pl.BlockSpec((bb,) + a.shape[1:], lambda i: (i,) + (0,) * (a.ndim - 1))
    ins = list(rows_in) + ([extra] if extra is not None else []) + [s]
    sbytes = bb * int(np.prod(s.shape[1:])) * 4
    rbytes = sum(bb * int(np.prod(a.shape[1:])) * 4 for a in ins[:-1]) + bb * TOKP * u_cols * 4
    return pl.pallas_call(
        kern,
        grid=(nb // bb,),
        in_specs=[blk(a) for a in ins],
        out_specs=[pl.BlockSpec((bb, TOKP, u_cols), lambda i: (i, 0, 0)), blk(s)],
        out_shape=[jax.ShapeDtypeStruct((nb, TOKP, u_cols), F32),
                   jax.ShapeDtypeStruct(s.shape, F32)],
        compiler_params=_cparams(("parallel",), 5 * sbytes + 3 * rbytes + _mib(8)),
        name=name,
    )(*ins)


def _tok(x, t, nb):
    return x[t * nb:(t + 1) * nb]


def _head_sums(x, width):
    r, n = x.shape
    tiles = []
    for h in range(n // width):
        s = jnp.sum(x[:, h * width:(h + 1) * width], axis=1, keepdims=True)
        tiles.append(jnp.broadcast_to(s, (r, width)))
    return jnp.concatenate(tiles, axis=1)


def _ssd_sample_pre_kernel(xs_ref, bc_ref, dt_ref, prevx_ref, prevb_ref, cwx_ref, cwb_ref,
                           cbx_ref, cbb_ref, dtb_ref, alog_ref,
                           xs_out, bc_out, vw_out, oi_out, ecum_out, elast_out, tailx_out, tailb_out,
                           bufx_ref, bufb_ref, *, nb, ntok):
    n = (SSD_CONV - 1) * nb
    offx = bufx_ref.shape[0] - nb * ntok
    bufx_ref[offx - n:offx, :] = prevx_ref[...]
    bufb_ref[offx - n:offx, :] = prevb_ref[...]
    xs_c, tailx = _causal_conv(bufx_ref, xs_ref[...], cwx_ref, cbx_ref, nb, SSD_CONV)
    bc_c, tailb = _causal_conv(bufb_ref, bc_ref[...], cwb_ref, cbb_ref, nb, SSD_CONV)
    xs, bc, dt, la = _ssd_gates(xs_c, bc_c, dt_ref[...], dtb_ref[...], alog_ref[...])
    tailx_out[...] = tailx
    tailb_out[...] = tailb
    xs_out[...] = xs
    bc_out[...] = bc
    cums = []
    for t in range(ntok):
        lt = _tok(la, t, nb)
        cums.append(lt if t == 0 else cums[-1] + lt)
    cumx = [_expand_heads64(cm, SSD_HEADS) for cm in cums]
    v = xs * _expand_heads64(dt, SSD_HEADS)
    kw = SSD_GROUPS * SSD_STATE
    for t in range(ntok):
        ct = _tok(bc, t, nb)[:, kw:2 * kw]
        acc = None
        for t2 in range(t + 1):
            bt = _tok(bc, t2, nb)[:, 0:kw]
            sc = _head_sums(ct * bt, SSD_STATE)
            scx = jnp.concatenate(
                [jnp.concatenate([sc[:, g * SSD_STATE:(g + 1) * SSD_STATE]] * 4, axis=1)
                 for g in range(SSD_GROUPS)], axis=1)
            term = scx * jnp.exp(cumx[t] - cumx[t2]) * _tok(v, t2, nb)
            acc = term if acc is None else acc + term
        oi_out[t * nb:(t + 1) * nb, :] = acc
        ecum_out[t * nb:(t + 1) * nb, :] = jnp.exp(cumx[t])
        vw_out[t * nb:(t + 1) * nb, :] = _tok(v, t, nb) * jnp.exp(cumx[ntok - 1] - cumx[t])
    elast_out[...] = jnp.exp(cums[ntok - 1])


def _ssd_sample_post_kernel(oi_ref, u_ref, ecum_ref, xs_ref, z_ref, dx_ref, nw_ref, y_ref):
    o = oi_ref[...] + ecum_ref[...] * u_ref[...]
    y_ref[...] = _ssd_finish(o, xs_ref[...], z_ref[...], dx_ref[...], nw_ref[...]).astype(BF16)


def _ret_sample_pre_kernel(q_ref, k_ref, v_ref, cos_ref, sin_ref, qd_out, kd_out, oi_out, *, nb, ntok):
    rows = nb * ntok
    cos = jnp.concatenate([jnp.broadcast_to(cos_ref[t:t + 1, :], (nb, RET_DK)) for t in range(ntok)], axis=0)
    sin = jnp.concatenate([jnp.broadcast_to(sin_ref[t:t + 1, :], (nb, RET_DK)) for t in range(ntok)], axis=0)
    qr, kr = [], []
    for h in range(RET_HEADS):
        sl = slice(h * RET_DK, (h + 1) * RET_DK)
        qr.append(_rotary(q_ref[:, sl], cos, sin))
        kr.append(_rotary(k_ref[:, sl], cos, sin) * (RET_DK ** -0.5))
    qr = jnp.concatenate(qr, axis=1)
    kr = jnp.concatenate(kr, axis=1)
    v = v_ref[...]
    for t in range(ntok):
        qt = _tok(qr, t, nb)
        acc = None
        for t2 in range(t + 1):
            sc = _head_sums(qt * _tok(kr, t2, nb), RET_DK)
            vt = _tok(v, t2, nb)
            tiles = []
            for h in range(RET_HEADS):
                dec = math.exp(RET_LOG_GAMMA[h] * (t - t2))
                s = sc[:, h * RET_DK:(h + 1) * RET_DK] * dec
                tiles.append(jnp.concatenate([s, s], axis=1) * vt[:, h * RET_DV:(h + 1) * RET_DV])
            term = jnp.concatenate(tiles, axis=1)
            acc = term if acc is None else acc + term
        oi_out[t * nb:(t + 1) * nb, :] = acc
        qd = jnp.concatenate([qt[:, h * RET_DK:(h + 1) * RET_DK] * math.exp(RET_LOG_GAMMA[h] * (t + 1))
                              for h in range(RET_HEADS)], axis=1)
        kt = _tok(kr, t, nb)
        kd = jnp.concatenate([kt[:, h * RET_DK:(h + 1) * RET_DK] * math.exp(RET_LOG_GAMMA[h] * (ntok - 1 - t))
                              for h in range(RET_HEADS)], axis=1)
        qd_out[t * nb:(t + 1) * nb, :] = qd
        kd_out[t * nb:(t + 1) * nb, :] = kd


def _ret_sample_post_kernel(oi_ref, u_ref, g_ref, o_ref):
    o = oi_ref[...] + u_ref[...]
    g = g_ref[...]
    for h in range(RET_HEADS):
        sl = slice(h * RET_DV, (h + 1) * RET_DV)
        o_ref[:, sl] = (_rms(o[:, sl]) * _silu(g[:, sl])).astype(BF16)


def _hg_sample_pre_kernel(hq_ref, hf_ref, hi_ref, lb_ref, qd_out, kd_out, dl_out, oi_out, *, nb, ntok):
    q, _, kk, lg = _hg_gates(hq_ref[...], hf_ref[...], lb_ref[...])
    v = hi_ref[...]
    cums = []
    for t in range(ntok):
        lt = _tok(lg, t, nb)
        cums.append(lt if t == 0 else cums[-1] + lt)
    for t in range(ntok):
        qt = _tok(q, t, nb)
        acc = None
        for t2 in range(t + 1):
            w = qt * _tok(kk, t2, nb)
            if t2 < t:
                w = w * jnp.exp(cums[t] - cums[t2])
            term = _head_sums(w, HG_DK) * _tok(v, t2, nb)
            acc = term if acc is None else acc + term
        oi_out[t * nb:(t + 1) * nb, :] = acc
        qd_out[t * nb:(t + 1) * nb, :] = qt * jnp.exp(cums[t])
        kd_out[t * nb:(t + 1) * nb, :] = _tok(kk, t, nb) * jnp.exp(cums[ntok - 1] - cums[t])
    dl_out[...] = jnp.exp(cums[ntok - 1])


def _hg_sample_post_kernel(oi_ref, u_ref, hg_ref, nw_ref, og_ref):
    o = oi_ref[...] + u_ref[...]
    hg = hg_ref[...]
    for h in range(HG_HEADS):
        sl = slice(h * HG_DV, (h + 1) * HG_DV)
        og_ref[:, sl] = (_rms(o[:, sl]) * nw_ref[...] * _silu(hg[:, sl])).astype(BF16)


def _lru_sample_kernel(rx_ref, rg_ref, prev_ref, h0_ref, cw_ref, cb_ref, wr_ref, br_ref, wi_ref,
                       bi_ref, ap_ref, yl_out, hfin_out, tail_out, buf_ref, *, nb, ntok):
    rows = nb * ntok
    n = (LRU_CONV - 1) * nb
    off = buf_ref.shape[0] - rows
    buf_ref[off - n:off, :] = prev_ref[...]
    xr, tail = _causal_conv(buf_ref, rx_ref[...], cw_ref, cb_ref, nb, LRU_CONV)
    tail_out[...] = tail
    a, mult, gi = _lru_gates(xr, wr_ref, br_ref[...], wi_ref, bi_ref[...], ap_ref[...])
    b = mult * gi * xr
    h = h0_ref[...]
    for t in range(ntok):
        h = _tok(a, t, nb) * h + _tok(b, t, nb)
        yl_out[t * nb:(t + 1) * nb, :] = (h * _gelu_tanh(rg_ref[t * nb:(t + 1) * nb, :])).astype(BF16)
    hfin_out[...] = h


def _whole(shape):
    return pl.BlockSpec(shape, lambda i: (0,) * len(shape))


def _colblock(rows, width, idx):
    return pl.BlockSpec((rows, width), functools.partial(lambda i, k: (0, k), k=idx))


def _call_whole(kern, in_arrays, in_specs, out_shapes, scratch, name, vmem):
    return pl.pallas_call(
        kern, grid=(1,), in_specs=in_specs,
        out_specs=[_whole(s.shape) for s in out_shapes],
        out_shape=out_shapes, scratch_shapes=scratch,
        compiler_params=_cparams(("arbitrary",), vmem), name=name,
    )(*in_arrays)


def _to_token_major(s):
    nb, w, c = s.shape
    return jnp.transpose(s, (1, 0, 2)).reshape(w * nb, c)


def _from_token_major(x, nb):
    w = x.shape[0] // nb
    return jnp.transpose(x.reshape(w, nb, x.shape[1]), (1, 0, 2))


def _rows_to_batch(x, nb):
    ntok = x.shape[0] // nb
    x = jnp.transpose(x.reshape(ntok, nb, x.shape[1]), (1, 0, 2))
    return jnp.pad(x, ((0, 0), (0, TOKP - ntok), (0, 0)))


def _batch_to_rows(u, ntok):
    nb = u.shape[0]
    return jnp.transpose(u[:, :ntok], (1, 0, 2)).reshape(ntok * nb, u.shape[2])


def ssd_sample(pe, z, s_ssm, s_conv, prm, nb, ntok):
    rows = nb * ntok
    n = (SSD_CONV - 1) * nb
    prev = _to_token_major(s_conv)
    off = _conv_off(nb, SSD_CONV)
    f = lambda *shape: jax.ShapeDtypeStruct(shape, F32)
    outs = _call_whole(
        functools.partial(_ssd_sample_pre_kernel, nb=nb, ntok=ntok),
        [pe, pe, pe, prev, prev, prm["ssd_cw"], prm["ssd_cw"], prm["ssd_cb"], prm["ssd_cb"],
         prm["ssd_dtb"], prm["ssd_alog"]],
        [_colblock(rows, D_SSD, 1), _colblock(rows, SSD_BC, 10), _colblock(rows, LANES, DT_COL // LANES),
         _colblock(n, D_SSD, 0), _colblock(n, SSD_BC, 2),
         _colblock(SSD_CONV, D_SSD, 0), _colblock(SSD_CONV, SSD_BC, 2),
         _colblock(1, D_SSD, 0), _colblock(1, SSD_BC, 2), _whole((1, LANES)), _whole((1, LANES))],
        [f(rows, D_SSD), f(rows, SSD_BC), f(rows, D_SSD), f(rows, D_SSD), f(rows, D_SSD),
         f(nb, LANES), f(n, D_SSD), f(n, SSD_BC)],
        [pltpu.VMEM((off + rows, D_SSD), F32), pltpu.VMEM((off + rows, SSD_BC), F32)],
        "ssd_sample_pre", _mib(56))
    xs, bc, vw, oi, ecum, elast, tailx, tailb = outs
    kw = SSD_GROUPS * SSD_STATE
    dec = jnp.broadcast_to(elast[:, :SSD_HEADS, None], (nb, SSD_HEADS, SSD_STATE))
    st = jnp.swapaxes(s_ssm, -1, -2)
    u, st_new = _state_call(_ssd_state_kernel,
                            [_rows_to_batch(bc[:, kw:], nb), _rows_to_batch(bc[:, :kw], nb),
                             _rows_to_batch(vw, nb)], dec, st, D_SSD, 4, "ssd_state")
    s_new = jnp.swapaxes(st_new, -1, -2)
    u = _batch_to_rows(u, ntok)
    (y,) = _call_whole(
        _ssd_sample_post_kernel, [oi, u, ecum, xs, z, prm["ssd_dx"], prm["ssd_nw"]],
        [_whole((rows, D_SSD))] * 4 + [_colblock(rows, D_SSD, 0), _whole((1, D_SSD)), _whole((1, D_SSD))],
        [jax.ShapeDtypeStruct((rows, D_SSD), BF16)], [], "ssd_sample_post", _mib(48))
    conv_new = _from_token_major(jnp.concatenate([tailx, tailb], axis=1), nb)
    return y, s_new, conv_new


def ret_sample(pe, s_ret, cos, sin, nb, ntok):
    rows = nb * ntok
    f = lambda *shape: jax.ShapeDtypeStruct(shape, F32)
    qd, kd, oi = _call_whole(
        functools.partial(_ret_sample_pre_kernel, nb=nb, ntok=ntok),
        [pe, pe, pe, cos, sin],
        [_colblock(rows, RET_QK, 8), _colblock(rows, RET_QK, 9), _colblock(rows, D_RET, 2),
         _whole(cos.shape), _whole(sin.shape)],
        [f(rows, RET_QK), f(rows, RET_QK), f(rows, D_RET)], [], "ret_sample_pre", _mib(48))
    v = _rows_to_batch(pe[:, 2 * D_RET:3 * D_RET], nb)
    u, s_new = _state_call(functools.partial(_ret_state_kernel, ntok=ntok),
                           [_rows_to_batch(qd, nb), _rows_to_batch(kd, nb), v], None, s_ret,
                           D_RET, 4, "ret_state")
    u = _batch_to_rows(u, ntok)
    (o,) = _call_whole(
        _ret_sample_post_kernel, [oi, u, pe],
        [_whole((rows, D_RET)), _whole((rows, D_RET)), _colblock(rows, D_RET, 3)],
        [jax.ShapeDtypeStruct((rows, D_RET), BF16)], [], "ret_sample_post", _mib(40))
    return o, s_new


def hg_sample(po, s_hg, lb, nw, nb, ntok):
    rows = nb * ntok
    f = lambda *shape: jax.ShapeDtypeStruct(shape, F32)
    qd, kd, dl, oi = _call_whole(
        functools.partial(_hg_sample_pre_kernel, nb=nb, ntok=ntok),
        [po, po, po, lb],
        [_colblock(rows, D_HG, 0), _colblock(rows, D_HG, 1), _colblock(rows, D_HG, 2), _whole((1, D_HG))],
        [f(rows, D_HG), f(rows, D_HG), f(nb, D_HG), f(rows, D_HG)], [], "hg_sample_pre", _mib(48))
    dcol = jnp.pad(jnp.transpose(dl.reshape(nb, HG_HEADS, HG_DK), (0, 2, 1)),
                   ((0, 0), (0, 0), (0, LANES - HG_HEADS)))
    v = _rows_to_batch(po[:, 2 * D_HG:3 * D_HG], nb)
    u, s_new = _state_call(_hg_state_kernel, [_rows_to_batch(qd, nb), _rows_to_batch(kd, nb), v],
                           dcol, s_hg, D_HG, 4, "hg_state")
    u = _batch_to_rows(u, ntok)
    (og,) = _call_whole(
        _hg_sample_post_kernel, [oi, u, po, nw],
        [_whole((rows, D_HG)), _whole((rows, D_HG)), _colblock(rows, D_HG, 3), _whole((1, HG_DV))],
        [jax.ShapeDtypeStruct((rows, D_HG), BF16)], [], "hg_sample_post", _mib(40))
    return og, s_new


def lru_sample(po, s_lru, s_lconv, prm, nb, ntok):
    rows = nb * ntok
    n = (LRU_CONV - 1) * nb
    off = _conv_off(nb, LRU_CONV)
    prev = _to_token_major(s_lconv)
    yl, hfin, tail = _call_whole(
        functools.partial(_lru_sample_kernel, nb=nb, ntok=ntok),
        [po, po, prev, s_lru, prm["lru_cw"], prm["lru_cb"], prm["lru_wr"], prm["lru_br"], prm["lru_wi"],
         prm["lru_bi"], prm["lru_ap"]],
        [_colblock(rows, D_RNN, 4), _colblock(rows, D_RNN, 5), _whole((n, D_RNN)), _whole((nb, D_RNN)),
         _whole((LRU_CONV, D_RNN)), _whole((1, D_RNN)), _whole((LRU_BLOCKS, LRU_BW, LRU_BW)),
         _whole((1, D_RNN)), _whole((LRU_BLOCKS, LRU_BW, LRU_BW)), _whole((1, D_RNN)), _whole((1, D_RNN))],
        [jax.ShapeDtypeStruct((rows, D_RNN), BF16), jax.ShapeDtypeStruct((nb, D_RNN), F32),
         jax.ShapeDtypeStruct((n, D_RNN), F32)],
        [pltpu.VMEM((off + rows, D_RNN), F32)], "lru_sample", _mib(48))
    return yl, hfin, _from_token_major(tail, nb)


def _rope_tables(pos):
    half = RET_DK // 2
    inv = ROPE_BASE ** (-jnp.arange(half, dtype=F32) / half)
    ang = pos.astype(F32)[:, None] * inv[None, :]
    cos, sin = jnp.cos(ang), jnp.sin(ang)
    return jnp.concatenate([cos, cos], axis=1), jnp.concatenate([-sin, sin], axis=1)


def _pad_lanes(v):
    return jnp.pad(v.astype(F32), (0, LANES - v.shape[0])).reshape(1, LANES)


def _prepare(p):
    lbs = jnp.cumsum(jax.nn.softmax(p["hg_lower_bounds"].astype(F32), axis=0), axis=0)
    lbs = lbs - lbs[0]
    return {
        "even_wt": jnp.swapaxes(p["even_w_in"][0], 0, 1),
        "ssd_cw": p["ssd_conv_w"][0], "ssd_cb": p["ssd_conv_b"][0].reshape(1, -1),
        "ssd_dtb": _pad_lanes(p["ssd_dt_bias"][0]), "ssd_alog": _pad_lanes(p["ssd_A_log"][0]),
        "ssd_dx": jnp.repeat(p["ssd_D"][0], SSD_HEAD_DIM).reshape(1, D_SSD),
        "ssd_nw": p["ssd_norm_w"][0].reshape(1, D_SSD),
        "hg_lb": lbs[1].reshape(1, D_HG), "hg_nw": p["hg_norm_w"][0].reshape(1, HG_DV),
        "lru_cw": p["lru_conv_w"][0], "lru_cb": p["lru_conv_b"][0].reshape(1, D_RNN),
        "lru_wr": p["lru_w_r"][0], "lru_br": p["lru_b_r"][0].reshape(1, D_RNN),
        "lru_wi": p["lru_w_i"][0], "lru_bi": p["lru_b_i"][0].reshape(1, D_RNN),
        "lru_ap": p["lru_a_param"][0].reshape(1, D_RNN),
    }


def _even_row_offsets(bn):
    o_xbc, o_dt = D_SSD, 2 * D_SSD + SSD_BC
    o_q = o_dt + SSD_HEADS
    o_k, o_v = o_q + RET_QK, o_q + 2 * RET_QK
    o_g = o_v + D_RET
    segs = [(0, D_SSD), (o_xbc, D_SSD), (o_v, D_RET), (o_g, D_RET), (o_q, RET_QK), (o_k, RET_QK),
            (o_xbc + D_SSD, SSD_BC), (o_dt, bn)]
    offs = [start + i for start, width in segs for i in range(0, width, bn)]
    assert len(offs) * bn == EVEN_PACKED and offs[-1] + bn <= D_IN_EVEN
    return offs


def _even_proj(x, p, prm, bm):
    return mm_in_t(x, p["norm_mix"][0], prm["even_wt"], _even_row_offsets(512), bm, 512)


def _ffn(x, p, prm, l, prev, shift, group_rows, bm, bf=512):
    act, st = ffn_up(x, p["norm_ffn"][l], p["ffn_w_up"], l, p["ffn_conv_w"][l], p["ffn_conv_b"][l],
                     prev, shift, group_rows, bm, bf)
    return mm_out([act], p["ffn_w_down"], l, x, min(bm, 1024), 256), st


def _trunk_prompt(x, p, prm, nseq, seq):
    m = nseq * seq
    bm = min(2048, seq)
    cos, sin = _rope_tables(jnp.arange(seq, dtype=jnp.int32))
    pe = _even_proj(x, p, prm, bm)
    (y, ssm, ssm_conv), (o, ret) = _run_parts(
        [ssd_prompt(pe, prm, nseq, seq, MIX_CHUNK), ret_prompt(pe, cos, sin, nseq, seq, MIX_CHUNK)],
        (nseq, seq // MIX_CHUNK), "even_prompt", _mib(56))
    x = mm_out([y, o], p["even_w_out"], 0, x, min(bm, 1024), 512)
    zeros_ffn = jnp.zeros((nseq, FFN_CONV - 1, D_FF), F32)
    x, ffn0 = _ffn(x, p, prm, 0, zeros_ffn, 1, seq, bm, 256)
    po = mm_in(x, p["norm_mix"][1], p["odd_w_in"][0], bm, 512)
    (og, hgrn), (yl, lru, lru_conv) = _run_parts(
        [hg_prompt(po, prm["hg_lb"], prm["hg_nw"], nseq, seq, MIX_CHUNK),
         lru_prompt(po, prm, nseq, seq, MIX_CHUNK)],
        (nseq, seq // MIX_CHUNK), "odd_prompt", _mib(56))
    x = mm_out([og, yl], p["odd_w_out"], 0, x, min(bm, 1024), 512)
    x, ffn1 = _ffn(x, p, prm, 1, zeros_ffn, 1, seq, bm, 256)
    y_out = rmsnorm(x, p["norm_final"], min(512, m))
    return (y_out.reshape(nseq, seq, D_MODEL), ssm[None], ssm_conv[None], ret[None], hgrn[None],
            lru.reshape(1, nseq, D_RNN), lru_conv[None], jnp.stack([ffn0, ffn1]))


def _trunk_sample(x, st, p, prm, nb, ntok):
    rows = nb * ntok
    cos, sin = _rope_tables(PAST_LEN + jnp.arange(ntok, dtype=jnp.int32))
    pe = _even_proj(x, p, prm, rows)
    y, ssm, ssm_conv = ssd_sample(pe, pe, st["ssm"][0], st["ssm_conv"][0], prm, nb, ntok)
    o, ret = ret_sample(pe, st["ret"][0], cos, sin, nb, ntok)
    x = mm_out([y, o], p["even_w_out"], 0, x, rows, 512)
    x, ffn0 = _ffn(x, p, prm, 0, _to_token_major(st["ffn_conv"][0])[None], nb, rows, rows)
    po = mm_in(x, p["norm_mix"][1], p["odd_w_in"][0], rows, 512)
    og, hgrn = hg_sample(po, st["hgrn"][0], prm["hg_lb"], prm["hg_nw"], nb, ntok)
    yl, lru, lru_conv = lru_sample(po, st["lru"][0], st["lru_conv"][0], prm, nb, ntok)
    x = mm_out([og, yl], p["odd_w_out"], 0, x, rows, 512)
    x, ffn1 = _ffn(x, p, prm, 1, _to_token_major(st["ffn_conv"][1])[None], nb, rows, rows)
    y_out = rmsnorm(x, p["norm_final"], rows)
    y_out = jnp.transpose(y_out.reshape(ntok, nb, D_MODEL), (1, 0, 2))
    ffn = jnp.stack([_from_token_major(ffn0[0], nb), _from_token_major(ffn1[0], nb)])
    return (y_out, ssm[None], ssm_conv[None], ret[None], hgrn[None], lru[None], lru_conv[None], ffn)


def kernel(x_prompt, x_sample, state_ssm, state_ssm_conv, state_ret, state_hgrn, state_lru, state_lru_conv, state_ffn_conv, norm_mix, norm_ffn, norm_final, even_w_in, ssd_conv_w, ssd_conv_b, ssd_dt_bias, ssd_A_log, ssd_D, ssd_norm_w, even_w_out, odd_w_in, hg_lower_bounds, hg_norm_w, lru_conv_w, lru_conv_b, lru_w_r, lru_b_r, lru_w_i, lru_b_i, lru_a_param, odd_w_out, ffn_w_up, ffn_conv_w, ffn_conv_b, ffn_w_down):
    p = {
        "norm_mix": norm_mix, "norm_ffn": norm_ffn, "norm_final": norm_final,
        "even_w_in": even_w_in, "ssd_conv_w": ssd_conv_w, "ssd_conv_b": ssd_conv_b,
        "ssd_dt_bias": ssd_dt_bias, "ssd_A_log": ssd_A_log, "ssd_D": ssd_D,
        "ssd_norm_w": ssd_norm_w, "even_w_out": even_w_out, "odd_w_in": odd_w_in,
        "hg_lower_bounds": hg_lower_bounds, "hg_norm_w": hg_norm_w,
        "lru_conv_w": lru_conv_w, "lru_conv_b": lru_conv_b, "lru_w_r": lru_w_r,
        "lru_b_r": lru_b_r, "lru_w_i": lru_w_i, "lru_b_i": lru_b_i,
        "lru_a_param": lru_a_param, "odd_w_out": odd_w_out, "ffn_w_up": ffn_w_up,
        "ffn_conv_w": ffn_conv_w, "ffn_conv_b": ffn_conv_b, "ffn_w_down": ffn_w_down,
    }
    prm = _prepare(p)
    nseq, seq, _ = x_prompt.shape
    nb, ntok, _ = x_sample.shape
    st = {"ssm": state_ssm, "ssm_conv": state_ssm_conv, "ret": state_ret, "hgrn": state_hgrn,
          "lru": state_lru, "lru_conv": state_lru_conv, "ffn_conv": state_ffn_conv}
    yp = _trunk_prompt(x_prompt.reshape(nseq * seq, D_MODEL), p, prm, nseq, seq)
    xs_tm = jnp.transpose(x_sample, (1, 0, 2)).reshape(ntok * nb, D_MODEL)
    ys = _trunk_sample(xs_tm, st, p, prm, nb, ntok)
    return (yp[0], ys[0], yp[1], ys[1], yp[2], ys[2], yp[3], ys[3], yp[4], ys[4],
            yp[5], ys[5], yp[6], ys[6], yp[7], ys[7])
```

```python
import functools
import math

import numpy as np
import jax
import jax.numpy as jnp
from jax import lax
from jax.experimental import pallas as pl
from jax.experimental.pallas import tpu as pltpu

F32 = jnp.float32
BF16 = jnp.bfloat16
EPS = 1e-6
LOG2E = math.log2(math.e)

D_MODEL = 2048
PAST_LEN = 16384
SSD_HEADS = 32
SSD_HEAD_DIM = 64
D_SSD = SSD_HEADS * SSD_HEAD_DIM
SSD_GROUPS = 4
SSD_STATE = 128
SSD_CONV = 4
SSD_BC = 2 * SSD_GROUPS * SSD_STATE
RET_HEADS = 8
RET_DK = 128
RET_DV = 256
RET_QK = RET_HEADS * RET_DK
D_RET = RET_HEADS * RET_DV
ROPE_BASE = 10000.0
HG_HEADS = 16
HG_DK = 128
HG_DV = 128
D_HG = HG_HEADS * HG_DV
D_RNN = 2048
LRU_BLOCKS = 8
LRU_BW = D_RNN // LRU_BLOCKS
LRU_CONV = 4
LRU_C = 8.0
D_FF = 5632
FFN_CONV = 3
D_IN_EVEN = D_SSD + (D_SSD + SSD_BC) + SSD_HEADS + 2 * RET_QK + 2 * D_RET
EVEN_PACKED = 11776
DT_COL = 11264

V7X_VMEM_BYTES = 64 * 1024 * 1024
V7X_VMEM_CAP = 60 * 1024 * 1024
LANES = 128
SUBLANES = 8

RET_LOG_GAMMA = [float(v) for v in np.log1p(-np.exp(np.linspace(
    math.log(1.0 / 32.0), math.log(1.0 / 512.0), RET_HEADS, dtype=np.float32))).astype(np.float32)]

NT = (((1,), (1,)), ((), ()))
TN = (((0,), (0,)), ((), ()))


def _cparams(sem, vmem_bytes):
    return pltpu.CompilerParams(dimension_semantics=sem,
                                vmem_limit_bytes=int(min(V7X_VMEM_CAP, vmem_bytes)))


def _mib(n):
    return n * 1024 * 1024


def _silu(x):
    return x * jax.nn.sigmoid(x)


def _softplus(x):
    return jnp.maximum(x, 0.0) + jnp.log1p(jnp.exp(-jnp.abs(x)))


def _gelu_tanh(x):
    return 0.5 * x * (1.0 + jnp.tanh(math.sqrt(2.0 / math.pi) * (x + 0.044715 * (x * x * x))))


def _rms(x):
    return x * lax.rsqrt(jnp.mean(x * x, axis=-1, keepdims=True) + EPS)


def _dot(a, b):
    return jnp.dot(a, b, preferred_element_type=F32)


def _tril_ones(n):
    r = lax.broadcasted_iota(jnp.int32, (n, n), 0)
    c = lax.broadcasted_iota(jnp.int32, (n, n), 1)
    return (r >= c).astype(F32)


def _cumsum_rows(x):
    return jnp.dot(_tril_ones(x.shape[0]), x, precision=lax.Precision.HIGHEST,
                   preferred_element_type=F32)


def _causal_conv(buf_ref, x, w_ref, b_ref, shift, width):
    rows = x.shape[0]
    off = buf_ref.shape[0] - rows
    n = (width - 1) * shift
    buf_ref[off:off + rows, :] = x
    acc = None
    for j in range(width):
        start = off - (width - 1 - j) * shift
        term = buf_ref[start:start + rows, :] * w_ref[j:j + 1, :]
        acc = term if acc is None else acc + term
    out = b_ref[...] + acc
    tail = buf_ref[off + rows - n:off + rows, :]
    buf_ref[off - n:off, :] = tail
    return out, tail


def _conv_off(shift, width):
    n = (width - 1) * shift
    return -(-n // SUBLANES) * SUBLANES


NORM_ROWS = 256


def _norm_rows_to(x_ref, nw_ref, xn_ref):
    rows = x_ref.shape[0]
    step = min(NORM_ROWS, rows)

    def body(i, carry):
        r = pl.multiple_of(i * step, step)
        xn_ref[pl.ds(r, step), :] = (_rms(x_ref[pl.ds(r, step), :]) * nw_ref[...]).astype(BF16)
        return carry

    lax.fori_loop(0, rows // step, body, 0)


def _mm_in_kernel(x_ref, nw_ref, w_ref, o_ref, xn_ref):
    @pl.when(pl.program_id(1) == 0)
    def _():
        _norm_rows_to(x_ref, nw_ref, xn_ref)
    o_ref[...] = _dot(xn_ref[...], w_ref[...].astype(BF16))


def _mm_in_vmem(bm, d, bn):
    return (bm * d * 4 + bm * d * 2 + 2 * d * bn * 4 + d * bn * 2 + 2 * bm * bn * 4
            + 4 * NORM_ROWS * d * 4 + _mib(6))


def mm_in(x, nw, w, bm, bn):
    m, d = x.shape
    n = w.shape[1]
    vmem = _mm_in_vmem(bm, d, bn)
    return pl.pallas_call(
        _mm_in_kernel,
        grid=(m // bm, n // bn),
        in_specs=[pl.BlockSpec((bm, d), lambda i, j: (i, 0), pipeline_mode=pl.Buffered(1)),
                  pl.BlockSpec((1, d), lambda i, j: (0, 0)),
                  pl.BlockSpec((d, bn), lambda i, j: (0, j))],
        out_specs=pl.BlockSpec((bm, bn), lambda i, j: (i, j)),
        out_shape=jax.ShapeDtypeStruct((m, n), F32),
        scratch_shapes=[pltpu.VMEM((bm, d), BF16)],
        compiler_params=_cparams(("parallel", "arbitrary"), vmem),
        name="mm_in",
    )(x, nw.reshape(1, d), w)


def _mm_in_t_kernel(offs_ref, x_ref, nw_ref, wt_ref, o_ref, xn_ref):
    del offs_ref
    @pl.when(pl.program_id(1) == 0)
    def _():
        _norm_rows_to(x_ref, nw_ref, xn_ref)
    o_ref[...] = lax.dot_general(xn_ref[...], wt_ref[...].astype(BF16), NT, preferred_element_type=F32)


ROW_ALIGN = 32


def mm_in_t(x, nw, wt, row_offsets, bm, bn):
    m, d = x.shape
    nblk = len(row_offsets)
    assert all(o % ROW_ALIGN == 0 for o in row_offsets)
    vmem = _mm_in_vmem(bm, d, bn)
    grid_spec = pltpu.PrefetchScalarGridSpec(
        num_scalar_prefetch=1,
        grid=(m // bm, nblk),
        in_specs=[pl.BlockSpec((bm, d), lambda i, j, offs: (i, 0), pipeline_mode=pl.Buffered(1)),
                  pl.BlockSpec((1, d), lambda i, j, offs: (0, 0)),
                  pl.BlockSpec((pl.Element(bn), pl.Element(d)),
                               lambda i, j, offs: (offs[j] * ROW_ALIGN, 0))],
        out_specs=pl.BlockSpec((bm, bn), lambda i, j, offs: (i, j)),
        scratch_shapes=[pltpu.VMEM((bm, d), BF16)])
    return pl.pallas_call(
        _mm_in_t_kernel,
        grid_spec=grid_spec,
        out_shape=jax.ShapeDtypeStruct((m, nblk * bn), F32),
        compiler_params=_cparams(("parallel", "arbitrary"), vmem),
        name="mm_in_t",
    )(jnp.asarray([o // ROW_ALIGN for o in row_offsets], jnp.int32), x, nw.reshape(1, d), wt)


def _mm_out_kernel(*refs, nparts):
    a_refs = refs[:nparts]
    w_refs = refs[nparts:2 * nparts]
    r_ref = refs[2 * nparts]
    o_ref = refs[2 * nparts + 1]
    acc = r_ref[...]
    for a_ref, w_ref in zip(a_refs, w_refs):
        acc = acc + _dot(a_ref[...], w_ref[...].astype(BF16))
    o_ref[...] = acc


def mm_out(parts, w, layer, resid, bm, bn):
    nparts = len(parts)
    m, kp = parts[0].shape
    n = w.shape[2]
    vmem = nparts * (2 * bm * kp * 2 + 2 * kp * bn * 4 + kp * bn * 2) + 4 * bm * bn * 4 + 2 * bm * bn * 4 + _mib(6)
    in_specs = [pl.BlockSpec((bm, kp), lambda i, j: (i, 0)) for _ in range(nparts)]
    in_specs += [pl.BlockSpec((None, kp, bn), functools.partial(lambda i, j, p: (layer, p, j), p=p))
                 for p in range(nparts)]
    in_specs += [pl.BlockSpec((bm, bn), lambda i, j: (i, j))]
    return pl.pallas_call(
        functools.partial(_mm_out_kernel, nparts=nparts),
        grid=(m // bm, n // bn),
        in_specs=in_specs,
        out_specs=pl.BlockSpec((bm, bn), lambda i, j: (i, j)),
        out_shape=jax.ShapeDtypeStruct((m, n), F32),
        compiler_params=_cparams(("parallel", "parallel"), vmem),
        name="mm_out",
    )(*parts, *([w] * nparts), resid)


def _rmsnorm_kernel(x_ref, nw_ref, o_ref):
    o_ref[...] = _rms(x_ref[...]) * nw_ref[...]


def rmsnorm(x, nw, bm):
    m, d = x.shape
    return pl.pallas_call(
        _rmsnorm_kernel,
        grid=(m // bm,),
        in_specs=[pl.BlockSpec((bm, d), lambda i: (i, 0)), pl.BlockSpec((1, d), lambda i: (0, 0))],
        out_specs=pl.BlockSpec((bm, d), lambda i: (i, 0)),
        out_shape=jax.ShapeDtypeStruct((m, d), F32),
        compiler_params=_cparams(("parallel",), 6 * bm * d * 4 + _mib(4)),
        name="rmsnorm",
    )(x, nw.reshape(1, d))


FFN_SLAB = 512


def _ffn_up_kernel(x_ref, nw_ref, wg_ref, wu_ref, cw_ref, cb_ref, prev_ref,
                   act_ref, st_ref, xn_ref, gbuf_ref, carry_ref, *, shift, blocks_per_group):
    i = pl.program_id(0)
    j = pl.program_id(1)
    bm = x_ref.shape[0]
    n = (FFN_CONV - 1) * shift
    off = gbuf_ref.shape[0] - bm

    @pl.when(j == 0)
    def _():
        _norm_rows_to(x_ref, nw_ref, xn_ref)

    first = (i % blocks_per_group) == 0

    @pl.when(first)
    def _():
        gbuf_ref[off - n:off, :] = prev_ref[0]

    @pl.when(jnp.logical_not(first))
    def _():
        gbuf_ref[off - n:off, :] = carry_ref[j]

    wg = wg_ref[...].astype(BF16)
    wu = wu_ref[...].astype(BF16)
    slab = min(FFN_SLAB, bm)
    for s in range(bm // slab):
        r0 = s * slab
        xs = xn_ref[r0:r0 + slab, :]
        gbuf_ref[off + r0:off + r0 + slab, :] = _dot(xs, wg)
        u = _dot(xs, wu)
        gc = cb_ref[...]
        for t in range(FFN_CONV):
            start = off + r0 - (FFN_CONV - 1 - t) * shift
            gc = gc + gbuf_ref[start:start + slab, :] * cw_ref[t:t + 1, :]
        act_ref[r0:r0 + slab, :] = (_silu(gc) * u).astype(BF16)
    tail = gbuf_ref[off + bm - n:off + bm, :]
    carry_ref[j] = tail
    st_ref[0] = tail


def ffn_up(x, nw, w_up, layer, cw, cb, prev, shift, group_rows, bm, bf):
    m, d = x.shape
    f = cw.shape[1]
    n = (FFN_CONV - 1) * shift
    off = _conv_off(shift, FFN_CONV)
    bpg = group_rows // bm
    nf = f // bf
    vmem = (bm * d * 4 + bm * d * 2 + 4 * d * bf * 4 + 2 * d * bf * 2 + 2 * bm * bf * 2
            + (off + bm) * bf * 4 + nf * max(n, SUBLANES) * bf * 4 + 4 * n * bf * 4 + 5 * bm * bf * 4
            + 4 * NORM_ROWS * d * 4 + _mib(6))
    act, st = pl.pallas_call(
        functools.partial(_ffn_up_kernel, shift=shift, blocks_per_group=bpg),
        grid=(m // bm, nf),
        in_specs=[pl.BlockSpec((bm, d), lambda i, j: (i, 0), pipeline_mode=pl.Buffered(1)),
                  pl.BlockSpec((1, d), lambda i, j: (0, 0)),
                  pl.BlockSpec((None, d, bf), lambda i, j: (layer, 0, j)),
                  pl.BlockSpec((None, d, bf), lambda i, j: (layer, 0, j + nf)),
                  pl.BlockSpec((FFN_CONV, bf), lambda i, j: (0, j)),
                  pl.BlockSpec((1, bf), lambda i, j: (0, j)),
                  pl.BlockSpec((1, n, bf), lambda i, j: (i // bpg, 0, j))],
        out_specs=[pl.BlockSpec((bm, bf), lambda i, j: (i, j)),
                   pl.BlockSpec((1, n, bf), lambda i, j: (i, 0, j))],
        out_shape=[jax.ShapeDtypeStruct((m, f), BF16),
                   jax.ShapeDtypeStruct((m // bm, n, f), F32)],
        scratch_shapes=[pltpu.VMEM((bm, d), BF16),
                        pltpu.VMEM((off + bm, bf), F32),
                        pltpu.VMEM((nf, n, bf), F32)],
        compiler_params=_cparams(("arbitrary", "arbitrary"), vmem),
        name="ffn_up",
    )(x, nw.reshape(1, d), w_up, w_up, cw, cb.reshape(1, f), prev)
    return act, st[bpg - 1::bpg]


def _expand_heads64(x, nheads):
    r = x.shape[0]
    lo = lax.broadcasted_iota(jnp.int32, (r, LANES), 1) < SSD_HEAD_DIM
    tiles = []
    for p in range(nheads // 2):
        a0 = jnp.broadcast_to(x[:, 2 * p:2 * p + 1], (r, LANES))
        a1 = jnp.broadcast_to(x[:, 2 * p + 1:2 * p + 2], (r, LANES))
        tiles.append(jnp.where(lo, a0, a1))
    return jnp.concatenate(tiles, axis=1)


def _ssd_gates(xs_c, bc_c, dt_raw, dtb, alog):
    xs = _silu(xs_c)
    bc = _silu(bc_c)
    dt = _softplus(dt_raw + dtb)
    la = dt * (-jnp.exp(alog))
    return xs, bc, dt, la


def _ssd_finish(o, xs, z, dx, nw):
    y = (o + dx * xs) * _silu(z)
    gw = D_SSD // SSD_GROUPS
    y = jnp.concatenate([_rms(y[:, g * gw:(g + 1) * gw]) for g in range(SSD_GROUPS)], axis=1)
    return y * nw


def _ssd_prompt_kernel(z_ref, xs_ref, bc_ref, dt_ref, cwx_ref, cwb_ref, cbx_ref, cbb_ref,
                       dtb_ref, alog_ref, dx_ref, nw_ref,
                       y_ref, sfin_ref, cfin_ref, s_ref, bufx_ref, bufb_ref):
    c = pl.program_id(1)
    nc = pl.num_programs(1)
    rows = xs_ref.shape[0]
    gw = D_SSD // SSD_GROUPS
    hpg = SSD_HEADS // SSD_GROUPS

    @pl.when(c == 0)
    def _():
        s_ref[...] = jnp.zeros_like(s_ref)
        bufx_ref[0:SUBLANES, :] = jnp.zeros((SUBLANES, D_SSD), F32)
        bufb_ref[0:SUBLANES, :] = jnp.zeros((SUBLANES, SSD_BC), F32)

    yield
    xs_c, tailx = _causal_conv(bufx_ref, xs_ref[...], cwx_ref, cbx_ref, 1, SSD_CONV)
    bc_c, tailb = _causal_conv(bufb_ref, bc_ref[...], cwb_ref, cbb_ref, 1, SSD_CONV)
    xs, bc, dt, la = _ssd_gates(xs_c, bc_c, dt_ref[...], dtb_ref[...], alog_ref[...])
    cum = _cumsum_rows(la * LOG2E)
    cum_t = cum.T
    cumx = _expand_heads64(cum, SSD_HEADS)
    dtx = _expand_heads64(dt, SSD_HEADS)
    lastx = cumx[rows - 1:rows, :]
    ecum = jnp.exp2(cumx)
    wx = jnp.exp2(lastx - cumx)
    elast = jnp.exp2(lastx)
    v_all = xs * dtx
    vw_all = v_all * wx

    ri = lax.broadcasted_iota(jnp.int32, (rows, rows), 0)
    ci = lax.broadcasted_iota(jnp.int32, (rows, rows), 1)
    causal = ri >= ci
    lo = lax.broadcasted_iota(jnp.int32, (rows, LANES), 1) < SSD_HEAD_DIM

    o_groups = []
    for g in range(SSD_GROUPS):
        kb = bc[:, g * SSD_STATE:(g + 1) * SSD_STATE].astype(BF16)
        qb = bc[:, (SSD_GROUPS + g) * SSD_STATE:(SSD_GROUPS + g + 1) * SSD_STATE].astype(BF16)
        qk = lax.dot_general(qb, kb, NT, preferred_element_type=F32)
        o_tiles = []
        for p in range(hpg // 2):
            acc = None
            for q in range(2):
                h = g * hpg + 2 * p + q
                diff = cum[:, h:h + 1] - cum_t[h:h + 1, :]
                dec = jnp.exp2(jnp.where(causal, diff, -1e30))
                pm = (qk * dec).astype(BF16)
                col = g * gw + p * LANES
                vp = v_all[:, col:col + LANES]
                vh = jnp.where(lo, vp, 0.0) if q == 0 else jnp.where(lo, 0.0, vp)
                t = _dot(pm, vh.astype(BF16))
                acc = t if acc is None else acc + t
            o_tiles.append(acc)
        o_intra = jnp.concatenate(o_tiles, axis=1)
        sg = s_ref[g]
        sl = slice(g * gw, (g + 1) * gw)
        o_inter = _dot(qb, sg.astype(BF16)) * ecum[:, sl]
        upd = lax.dot_general(kb, vw_all[:, sl].astype(BF16), TN, preferred_element_type=F32)
        s_ref[g] = sg * elast[:, sl] + upd
        o_groups.append(o_intra + o_inter)
    o = jnp.concatenate(o_groups, axis=1)
    y_ref[...] = _ssd_finish(o, xs, z_ref[...], dx_ref[...], nw_ref[...]).astype(BF16)

    yield
    @pl.when(c == nc - 1)
    def _():
        for h in range(SSD_HEADS):
            g, hl = divmod(h, hpg)
            sfin_ref[0, h] = s_ref[g, :, hl * SSD_HEAD_DIM:(hl + 1) * SSD_HEAD_DIM]
        cfin_ref[0, :, 0:D_SSD] = tailx
        cfin_ref[0, :, D_SSD:D_SSD + SSD_BC] = tailb


def ssd_prompt(pe, prm, nseq, seq, chunk):
    m = nseq * seq
    nc = seq // chunk
    row = lambda b, c: b * nc + c
    full = lambda shp: pl.BlockSpec(shp, lambda b, c: (0,) * len(shp))
    in_specs = [
        pl.BlockSpec((chunk, D_SSD), lambda b, c: (row(b, c), 0)),
        pl.BlockSpec((chunk, D_SSD), lambda b, c: (row(b, c), 1)),
        pl.BlockSpec((chunk, SSD_BC), lambda b, c: (row(b, c), 10)),
        pl.BlockSpec((chunk, LANES), lambda b, c: (row(b, c), DT_COL // LANES)),
        pl.BlockSpec((SSD_CONV, D_SSD), lambda b, c: (0, 0)),
        pl.BlockSpec((SSD_CONV, SSD_BC), lambda b, c: (0, 2)),
        pl.BlockSpec((1, D_SSD), lambda b, c: (0, 0)),
        pl.BlockSpec((1, SSD_BC), lambda b, c: (0, 2)),
        full((1, LANES)), full((1, LANES)), full((1, D_SSD)), full((1, D_SSD)),
    ]
    return dict(
        kernel=_ssd_prompt_kernel,
        arrays=[pe, pe, pe, pe, prm["ssd_cw"], prm["ssd_cw"], prm["ssd_cb"], prm["ssd_cb"],
                prm["ssd_dtb"], prm["ssd_alog"], prm["ssd_dx"], prm["ssd_nw"]],
        in_specs=in_specs,
        out_specs=[pl.BlockSpec((chunk, D_SSD), lambda b, c: (row(b, c), 0)),
                   pl.BlockSpec((1, SSD_HEADS, SSD_STATE, SSD_HEAD_DIM), lambda b, c: (b, 0, 0, 0)),
                   pl.BlockSpec((1, SSD_CONV - 1, D_SSD + SSD_BC), lambda b, c: (b, 0, 0))],
        out_shape=[jax.ShapeDtypeStruct((m, D_SSD), BF16),
                   jax.ShapeDtypeStruct((nseq, SSD_HEADS, SSD_STATE, SSD_HEAD_DIM), F32),
                   jax.ShapeDtypeStruct((nseq, SSD_CONV - 1, D_SSD + SSD_BC), F32)],
        scratch=[pltpu.VMEM((SSD_GROUPS, SSD_STATE, D_SSD // SSD_GROUPS), F32),
                 pltpu.VMEM((SUBLANES + chunk, D_SSD), F32),
                 pltpu.VMEM((SUBLANES + chunk, SSD_BC), F32)])


MIX_CHUNK = 128


def _run_parts(parts, grid, name, vmem):
    counts = [(len(p["arrays"]), len(p["out_shape"]), len(p["scratch"])) for p in parts]
    n_in = sum(c[0] for c in counts)
    n_out = sum(c[1] for c in counts)

    def body(*refs):
        i, o, s = 0, n_in, n_in + n_out
        gens = []
        for p, (a, b, c) in zip(parts, counts):
            gens.append(p["kernel"](*refs[i:i + a], *refs[o:o + b], *refs[s:s + c]))
            i, o, s = i + a, o + b, s + c
        for _ in range(3):
            for g in gens:
                next(g, None)

    outs = pl.pallas_call(
        body,
        grid=grid,
        in_specs=[sp for p in parts for sp in p["in_specs"]],
        out_specs=[sp for p in parts for sp in p["out_specs"]],
        out_shape=[sh for p in parts for sh in p["out_shape"]],
        scratch_shapes=[sc for p in parts for sc in p["scratch"]],
        compiler_params=_cparams(("parallel", "arbitrary"), vmem),
        name=name,
    )(*[a for p in parts for a in p["arrays"]])
    res, k = [], 0
    for _, b, _ in counts:
        res.append(outs[k:k + b])
        k += b
    return res


def _rotary(x, cos, sin_signed):
    return x * cos + pltpu.roll(x, RET_DK // 2, 1) * sin_signed


def _ret_prompt_kernel(q_ref, k_ref, v_ref, g_ref, cos_ref, sin_ref, o_ref, sfin_ref, s_ref):
    c = pl.program_id(1)
    nc = pl.num_programs(1)
    rows = q_ref.shape[0]

    @pl.when(c == 0)
    def _():
        s_ref[...] = jnp.zeros_like(s_ref)

    yield
    cos = cos_ref[...]
    sin = sin_ref[...]
    ri = lax.broadcasted_iota(jnp.int32, (rows, rows), 0)
    ci = lax.broadcasted_iota(jnp.int32, (rows, rows), 1)
    dij = jnp.where(ri >= ci, (ri - ci).astype(F32), 1e30)
    tk = lax.broadcasted_iota(jnp.int32, (rows, RET_DK), 0).astype(F32)
    tv = lax.broadcasted_iota(jnp.int32, (rows, RET_DV), 0).astype(F32)
    for h in range(RET_HEADS):
        lg = RET_LOG_GAMMA[h]
        lg2 = lg * LOG2E
        qr = _rotary(q_ref[:, h * RET_DK:(h + 1) * RET_DK], cos, sin)
        kr = _rotary(k_ref[:, h * RET_DK:(h + 1) * RET_DK], cos, sin) * (RET_DK ** -0.5)
        qb = qr.astype(BF16)
        s = lax.dot_general(qb, kr.astype(BF16), NT, preferred_element_type=F32)
        pm = (s * jnp.exp2(lg2 * dij)).astype(BF16)
        vb = v_ref[:, h * RET_DV:(h + 1) * RET_DV].astype(BF16)
        sh = s_ref[h]
        o = _dot(pm, vb) + _dot(qb, sh.astype(BF16)) * jnp.exp2(lg2 * (tv + 1.0))
        kw = (kr * jnp.exp2(lg2 * ((rows - 1.0) - tk))).astype(BF16)
        s_ref[h] = sh * math.exp(lg * rows) + lax.dot_general(kw, vb, TN, preferred_element_type=F32)
        gh = g_ref[:, h * RET_DV:(h + 1) * RET_DV]
        o_ref[:, h * RET_DV:(h + 1) * RET_DV] = (_rms(o) * _silu(gh)).astype(BF16)

    yield
    @pl.when(c == nc - 1)
    def _():
        sfin_ref[0] = s_ref[...]


def ret_prompt(pe, cos, sin, nseq, seq, chunk):
    m = nseq * seq
    nc = seq // chunk
    row = lambda b, c: b * nc + c
    return dict(
        kernel=_ret_prompt_kernel,
        arrays=[pe, pe, pe, pe, cos, sin],
        in_specs=[pl.BlockSpec((chunk, RET_QK), lambda b, c: (row(b, c), 8)),
                  pl.BlockSpec((chunk, RET_QK), lambda b, c: (row(b, c), 9)),
                  pl.BlockSpec((chunk, D_RET), lambda b, c: (row(b, c), 2)),
                  pl.BlockSpec((chunk, D_RET), lambda b, c: (row(b, c), 3)),
                  pl.BlockSpec((chunk, RET_DK), lambda b, c: (c, 0)),
                  pl.BlockSpec((chunk, RET_DK), lambda b, c: (c, 0))],
        out_specs=[pl.BlockSpec((chunk, D_RET), lambda b, c: (row(b, c), 0)),
                   pl.BlockSpec((1, RET_HEADS, RET_DK, RET_DV), lambda b, c: (b, 0, 0, 0))],
        out_shape=[jax.ShapeDtypeStruct((m, D_RET), BF16),
                   jax.ShapeDtypeStruct((nseq, RET_HEADS, RET_DK, RET_DV), F32)],
        scratch=[pltpu.VMEM((RET_HEADS, RET_DK, RET_DV), F32)])


HG_HB = HG_HEADS


def _hg_gates(hq, hf, lb):
    q = _silu(hq)
    f = lb + (1.0 - lb) * jax.nn.sigmoid(hf)
    return q, f, 1.0 - f, jnp.log(f)


def _hg_tables(rows):
    r = np.arange(rows)[:, None]
    t = np.arange(rows)[None, :]
    sums = [t <= r]
    masks = []
    s = rows // 2
    while s >= 1:
        blk, pos = r // (2 * s), r % (2 * s)
        ref = blk * 2 * s + s - 1
        upper = pos >= s
        sums.append(np.where(upper, (t > ref) & (t <= r), (t > r) & (t <= ref)))
        masks.append((blk == t // (2 * s)) & upper & (t % (2 * s) < s))
        s //= 2
    masks.append(r == t)
    return (np.concatenate(sums, axis=0).astype(np.float32),
            np.stack(masks).astype(np.float32))


def _hg_prompt_kernel(hq_ref, hf_ref, hi_ref, hg_ref, lb_ref, nw_ref, sums_ref, masks_ref,
                      og_ref, sfin_ref, st_ref):
    c = pl.program_id(1)
    nc = pl.num_programs(1)
    rows = hq_ref.shape[0]
    nlev = masks_ref.shape[0] - 1

    @pl.when(c == 0)
    def _():
        st_ref[...] = jnp.zeros_like(st_ref)

    yield
    q_all, _, kk_all, lg_all = _hg_gates(hq_ref[...], hf_ref[...], lb_ref[...])
    lg_all = lg_all * LOG2E
    lg_hi = lg_all.astype(BF16)
    lg_lo = (lg_all - lg_hi.astype(F32)).astype(BF16)
    sums = sums_ref[...]
    dall = _dot(sums, lg_hi) + _dot(sums, lg_lo)
    cum_all = dall[0:rows]
    ecum_all = jnp.exp2(cum_all)
    for h in range(HG_HB):
        sl = slice(h * HG_DK, (h + 1) * HG_DK)
        q = q_all[:, sl]
        kk = kk_all[:, sl]
        cum = cum_all[:, sl]
        vb = hi_ref[:, sl].astype(BF16)
        last = cum[rows - 1:rows, :]
        st = st_ref[h]
        o = lax.dot_general((q * ecum_all[:, sl]).astype(BF16), st.astype(BF16), NT,
                            preferred_element_type=F32)
        kt = (kk * jnp.exp2(last - cum)).astype(BF16)
        st_ref[h] = st * jnp.exp2(last) + lax.dot_general(vb, kt, TN, preferred_element_type=F32)
        a = masks_ref[nlev] * lax.dot_general(q.astype(BF16), kk.astype(BF16), NT,
                                              preferred_element_type=F32)
        for l in range(nlev):
            e = jnp.exp2(dall[(l + 1) * rows:(l + 2) * rows, sl])
            a = a + masks_ref[l] * lax.dot_general((q * e).astype(BF16), (kk * e).astype(BF16), NT,
                                                   preferred_element_type=F32)
        o = o + _dot(a.astype(BF16), vb)
        og_ref[:, sl] = (_rms(o) * nw_ref[...] * _silu(hg_ref[:, sl])).astype(BF16)

    yield
    @pl.when(c == nc - 1)
    def _():
        for h in range(HG_HB):
            sfin_ref[0, h] = st_ref[h].T


def hg_prompt(po, lb, nw, nseq, seq, chunk):
    m = nseq * seq
    nc = seq // chunk
    row = lambda b, c: b * nc + c
    sums_np, masks_np = _hg_tables(chunk)
    sums = jnp.asarray(sums_np, BF16)
    masks = jnp.asarray(masks_np, F32)
    return dict(
        kernel=_hg_prompt_kernel,
        arrays=[po, po, po, po, lb, nw, sums, masks],
        in_specs=[pl.BlockSpec((chunk, D_HG), lambda b, c: (row(b, c), 0)),
                  pl.BlockSpec((chunk, D_HG), lambda b, c: (row(b, c), 1)),
                  pl.BlockSpec((chunk, D_HG), lambda b, c: (row(b, c), 2)),
                  pl.BlockSpec((chunk, D_HG), lambda b, c: (row(b, c), 3)),
                  pl.BlockSpec((1, D_HG), lambda b, c: (0, 0)),
                  pl.BlockSpec((1, HG_DV), lambda b, c: (0, 0)),
                  pl.BlockSpec(sums.shape, lambda b, c: (0, 0)),
                  pl.BlockSpec(masks.shape, lambda b, c: (0, 0, 0))],
        out_specs=[pl.BlockSpec((chunk, D_HG), lambda b, c: (row(b, c), 0)),
                   pl.BlockSpec((1, HG_HEADS, HG_DK, HG_DV), lambda b, c: (b, 0, 0, 0))],
        out_shape=[jax.ShapeDtypeStruct((m, D_HG), BF16),
                   jax.ShapeDtypeStruct((nseq, HG_HEADS, HG_DK, HG_DV), F32)],
        scratch=[pltpu.VMEM((HG_HEADS, HG_DV, HG_DK), F32)])


def _lin_scan(a, b, h0):
    rows, cols = a.shape
    groups = rows // SUBLANES
    a3 = a.reshape(groups, SUBLANES, cols)
    b3 = b.reshape(groups, SUBLANES, cols)
    si = lax.broadcasted_iota(jnp.int32, a3.shape, 1)
    d = 1
    while d < SUBLANES:
        keep = si >= d
        a_s = jnp.where(keep, pltpu.roll(a3, d, 1), 1.0)
        b_s = jnp.where(keep, pltpu.roll(b3, d, 1), 0.0)
        b3 = a3 * b_s + b3
        a3 = a3 * a_s
        d *= 2
    h_in = jnp.broadcast_to(h0, (SUBLANES, cols))
    out = []
    for g in range(groups):
        hg = a3[g] * h_in + b3[g]
        out.append(hg)
        h_in = jnp.broadcast_to(hg[SUBLANES - 1:SUBLANES, :], (SUBLANES, cols))
    return jnp.concatenate(out, axis=0), h_in[0:1, :]


def _lru_gates(xr, wr_ref, br, wi_ref, bi, ap):
    xb = xr.astype(BF16)
    r_parts, i_parts = [], []
    for n in range(LRU_BLOCKS):
        xn = xb[:, n * LRU_BW:(n + 1) * LRU_BW]
        r_parts.append(_dot(xn, wr_ref[n].astype(BF16)))
        i_parts.append(_dot(xn, wi_ref[n].astype(BF16)))
    r = jax.nn.sigmoid(jnp.concatenate(r_parts, axis=1) + br)
    gi = jax.nn.sigmoid(jnp.concatenate(i_parts, axis=1) + bi)
    la = -LRU_C * r * _softplus(-ap)
    a = jnp.exp(la)
    th = jnp.tanh(la)
    mult = jnp.sqrt(-2.0 * th / (1.0 - th))
    return a, mult, gi


def _lru_prompt_kernel(rx_ref, rg_ref, cw_ref, cb_ref, wr_ref, br_ref, wi_ref, bi_ref, ap_ref,
                       yl_ref, hfin_ref, cfin_ref, buf_ref, hc_ref):
    c = pl.program_id(1)
    nc = pl.num_programs(1)
    rows = rx_ref.shape[0]

    @pl.when(c == 0)
    def _():
        buf_ref[0:SUBLANES, :] = jnp.zeros((SUBLANES, D_RNN), F32)
        hc_ref[...] = jnp.zeros_like(hc_ref)

    yield
    xr, tail = _causal_conv(buf_ref, rx_ref[...], cw_ref, cb_ref, 1, LRU_CONV)
    a, mult, gi = _lru_gates(xr, wr_ref, br_ref[...], wi_ref, bi_ref[...], ap_ref[...])
    ri = lax.broadcasted_iota(jnp.int32, (rows, D_RNN), 0)
    mult = jnp.where(jnp.logical_and(c == 0, ri == 0), 1.0, mult)
    hs, hlast = _lin_scan(a, mult * gi * xr, hc_ref[0:1, :])
    hc_ref[0:1, :] = hlast
    yl_ref[...] = (hs * _gelu_tanh(rg_ref[...])).astype(BF16)

    yield
    @pl.when(c == nc - 1)
    def _():
        hfin_ref[0] = hlast
        cfin_ref[0] = tail


def lru_prompt(po, prm, nseq, seq, chunk):
    m = nseq * seq
    nc = seq // chunk
    row = lambda b, c: b * nc + c
    full = lambda shp: pl.BlockSpec(shp, lambda b, c: (0,) * len(shp))
    return dict(
        kernel=_lru_prompt_kernel,
        arrays=[po, po, prm["lru_cw"], prm["lru_cb"], prm["lru_wr"], prm["lru_br"], prm["lru_wi"],
                prm["lru_bi"], prm["lru_ap"]],
        in_specs=[pl.BlockSpec((chunk, D_RNN), lambda b, c: (row(b, c), 4)),
                  pl.BlockSpec((chunk, D_RNN), lambda b, c: (row(b, c), 5)),
                  full((LRU_CONV, D_RNN)), full((1, D_RNN)),
                  full((LRU_BLOCKS, LRU_BW, LRU_BW)), full((1, D_RNN)),
                  full((LRU_BLOCKS, LRU_BW, LRU_BW)), full((1, D_RNN)), full((1, D_RNN))],
        out_specs=[pl.BlockSpec((chunk, D_RNN), lambda b, c: (row(b, c), 0)),
                   pl.BlockSpec((1, 1, D_RNN), lambda b, c: (b, 0, 0)),
                   pl.BlockSpec((1, LRU_CONV - 1, D_RNN), lambda b, c: (b, 0, 0))],
        out_shape=[jax.ShapeDtypeStruct((m, D_RNN), BF16),
                   jax.ShapeDtypeStruct((nseq, 1, D_RNN), F32),
                   jax.ShapeDtypeStruct((nseq, LRU_CONV - 1, D_RNN), F32)],
        scratch=[pltpu.VMEM((SUBLANES + chunk, D_RNN), F32),
                 pltpu.VMEM((SUBLANES, D_RNN), F32)])


TOKP = SUBLANES


def _ssd_state_kernel(q_ref, k_ref, vw_ref, dec_ref, st_ref, u_ref, snew_ref):
    hpg = SSD_HEADS // SSD_GROUPS
    gw = hpg * SSD_HEAD_DIM

    def body(b, carry):
        q = q_ref[b].astype(BF16)
        k = k_ref[b].astype(BF16)
        vw = vw_ref[b].astype(BF16)
        for g in range(SSD_GROUPS):
            ks = slice(g * SSD_STATE, (g + 1) * SSD_STATE)
            stg = st_ref[b, g * hpg:(g + 1) * hpg].reshape(gw, SSD_STATE)
            u_ref[b, :, g * gw:(g + 1) * gw] = lax.dot_general(
                q[:, ks], stg.astype(BF16), NT, preferred_element_type=F32)
            upd = lax.dot_general(vw[:, g * gw:(g + 1) * gw], k[:, ks], TN, preferred_element_type=F32)
            for hl in range(hpg):
                h = g * hpg + hl
                snew_ref[b, h] = (st_ref[b, h] * dec_ref[b, h:h + 1, :]
                                  + upd[hl * SSD_HEAD_DIM:(hl + 1) * SSD_HEAD_DIM, :])
        return carry

    lax.fori_loop(0, st_ref.shape[0], body, 0)


def _ret_state_kernel(q_ref, k_ref, v_ref, s_ref, u_ref, snew_ref, *, ntok):
    def body(b, carry):
        q = q_ref[b].astype(BF16)
        k = k_ref[b].astype(BF16)
        v = v_ref[b].astype(BF16)
        for h in range(RET_HEADS):
            ks = slice(h * RET_DK, (h + 1) * RET_DK)
            vs = slice(h * RET_DV, (h + 1) * RET_DV)
            s0 = s_ref[b, h]
            u_ref[b, :, vs] = _dot(q[:, ks], s0.astype(BF16))
            upd = lax.dot_general(k[:, ks], v[:, vs], TN, preferred_element_type=F32)
            snew_ref[b, h] = s0 * math.exp(RET_LOG_GAMMA[h] * ntok) + upd
        return carry

    lax.fori_loop(0, s_ref.shape[0], body, 0)


def _hg_state_kernel(q_ref, k_ref, v_ref, dcol_ref, s_ref, u_ref, snew_ref):
    def body(b, carry):
        q = q_ref[b].astype(BF16)
        k = k_ref[b].astype(BF16)
        v = v_ref[b].astype(BF16)
        dc = dcol_ref[b]
        for h in range(HG_HEADS):
            ks = slice(h * HG_DK, (h + 1) * HG_DK)
            s0 = s_ref[b, h]
            u_ref[b, :, ks] = _dot(q[:, ks], s0.astype(BF16))
            upd = lax.dot_general(k[:, ks], v[:, ks], TN, preferred_element_type=F32)
            snew_ref[b, h] = s0 * dc[:, h:h + 1] + upd
        return carry

    lax.fori_loop(0, s_ref.shape[0], body, 0)


def _state_part(kern, rows_in, extra, s, u_cols, nseq, nc):
    nb = s.shape[0]
    bb = nb // (nseq * nc)
    assert bb * nseq * nc == nb
    blk = lambda a: pl.BlockSpec((bb,) + a.shape[1:],
                                 lambda b, c: (b * nc + c,) + (0,) * (a.ndim - 1))
    ins = list(rows_in) + ([extra] if extra is not None else []) + [s]

    def phases(*refs):
        yield
        kern(*refs)
        yield

    return dict(kernel=phases, arrays=ins, in_specs=[blk(a) for a in ins],
                out_specs=[pl.BlockSpec((bb, TOKP, u_cols), lambda b, c: (b * nc + c, 0, 0)), blk(s)],
                out_shape=[jax.ShapeDtypeStruct((nb, TOKP, u_cols), F32),
                           jax.ShapeDtypeStruct(s.shape, F32)],
                scratch=[])


def _tok(x, t, nb):
    return x[t * nb:(t + 1) * nb]


def _head_sums(x, width):
    r, n = x.shape
    tiles = []
    for h in range(n // width):
        s = jnp.sum(x[:, h * width:(h + 1) * width], axis=1, keepdims=True)
        tiles.append(jnp.broadcast_to(s, (r, width)))
    return jnp.concatenate(tiles, axis=1)


def _ssd_sample_pre_kernel(xs_ref, bc_ref, dt_ref, prevx_ref, prevb_ref, cwx_ref, cwb_ref,
                           cbx_ref, cbb_ref, dtb_ref, alog_ref,
                           xs_out, bc_out, vw_out, oi_out, ecum_out, elast_out, tailx_out, tailb_out,
                           bufx_ref, bufb_ref, *, nb, ntok):
    n = (SSD_CONV - 1) * nb
    offx = bufx_ref.shape[0] - nb * ntok
    bufx_ref[offx - n:offx, :] = prevx_ref[...]
    bufb_ref[offx - n:offx, :] = prevb_ref[...]
    xs_c, tailx = _causal_conv(bufx_ref, xs_ref[...], cwx_ref, cbx_ref, nb, SSD_CONV)
    bc_c, tailb = _causal_conv(bufb_ref, bc_ref[...], cwb_ref, cbb_ref, nb, SSD_CONV)
    xs, bc, dt, la = _ssd_gates(xs_c, bc_c, dt_ref[...], dtb_ref[...], alog_ref[...])
    tailx_out[...] = tailx
    tailb_out[...] = tailb
    xs_out[...] = xs
    bc_out[...] = bc
    cums = []
    for t in range(ntok):
        lt = _tok(la, t, nb)
        cums.append(lt if t == 0 else cums[-1] + lt)
    cumx = [_expand_heads64(cm, SSD_HEADS) for cm in cums]
    v = xs * _expand_heads64(dt, SSD_HEADS)
    kw = SSD_GROUPS * SSD_STATE
    for t in range(ntok):
        ct = _tok(bc, t, nb)[:, kw:2 * kw]
        acc = None
        for t2 in range(t + 1):
            bt = _tok(bc, t2, nb)[:, 0:kw]
            sc = _head_sums(ct * bt, SSD_STATE)
            scx = jnp.concatenate(
                [jnp.concatenate([sc[:, g * SSD_STATE:(g + 1) * SSD_STATE]] * 4, axis=1)
                 for g in range(SSD_GROUPS)], axis=1)
            term = scx * jnp.exp(cumx[t] - cumx[t2]) * _tok(v, t2, nb)
            acc = term if acc is None else acc + term
        oi_out[t * nb:(t + 1) * nb, :] = acc
        ecum_out[t * nb:(t + 1) * nb, :] = jnp.exp(cumx[t])
        vw_out[t * nb:(t + 1) * nb, :] = _tok(v, t, nb) * jnp.exp(cumx[ntok - 1] - cumx[t])
    elast_out[...] = jnp.exp(cums[ntok - 1])


def _ssd_sample_post_kernel(oi_ref, u_ref, ecum_ref, xs_ref, z_ref, dx_ref, nw_ref, y_ref):
    o = oi_ref[...] + ecum_ref[...] * u_ref[...]
    y_ref[...] = _ssd_finish(o, xs_ref[...], z_ref[...], dx_ref[...], nw_ref[...]).astype(BF16)


def _ret_sample_pre_kernel(q_ref, k_ref, v_ref, cos_ref, sin_ref, qd_out, kd_out, oi_out, *, nb, ntok):
    rows = nb * ntok
    cos = jnp.concatenate([jnp.broadcast_to(cos_ref[t:t + 1, :], (nb, RET_DK)) for t in range(ntok)], axis=0)
    sin = jnp.concatenate([jnp.broadcast_to(sin_ref[t:t + 1, :], (nb, RET_DK)) for t in range(ntok)], axis=0)
    qr, kr = [], []
    for h in range(RET_HEADS):
        sl = slice(h * RET_DK, (h + 1) * RET_DK)
        qr.append(_rotary(q_ref[:, sl], cos, sin))
        kr.append(_rotary(k_ref[:, sl], cos, sin) * (RET_DK ** -0.5))
    qr = jnp.concatenate(qr, axis=1)
    kr = jnp.concatenate(kr, axis=1)
    v = v_ref[...]
    for t in range(ntok):
        qt = _tok(qr, t, nb)
        acc = None
        for t2 in range(t + 1):
            sc = _head_sums(qt * _tok(kr, t2, nb), RET_DK)
            vt = _tok(v, t2, nb)
            tiles = []
            for h in range(RET_HEADS):
                dec = math.exp(RET_LOG_GAMMA[h] * (t - t2))
                s = sc[:, h * RET_DK:(h + 1) * RET_DK] * dec
                tiles.append(jnp.concatenate([s, s], axis=1) * vt[:, h * RET_DV:(h + 1) * RET_DV])
            term = jnp.concatenate(tiles, axis=1)
            acc = term if acc is None else acc + term
        oi_out[t * nb:(t + 1) * nb, :] = acc
        qd = jnp.concatenate([qt[:, h * RET_DK:(h + 1) * RET_DK] * math.exp(RET_LOG_GAMMA[h] * (t + 1))
                              for h in range(RET_HEADS)], axis=1)
        kt = _tok(kr, t, nb)
        kd = jnp.concatenate([kt[:, h * RET_DK:(h + 1) * RET_DK] * math.exp(RET_LOG_GAMMA[h] * (ntok - 1 - t))
                              for h in range(RET_HEADS)], axis=1)
        qd_out[t * nb:(t + 1) * nb, :] = qd
        kd_out[t * nb:(t + 1) * nb, :] = kd


def _ret_sample_post_kernel(oi_ref, u_ref, g_ref, o_ref):
    o = oi_ref[...] + u_ref[...]
    g = g_ref[...]
    for h in range(RET_HEADS):
        sl = slice(h * RET_DV, (h + 1) * RET_DV)
        o_ref[:, sl] = (_rms(o[:, sl]) * _silu(g[:, sl])).astype(BF16)


def _hg_sample_pre_kernel(hq_ref, hf_ref, hi_ref, lb_ref, qd_out, kd_out, dl_out, oi_out, *, nb, ntok):
    q, _, kk, lg = _hg_gates(hq_ref[...], hf_ref[...], lb_ref[...])
    v = hi_ref[...]
    cums = []
    for t in range(ntok):
        lt = _tok(lg, t, nb)
        cums.append(lt if t == 0 else cums[-1] + lt)
    for t in range(ntok):
        qt = _tok(q, t, nb)
        acc = None
        for t2 in range(t + 1):
            w = qt * _tok(kk, t2, nb)
            if t2 < t:
                w = w * jnp.exp(cums[t] - cums[t2])
            term = _head_sums(w, HG_DK) * _tok(v, t2, nb)
            acc = term if acc is None else acc + term
        oi_out[t * nb:(t + 1) * nb, :] = acc
        qd_out[t * nb:(t + 1) * nb, :] = qt * jnp.exp(cums[t])
        kd_out[t * nb:(t + 1) * nb, :] = _tok(kk, t, nb) * jnp.exp(cums[ntok - 1] - cums[t])
    dl_out[...] = jnp.exp(cums[ntok - 1])


def _hg_sample_post_kernel(oi_ref, u_ref, hg_ref, nw_ref, og_ref):
    o = oi_ref[...] + u_ref[...]
    hg = hg_ref[...]
    for h in range(HG_HEADS):
        sl = slice(h * HG_DV, (h + 1) * HG_DV)
        og_ref[:, sl] = (_rms(o[:, sl]) * nw_ref[...] * _silu(hg[:, sl])).astype(BF16)


def _lru_sample_kernel(rx_ref, rg_ref, prev_ref, h0_ref, cw_ref, cb_ref, wr_ref, br_ref, wi_ref,
                       bi_ref, ap_ref, yl_out, hfin_out, tail_out, buf_ref, *, nb, ntok):
    rows = nb * ntok
    n = (LRU_CONV - 1) * nb
    off = buf_ref.shape[0] - rows
    buf_ref[off - n:off, :] = prev_ref[...]
    xr, tail = _causal_conv(buf_ref, rx_ref[...], cw_ref, cb_ref, nb, LRU_CONV)
    tail_out[...] = tail
    a, mult, gi = _lru_gates(xr, wr_ref, br_ref[...], wi_ref, bi_ref[...], ap_ref[...])
    b = mult * gi * xr
    h = h0_ref[...]
    for t in range(ntok):
        h = _tok(a, t, nb) * h + _tok(b, t, nb)
        yl_out[t * nb:(t + 1) * nb, :] = (h * _gelu_tanh(rg_ref[t * nb:(t + 1) * nb, :])).astype(BF16)
    hfin_out[...] = h


def _whole(shape):
    return pl.BlockSpec(shape, lambda i: (0,) * len(shape))


def _colblock(rows, width, idx):
    return pl.BlockSpec((rows, width), functools.partial(lambda i, k: (0, k), k=idx))


def _call_whole(kern, in_arrays, in_specs, out_shapes, scratch, name, vmem):
    return pl.pallas_call(
        kern, grid=(1,), in_specs=in_specs,
        out_specs=[_whole(s.shape) for s in out_shapes],
        out_shape=out_shapes, scratch_shapes=scratch,
        compiler_params=_cparams(("arbitrary",), vmem), name=name,
    )(*in_arrays)


def _to_token_major(s):
    nb, w, c = s.shape
    return jnp.transpose(s, (1, 0, 2)).reshape(w * nb, c)


def _from_token_major(x, nb):
    w = x.shape[0] // nb
    return jnp.transpose(x.reshape(w, nb, x.shape[1]), (1, 0, 2))


def _rows_to_batch(x, nb):
    ntok = x.shape[0] // nb
    x = jnp.transpose(x.reshape(ntok, nb, x.shape[1]), (1, 0, 2))
    return jnp.pad(x, ((0, 0), (0, TOKP - ntok), (0, 0)))


def _batch_to_rows(u, ntok):
    nb = u.shape[0]
    return jnp.transpose(u[:, :ntok], (1, 0, 2)).reshape(ntok * nb, u.shape[2])


def ssd_sample_pre(pe, s_ssm, s_conv, prm, nb, ntok, nseq, nc):
    rows = nb * ntok
    n = (SSD_CONV - 1) * nb
    prev = _to_token_major(s_conv)
    off = _conv_off(nb, SSD_CONV)
    f = lambda *shape: jax.ShapeDtypeStruct(shape, F32)
    outs = _call_whole(
        functools.partial(_ssd_sample_pre_kernel, nb=nb, ntok=ntok),
        [pe, pe, pe, prev, prev, prm["ssd_cw"], prm["ssd_cw"], prm["ssd_cb"], prm["ssd_cb"],
         prm["ssd_dtb"], prm["ssd_alog"]],
        [_colblock(rows, D_SSD, 1), _colblock(rows, SSD_BC, 10), _colblock(rows, LANES, DT_COL // LANES),
         _colblock(n, D_SSD, 0), _colblock(n, SSD_BC, 2),
         _colblock(SSD_CONV, D_SSD, 0), _colblock(SSD_CONV, SSD_BC, 2),
         _colblock(1, D_SSD, 0), _colblock(1, SSD_BC, 2), _whole((1, LANES)), _whole((1, LANES))],
        [f(rows, D_SSD), f(rows, SSD_BC), f(rows, D_SSD), f(rows, D_SSD), f(rows, D_SSD),
         f(nb, LANES), f(n, D_SSD), f(n, SSD_BC)],
        [pltpu.VMEM((off + rows, D_SSD), F32), pltpu.VMEM((off + rows, SSD_BC), F32)],
        "ssd_sample_pre", _mib(56))
    xs, bc, vw, oi, ecum, elast, tailx, tailb = outs
    kw = SSD_GROUPS * SSD_STATE
    dec = jnp.broadcast_to(elast[:, :SSD_HEADS, None], (nb, SSD_HEADS, SSD_STATE))
    st = jnp.swapaxes(s_ssm, -1, -2)
    part = _state_part(_ssd_state_kernel,
                       [_rows_to_batch(bc[:, kw:], nb), _rows_to_batch(bc[:, :kw], nb),
                        _rows_to_batch(vw, nb)], dec, st, D_SSD, nseq, nc)
    return dict(oi=oi, ecum=ecum, xs=xs, tailx=tailx, tailb=tailb), part


def ssd_sample_post(ctx, u, st_new, pe, prm, nb, ntok):
    rows = nb * ntok
    s_new = jnp.swapaxes(st_new, -1, -2)
    u = _batch_to_rows(u, ntok)
    (y,) = _call_whole(
        _ssd_sample_post_kernel, [ctx["oi"], u, ctx["ecum"], ctx["xs"], pe, prm["ssd_dx"], prm["ssd_nw"]],
        [_whole((rows, D_SSD))] * 4 + [_colblock(rows, D_SSD, 0), _whole((1, D_SSD)), _whole((1, D_SSD))],
        [jax.ShapeDtypeStruct((rows, D_SSD), BF16)], [], "ssd_sample_post", _mib(48))
    conv_new = _from_token_major(jnp.concatenate([ctx["tailx"], ctx["tailb"]], axis=1), nb)
    return y, s_new, conv_new


def ret_sample_pre(pe, s_ret, cos, sin, nb, ntok, nseq, nc):
    rows = nb * ntok
    f = lambda *shape: jax.ShapeDtypeStruct(shape, F32)
    qd, kd, oi = _call_whole(
        functools.partial(_ret_sample_pre_kernel, nb=nb, ntok=ntok),
        [pe, pe, pe, cos, sin],
        [_colblock(rows, RET_QK, 8), _colblock(rows, RET_QK, 9), _colblock(rows, D_RET, 2),
         _whole(cos.shape), _whole(sin.shape)],
        [f(rows, RET_QK), f(rows, RET_QK), f(rows, D_RET)], [], "ret_sample_pre", _mib(48))
    v = _rows_to_batch(pe[:, 2 * D_RET:3 * D_RET], nb)
    part = _state_part(functools.partial(_ret_state_kernel, ntok=ntok),
                       [_rows_to_batch(qd, nb), _rows_to_batch(kd, nb), v], None, s_ret,
                       D_RET, nseq, nc)
    return oi, part


def ret_sample_post(oi, u, pe, nb, ntok):
    rows = nb * ntok
    u = _batch_to_rows(u, ntok)
    (o,) = _call_whole(
        _ret_sample_post_kernel, [oi, u, pe],
        [_whole((rows, D_RET)), _whole((rows, D_RET)), _colblock(rows, D_RET, 3)],
        [jax.ShapeDtypeStruct((rows, D_RET), BF16)], [], "ret_sample_post", _mib(40))
    return o


def hg_sample_pre(po, s_hg, lb, nb, ntok, nseq, nc):
    rows = nb * ntok
    f = lambda *shape: jax.ShapeDtypeStruct(shape, F32)
    qd, kd, dl, oi = _call_whole(
        functools.partial(_hg_sample_pre_kernel, nb=nb, ntok=ntok),
        [po, po, po, lb],
        [_colblock(rows, D_HG, 0), _colblock(rows, D_HG, 1), _colblock(rows, D_HG, 2), _whole((1, D_HG))],
        [f(rows, D_HG), f(rows, D_HG), f(nb, D_HG), f(rows, D_HG)], [], "hg_sample_pre", _mib(48))
    dcol = jnp.pad(jnp.transpose(dl.reshape(nb, HG_HEADS, HG_DK), (0, 2, 1)),
                   ((0, 0), (0, 0), (0, LANES - HG_HEADS)))
    v = _rows_to_batch(po[:, 2 * D_HG:3 * D_HG], nb)
    part = _state_part(_hg_state_kernel, [_rows_to_batch(qd, nb), _rows_to_batch(kd, nb), v],
                       dcol, s_hg, D_HG, nseq, nc)
    return oi, part


def hg_sample_post(oi, u, po, nw, nb, ntok):
    rows = nb * ntok
    u = _batch_to_rows(u, ntok)
    (og,) = _call_whole(
        _hg_sample_post_kernel, [oi, u, po, nw],
        [_whole((rows, D_HG)), _whole((rows, D_HG)), _colblock(rows, D_HG, 3), _whole((1, HG_DV))],
        [jax.ShapeDtypeStruct((rows, D_HG), BF16)], [], "hg_sample_post", _mib(40))
    return og


def lru_sample(po, s_lru, s_lconv, prm, nb, ntok):
    rows = nb * ntok
    n = (LRU_CONV - 1) * nb
    off = _conv_off(nb, LRU_CONV)
    prev = _to_token_major(s_lconv)
    yl, hfin, tail = _call_whole(
        functools.partial(_lru_sample_kernel, nb=nb, ntok=ntok),
        [po, po, prev, s_lru, prm["lru_cw"], prm["lru_cb"], prm["lru_wr"], prm["lru_br"], prm["lru_wi"],
         prm["lru_bi"], prm["lru_ap"]],
        [_colblock(rows, D_RNN, 4), _colblock(rows, D_RNN, 5), _whole((n, D_RNN)), _whole((nb, D_RNN)),
         _whole((LRU_CONV, D_RNN)), _whole((1, D_RNN)), _whole((LRU_BLOCKS, LRU_BW, LRU_BW)),
         _whole((1, D_RNN)), _whole((LRU_BLOCKS, LRU_BW, LRU_BW)), _whole((1, D_RNN)), _whole((1, D_RNN))],
        [jax.ShapeDtypeStruct((rows, D_RNN), BF16), jax.ShapeDtypeStruct((nb, D_RNN), F32),
         jax.ShapeDtypeStruct((n, D_RNN), F32)],
        [pltpu.VMEM((off + rows, D_RNN), F32)], "lru_sample", _mib(48))
    return yl, hfin, _from_token_major(tail, nb)


def _rope_tables(pos):
    half = RET_DK // 2
    inv = ROPE_BASE ** (-jnp.arange(half, dtype=F32) / half)
    ang = pos.astype(F32)[:, None] * inv[None, :]
    cos, sin = jnp.cos(ang), jnp.sin(ang)
    return jnp.concatenate([cos, cos], axis=1), jnp.concatenate([-sin, sin], axis=1)


def _pad_lanes(v):
    return jnp.pad(v.astype(F32), (0, LANES - v.shape[0])).reshape(1, LANES)


def _prepare(p):
    lbs = jnp.cumsum(jax.nn.softmax(p["hg_lower_bounds"].astype(F32), axis=0), axis=0)
    lbs = lbs - lbs[0]
    return {
        "even_wt": jnp.swapaxes(p["even_w_in"][0], 0, 1),
        "ssd_cw": p["ssd_conv_w"][0], "ssd_cb": p["ssd_conv_b"][0].reshape(1, -1),
        "ssd_dtb": _pad_lanes(p["ssd_dt_bias"][0]), "ssd_alog": _pad_lanes(p["ssd_A_log"][0]),
        "ssd_dx": jnp.repeat(p["ssd_D"][0], SSD_HEAD_DIM).reshape(1, D_SSD),
        "ssd_nw": p["ssd_norm_w"][0].reshape(1, D_SSD),
        "hg_lb": lbs[1].reshape(1, D_HG), "hg_nw": p["hg_norm_w"][0].reshape(1, HG_DV),
        "lru_cw": p["lru_conv_w"][0], "lru_cb": p["lru_conv_b"][0].reshape(1, D_RNN),
        "lru_wr": p["lru_w_r"][0], "lru_br": p["lru_b_r"][0].reshape(1, D_RNN),
        "lru_wi": p["lru_w_i"][0], "lru_bi": p["lru_b_i"][0].reshape(1, D_RNN),
        "lru_ap": p["lru_a_param"][0].reshape(1, D_RNN),
    }


def _even_row_offsets(bn):
    o_xbc, o_dt = D_SSD, 2 * D_SSD + SSD_BC
    o_q = o_dt + SSD_HEADS
    o_k, o_v = o_q + RET_QK, o_q + 2 * RET_QK
    o_g = o_v + D_RET
    segs = [(0, D_SSD), (o_xbc, D_SSD), (o_v, D_RET), (o_g, D_RET), (o_q, RET_QK), (o_k, RET_QK),
            (o_xbc + D_SSD, SSD_BC), (o_dt, bn)]
    offs = [start + i for start, width in segs for i in range(0, width, bn)]
    assert len(offs) * bn == EVEN_PACKED and offs[-1] + bn <= D_IN_EVEN
    return offs


def _even_proj(x, p, prm, bm):
    return mm_in_t(x, p["norm_mix"][0], prm["even_wt"], _even_row_offsets(512), bm, 512)


def _ffn(x, p, prm, l, prev, shift, group_rows, bm, bf=512):
    act, st = ffn_up(x, p["norm_ffn"][l], p["ffn_w_up"], l, p["ffn_conv_w"][l], p["ffn_conv_b"][l],
                     prev, shift, group_rows, bm, bf)
    return mm_out([act], p["ffn_w_down"], l, x, min(bm, 1024), 256), st


def _trunks(xp, xs, st, p, prm, nseq, seq, nb, ntok):
    m = nseq * seq
    rows = nb * ntok
    bm = min(2048, seq)
    nc = seq // MIX_CHUNK
    grid = (nseq, nc)
    cos_p, sin_p = _rope_tables(jnp.arange(seq, dtype=jnp.int32))
    cos_s, sin_s = _rope_tables(PAST_LEN + jnp.arange(ntok, dtype=jnp.int32))
    zeros_ffn = jnp.zeros((nseq, FFN_CONV - 1, D_FF), F32)

    pe_s = _even_proj(xs, p, prm, rows)
    ssd_ctx, ssd_part = ssd_sample_pre(pe_s, st["ssm"][0], st["ssm_conv"][0], prm, nb, ntok, nseq, nc)
    ret_oi, ret_part = ret_sample_pre(pe_s, st["ret"][0], cos_s, sin_s, nb, ntok, nseq, nc)
    pe_p = _even_proj(xp, p, prm, bm)
    (y, ssm_p, ssm_conv_p), (o, ret_p), (u_ssd, st_ssd), (u_ret, ret_s) = _run_parts(
        [ssd_prompt(pe_p, prm, nseq, seq, MIX_CHUNK), ret_prompt(pe_p, cos_p, sin_p, nseq, seq, MIX_CHUNK),
         ssd_part, ret_part], grid, "even_mix", _mib(58))
    xp = mm_out([y, o], p["even_w_out"], 0, xp, min(bm, 1024), 512)
    xp, ffn0_p = _ffn(xp, p, prm, 0, zeros_ffn, 1, seq, bm, 256)
    y_s, ssm_s, ssm_conv_s = ssd_sample_post(ssd_ctx, u_ssd, st_ssd, pe_s, prm, nb, ntok)
    o_s = ret_sample_post(ret_oi, u_ret, pe_s, nb, ntok)
    xs = mm_out([y_s, o_s], p["even_w_out"], 0, xs, rows, 512)
    xs, ffn0_s = _ffn(xs, p, prm, 0, _to_token_major(st["ffn_conv"][0])[None], nb, rows, rows)

    po_s = mm_in(xs, p["norm_mix"][1], p["odd_w_in"][0], rows, 512)
    hg_oi, hg_part = hg_sample_pre(po_s, st["hgrn"][0], prm["hg_lb"], nb, ntok, nseq, nc)
    yl_s, lru_s, lru_conv_s = lru_sample(po_s, st["lru"][0], st["lru_conv"][0], prm, nb, ntok)
    po_p = mm_in(xp, p["norm_mix"][1], p["odd_w_in"][0], bm, 512)
    (og, hgrn_p), (yl, lru_p, lru_conv_p), (u_hg, hgrn_s) = _run_parts(
        [hg_prompt(po_p, prm["hg_lb"], prm["hg_nw"], nseq, seq, MIX_CHUNK),
         lru_prompt(po_p, prm, nseq, seq, MIX_CHUNK), hg_part], grid, "odd_mix", _mib(58))
    xp = mm_out([og, yl], p["odd_w_out"], 0, xp, min(bm, 1024), 512)
    xp, ffn1_p = _ffn(xp, p, prm, 1, zeros_ffn, 1, seq, bm, 256)
    og_s = hg_sample_post(hg_oi, u_hg, po_s, prm["hg_nw"], nb, ntok)
    xs = mm_out([og_s, yl_s], p["odd_w_out"], 0, xs, rows, 512)
    xs, ffn1_s = _ffn(xs, p, prm, 1, _to_token_major(st["ffn_conv"][1])[None], nb, rows, rows)

    y_p = rmsnorm(xp, p["norm_final"], min(512, m)).reshape(nseq, seq, D_MODEL)
    y_s = jnp.transpose(rmsnorm(xs, p["norm_final"], rows).reshape(ntok, nb, D_MODEL), (1, 0, 2))
    ffn_s = jnp.stack([_from_token_major(ffn0_s[0], nb), _from_token_major(ffn1_s[0], nb)])
    return (y_p, y_s, ssm_p[None], ssm_s[None], ssm_conv_p[None], ssm_conv_s[None],
            ret_p[None], ret_s[None], hgrn_p[None], hgrn_s[None],
            lru_p.reshape(1, nseq, D_RNN), lru_s[None], lru_conv_p[None], lru_conv_s[None],
            jnp.stack([ffn0_p, ffn1_p]), ffn_s)


def kernel(x_prompt, x_sample, state_ssm, state_ssm_conv, state_ret, state_hgrn, state_lru, state_lru_conv, state_ffn_conv, norm_mix, norm_ffn, norm_final, even_w_in, ssd_conv_w, ssd_conv_b, ssd_dt_bias, ssd_A_log, ssd_D, ssd_norm_w, even_w_out, odd_w_in, hg_lower_bounds, hg_norm_w, lru_conv_w, lru_conv_b, lru_w_r, lru_b_r, lru_w_i, lru_b_i, lru_a_param, odd_w_out, ffn_w_up, ffn_conv_w, ffn_conv_b, ffn_w_down):
    p = {
        "norm_mix": norm_mix, "norm_ffn": norm_ffn, "norm_final": norm_final,
        "even_w_in": even_w_in, "ssd_conv_w": ssd_conv_w, "ssd_conv_b": ssd_conv_b,
        "ssd_dt_bias": ssd_dt_bias, "ssd_A_log": ssd_A_log, "ssd_D": ssd_D,
        "ssd_norm_w": ssd_norm_w, "even_w_out": even_w_out, "odd_w_in": odd_w_in,
        "hg_lower_bounds": hg_lower_bounds, "hg_norm_w": hg_norm_w,
        "lru_conv_w": lru_conv_w, "lru_conv_b": lru_conv_b, "lru_w_r": lru_w_r,
        "lru_b_r": lru_b_r, "lru_w_i": lru_w_i, "lru_b_i": lru_b_i,
        "lru_a_param": lru_a_param, "odd_w_out": odd_w_out, "ffn_w_up": ffn_w_up,
        "ffn_conv_w": ffn_conv_w, "ffn_conv_b": ffn_conv_b, "ffn_w_down": ffn_w_down,
    }
    prm = _prepare(p)
    nseq, seq, _ = x_prompt.shape
    nb, ntok, _ = x_sample.shape
    st = {"ssm": state_ssm, "ssm_conv": state_ssm_conv, "ret": state_ret, "hgrn": state_hgrn,
          "lru": state_lru, "lru_conv": state_lru_conv, "ffn_conv": state_ffn_conv}
    xs_tm = jnp.transpose(x_sample, (1, 0, 2)).reshape(ntok * nb, D_MODEL)
    return _trunks(x_prompt.reshape(nseq * seq, D_MODEL), xs_tm, st, p, prm, nseq, seq, nb, ntok)
```

```python
import functools
import math

import numpy as np
import jax
import jax.numpy as jnp
from jax import lax
from jax.experimental import pallas as pl
from jax.experimental.pallas import tpu as pltpu

F32 = jnp.float32
BF16 = jnp.bfloat16
EPS = 1e-6
LOG2E = math.log2(math.e)

D_MODEL = 2048
PAST_LEN = 16384
SSD_HEADS = 32
SSD_HEAD_DIM = 64
D_SSD = SSD_HEADS * SSD_HEAD_DIM
SSD_GROUPS = 4
SSD_STATE = 128
SSD_CONV = 4
SSD_BC = 2 * SSD_GROUPS * SSD_STATE
RET_HEADS = 8
RET_DK = 128
RET_DV = 256
RET_QK = RET_HEADS * RET_DK
D_RET = RET_HEADS * RET_DV
ROPE_BASE = 10000.0
HG_HEADS = 16
HG_DK = 128
HG_DV = 128
D_HG = HG_HEADS * HG_DV
D_RNN = 2048
LRU_BLOCKS = 8
LRU_BW = D_RNN // LRU_BLOCKS
LRU_CONV = 4
LRU_C = 8.0
D_FF = 5632
FFN_CONV = 3
D_IN_EVEN = D_SSD + (D_SSD + SSD_BC) + SSD_HEADS + 2 * RET_QK + 2 * D_RET
EVEN_PACKED = 11776
DT_COL = 11264

V7X_VMEM_BYTES = 64 * 1024 * 1024
V7X_VMEM_CAP = 60 * 1024 * 1024
LANES = 128
SUBLANES = 8

RET_LOG_GAMMA = [float(v) for v in np.log1p(-np.exp(np.linspace(
    math.log(1.0 / 32.0), math.log(1.0 / 512.0), RET_HEADS, dtype=np.float32))).astype(np.float32)]

NT = (((1,), (1,)), ((), ()))
TN = (((0,), (0,)), ((), ()))


def _cparams(sem, vmem_bytes):
    return pltpu.CompilerParams(dimension_semantics=sem,
                                vmem_limit_bytes=int(min(V7X_VMEM_CAP, vmem_bytes)))


def _mib(n):
    return n * 1024 * 1024


def _silu(x):
    return x * jax.nn.sigmoid(x)


def _softplus(x):
    return jnp.maximum(x, 0.0) + jnp.log1p(jnp.exp(-jnp.abs(x)))


def _gelu_tanh(x):
    return 0.5 * x * (1.0 + jnp.tanh(math.sqrt(2.0 / math.pi) * (x + 0.044715 * (x * x * x))))


def _rms(x):
    return x * lax.rsqrt(jnp.mean(x * x, axis=-1, keepdims=True) + EPS)


def _dot(a, b):
    return jnp.dot(a, b, preferred_element_type=F32)


def _tril_ones(n):
    r = lax.broadcasted_iota(jnp.int32, (n, n), 0)
    c = lax.broadcasted_iota(jnp.int32, (n, n), 1)
    return (r >= c).astype(F32)


def _cumsum_rows(x):
    return jnp.dot(_tril_ones(x.shape[0]), x, precision=lax.Precision.HIGHEST,
                   preferred_element_type=F32)


def _shifted_rows(buf_ref, start, rows, back):
    if back % SUBLANES == 0:
        return buf_ref[start - back:start - back + rows, :]
    assert back < SUBLANES and start % SUBLANES == 0
    ext = buf_ref[start - SUBLANES:start + rows, :]
    return pltpu.roll(ext, back, 0)[SUBLANES:, :]


def _causal_conv(buf_ref, x, w_ref, b_ref, shift, width):
    rows = x.shape[0]
    off = buf_ref.shape[0] - rows
    n = (width - 1) * shift
    buf_ref[off:off + rows, :] = x
    acc = None
    for j in range(width):
        term = _shifted_rows(buf_ref, off, rows, (width - 1 - j) * shift) * w_ref[j:j + 1, :]
        acc = term if acc is None else acc + term
    out = b_ref[...] + acc
    tail = buf_ref[off + rows - n:off + rows, :]
    buf_ref[off - n:off, :] = tail
    return out, tail


def _conv_off(shift, width):
    n = (width - 1) * shift
    return -(-n // SUBLANES) * SUBLANES


NORM_ROWS = 256


def _norm_rows_to(x_ref, nw_ref, xn_ref):
    rows = x_ref.shape[0]
    step = min(NORM_ROWS, rows)

    def body(i, carry):
        r = pl.multiple_of(i * step, step)
        xn_ref[pl.ds(r, step), :] = (_rms(x_ref[pl.ds(r, step), :]) * nw_ref[...]).astype(BF16)
        return carry

    lax.fori_loop(0, rows // step, body, 0)


def _mm_in_kernel(x_ref, nw_ref, w_ref, o_ref, xn_ref):
    @pl.when(pl.program_id(1) == 0)
    def _():
        _norm_rows_to(x_ref, nw_ref, xn_ref)
    o_ref[...] = _dot(xn_ref[...], w_ref[...].astype(BF16))


def _mm_in_vmem(bm, d, bn):
    return (bm * d * 4 + bm * d * 2 + 2 * d * bn * 4 + d * bn * 2 + 2 * bm * bn * 4
            + 4 * NORM_ROWS * d * 4 + _mib(6))


def mm_in(x, nw, w, bm, bn):
    m, d = x.shape
    n = w.shape[1]
    vmem = _mm_in_vmem(bm, d, bn)
    return pl.pallas_call(
        _mm_in_kernel,
        grid=(m // bm, n // bn),
        in_specs=[pl.BlockSpec((bm, d), lambda i, j: (i, 0), pipeline_mode=pl.Buffered(1)),
                  pl.BlockSpec((1, d), lambda i, j: (0, 0)),
                  pl.BlockSpec((d, bn), lambda i, j: (0, j))],
        out_specs=pl.BlockSpec((bm, bn), lambda i, j: (i, j)),
        out_shape=jax.ShapeDtypeStruct((m, n), F32),
        scratch_shapes=[pltpu.VMEM((bm, d), BF16)],
        compiler_params=_cparams(("parallel", "arbitrary"), vmem),
        name="mm_in",
    )(x, nw.reshape(1, d), w)


def _mm_in_t_kernel(offs_ref, x_ref, nw_ref, wt_ref, o_ref, xn_ref):
    del offs_ref
    @pl.when(pl.program_id(1) == 0)
    def _():
        _norm_rows_to(x_ref, nw_ref, xn_ref)
    o_ref[...] = lax.dot_general(xn_ref[...], wt_ref[...].astype(BF16), NT, preferred_element_type=F32)


ROW_ALIGN = 32


def mm_in_t(x, nw, wt, row_offsets, bm, bn):
    m, d = x.shape
    nblk = len(row_offsets)
    assert all(o % ROW_ALIGN == 0 for o in row_offsets)
    vmem = _mm_in_vmem(bm, d, bn)
    grid_spec = pltpu.PrefetchScalarGridSpec(
        num_scalar_prefetch=1,
        grid=(m // bm, nblk),
        in_specs=[pl.BlockSpec((bm, d), lambda i, j, offs: (i, 0), pipeline_mode=pl.Buffered(1)),
                  pl.BlockSpec((1, d), lambda i, j, offs: (0, 0)),
                  pl.BlockSpec((pl.Element(bn), pl.Element(d)),
                               lambda i, j, offs: (offs[j] * ROW_ALIGN, 0))],
        out_specs=pl.BlockSpec((bm, bn), lambda i, j, offs: (i, j)),
        scratch_shapes=[pltpu.VMEM((bm, d), BF16)])
    return pl.pallas_call(
        _mm_in_t_kernel,
        grid_spec=grid_spec,
        out_shape=jax.ShapeDtypeStruct((m, nblk * bn), F32),
        compiler_params=_cparams(("parallel", "arbitrary"), vmem),
        name="mm_in_t",
    )(jnp.asarray([o // ROW_ALIGN for o in row_offsets], jnp.int32), x, nw.reshape(1, d), wt)


def _mm_out_kernel(*refs, nparts):
    a_refs = refs[:nparts]
    w_refs = refs[nparts:2 * nparts]
    r_ref = refs[2 * nparts]
    o_ref = refs[2 * nparts + 1]
    acc = r_ref[...]
    for a_ref, w_ref in zip(a_refs, w_refs):
        acc = acc + _dot(a_ref[...], w_ref[...].astype(BF16))
    o_ref[...] = acc


def mm_out(parts, w, layer, resid, bm, bn):
    nparts = len(parts)
    m, kp = parts[0].shape
    n = w.shape[2]
    vmem = nparts * (2 * bm * kp * 2 + 2 * kp * bn * 4 + kp * bn * 2) + 4 * bm * bn * 4 + 2 * bm * bn * 4 + _mib(6)
    in_specs = [pl.BlockSpec((bm, kp), lambda i, j: (i, 0)) for _ in range(nparts)]
    in_specs += [pl.BlockSpec((None, kp, bn), functools.partial(lambda i, j, p: (layer, p, j), p=p))
                 for p in range(nparts)]
    in_specs += [pl.BlockSpec((bm, bn), lambda i, j: (i, j))]
    return pl.pallas_call(
        functools.partial(_mm_out_kernel, nparts=nparts),
        grid=(m // bm, n // bn),
        in_specs=in_specs,
        out_specs=pl.BlockSpec((bm, bn), lambda i, j: (i, j)),
        out_shape=jax.ShapeDtypeStruct((m, n), F32),
        compiler_params=_cparams(("parallel", "parallel"), vmem),
        name="mm_out",
    )(*parts, *([w] * nparts), resid)


def _rmsnorm_kernel(x_ref, nw_ref, o_ref):
    o_ref[...] = _rms(x_ref[...]) * nw_ref[...]


def rmsnorm(x, nw, bm):
    m, d = x.shape
    return pl.pallas_call(
        _rmsnorm_kernel,
        grid=(m // bm,),
        in_specs=[pl.BlockSpec((bm, d), lambda i: (i, 0)), pl.BlockSpec((1, d), lambda i: (0, 0))],
        out_specs=pl.BlockSpec((bm, d), lambda i: (i, 0)),
        out_shape=jax.ShapeDtypeStruct((m, d), F32),
        compiler_params=_cparams(("parallel",), 6 * bm * d * 4 + _mib(4)),
        name="rmsnorm",
    )(x, nw.reshape(1, d))


FFN_SLAB = 512


def _ffn_up_kernel(x_ref, nw_ref, wg_ref, wu_ref, cw_ref, cb_ref, prev_ref,
                   act_ref, st_ref, xn_ref, gbuf_ref, carry_ref, *, shift, blocks_per_group):
    i = pl.program_id(0)
    j = pl.program_id(1)
    bm = x_ref.shape[0]
    n = (FFN_CONV - 1) * shift
    off = gbuf_ref.shape[0] - bm

    @pl.when(j == 0)
    def _():
        _norm_rows_to(x_ref, nw_ref, xn_ref)

    first = (i % blocks_per_group) == 0

    @pl.when(first)
    def _():
        gbuf_ref[off - n:off, :] = prev_ref[0]

    @pl.when(jnp.logical_not(first))
    def _():
        gbuf_ref[off - n:off, :] = carry_ref[j]

    wg = wg_ref[...].astype(BF16)
    wu = wu_ref[...].astype(BF16)
    slab = min(FFN_SLAB, bm)
    for s in range(bm // slab):
        r0 = s * slab
        xs = xn_ref[r0:r0 + slab, :]
        gbuf_ref[off + r0:off + r0 + slab, :] = _dot(xs, wg)
        u = _dot(xs, wu)
        gc = cb_ref[...]
        for t in range(FFN_CONV):
            back = (FFN_CONV - 1 - t) * shift
            gc = gc + _shifted_rows(gbuf_ref, off + r0, slab, back) * cw_ref[t:t + 1, :]
        act_ref[r0:r0 + slab, :] = (_silu(gc) * u).astype(BF16)
    tail = gbuf_ref[off + bm - n:off + bm, :]
    carry_ref[j] = tail
    st_ref[0] = tail


def ffn_up(x, nw, w_up, layer, cw, cb, prev, shift, group_rows, bm, bf):
    m, d = x.shape
    f = cw.shape[1]
    n = (FFN_CONV - 1) * shift
    off = _conv_off(shift, FFN_CONV)
    bpg = group_rows // bm
    nf = f // bf
    vmem = (bm * d * 4 + bm * d * 2 + 4 * d * bf * 4 + 2 * d * bf * 2 + 2 * bm * bf * 2
            + (off + bm) * bf * 4 + nf * max(n, SUBLANES) * bf * 4 + 4 * n * bf * 4 + 5 * bm * bf * 4
            + 4 * NORM_ROWS * d * 4 + _mib(6))
    act, st = pl.pallas_call(
        functools.partial(_ffn_up_kernel, shift=shift, blocks_per_group=bpg),
        grid=(m // bm, nf),
        in_specs=[pl.BlockSpec((bm, d), lambda i, j: (i, 0), pipeline_mode=pl.Buffered(1)),
                  pl.BlockSpec((1, d), lambda i, j: (0, 0)),
                  pl.BlockSpec((None, d, bf), lambda i, j: (layer, 0, j)),
                  pl.BlockSpec((None, d, bf), lambda i, j: (layer, 0, j + nf)),
                  pl.BlockSpec((FFN_CONV, bf), lambda i, j: (0, j)),
                  pl.BlockSpec((1, bf), lambda i, j: (0, j)),
                  pl.BlockSpec((1, n, bf), lambda i, j: (i // bpg, 0, j))],
        out_specs=[pl.BlockSpec((bm, bf), lambda i, j: (i, j)),
                   pl.BlockSpec((1, n, bf), lambda i, j: (i, 0, j))],
        out_shape=[jax.ShapeDtypeStruct((m, f), BF16),
                   jax.ShapeDtypeStruct((m // bm, n, f), F32)],
        scratch_shapes=[pltpu.VMEM((bm, d), BF16),
                        pltpu.VMEM((off + bm, bf), F32),
                        pltpu.VMEM((nf, n, bf), F32)],
        compiler_params=_cparams(("arbitrary", "arbitrary"), vmem),
        name="ffn_up",
    )(x, nw.reshape(1, d), w_up, w_up, cw, cb.reshape(1, f), prev)
    return act, st[bpg - 1::bpg]


def _expand_heads64(x, nheads):
    r = x.shape[0]
    lo = lax.broadcasted_iota(jnp.int32, (r, LANES), 1) < SSD_HEAD_DIM
    tiles = []
    for p in range(nheads // 2):
        a0 = jnp.broadcast_to(x[:, 2 * p:2 * p + 1], (r, LANES))
        a1 = jnp.broadcast_to(x[:, 2 * p + 1:2 * p + 2], (r, LANES))
        tiles.append(jnp.where(lo, a0, a1))
    return jnp.concatenate(tiles, axis=1)


def _ssd_gates(xs_c, bc_c, dt_raw, dtb, alog):
    xs = _silu(xs_c)
    bc = _silu(bc_c)
    dt = _softplus(dt_raw + dtb)
    la = dt * (-jnp.exp(alog))
    return xs, bc, dt, la


def _ssd_finish(o, xs, z, dx, nw):
    y = (o + dx * xs) * _silu(z)
    gw = D_SSD // SSD_GROUPS
    y = jnp.concatenate([_rms(y[:, g * gw:(g + 1) * gw]) for g in range(SSD_GROUPS)], axis=1)
    return y * nw


def _ssd_prompt_kernel(z_ref, xs_ref, bc_ref, dt_ref, cwx_ref, cwb_ref, cbx_ref, cbb_ref,
                       dtb_ref, alog_ref, dx_ref, nw_ref,
                       y_ref, sfin_ref, cfin_ref, s_ref, bufx_ref, bufb_ref):
    c = pl.program_id(1)
    nc = pl.num_programs(1)
    rows = xs_ref.shape[0]
    gw = D_SSD // SSD_GROUPS
    hpg = SSD_HEADS // SSD_GROUPS

    @pl.when(c == 0)
    def _():
        s_ref[...] = jnp.zeros_like(s_ref)
        bufx_ref[0:SUBLANES, :] = jnp.zeros((SUBLANES, D_SSD), F32)
        bufb_ref[0:SUBLANES, :] = jnp.zeros((SUBLANES, SSD_BC), F32)

    yield
    xs_c, tailx = _causal_conv(bufx_ref, xs_ref[...], cwx_ref, cbx_ref, 1, SSD_CONV)
    bc_c, tailb = _causal_conv(bufb_ref, bc_ref[...], cwb_ref, cbb_ref, 1, SSD_CONV)
    xs, bc, dt, la = _ssd_gates(xs_c, bc_c, dt_ref[...], dtb_ref[...], alog_ref[...])
    cum = _cumsum_rows(la * LOG2E)
    cum_t = cum.T
    cumx = _expand_heads64(cum, SSD_HEADS)
    dtx = _expand_heads64(dt, SSD_HEADS)
    lastx = cumx[rows - 1:rows, :]
    ecum = jnp.exp2(cumx)
    wx = jnp.exp2(lastx - cumx)
    elast = jnp.exp2(lastx)
    v_all = xs * dtx
    vw_all = v_all * wx

    ri = lax.broadcasted_iota(jnp.int32, (rows, rows), 0)
    ci = lax.broadcasted_iota(jnp.int32, (rows, rows), 1)
    causal = ri >= ci
    lo = lax.broadcasted_iota(jnp.int32, (rows, LANES), 1) < SSD_HEAD_DIM

    o_groups = []
    for g in range(SSD_GROUPS):
        kb = bc[:, g * SSD_STATE:(g + 1) * SSD_STATE].astype(BF16)
        qb = bc[:, (SSD_GROUPS + g) * SSD_STATE:(SSD_GROUPS + g + 1) * SSD_STATE].astype(BF16)
        qk = lax.dot_general(qb, kb, NT, preferred_element_type=F32)
        o_tiles = []
        for p in range(hpg // 2):
            acc = None
            for q in range(2):
                h = g * hpg + 2 * p + q
                diff = cum[:, h:h + 1] - cum_t[h:h + 1, :]
                dec = jnp.exp2(jnp.where(causal, diff, -1e30))
                pm = (qk * dec).astype(BF16)
                col = g * gw + p * LANES
                vp = v_all[:, col:col + LANES]
                vh = jnp.where(lo, vp, 0.0) if q == 0 else jnp.where(lo, 0.0, vp)
                t = _dot(pm, vh.astype(BF16))
                acc = t if acc is None else acc + t
            o_tiles.append(acc)
        o_intra = jnp.concatenate(o_tiles, axis=1)
        sg = s_ref[g]
        sl = slice(g * gw, (g + 1) * gw)
        o_inter = _dot(qb, sg.astype(BF16)) * ecum[:, sl]
        upd = lax.dot_general(kb, vw_all[:, sl].astype(BF16), TN, preferred_element_type=F32)
        s_ref[g] = sg * elast[:, sl] + upd
        o_groups.append(o_intra + o_inter)
    o = jnp.concatenate(o_groups, axis=1)
    y_ref[...] = _ssd_finish(o, xs, z_ref[...], dx_ref[...], nw_ref[...]).astype(BF16)

    yield
    @pl.when(c == nc - 1)
    def _():
        for h in range(SSD_HEADS):
            g, hl = divmod(h, hpg)
            sfin_ref[0, h] = s_ref[g, :, hl * SSD_HEAD_DIM:(hl + 1) * SSD_HEAD_DIM]
        cfin_ref[0, :, 0:D_SSD] = tailx
        cfin_ref[0, :, D_SSD:D_SSD + SSD_BC] = tailb


def ssd_prompt(pe, prm, nseq, seq, chunk):
    m = nseq * seq
    nc = seq // chunk
    row = lambda b, c: b * nc + c
    full = lambda shp: pl.BlockSpec(shp, lambda b, c: (0,) * len(shp))
    in_specs = [
        pl.BlockSpec((chunk, D_SSD), lambda b, c: (row(b, c), 0)),
        pl.BlockSpec((chunk, D_SSD), lambda b, c: (row(b, c), 1)),
        pl.BlockSpec((chunk, SSD_BC), lambda b, c: (row(b, c), 10)),
        pl.BlockSpec((chunk, LANES), lambda b, c: (row(b, c), DT_COL // LANES)),
        pl.BlockSpec((SSD_CONV, D_SSD), lambda b, c: (0, 0)),
        pl.BlockSpec((SSD_CONV, SSD_BC), lambda b, c: (0, 2)),
        pl.BlockSpec((1, D_SSD), lambda b, c: (0, 0)),
        pl.BlockSpec((1, SSD_BC), lambda b, c: (0, 2)),
        full((1, LANES)), full((1, LANES)), full((1, D_SSD)), full((1, D_SSD)),
    ]
    return dict(
        kernel=_ssd_prompt_kernel,
        arrays=[pe, pe, pe, pe, prm["ssd_cw"], prm["ssd_cw"], prm["ssd_cb"], prm["ssd_cb"],
                prm["ssd_dtb"], prm["ssd_alog"], prm["ssd_dx"], prm["ssd_nw"]],
        in_specs=in_specs,
        out_specs=[pl.BlockSpec((chunk, D_SSD), lambda b, c: (row(b, c), 0)),
                   pl.BlockSpec((1, SSD_HEADS, SSD_STATE, SSD_HEAD_DIM), lambda b, c: (b, 0, 0, 0)),
                   pl.BlockSpec((1, SSD_CONV - 1, D_SSD + SSD_BC), lambda b, c: (b, 0, 0))],
        out_shape=[jax.ShapeDtypeStruct((m, D_SSD), BF16),
                   jax.ShapeDtypeStruct((nseq, SSD_HEADS, SSD_STATE, SSD_HEAD_DIM), F32),
                   jax.ShapeDtypeStruct((nseq, SSD_CONV - 1, D_SSD + SSD_BC), F32)],
        scratch=[pltpu.VMEM((SSD_GROUPS, SSD_STATE, D_SSD // SSD_GROUPS), F32),
                 pltpu.VMEM((SUBLANES + chunk, D_SSD), F32),
                 pltpu.VMEM((SUBLANES + chunk, SSD_BC), F32)])


MIX_CHUNK = 128


def _run_parts(parts, grid, name, vmem):
    counts = [(len(p["arrays"]), len(p["out_shape"]), len(p["scratch"])) for p in parts]
    n_in = sum(c[0] for c in counts)
    n_out = sum(c[1] for c in counts)

    def body(*refs):
        i, o, s = 0, n_in, n_in + n_out
        gens = []
        for p, (a, b, c) in zip(parts, counts):
            gens.append(p["kernel"](*refs[i:i + a], *refs[o:o + b], *refs[s:s + c]))
            i, o, s = i + a, o + b, s + c
        for _ in range(3):
            for g in gens:
                next(g, None)

    outs = pl.pallas_call(
        body,
        grid=grid,
        in_specs=[sp for p in parts for sp in p["in_specs"]],
        out_specs=[sp for p in parts for sp in p["out_specs"]],
        out_shape=[sh for p in parts for sh in p["out_shape"]],
        scratch_shapes=[sc for p in parts for sc in p["scratch"]],
        compiler_params=_cparams(("parallel", "arbitrary"), vmem),
        name=name,
    )(*[a for p in parts for a in p["arrays"]])
    res, k = [], 0
    for _, b, _ in counts:
        res.append(outs[k:k + b])
        k += b
    return res


def _rotary(x, cos, sin_signed):
    return x * cos + pltpu.roll(x, RET_DK // 2, 1) * sin_signed


def _ret_prompt_kernel(q_ref, k_ref, v_ref, g_ref, cos_ref, sin_ref, o_ref, sfin_ref, s_ref):
    c = pl.program_id(1)
    nc = pl.num_programs(1)
    rows = q_ref.shape[0]

    @pl.when(c == 0)
    def _():
        s_ref[...] = jnp.zeros_like(s_ref)

    yield
    cos = cos_ref[...]
    sin = sin_ref[...]
    ri = lax.broadcasted_iota(jnp.int32, (rows, rows), 0)
    ci = lax.broadcasted_iota(jnp.int32, (rows, rows), 1)
    dij = jnp.where(ri >= ci, (ri - ci).astype(F32), 1e30)
    tk = lax.broadcasted_iota(jnp.int32, (rows, RET_DK), 0).astype(F32)
    tv = lax.broadcasted_iota(jnp.int32, (rows, RET_DV), 0).astype(F32)
    for h in range(RET_HEADS):
        lg = RET_LOG_GAMMA[h]
        lg2 = lg * LOG2E
        qr = _rotary(q_ref[:, h * RET_DK:(h + 1) * RET_DK], cos, sin)
        kr = _rotary(k_ref[:, h * RET_DK:(h + 1) * RET_DK], cos, sin) * (RET_DK ** -0.5)
        qb = qr.astype(BF16)
        s = lax.dot_general(qb, kr.astype(BF16), NT, preferred_element_type=F32)
        pm = (s * jnp.exp2(lg2 * dij)).astype(BF16)
        vb = v_ref[:, h * RET_DV:(h + 1) * RET_DV].astype(BF16)
        sh = s_ref[h]
        o = _dot(pm, vb) + _dot(qb, sh.astype(BF16)) * jnp.exp2(lg2 * (tv + 1.0))
        kw = (kr * jnp.exp2(lg2 * ((rows - 1.0) - tk))).astype(BF16)
        s_ref[h] = sh * math.exp(lg * rows) + lax.dot_general(kw, vb, TN, preferred_element_type=F32)
        gh = g_ref[:, h * RET_DV:(h + 1) * RET_DV]
        o_ref[:, h * RET_DV:(h + 1) * RET_DV] = (_rms(o) * _silu(gh)).astype(BF16)

    yield
    @pl.when(c == nc - 1)
    def _():
        sfin_ref[0] = s_ref[...]


def ret_prompt(pe, cos, sin, nseq, seq, chunk):
    m = nseq * seq
    nc = seq // chunk
    row = lambda b, c: b * nc + c
    return dict(
        kernel=_ret_prompt_kernel,
        arrays=[pe, pe, pe, pe, cos, sin],
        in_specs=[pl.BlockSpec((chunk, RET_QK), lambda b, c: (row(b, c), 8)),
                  pl.BlockSpec((chunk, RET_QK), lambda b, c: (row(b, c), 9)),
                  pl.BlockSpec((chunk, D_RET), lambda b, c: (row(b, c), 2)),
                  pl.BlockSpec((chunk, D_RET), lambda b, c: (row(b, c), 3)),
                  pl.BlockSpec((chunk, RET_DK), lambda b, c: (c, 0)),
                  pl.BlockSpec((chunk, RET_DK), lambda b, c: (c, 0))],
        out_specs=[pl.BlockSpec((chunk, D_RET), lambda b, c: (row(b, c), 0)),
                   pl.BlockSpec((1, RET_HEADS, RET_DK, RET_DV), lambda b, c: (b, 0, 0, 0))],
        out_shape=[jax.ShapeDtypeStruct((m, D_RET), BF16),
                   jax.ShapeDtypeStruct((nseq, RET_HEADS, RET_DK, RET_DV), F32)],
        scratch=[pltpu.VMEM((RET_HEADS, RET_DK, RET_DV), F32)])


HG_HB = HG_HEADS


def _hg_gates(hq, hf, lb):
    q = _silu(hq)
    f = lb + (1.0 - lb) * jax.nn.sigmoid(hf)
    return q, f, 1.0 - f, jnp.log(f)


def _hg_tables(rows):
    r = np.arange(rows)[:, None]
    t = np.arange(rows)[None, :]
    sums = [t <= r]
    masks = []
    s = rows // 2
    while s >= 1:
        blk, pos = r // (2 * s), r % (2 * s)
        ref = blk * 2 * s + s - 1
        upper = pos >= s
        sums.append(np.where(upper, (t > ref) & (t <= r), (t > r) & (t <= ref)))
        masks.append((blk == t // (2 * s)) & upper & (t % (2 * s) < s))
        s //= 2
    masks.append(r == t)
    return (np.concatenate(sums, axis=0).astype(np.float32),
            np.stack(masks).astype(np.float32))


def _hg_prompt_kernel(hq_ref, hf_ref, hi_ref, hg_ref, lb_ref, nw_ref, sums_ref, masks_ref,
                      og_ref, sfin_ref, st_ref):
    c = pl.program_id(1)
    nc = pl.num_programs(1)
    rows = hq_ref.shape[0]
    nlev = masks_ref.shape[0] - 1

    @pl.when(c == 0)
    def _():
        st_ref[...] = jnp.zeros_like(st_ref)

    yield
    q_all, _, kk_all, lg_all = _hg_gates(hq_ref[...], hf_ref[...], lb_ref[...])
    lg_all = lg_all * LOG2E
    lg_hi = lg_all.astype(BF16)
    lg_lo = (lg_all - lg_hi.astype(F32)).astype(BF16)
    sums = sums_ref[...]
    dall = _dot(sums, lg_hi) + _dot(sums, lg_lo)
    cum_all = dall[0:rows]
    ecum_all = jnp.exp2(cum_all)
    for h in range(HG_HB):
        sl = slice(h * HG_DK, (h + 1) * HG_DK)
        q = q_all[:, sl]
        kk = kk_all[:, sl]
        cum = cum_all[:, sl]
        vb = hi_ref[:, sl].astype(BF16)
        last = cum[rows - 1:rows, :]
        st = st_ref[h]
        o = lax.dot_general((q * ecum_all[:, sl]).astype(BF16), st.astype(BF16), NT,
                            preferred_element_type=F32)
        kt = (kk * jnp.exp2(last - cum)).astype(BF16)
        st_ref[h] = st * jnp.exp2(last) + lax.dot_general(vb, kt, TN, preferred_element_type=F32)
        a = masks_ref[nlev] * lax.dot_general(q.astype(BF16), kk.astype(BF16), NT,
                                              preferred_element_type=F32)
        for l in range(nlev):
            e = jnp.exp2(dall[(l + 1) * rows:(l + 2) * rows, sl])
            a = a + masks_ref[l] * lax.dot_general((q * e).astype(BF16), (kk * e).astype(BF16), NT,
                                                   preferred_element_type=F32)
        o = o + _dot(a.astype(BF16), vb)
        og_ref[:, sl] = (_rms(o) * nw_ref[...] * _silu(hg_ref[:, sl])).astype(BF16)

    yield
    @pl.when(c == nc - 1)
    def _():
        for h in range(HG_HB):
            sfin_ref[0, h] = st_ref[h].T


def hg_prompt(po, lb, nw, nseq, seq, chunk):
    m = nseq * seq
    nc = seq // chunk
    row = lambda b, c: b * nc + c
    sums_np, masks_np = _hg_tables(chunk)
    sums = jnp.asarray(sums_np, BF16)
    masks = jnp.asarray(masks_np, F32)
    return dict(
        kernel=_hg_prompt_kernel,
        arrays=[po, po, po, po, lb, nw, sums, masks],
        in_specs=[pl.BlockSpec((chunk, D_HG), lambda b, c: (row(b, c), 0)),
                  pl.BlockSpec((chunk, D_HG), lambda b, c: (row(b, c), 1)),
                  pl.BlockSpec((chunk, D_HG), lambda b, c: (row(b, c), 2)),
                  pl.BlockSpec((chunk, D_HG), lambda b, c: (row(b, c), 3)),
                  pl.BlockSpec((1, D_HG), lambda b, c: (0, 0)),
                  pl.BlockSpec((1, HG_DV), lambda b, c: (0, 0)),
                  pl.BlockSpec(sums.shape, lambda b, c: (0, 0)),
                  pl.BlockSpec(masks.shape, lambda b, c: (0, 0, 0))],
        out_specs=[pl.BlockSpec((chunk, D_HG), lambda b, c: (row(b, c), 0)),
                   pl.BlockSpec((1, HG_HEADS, HG_DK, HG_DV), lambda b, c: (b, 0, 0, 0))],
        out_shape=[jax.ShapeDtypeStruct((m, D_HG), BF16),
                   jax.ShapeDtypeStruct((nseq, HG_HEADS, HG_DK, HG_DV), F32)],
        scratch=[pltpu.VMEM((HG_HEADS, HG_DV, HG_DK), F32)])


def _lin_scan(a, b, h0):
    rows, cols = a.shape
    groups = rows // SUBLANES
    a3 = a.reshape(groups, SUBLANES, cols)
    b3 = b.reshape(groups, SUBLANES, cols)
    si = lax.broadcasted_iota(jnp.int32, a3.shape, 1)
    d = 1
    while d < SUBLANES:
        keep = si >= d
        a_s = jnp.where(keep, pltpu.roll(a3, d, 1), 1.0)
        b_s = jnp.where(keep, pltpu.roll(b3, d, 1), 0.0)
        b3 = a3 * b_s + b3
        a3 = a3 * a_s
        d *= 2
    h_in = jnp.broadcast_to(h0, (SUBLANES, cols))
    out = []
    for g in range(groups):
        hg = a3[g] * h_in + b3[g]
        out.append(hg)
        h_in = jnp.broadcast_to(hg[SUBLANES - 1:SUBLANES, :], (SUBLANES, cols))
    return jnp.concatenate(out, axis=0), h_in[0:1, :]


def _lru_gates(xr, wr_ref, br, wi_ref, bi, ap):
    xb = xr.astype(BF16)
    r_parts, i_parts = [], []
    for n in range(LRU_BLOCKS):
        xn = xb[:, n * LRU_BW:(n + 1) * LRU_BW]
        r_parts.append(_dot(xn, wr_ref[n].astype(BF16)))
        i_parts.append(_dot(xn, wi_ref[n].astype(BF16)))
    r = jax.nn.sigmoid(jnp.concatenate(r_parts, axis=1) + br)
    gi = jax.nn.sigmoid(jnp.concatenate(i_parts, axis=1) + bi)
    la = -LRU_C * r * _softplus(-ap)
    a = jnp.exp(la)
    th = jnp.tanh(la)
    mult = jnp.sqrt(-2.0 * th / (1.0 - th))
    return a, mult, gi


def _lru_prompt_kernel(rx_ref, rg_ref, cw_ref, cb_ref, wr_ref, br_ref, wi_ref, bi_ref, ap_ref,
                       yl_ref, hfin_ref, cfin_ref, buf_ref, hc_ref):
    c = pl.program_id(1)
    nc = pl.num_programs(1)
    rows = rx_ref.shape[0]

    @pl.when(c == 0)
    def _():
        buf_ref[0:SUBLANES, :] = jnp.zeros((SUBLANES, D_RNN), F32)
        hc_ref[...] = jnp.zeros_like(hc_ref)

    yield
    xr, tail = _causal_conv(buf_ref, rx_ref[...], cw_ref, cb_ref, 1, LRU_CONV)
    a, mult, gi = _lru_gates(xr, wr_ref, br_ref[...], wi_ref, bi_ref[...], ap_ref[...])
    ri = lax.broadcasted_iota(jnp.int32, (rows, D_RNN), 0)
    mult = jnp.where(jnp.logical_and(c == 0, ri == 0), 1.0, mult)
    hs, hlast = _lin_scan(a, mult * gi * xr, hc_ref[0:1, :])
    hc_ref[0:1, :] = hlast
    yl_ref[...] = (hs * _gelu_tanh(rg_ref[...])).astype(BF16)

    yield
    @pl.when(c == nc - 1)
    def _():
        hfin_ref[0] = hlast
        cfin_ref[0] = tail


def lru_prompt(po, prm, nseq, seq, chunk):
    m = nseq * seq
    nc = seq // chunk
    row = lambda b, c: b * nc + c
    full = lambda shp: pl.BlockSpec(shp, lambda b, c: (0,) * len(shp))
    return dict(
        kernel=_lru_prompt_kernel,
        arrays=[po, po, prm["lru_cw"], prm["lru_cb"], prm["lru_wr"], prm["lru_br"], prm["lru_wi"],
                prm["lru_bi"], prm["lru_ap"]],
        in_specs=[pl.BlockSpec((chunk, D_RNN), lambda b, c: (row(b, c), 4)),
                  pl.BlockSpec((chunk, D_RNN), lambda b, c: (row(b, c), 5)),
                  full((LRU_CONV, D_RNN)), full((1, D_RNN)),
                  full((LRU_BLOCKS, LRU_BW, LRU_BW)), full((1, D_RNN)),
                  full((LRU_BLOCKS, LRU_BW, LRU_BW)), full((1, D_RNN)), full((1, D_RNN))],
        out_specs=[pl.BlockSpec((chunk, D_RNN), lambda b, c: (row(b, c), 0)),
                   pl.BlockSpec((1, 1, D_RNN), lambda b, c: (b, 0, 0)),
                   pl.BlockSpec((1, LRU_CONV - 1, D_RNN), lambda b, c: (b, 0, 0))],
        out_shape=[jax.ShapeDtypeStruct((m, D_RNN), BF16),
                   jax.ShapeDtypeStruct((nseq, 1, D_RNN), F32),
                   jax.ShapeDtypeStruct((nseq, LRU_CONV - 1, D_RNN), F32)],
        scratch=[pltpu.VMEM((SUBLANES + chunk, D_RNN), F32),
                 pltpu.VMEM((SUBLANES, D_RNN), F32)])


TOKP = SUBLANES


def _ssd_state_kernel(q_ref, k_ref, vw_ref, dec_ref, st_ref, u_ref, snew_ref):
    hpg = SSD_HEADS // SSD_GROUPS
    gw = hpg * SSD_HEAD_DIM

    def body(b, carry):
        q = q_ref[b].astype(BF16)
        k = k_ref[b].astype(BF16)
        vw = vw_ref[b].astype(BF16)
        for g in range(SSD_GROUPS):
            ks = slice(g * SSD_STATE, (g + 1) * SSD_STATE)
            stg = st_ref[b, g * hpg:(g + 1) * hpg].reshape(gw, SSD_STATE)
            u_ref[b, :, g * gw:(g + 1) * gw] = lax.dot_general(
                q[:, ks], stg.astype(BF16), NT, preferred_element_type=F32)
            upd = lax.dot_general(vw[:, g * gw:(g + 1) * gw], k[:, ks], TN, preferred_element_type=F32)
            for hl in range(hpg):
                h = g * hpg + hl
                snew_ref[b, h] = (st_ref[b, h] * dec_ref[b, h:h + 1, :]
                                  + upd[hl * SSD_HEAD_DIM:(hl + 1) * SSD_HEAD_DIM, :])
        return carry

    lax.fori_loop(0, st_ref.shape[0], body, 0)


def _ret_state_kernel(q_ref, k_ref, v_ref, s_ref, u_ref, snew_ref, *, ntok):
    def body(b, carry):
        q = q_ref[b].astype(BF16)
        k = k_ref[b].astype(BF16)
        v = v_ref[b].astype(BF16)
        for h in range(RET_HEADS):
            ks = slice(h * RET_DK, (h + 1) * RET_DK)
            vs = slice(h * RET_DV, (h + 1) * RET_DV)
            s0 = s_ref[b, h]
            u_ref[b, :, vs] = _dot(q[:, ks], s0.astype(BF16))
            upd = lax.dot_general(k[:, ks], v[:, vs], TN, preferred_element_type=F32)
            snew_ref[b, h] = s0 * math.exp(RET_LOG_GAMMA[h] * ntok) + upd
        return carry

    lax.fori_loop(0, s_ref.shape[0], body, 0)


def _hg_state_kernel(q_ref, k_ref, v_ref, dcol_ref, s_ref, u_ref, snew_ref):
    def body(b, carry):
        q = q_ref[b].astype(BF16)
        k = k_ref[b].astype(BF16)
        v = v_ref[b].astype(BF16)
        dc = dcol_ref[b]
        for h in range(HG_HEADS):
            ks = slice(h * HG_DK, (h + 1) * HG_DK)
            s0 = s_ref[b, h]
            u_ref[b, :, ks] = _dot(q[:, ks], s0.astype(BF16))
            upd = lax.dot_general(k[:, ks], v[:, ks], TN, preferred_element_type=F32)
            snew_ref[b, h] = s0 * dc[:, h:h + 1] + upd
        return carry

    lax.fori_loop(0, s_ref.shape[0], body, 0)


def _state_part(kern, rows_in, extra, s, u_cols, nseq, nc):
    nb = s.shape[0]
    bb = nb // (nseq * nc)
    assert bb * nseq * nc == nb
    blk = lambda a: pl.BlockSpec((bb,) + a.shape[1:],
                                 lambda b, c: (b * nc + c,) + (0,) * (a.ndim - 1))
    ins = list(rows_in) + ([extra] if extra is not None else []) + [s]

    def phases(*refs):
        yield
        kern(*refs)
        yield

    return dict(kernel=phases, arrays=ins, in_specs=[blk(a) for a in ins],
                out_specs=[pl.BlockSpec((bb, TOKP, u_cols), lambda b, c: (b * nc + c, 0, 0)), blk(s)],
                out_shape=[jax.ShapeDtypeStruct((nb, TOKP, u_cols), F32),
                           jax.ShapeDtypeStruct(s.shape, F32)],
                scratch=[])


def _tok(x, t, nb):
    return x[t * nb:(t + 1) * nb]


def _head_sums(x, width):
    r, n = x.shape
    tiles = []
    for h in range(n // width):
        s = jnp.sum(x[:, h * width:(h + 1) * width], axis=1, keepdims=True)
        tiles.append(jnp.broadcast_to(s, (r, width)))
    return jnp.concatenate(tiles, axis=1)


def _ssd_sample_pre_kernel(xs_ref, bc_ref, dt_ref, prevx_ref, prevb_ref, cwx_ref, cwb_ref,
                           cbx_ref, cbb_ref, dtb_ref, alog_ref,
                           xs_out, bc_out, vw_out, oi_out, ecum_out, elast_out, tailx_out, tailb_out,
                           bufx_ref, bufb_ref, *, nb, ntok):
    n = (SSD_CONV - 1) * nb
    offx = bufx_ref.shape[0] - nb * ntok
    bufx_ref[offx - n:offx, :] = prevx_ref[...]
    bufb_ref[offx - n:offx, :] = prevb_ref[...]
    xs_c, tailx = _causal_conv(bufx_ref, xs_ref[...], cwx_ref, cbx_ref, nb, SSD_CONV)
    bc_c, tailb = _causal_conv(bufb_ref, bc_ref[...], cwb_ref, cbb_ref, nb, SSD_CONV)
    xs, bc, dt, la = _ssd_gates(xs_c, bc_c, dt_ref[...], dtb_ref[...], alog_ref[...])
    tailx_out[...] = tailx
    tailb_out[...] = tailb
    xs_out[...] = xs
    bc_out[...] = bc
    cums = []
    for t in range(ntok):
        lt = _tok(la, t, nb)
        cums.append(lt if t == 0 else cums[-1] + lt)
    cumx = [_expand_heads64(cm, SSD_HEADS) for cm in cums]
    v = xs * _expand_heads64(dt, SSD_HEADS)
    kw = SSD_GROUPS * SSD_STATE
    for t in range(ntok):
        ct = _tok(bc, t, nb)[:, kw:2 * kw]
        acc = None
        for t2 in range(t + 1):
            bt = _tok(bc, t2, nb)[:, 0:kw]
            sc = _head_sums(ct * bt, SSD_STATE)
            scx = jnp.concatenate(
                [jnp.concatenate([sc[:, g * SSD_STATE:(g + 1) * SSD_STATE]] * 4, axis=1)
                 for g in range(SSD_GROUPS)], axis=1)
            term = scx * jnp.exp(cumx[t] - cumx[t2]) * _tok(v, t2, nb)
            acc = term if acc is None else acc + term
        oi_out[t * nb:(t + 1) * nb, :] = acc
        ecum_out[t * nb:(t + 1) * nb, :] = jnp.exp(cumx[t])
        vw_out[t * nb:(t + 1) * nb, :] = _tok(v, t, nb) * jnp.exp(cumx[ntok - 1] - cumx[t])
    elast_out[...] = jnp.exp(cums[ntok - 1])


def _ssd_sample_post_kernel(oi_ref, u_ref, ecum_ref, xs_ref, z_ref, dx_ref, nw_ref, y_ref):
    o = oi_ref[...] + ecum_ref[...] * u_ref[...]
    y_ref[...] = _ssd_finish(o, xs_ref[...], z_ref[...], dx_ref[...], nw_ref[...]).astype(BF16)


def _ret_sample_pre_kernel(q_ref, k_ref, v_ref, cos_ref, sin_ref, qd_out, kd_out, oi_out, *, nb, ntok):
    rows = nb * ntok
    cos = jnp.concatenate([jnp.broadcast_to(cos_ref[t:t + 1, :], (nb, RET_DK)) for t in range(ntok)], axis=0)
    sin = jnp.concatenate([jnp.broadcast_to(sin_ref[t:t + 1, :], (nb, RET_DK)) for t in range(ntok)], axis=0)
    qr, kr = [], []
    for h in range(RET_HEADS):
        sl = slice(h * RET_DK, (h + 1) * RET_DK)
        qr.append(_rotary(q_ref[:, sl], cos, sin))
        kr.append(_rotary(k_ref[:, sl], cos, sin) * (RET_DK ** -0.5))
    qr = jnp.concatenate(qr, axis=1)
    kr = jnp.concatenate(kr, axis=1)
    v = v_ref[...]
    for t in range(ntok):
        qt = _tok(qr, t, nb)
        acc = None
        for t2 in range(t + 1):
            sc = _head_sums(qt * _tok(kr, t2, nb), RET_DK)
            vt = _tok(v, t2, nb)
            tiles = []
            for h in range(RET_HEADS):
                dec = math.exp(RET_LOG_GAMMA[h] * (t - t2))
                s = sc[:, h * RET_DK:(h + 1) * RET_DK] * dec
                tiles.append(jnp.concatenate([s, s], axis=1) * vt[:, h * RET_DV:(h + 1) * RET_DV])
            term = jnp.concatenate(tiles, axis=1)
            acc = term if acc is None else acc + term
        oi_out[t * nb:(t + 1) * nb, :] = acc
        qd = jnp.concatenate([qt[:, h * RET_DK:(h + 1) * RET_DK] * math.exp(RET_LOG_GAMMA[h] * (t + 1))
                              for h in range(RET_HEADS)], axis=1)
        kt = _tok(kr, t, nb)
        kd = jnp.concatenate([kt[:, h * RET_DK:(h + 1) * RET_DK] * math.exp(RET_LOG_GAMMA[h] * (ntok - 1 - t))
                              for h in range(RET_HEADS)], axis=1)
        qd_out[t * nb:(t + 1) * nb, :] = qd
        kd_out[t * nb:(t + 1) * nb, :] = kd


def _ret_sample_post_kernel(oi_ref, u_ref, g_ref, o_ref):
    o = oi_ref[...] + u_ref[...]
    g = g_ref[...]
    for h in range(RET_HEADS):
        sl = slice(h * RET_DV, (h + 1) * RET_DV)
        o_ref[:, sl] = (_rms(o[:, sl]) * _silu(g[:, sl])).astype(BF16)


def _hg_sample_pre_kernel(hq_ref, hf_ref, hi_ref, lb_ref, qd_out, kd_out, dl_out, oi_out, *, nb, ntok):
    q, _, kk, lg = _hg_gates(hq_ref[...], hf_ref[...], lb_ref[...])
    v = hi_ref[...]
    cums = []
    for t in range(ntok):
        lt = _tok(lg, t, nb)
        cums.append(lt if t == 0 else cums[-1] + lt)
    for t in range(ntok):
        qt = _tok(q, t, nb)
        acc = None
        for t2 in range(t + 1):
            w = qt * _tok(kk, t2, nb)
            if t2 < t:
                w = w * jnp.exp(cums[t] - cums[t2])
            term = _head_sums(w, HG_DK) * _tok(v, t2, nb)
            acc = term if acc is None else acc + term
        oi_out[t * nb:(t + 1) * nb, :] = acc
        qd_out[t * nb:(t + 1) * nb, :] = qt * jnp.exp(cums[t])
        kd_out[t * nb:(t + 1) * nb, :] = _tok(kk, t, nb) * jnp.exp(cums[ntok - 1] - cums[t])
    dl_out[...] = jnp.exp(cums[ntok - 1])


def _hg_sample_post_kernel(oi_ref, u_ref, hg_ref, nw_ref, og_ref):
    o = oi_ref[...] + u_ref[...]
    hg = hg_ref[...]
    for h in range(HG_HEADS):
        sl = slice(h * HG_DV, (h + 1) * HG_DV)
        og_ref[:, sl] = (_rms(o[:, sl]) * nw_ref[...] * _silu(hg[:, sl])).astype(BF16)


def _lru_sample_kernel(rx_ref, rg_ref, prev_ref, h0_ref, cw_ref, cb_ref, wr_ref, br_ref, wi_ref,
                       bi_ref, ap_ref, yl_out, hfin_out, tail_out, buf_ref, *, nb, ntok):
    rows = nb * ntok
    n = (LRU_CONV - 1) * nb
    off = buf_ref.shape[0] - rows
    buf_ref[off - n:off, :] = prev_ref[...]
    xr, tail = _causal_conv(buf_ref, rx_ref[...], cw_ref, cb_ref, nb, LRU_CONV)
    tail_out[...] = tail
    a, mult, gi = _lru_gates(xr, wr_ref, br_ref[...], wi_ref, bi_ref[...], ap_ref[...])
    b = mult * gi * xr
    h = h0_ref[...]
    for t in range(ntok):
        h = _tok(a, t, nb) * h + _tok(b, t, nb)
        yl_out[t * nb:(t + 1) * nb, :] = (h * _gelu_tanh(rg_ref[t * nb:(t + 1) * nb, :])).astype(BF16)
    hfin_out[...] = h


def _whole(shape):
    return pl.BlockSpec(shape, lambda i: (0,) * len(shape))


def _colblock(rows, width, idx):
    return pl.BlockSpec((rows, width), functools.partial(lambda i, k: (0, k), k=idx))


def _call_whole(kern, in_arrays, in_specs, out_shapes, scratch, name, vmem):
    return pl.pallas_call(
        kern, grid=(1,), in_specs=in_specs,
        out_specs=[_whole(s.shape) for s in out_shapes],
        out_shape=out_shapes, scratch_shapes=scratch,
        compiler_params=_cparams(("arbitrary",), vmem), name=name,
    )(*in_arrays)


def _to_token_major(s):
    nb, w, c = s.shape
    return jnp.transpose(s, (1, 0, 2)).reshape(w * nb, c)


def _from_token_major(x, nb):
    w = x.shape[0] // nb
    return jnp.transpose(x.reshape(w, nb, x.shape[1]), (1, 0, 2))


def _rows_to_batch(x, nb):
    ntok = x.shape[0] // nb
    x = jnp.transpose(x.reshape(ntok, nb, x.shape[1]), (1, 0, 2))
    return jnp.pad(x, ((0, 0), (0, TOKP - ntok), (0, 0)))


def _batch_to_rows(u, ntok):
    nb = u.shape[0]
    return jnp.transpose(u[:, :ntok], (1, 0, 2)).reshape(ntok * nb, u.shape[2])


def ssd_sample_pre(pe, s_ssm, s_conv, prm, nb, ntok, nseq, nc):
    rows = nb * ntok
    n = (SSD_CONV - 1) * nb
    prev = _to_token_major(s_conv)
    off = _conv_off(nb, SSD_CONV)
    f = lambda *shape: jax.ShapeDtypeStruct(shape, F32)
    outs = _call_whole(
        functools.partial(_ssd_sample_pre_kernel, nb=nb, ntok=ntok),
        [pe, pe, pe, prev, prev, prm["ssd_cw"], prm["ssd_cw"], prm["ssd_cb"], prm["ssd_cb"],
         prm["ssd_dtb"], prm["ssd_alog"]],
        [_colblock(rows, D_SSD, 1), _colblock(rows, SSD_BC, 10), _colblock(rows, LANES, DT_COL // LANES),
         _colblock(n, D_SSD, 0), _colblock(n, SSD_BC, 2),
         _colblock(SSD_CONV, D_SSD, 0), _colblock(SSD_CONV, SSD_BC, 2),
         _colblock(1, D_SSD, 0), _colblock(1, SSD_BC, 2), _whole((1, LANES)), _whole((1, LANES))],
        [f(rows, D_SSD), f(rows, SSD_BC), f(rows, D_SSD), f(rows, D_SSD), f(rows, D_SSD),
         f(nb, LANES), f(n, D_SSD), f(n, SSD_BC)],
        [pltpu.VMEM((off + rows, D_SSD), F32), pltpu.VMEM((off + rows, SSD_BC), F32)],
        "ssd_sample_pre", _mib(56))
    xs, bc, vw, oi, ecum, elast, tailx, tailb = outs
    kw = SSD_GROUPS * SSD_STATE
    dec = jnp.broadcast_to(elast[:, :SSD_HEADS, None], (nb, SSD_HEADS, SSD_STATE))
    st = jnp.swapaxes(s_ssm, -1, -2)
    part = _state_part(_ssd_state_kernel,
                       [_rows_to_batch(bc[:, kw:], nb), _rows_to_batch(bc[:, :kw], nb),
                        _rows_to_batch(vw, nb)], dec, st, D_SSD, nseq, nc)
    return dict(oi=oi, ecum=ecum, xs=xs, tailx=tailx, tailb=tailb), part


def ssd_sample_post(ctx, u, st_new, pe, prm, nb, ntok):
    rows = nb * ntok
    s_new = jnp.swapaxes(st_new, -1, -2)
    u = _batch_to_rows(u, ntok)
    (y,) = _call_whole(
        _ssd_sample_post_kernel, [ctx["oi"], u, ctx["ecum"], ctx["xs"], pe, prm["ssd_dx"], prm["ssd_nw"]],
        [_whole((rows, D_SSD))] * 4 + [_colblock(rows, D_SSD, 0), _whole((1, D_SSD)), _whole((1, D_SSD))],
        [jax.ShapeDtypeStruct((rows, D_SSD), BF16)], [], "ssd_sample_post", _mib(48))
    conv_new = _from_token_major(jnp.concatenate([ctx["tailx"], ctx["tailb"]], axis=1), nb)
    return y, s_new, conv_new


def ret_sample_pre(pe, s_ret, cos, sin, nb, ntok, nseq, nc):
    rows = nb * ntok
    f = lambda *shape: jax.ShapeDtypeStruct(shape, F32)
    qd, kd, oi = _call_whole(
        functools.partial(_ret_sample_pre_kernel, nb=nb, ntok=ntok),
        [pe, pe, pe, cos, sin],
        [_colblock(rows, RET_QK, 8), _colblock(rows, RET_QK, 9), _colblock(rows, D_RET, 2),
         _whole(cos.shape), _whole(sin.shape)],
        [f(rows, RET_QK), f(rows, RET_QK), f(rows, D_RET)], [], "ret_sample_pre", _mib(48))
    v = _rows_to_batch(pe[:, 2 * D_RET:3 * D_RET], nb)
    part = _state_part(functools.partial(_ret_state_kernel, ntok=ntok),
                       [_rows_to_batch(qd, nb), _rows_to_batch(kd, nb), v], None, s_ret,
                       D_RET, nseq, nc)
    return oi, part


def ret_sample_post(oi, u, pe, nb, ntok):
    rows = nb * ntok
    u = _batch_to_rows(u, ntok)
    (o,) = _call_whole(
        _ret_sample_post_kernel, [oi, u, pe],
        [_whole((rows, D_RET)), _whole((rows, D_RET)), _colblock(rows, D_RET, 3)],
        [jax.ShapeDtypeStruct((rows, D_RET), BF16)], [], "ret_sample_post", _mib(40))
    return o


def hg_sample_pre(po, s_hg, lb, nb, ntok, nseq, nc):
    rows = nb * ntok
    f = lambda *shape: jax.ShapeDtypeStruct(shape, F32)
    qd, kd, dl, oi = _call_whole(
        functools.partial(_hg_sample_pre_kernel, nb=nb, ntok=ntok),
        [po, po, po, lb],
        [_colblock(rows, D_HG, 0), _colblock(rows, D_HG, 1), _colblock(rows, D_HG, 2), _whole((1, D_HG))],
        [f(rows, D_HG), f(rows, D_HG), f(nb, D_HG), f(rows, D_HG)], [], "hg_sample_pre", _mib(48))
    dcol = jnp.pad(jnp.transpose(dl.reshape(nb, HG_HEADS, HG_DK), (0, 2, 1)),
                   ((0, 0), (0, 0), (0, LANES - HG_HEADS)))
    v = _rows_to_batch(po[:, 2 * D_HG:3 * D_HG], nb)
    part = _state_part(_hg_state_kernel, [_rows_to_batch(qd, nb), _rows_to_batch(kd, nb), v],
                       dcol, s_hg, D_HG, nseq, nc)
    return oi, part


def hg_sample_post(oi, u, po, nw, nb, ntok):
    rows = nb * ntok
    u = _batch_to_rows(u, ntok)
    (og,) = _call_whole(
        _hg_sample_post_kernel, [oi, u, po, nw],
        [_whole((rows, D_HG)), _whole((rows, D_HG)), _colblock(rows, D_HG, 3), _whole((1, HG_DV))],
        [jax.ShapeDtypeStruct((rows, D_HG), BF16)], [], "hg_sample_post", _mib(40))
    return og


def lru_sample(po, s_lru, s_lconv, prm, nb, ntok):
    rows = nb * ntok
    n = (LRU_CONV - 1) * nb
    off = _conv_off(nb, LRU_CONV)
    prev = _to_token_major(s_lconv)
    yl, hfin, tail = _call_whole(
        functools.partial(_lru_sample_kernel, nb=nb, ntok=ntok),
        [po, po, prev, s_lru, prm["lru_cw"], prm["lru_cb"], prm["lru_wr"], prm["lru_br"], prm["lru_wi"],
         prm["lru_bi"], prm["lru_ap"]],
        [_colblock(rows, D_RNN, 4), _colblock(rows, D_RNN, 5), _whole((n, D_RNN)), _whole((nb, D_RNN)),
         _whole((LRU_CONV, D_RNN)), _whole((1, D_RNN)), _whole((LRU_BLOCKS, LRU_BW, LRU_BW)),
         _whole((1, D_RNN)), _whole((LRU_BLOCKS, LRU_BW, LRU_BW)), _whole((1, D_RNN)), _whole((1, D_RNN))],
        [jax.ShapeDtypeStruct((rows, D_RNN), BF16), jax.ShapeDtypeStruct((nb, D_RNN), F32),
         jax.ShapeDtypeStruct((n, D_RNN), F32)],
        [pltpu.VMEM((off + rows, D_RNN), F32)], "lru_sample", _mib(48))
    return yl, hfin, _from_token_major(tail, nb)


def _rope_tables(pos):
    half = RET_DK // 2
    inv = ROPE_BASE ** (-jnp.arange(half, dtype=F32) / half)
    ang = pos.astype(F32)[:, None] * inv[None, :]
    cos, sin = jnp.cos(ang), jnp.sin(ang)
    return jnp.concatenate([cos, cos], axis=1), jnp.concatenate([-sin, sin], axis=1)


def _pad_lanes(v):
    return jnp.pad(v.astype(F32), (0, LANES - v.shape[0])).reshape(1, LANES)


def _prepare(p):
    lbs = jnp.cumsum(jax.nn.softmax(p["hg_lower_bounds"].astype(F32), axis=0), axis=0)
    lbs = lbs - lbs[0]
    return {
        "even_wt": jnp.swapaxes(p["even_w_in"][0], 0, 1),
        "ssd_cw": p["ssd_conv_w"][0], "ssd_cb": p["ssd_conv_b"][0].reshape(1, -1),
        "ssd_dtb": _pad_lanes(p["ssd_dt_bias"][0]), "ssd_alog": _pad_lanes(p["ssd_A_log"][0]),
        "ssd_dx": jnp.repeat(p["ssd_D"][0], SSD_HEAD_DIM).reshape(1, D_SSD),
        "ssd_nw": p["ssd_norm_w"][0].reshape(1, D_SSD),
        "hg_lb": lbs[1].reshape(1, D_HG), "hg_nw": p["hg_norm_w"][0].reshape(1, HG_DV),
        "lru_cw": p["lru_conv_w"][0], "lru_cb": p["lru_conv_b"][0].reshape(1, D_RNN),
        "lru_wr": p["lru_w_r"][0], "lru_br": p["lru_b_r"][0].reshape(1, D_RNN),
        "lru_wi": p["lru_w_i"][0], "lru_bi": p["lru_b_i"][0].reshape(1, D_RNN),
        "lru_ap": p["lru_a_param"][0].reshape(1, D_RNN),
    }


def _even_row_offsets(bn):
    o_xbc, o_dt = D_SSD, 2 * D_SSD + SSD_BC
    o_q = o_dt + SSD_HEADS
    o_k, o_v = o_q + RET_QK, o_q + 2 * RET_QK
    o_g = o_v + D_RET
    segs = [(0, D_SSD), (o_xbc, D_SSD), (o_v, D_RET), (o_g, D_RET), (o_q, RET_QK), (o_k, RET_QK),
            (o_xbc + D_SSD, SSD_BC), (o_dt, bn)]
    offs = [start + i for start, width in segs for i in range(0, width, bn)]
    assert len(offs) * bn == EVEN_PACKED and offs[-1] + bn <= D_IN_EVEN
    return offs


def _even_proj(x, p, prm, bm):
    return mm_in_t(x, p["norm_mix"][0], prm["even_wt"], _even_row_offsets(512), bm, 512)


def _ffn(x, p, prm, l, prev, shift, group_rows, bm, bf=512):
    act, st = ffn_up(x, p["norm_ffn"][l], p["ffn_w_up"], l, p["ffn_conv_w"][l], p["ffn_conv_b"][l],
                     prev, shift, group_rows, bm, bf)
    return mm_out([act], p["ffn_w_down"], l, x, min(bm, 1024), 256), st


def _trunks(xp, xs, st, p, prm, nseq, seq, nb, ntok):
    m = nseq * seq
    rows = nb * ntok
    bm = min(2048, seq)
    nc = seq // MIX_CHUNK
    grid = (nseq, nc)
    cos_p, sin_p = _rope_tables(jnp.arange(seq, dtype=jnp.int32))
    cos_s, sin_s = _rope_tables(PAST_LEN + jnp.arange(ntok, dtype=jnp.int32))
    zeros_ffn = jnp.zeros((nseq, FFN_CONV - 1, D_FF), F32)

    pe_s = _even_proj(xs, p, prm, rows)
    ssd_ctx, ssd_part = ssd_sample_pre(pe_s, st["ssm"][0], st["ssm_conv"][0], prm, nb, ntok, nseq, nc)
    ret_oi, ret_part = ret_sample_pre(pe_s, st["ret"][0], cos_s, sin_s, nb, ntok, nseq, nc)
    pe_p = _even_proj(xp, p, prm, bm)
    (y, ssm_p, ssm_conv_p), (o, ret_p), (u_ssd, st_ssd), (u_ret, ret_s) = _run_parts(
        [ssd_prompt(pe_p, prm, nseq, seq, MIX_CHUNK), ret_prompt(pe_p, cos_p, sin_p, nseq, seq, MIX_CHUNK),
         ssd_part, ret_part], grid, "even_mix", _mib(58))
    xp = mm_out([y, o], p["even_w_out"], 0, xp, min(bm, 1024), 512)
    xp, ffn0_p = _ffn(xp, p, prm, 0, zeros_ffn, 1, seq, bm, 256)
    y_s, ssm_s, ssm_conv_s = ssd_sample_post(ssd_ctx, u_ssd, st_ssd, pe_s, prm, nb, ntok)
    o_s = ret_sample_post(ret_oi, u_ret, pe_s, nb, ntok)
    xs = mm_out([y_s, o_s], p["even_w_out"], 0, xs, rows, 512)
    xs, ffn0_s = _ffn(xs, p, prm, 0, _to_token_major(st["ffn_conv"][0])[None], nb, rows, rows)

    po_s = mm_in(xs, p["norm_mix"][1], p["odd_w_in"][0], rows, 512)
    hg_oi, hg_part = hg_sample_pre(po_s, st["hgrn"][0], prm["hg_lb"], nb, ntok, nseq, nc)
    yl_s, lru_s, lru_conv_s = lru_sample(po_s, st["lru"][0], st["lru_conv"][0], prm, nb, ntok)
    po_p = mm_in(xp, p["norm_mix"][1], p["odd_w_in"][0], bm, 512)
    (og, hgrn_p), (yl, lru_p, lru_conv_p), (u_hg, hgrn_s) = _run_parts(
        [hg_prompt(po_p, prm["hg_lb"], prm["hg_nw"], nseq, seq, MIX_CHUNK),
         lru_prompt(po_p, prm, nseq, seq, MIX_CHUNK), hg_part], grid, "odd_mix", _mib(58))
    xp = mm_out([og, yl], p["odd_w_out"], 0, xp, min(bm, 1024), 512)
    xp, ffn1_p = _ffn(xp, p, prm, 1, zeros_ffn, 1, seq, bm, 256)
    og_s = hg_sample_post(hg_oi, u_hg, po_s, prm["hg_nw"], nb, ntok)
    xs = mm_out([og_s, yl_s], p["odd_w_out"], 0, xs, rows, 512)
    xs, ffn1_s = _ffn(xs, p, prm, 1, _to_token_major(st["ffn_conv"][1])[None], nb, rows, rows)

    y_p = rmsnorm(xp, p["norm_final"], min(512, m)).reshape(nseq, seq, D_MODEL)
    y_s = jnp.transpose(rmsnorm(xs, p["norm_final"], rows).reshape(ntok, nb, D_MODEL), (1, 0, 2))
    ffn_s = jnp.stack([_from_token_major(ffn0_s[0], nb), _from_token_major(ffn1_s[0], nb)])
    return (y_p, y_s, ssm_p[None], ssm_s[None], ssm_conv_p[None], ssm_conv_s[None],
            ret_p[None], ret_s[None], hgrn_p[None], hgrn_s[None],
            lru_p.reshape(1, nseq, D_RNN), lru_s[None], lru_conv_p[None], lru_conv_s[None],
            jnp.stack([ffn0_p, ffn1_p]), ffn_s)


def kernel(x_prompt, x_sample, state_ssm, state_ssm_conv, state_ret, state_hgrn, state_lru, state_lru_conv, state_ffn_conv, norm_mix, norm_ffn, norm_final, even_w_in, ssd_conv_w, ssd_conv_b, ssd_dt_bias, ssd_A_log, ssd_D, ssd_norm_w, even_w_out, odd_w_in, hg_lower_bounds, hg_norm_w, lru_conv_w, lru_conv_b, lru_w_r, lru_b_r, lru_w_i, lru_b_i, lru_a_param, odd_w_out, ffn_w_up, ffn_conv_w, ffn_conv_b, ffn_w_down):
    p = {
        "norm_mix": norm_mix, "norm_ffn": norm_ffn, "norm_final": norm_final,
        "even_w_in": even_w_in, "ssd_conv_w": ssd_conv_w, "ssd_conv_b": ssd_conv_b,
        "ssd_dt_bias": ssd_dt_bias, "ssd_A_log": ssd_A_log, "ssd_D": ssd_D,
        "ssd_norm_w": ssd_norm_w, "even_w_out": even_w_out, "odd_w_in": odd_w_in,
        "hg_lower_bounds": hg_lower_bounds, "hg_norm_w": hg_norm_w,
        "lru_conv_w": lru_conv_w, "lru_conv_b": lru_conv_b, "lru_w_r": lru_w_r,
        "lru_b_r": lru_b_r, "lru_w_i": lru_w_i, "lru_b_i": lru_b_i,
        "lru_a_param": lru_a_param, "odd_w_out": odd_w_out, "ffn_w_up": ffn_w_up,
        "ffn_conv_w": ffn_conv_w, "ffn_conv_b": ffn_conv_b, "ffn_w_down": ffn_w_down,
    }
    prm = _prepare(p)
    nseq, seq, _ = x_prompt.shape
    nb, ntok, _ = x_sample.shape
    st = {"ssm": state_ssm, "ssm_conv": state_ssm_conv, "ret": state_ret, "hgrn": state_hgrn,
          "lru": state_lru, "lru_conv": state_lru_conv, "ffn_conv": state_ffn_conv}
    xs_tm = jnp.transpose(x_sample, (1, 0, 2)).reshape(ntok * nb, D_MODEL)
    return _trunks(x_prompt.reshape(nseq * seq, D_MODEL), xs_tm, st, p, prm, nseq, seq, nb, ntok)
```

```python
import functools
import math

import numpy as np
import jax
import jax.numpy as jnp
from jax import lax
from jax.experimental import pallas as pl
from jax.experimental.pallas import tpu as pltpu

F32 = jnp.float32
BF16 = jnp.bfloat16
EPS = 1e-6
LOG2E = math.log2(math.e)

D_MODEL = 2048
PAST_LEN = 16384
SSD_HEADS = 32
SSD_HEAD_DIM = 64
D_SSD = SSD_HEADS * SSD_HEAD_DIM
SSD_GROUPS = 4
SSD_STATE = 128
SSD_CONV = 4
SSD_BC = 2 * SSD_GROUPS * SSD_STATE
RET_HEADS = 8
RET_DK = 128
RET_DV = 256
RET_QK = RET_HEADS * RET_DK
D_RET = RET_HEADS * RET_DV
ROPE_BASE = 10000.0
HG_HEADS = 16
HG_DK = 128
HG_DV = 128
D_HG = HG_HEADS * HG_DV
D_RNN = 2048
LRU_BLOCKS = 8
LRU_BW = D_RNN // LRU_BLOCKS
LRU_CONV = 4
LRU_C = 8.0
D_FF = 5632
FFN_CONV = 3
D_IN_EVEN = D_SSD + (D_SSD + SSD_BC) + SSD_HEADS + 2 * RET_QK + 2 * D_RET
EVEN_PACKED = 11776
DT_COL = 11264

V7X_VMEM_BYTES = 64 * 1024 * 1024
V7X_VMEM_CAP = 60 * 1024 * 1024
LANES = 128
SUBLANES = 8

RET_LOG_GAMMA = [float(v) for v in np.log1p(-np.exp(np.linspace(
    math.log(1.0 / 32.0), math.log(1.0 / 512.0), RET_HEADS, dtype=np.float32))).astype(np.float32)]

NT = (((1,), (1,)), ((), ()))
TN = (((0,), (0,)), ((), ()))


def _cparams(sem, vmem_bytes):
    return pltpu.CompilerParams(dimension_semantics=sem,
                                vmem_limit_bytes=int(min(V7X_VMEM_CAP, vmem_bytes)))


def _mib(n):
    return n * 1024 * 1024


def _silu(x):
    return x * jax.nn.sigmoid(x)


def _softplus(x):
    return jnp.maximum(x, 0.0) + jnp.log1p(jnp.exp(-jnp.abs(x)))


def _gelu_tanh(x):
    return 0.5 * x * (1.0 + jnp.tanh(math.sqrt(2.0 / math.pi) * (x + 0.044715 * (x * x * x))))


def _rms(x):
    return x * lax.rsqrt(jnp.mean(x * x, axis=-1, keepdims=True) + EPS)


def _dot(a, b):
    return jnp.dot(a, b, preferred_element_type=F32)


def _tril_ones(n):
    r = lax.broadcasted_iota(jnp.int32, (n, n), 0)
    c = lax.broadcasted_iota(jnp.int32, (n, n), 1)
    return (r >= c).astype(F32)


def _cumsum_rows(x):
    return jnp.dot(_tril_ones(x.shape[0]), x, precision=lax.Precision.HIGHEST,
                   preferred_element_type=F32)


def _shifted_rows(buf_ref, start, rows, back):
    if back % SUBLANES == 0:
        return buf_ref[start - back:start - back + rows, :]
    assert back < SUBLANES and start % SUBLANES == 0
    ext = buf_ref[start - SUBLANES:start + rows, :]
    return pltpu.roll(ext, back, 0)[SUBLANES:, :]


def _causal_conv(buf_ref, x, w_ref, b_ref, shift, width):
    rows = x.shape[0]
    off = buf_ref.shape[0] - rows
    n = (width - 1) * shift
    buf_ref[off:off + rows, :] = x
    acc = None
    for j in range(width):
        term = _shifted_rows(buf_ref, off, rows, (width - 1 - j) * shift) * w_ref[j:j + 1, :]
        acc = term if acc is None else acc + term
    out = b_ref[...] + acc
    tail = buf_ref[off + rows - n:off + rows, :]
    buf_ref[off - n:off, :] = tail
    return out, tail


def _conv_off(shift, width):
    n = (width - 1) * shift
    return -(-n // SUBLANES) * SUBLANES


NORM_ROWS = 256


def _norm_rows_to(x_ref, nw_ref, xn_ref):
    rows = x_ref.shape[0]
    step = min(NORM_ROWS, rows)

    def body(i, carry):
        r = pl.multiple_of(i * step, step)
        xn_ref[pl.ds(r, step), :] = (_rms(x_ref[pl.ds(r, step), :]) * nw_ref[...]).astype(BF16)
        return carry

    lax.fori_loop(0, rows // step, body, 0)


def _mm_in_kernel(x_ref, nw_ref, w_ref, o_ref, xn_ref):
    @pl.when(pl.program_id(1) == 0)
    def _():
        _norm_rows_to(x_ref, nw_ref, xn_ref)
    o_ref[...] = _dot(xn_ref[...], w_ref[...].astype(BF16))


def _mm_in_vmem(bm, d, bn):
    return (bm * d * 4 + bm * d * 2 + 2 * d * bn * 4 + d * bn * 2 + 2 * bm * bn * 4
            + 4 * NORM_ROWS * d * 4 + _mib(6))


def mm_in(x, nw, w, bm, bn):
    m, d = x.shape
    n = w.shape[1]
    vmem = _mm_in_vmem(bm, d, bn)
    return pl.pallas_call(
        _mm_in_kernel,
        grid=(m // bm, n // bn),
        in_specs=[pl.BlockSpec((bm, d), lambda i, j: (i, 0), pipeline_mode=pl.Buffered(1)),
                  pl.BlockSpec((1, d), lambda i, j: (0, 0)),
                  pl.BlockSpec((d, bn), lambda i, j: (0, j))],
        out_specs=pl.BlockSpec((bm, bn), lambda i, j: (i, j)),
        out_shape=jax.ShapeDtypeStruct((m, n), F32),
        scratch_shapes=[pltpu.VMEM((bm, d), BF16)],
        compiler_params=_cparams(("parallel", "arbitrary"), vmem),
        name="mm_in",
    )(x, nw.reshape(1, d), w)


def _mm_in_t_kernel(offs_ref, x_ref, nw_ref, wt_ref, o_ref, xn_ref):
    del offs_ref
    @pl.when(pl.program_id(1) == 0)
    def _():
        _norm_rows_to(x_ref, nw_ref, xn_ref)
    o_ref[...] = lax.dot_general(xn_ref[...], wt_ref[...].astype(BF16), NT, preferred_element_type=F32)


ROW_ALIGN = 32


def mm_in_t(x, nw, wt, row_offsets, bm, bn):
    m, d = x.shape
    nblk = len(row_offsets)
    assert all(o % ROW_ALIGN == 0 for o in row_offsets)
    vmem = _mm_in_vmem(bm, d, bn)
    grid_spec = pltpu.PrefetchScalarGridSpec(
        num_scalar_prefetch=1,
        grid=(m // bm, nblk),
        in_specs=[pl.BlockSpec((bm, d), lambda i, j, offs: (i, 0), pipeline_mode=pl.Buffered(1)),
                  pl.BlockSpec((1, d), lambda i, j, offs: (0, 0)),
                  pl.BlockSpec((pl.Element(bn), pl.Element(d)),
                               lambda i, j, offs: (offs[j] * ROW_ALIGN, 0))],
        out_specs=pl.BlockSpec((bm, bn), lambda i, j, offs: (i, j)),
        scratch_shapes=[pltpu.VMEM((bm, d), BF16)])
    return pl.pallas_call(
        _mm_in_t_kernel,
        grid_spec=grid_spec,
        out_shape=jax.ShapeDtypeStruct((m, nblk * bn), F32),
        compiler_params=_cparams(("parallel", "arbitrary"), vmem),
        name="mm_in_t",
    )(jnp.asarray([o // ROW_ALIGN for o in row_offsets], jnp.int32), x, nw.reshape(1, d), wt)


def _mm_out_kernel(*refs, nparts):
    a_refs = refs[:nparts]
    w_refs = refs[nparts:2 * nparts]
    r_ref = refs[2 * nparts]
    o_ref = refs[2 * nparts + 1]
    acc = r_ref[...]
    for a_ref, w_ref in zip(a_refs, w_refs):
        acc = acc + _dot(a_ref[...], w_ref[...].astype(BF16))
    o_ref[...] = acc


def mm_out(parts, w, layer, resid, bm, bn):
    nparts = len(parts)
    m, kp = parts[0].shape
    n = w.shape[2]
    vmem = nparts * (2 * bm * kp * 2 + 2 * kp * bn * 4 + kp * bn * 2) + 4 * bm * bn * 4 + 2 * bm * bn * 4 + _mib(6)
    in_specs = [pl.BlockSpec((bm, kp), lambda i, j: (i, 0)) for _ in range(nparts)]
    in_specs += [pl.BlockSpec((None, kp, bn), functools.partial(lambda i, j, p: (layer, p, j), p=p))
                 for p in range(nparts)]
    in_specs += [pl.BlockSpec((bm, bn), lambda i, j: (i, j))]
    return pl.pallas_call(
        functools.partial(_mm_out_kernel, nparts=nparts),
        grid=(m // bm, n // bn),
        in_specs=in_specs,
        out_specs=pl.BlockSpec((bm, bn), lambda i, j: (i, j)),
        out_shape=jax.ShapeDtypeStruct((m, n), F32),
        compiler_params=_cparams(("parallel", "parallel"), vmem),
        name="mm_out",
    )(*parts, *([w] * nparts), resid)


def _rmsnorm_kernel(x_ref, nw_ref, o_ref):
    o_ref[...] = _rms(x_ref[...]) * nw_ref[...]


def rmsnorm(x, nw, bm):
    m, d = x.shape
    return pl.pallas_call(
        _rmsnorm_kernel,
        grid=(m // bm,),
        in_specs=[pl.BlockSpec((bm, d), lambda i: (i, 0)), pl.BlockSpec((1, d), lambda i: (0, 0))],
        out_specs=pl.BlockSpec((bm, d), lambda i: (i, 0)),
        out_shape=jax.ShapeDtypeStruct((m, d), F32),
        compiler_params=_cparams(("parallel",), 6 * bm * d * 4 + _mib(4)),
        name="rmsnorm",
    )(x, nw.reshape(1, d))


FFN_SLAB = 512


def _ffn_up_kernel(x_ref, nw_ref, wg_ref, wu_ref, cw_ref, cb_ref, prev_ref,
                   act_ref, st_ref, xn_ref, gbuf_ref, carry_ref, *, shift, blocks_per_group):
    i = pl.program_id(0)
    j = pl.program_id(1)
    bm = x_ref.shape[0]
    n = (FFN_CONV - 1) * shift
    off = gbuf_ref.shape[0] - bm

    @pl.when(j == 0)
    def _():
        _norm_rows_to(x_ref, nw_ref, xn_ref)

    first = (i % blocks_per_group) == 0

    @pl.when(first)
    def _():
        gbuf_ref[off - n:off, :] = prev_ref[0]

    @pl.when(jnp.logical_not(first))
    def _():
        gbuf_ref[off - n:off, :] = carry_ref[j]

    wg = wg_ref[...].astype(BF16)
    wu = wu_ref[...].astype(BF16)
    slab = min(FFN_SLAB, bm)
    for s in range(bm // slab):
        r0 = s * slab
        xs = xn_ref[r0:r0 + slab, :]
        gbuf_ref[off + r0:off + r0 + slab, :] = _dot(xs, wg)
        u = _dot(xs, wu)
        gc = cb_ref[...]
        for t in range(FFN_CONV):
            back = (FFN_CONV - 1 - t) * shift
            gc = gc + _shifted_rows(gbuf_ref, off + r0, slab, back) * cw_ref[t:t + 1, :]
        act_ref[r0:r0 + slab, :] = (_silu(gc) * u).astype(BF16)
    tail = gbuf_ref[off + bm - n:off + bm, :]
    carry_ref[j] = tail
    st_ref[0] = tail


def ffn_up(x, nw, w_up, layer, cw, cb, prev, shift, group_rows, bm, bf):
    m, d = x.shape
    f = cw.shape[1]
    n = (FFN_CONV - 1) * shift
    off = _conv_off(shift, FFN_CONV)
    bpg = group_rows // bm
    nf = f // bf
    vmem = (bm * d * 4 + bm * d * 2 + 4 * d * bf * 4 + 2 * d * bf * 2 + 2 * bm * bf * 2
            + (off + bm) * bf * 4 + nf * max(n, SUBLANES) * bf * 4 + 4 * n * bf * 4 + 5 * bm * bf * 4
            + 4 * NORM_ROWS * d * 4 + _mib(6))
    act, st = pl.pallas_call(
        functools.partial(_ffn_up_kernel, shift=shift, blocks_per_group=bpg),
        grid=(m // bm, nf),
        in_specs=[pl.BlockSpec((bm, d), lambda i, j: (i, 0), pipeline_mode=pl.Buffered(1)),
                  pl.BlockSpec((1, d), lambda i, j: (0, 0)),
                  pl.BlockSpec((None, d, bf), lambda i, j: (layer, 0, j)),
                  pl.BlockSpec((None, d, bf), lambda i, j: (layer, 0, j + nf)),
                  pl.BlockSpec((FFN_CONV, bf), lambda i, j: (0, j)),
                  pl.BlockSpec((1, bf), lambda i, j: (0, j)),
                  pl.BlockSpec((1, n, bf), lambda i, j: (i // bpg, 0, j))],
        out_specs=[pl.BlockSpec((bm, bf), lambda i, j: (i, j)),
                   pl.BlockSpec((1, n, bf), lambda i, j: (i, 0, j))],
        out_shape=[jax.ShapeDtypeStruct((m, f), BF16),
                   jax.ShapeDtypeStruct((m // bm, n, f), F32)],
        scratch_shapes=[pltpu.VMEM((bm, d), BF16),
                        pltpu.VMEM((off + bm, bf), F32),
                        pltpu.VMEM((nf, n, bf), F32)],
        compiler_params=_cparams(("arbitrary", "arbitrary"), vmem),
        name="ffn_up",
    )(x, nw.reshape(1, d), w_up, w_up, cw, cb.reshape(1, f), prev)
    return act, st[bpg - 1::bpg]


def _expand_heads64(x, nheads):
    r = x.shape[0]
    lo = lax.broadcasted_iota(jnp.int32, (r, LANES), 1) < SSD_HEAD_DIM
    tiles = []
    for p in range(nheads // 2):
        a0 = jnp.broadcast_to(x[:, 2 * p:2 * p + 1], (r, LANES))
        a1 = jnp.broadcast_to(x[:, 2 * p + 1:2 * p + 2], (r, LANES))
        tiles.append(jnp.where(lo, a0, a1))
    return jnp.concatenate(tiles, axis=1)


def _ssd_gates(xs_c, bc_c, dt_raw, dtb, alog):
    xs = _silu(xs_c)
    bc = _silu(bc_c)
    dt = _softplus(dt_raw + dtb)
    la = dt * (-jnp.exp(alog))
    return xs, bc, dt, la


def _ssd_finish(o, xs, z, dx, nw):
    y = (o + dx * xs) * _silu(z)
    gw = D_SSD // SSD_GROUPS
    y = jnp.concatenate([_rms(y[:, g * gw:(g + 1) * gw]) for g in range(SSD_GROUPS)], axis=1)
    return y * nw


def _ssd_prompt_kernel(z_ref, xs_ref, bc_ref, dt_ref, cwx_ref, cwb_ref, cbx_ref, cbb_ref,
                       dtb_ref, alog_ref, dx_ref, nw_ref,
                       y_ref, sfin_ref, cfin_ref, s_ref, bufx_ref, bufb_ref):
    c = pl.program_id(1)
    nc = pl.num_programs(1)
    rows = xs_ref.shape[0]
    gw = D_SSD // SSD_GROUPS
    hpg = SSD_HEADS // SSD_GROUPS

    @pl.when(c == 0)
    def _():
        s_ref[...] = jnp.zeros_like(s_ref)
        bufx_ref[0:SUBLANES, :] = jnp.zeros((SUBLANES, D_SSD), F32)
        bufb_ref[0:SUBLANES, :] = jnp.zeros((SUBLANES, SSD_BC), F32)

    yield
    xs_c, tailx = _causal_conv(bufx_ref, xs_ref[...], cwx_ref, cbx_ref, 1, SSD_CONV)
    bc_c, tailb = _causal_conv(bufb_ref, bc_ref[...], cwb_ref, cbb_ref, 1, SSD_CONV)
    xs, bc, dt, la = _ssd_gates(xs_c, bc_c, dt_ref[...], dtb_ref[...], alog_ref[...])
    cum = _cumsum_rows(la * LOG2E)
    cum_t = cum.T
    cumx = _expand_heads64(cum, SSD_HEADS)
    dtx = _expand_heads64(dt, SSD_HEADS)
    lastx = cumx[rows - 1:rows, :]
    ecum = jnp.exp2(cumx)
    wx = jnp.exp2(lastx - cumx)
    elast = jnp.exp2(lastx)
    v_all = xs * dtx
    vw_all = v_all * wx

    ri = lax.broadcasted_iota(jnp.int32, (rows, rows), 0)
    ci = lax.broadcasted_iota(jnp.int32, (rows, rows), 1)
    causal = ri >= ci
    lo = lax.broadcasted_iota(jnp.int32, (rows, LANES), 1) < SSD_HEAD_DIM

    o_groups = []
    for g in range(SSD_GROUPS):
        kb = bc[:, g * SSD_STATE:(g + 1) * SSD_STATE].astype(BF16)
        qb = bc[:, (SSD_GROUPS + g) * SSD_STATE:(SSD_GROUPS + g + 1) * SSD_STATE].astype(BF16)
        qk = lax.dot_general(qb, kb, NT, preferred_element_type=F32)
        o_tiles = []
        for p in range(hpg // 2):
            acc = None
            for q in range(2):
                h = g * hpg + 2 * p + q
                diff = cum[:, h:h + 1] - cum_t[h:h + 1, :]
                dec = jnp.exp2(jnp.where(causal, diff, -1e30))
                pm = (qk * dec).astype(BF16)
                col = g * gw + p * LANES
                vp = v_all[:, col:col + LANES]
                vh = jnp.where(lo, vp, 0.0) if q == 0 else jnp.where(lo, 0.0, vp)
                t = _dot(pm, vh.astype(BF16))
                acc = t if acc is None else acc + t
            o_tiles.append(acc)
        o_intra = jnp.concatenate(o_tiles, axis=1)
        sg = s_ref[g]
        sl = slice(g * gw, (g + 1) * gw)
        o_inter = _dot(qb, sg.astype(BF16)) * ecum[:, sl]
        upd = lax.dot_general(kb, vw_all[:, sl].astype(BF16), TN, preferred_element_type=F32)
        s_ref[g] = sg * elast[:, sl] + upd
        o_groups.append(o_intra + o_inter)
    o = jnp.concatenate(o_groups, axis=1)
    y_ref[...] = _ssd_finish(o, xs, z_ref[...], dx_ref[...], nw_ref[...]).astype(BF16)

    yield
    @pl.when(c == nc - 1)
    def _():
        for h in range(SSD_HEADS):
            g, hl = divmod(h, hpg)
            sfin_ref[0, h] = s_ref[g, :, hl * SSD_HEAD_DIM:(hl + 1) * SSD_HEAD_DIM]
        cfin_ref[0, :, 0:D_SSD] = tailx
        cfin_ref[0, :, D_SSD:D_SSD + SSD_BC] = tailb


def ssd_prompt(pe, prm, nseq, seq, chunk):
    m = nseq * seq
    nc = seq // chunk
    row = lambda b, c: b * nc + c
    full = lambda shp: pl.BlockSpec(shp, lambda b, c: (0,) * len(shp))
    in_specs = [
        pl.BlockSpec((chunk, D_SSD), lambda b, c: (row(b, c), 0)),
        pl.BlockSpec((chunk, D_SSD), lambda b, c: (row(b, c), 1)),
        pl.BlockSpec((chunk, SSD_BC), lambda b, c: (row(b, c), 10)),
        pl.BlockSpec((chunk, LANES), lambda b, c: (row(b, c), DT_COL // LANES)),
        pl.BlockSpec((SSD_CONV, D_SSD), lambda b, c: (0, 0)),
        pl.BlockSpec((SSD_CONV, SSD_BC), lambda b, c: (0, 2)),
        pl.BlockSpec((1, D_SSD), lambda b, c: (0, 0)),
        pl.BlockSpec((1, SSD_BC), lambda b, c: (0, 2)),
        full((1, LANES)), full((1, LANES)), full((1, D_SSD)), full((1, D_SSD)),
    ]
    return dict(
        kernel=_ssd_prompt_kernel,
        arrays=[pe, pe, pe, pe, prm["ssd_cw"], prm["ssd_cw"], prm["ssd_cb"], prm["ssd_cb"],
                prm["ssd_dtb"], prm["ssd_alog"], prm["ssd_dx"], prm["ssd_nw"]],
        in_specs=in_specs,
        out_specs=[pl.BlockSpec((chunk, D_SSD), lambda b, c: (row(b, c), 0)),
                   pl.BlockSpec((1, SSD_HEADS, SSD_STATE, SSD_HEAD_DIM), lambda b, c: (b, 0, 0, 0)),
                   pl.BlockSpec((1, SSD_CONV - 1, D_SSD + SSD_BC), lambda b, c: (b, 0, 0))],
        out_shape=[jax.ShapeDtypeStruct((m, D_SSD), BF16),
                   jax.ShapeDtypeStruct((nseq, SSD_HEADS, SSD_STATE, SSD_HEAD_DIM), F32),
                   jax.ShapeDtypeStruct((nseq, SSD_CONV - 1, D_SSD + SSD_BC), F32)],
        scratch=[pltpu.VMEM((SSD_GROUPS, SSD_STATE, D_SSD // SSD_GROUPS), F32),
                 pltpu.VMEM((SUBLANES + chunk, D_SSD), F32),
                 pltpu.VMEM((SUBLANES + chunk, SSD_BC), F32)])


MIX_CHUNK = 128


def _run_parts(parts, grid, name, vmem):
    counts = [(len(p["arrays"]), len(p["out_shape"]), len(p["scratch"])) for p in parts]
    n_in = sum(c[0] for c in counts)
    n_out = sum(c[1] for c in counts)

    def body(*refs):
        i, o, s = 0, n_in, n_in + n_out
        gens = []
        for p, (a, b, c) in zip(parts, counts):
            gens.append(p["kernel"](*refs[i:i + a], *refs[o:o + b], *refs[s:s + c]))
            i, o, s = i + a, o + b, s + c
        for _ in range(3):
            for g in gens:
                next(g, None)

    outs = pl.pallas_call(
        body,
        grid=grid,
        in_specs=[sp for p in parts for sp in p["in_specs"]],
        out_specs=[sp for p in parts for sp in p["out_specs"]],
        out_shape=[sh for p in parts for sh in p["out_shape"]],
        scratch_shapes=[sc for p in parts for sc in p["scratch"]],
        compiler_params=_cparams(("parallel", "arbitrary"), vmem),
        name=name,
    )(*[a for p in parts for a in p["arrays"]])
    res, k = [], 0
    for _, b, _ in counts:
        res.append(outs[k:k + b])
        k += b
    return res


def _rotary(x, cos, sin_signed):
    return x * cos + pltpu.roll(x, RET_DK // 2, 1) * sin_signed


def _ret_prompt_kernel(q_ref, k_ref, v_ref, g_ref, cos_ref, sin_ref, o_ref, sfin_ref, s_ref):
    c = pl.program_id(1)
    nc = pl.num_programs(1)
    rows = q_ref.shape[0]

    @pl.when(c == 0)
    def _():
        s_ref[...] = jnp.zeros_like(s_ref)

    yield
    cos = cos_ref[...]
    sin = sin_ref[...]
    ri = lax.broadcasted_iota(jnp.int32, (rows, rows), 0)
    ci = lax.broadcasted_iota(jnp.int32, (rows, rows), 1)
    dij = jnp.where(ri >= ci, (ri - ci).astype(F32), 1e30)
    tk = lax.broadcasted_iota(jnp.int32, (rows, RET_DK), 0).astype(F32)
    tv = lax.broadcasted_iota(jnp.int32, (rows, RET_DV), 0).astype(F32)
    for h in range(RET_HEADS):
        lg = RET_LOG_GAMMA[h]
        lg2 = lg * LOG2E
        qr = _rotary(q_ref[:, h * RET_DK:(h + 1) * RET_DK], cos, sin)
        kr = _rotary(k_ref[:, h * RET_DK:(h + 1) * RET_DK], cos, sin) * (RET_DK ** -0.5)
        qb = qr.astype(BF16)
        s = lax.dot_general(qb, kr.astype(BF16), NT, preferred_element_type=F32)
        pm = (s * jnp.exp2(lg2 * dij)).astype(BF16)
        vb = v_ref[:, h * RET_DV:(h + 1) * RET_DV].astype(BF16)
        sh = s_ref[h]
        o = _dot(pm, vb) + _dot(qb, sh.astype(BF16)) * jnp.exp2(lg2 * (tv + 1.0))
        kw = (kr * jnp.exp2(lg2 * ((rows - 1.0) - tk))).astype(BF16)
        s_ref[h] = sh * math.exp(lg * rows) + lax.dot_general(kw, vb, TN, preferred_element_type=F32)
        gh = g_ref[:, h * RET_DV:(h + 1) * RET_DV]
        o_ref[:, h * RET_DV:(h + 1) * RET_DV] = (_rms(o) * _silu(gh)).astype(BF16)

    yield
    @pl.when(c == nc - 1)
    def _():
        sfin_ref[0] = s_ref[...]


def ret_prompt(pe, cos, sin, nseq, seq, chunk):
    m = nseq * seq
    nc = seq // chunk
    row = lambda b, c: b * nc + c
    return dict(
        kernel=_ret_prompt_kernel,
        arrays=[pe, pe, pe, pe, cos, sin],
        in_specs=[pl.BlockSpec((chunk, RET_QK), lambda b, c: (row(b, c), 8)),
                  pl.BlockSpec((chunk, RET_QK), lambda b, c: (row(b, c), 9)),
                  pl.BlockSpec((chunk, D_RET), lambda b, c: (row(b, c), 2)),
                  pl.BlockSpec((chunk, D_RET), lambda b, c: (row(b, c), 3)),
                  pl.BlockSpec((chunk, RET_DK), lambda b, c: (c, 0)),
                  pl.BlockSpec((chunk, RET_DK), lambda b, c: (c, 0))],
        out_specs=[pl.BlockSpec((chunk, D_RET), lambda b, c: (row(b, c), 0)),
                   pl.BlockSpec((1, RET_HEADS, RET_DK, RET_DV), lambda b, c: (b, 0, 0, 0))],
        out_shape=[jax.ShapeDtypeStruct((m, D_RET), BF16),
                   jax.ShapeDtypeStruct((nseq, RET_HEADS, RET_DK, RET_DV), F32)],
        scratch=[pltpu.VMEM((RET_HEADS, RET_DK, RET_DV), F32)])


HG_HB = HG_HEADS


def _hg_gates(hq, hf, lb):
    q = _silu(hq)
    f = lb + (1.0 - lb) * jax.nn.sigmoid(hf)
    return q, f, 1.0 - f, jnp.log(f)


def _hg_tables(rows):
    r = np.arange(rows)[:, None]
    t = np.arange(rows)[None, :]
    sums = [t <= r]
    masks = []
    s = rows // 2
    while s >= 1:
        blk, pos = r // (2 * s), r % (2 * s)
        ref = blk * 2 * s + s - 1
        upper = pos >= s
        sums.append(np.where(upper, (t > ref) & (t <= r), (t > r) & (t <= ref)))
        masks.append((blk == t // (2 * s)) & upper & (t % (2 * s) < s))
        s //= 2
    masks.append(r == t)
    return (np.concatenate(sums, axis=0).astype(np.float32),
            np.stack(masks).astype(np.float32))


def _hg_prompt_kernel(hq_ref, hf_ref, hi_ref, hg_ref, lb_ref, nw_ref, sums_ref, masks_ref,
                      og_ref, sfin_ref, st_ref):
    c = pl.program_id(1)
    nc = pl.num_programs(1)
    rows = hq_ref.shape[0]
    nlev = masks_ref.shape[0] - 1

    @pl.when(c == 0)
    def _():
        st_ref[...] = jnp.zeros_like(st_ref)

    yield
    q_all, _, kk_all, lg_all = _hg_gates(hq_ref[...], hf_ref[...], lb_ref[...])
    lg_all = lg_all * LOG2E
    lg_hi = lg_all.astype(BF16)
    lg_lo = (lg_all - lg_hi.astype(F32)).astype(BF16)
    sums = sums_ref[...]
    dall = _dot(sums, lg_hi) + _dot(sums, lg_lo)
    cum_all = dall[0:rows]
    ecum_all = jnp.exp2(cum_all)
    for h in range(HG_HB):
        sl = slice(h * HG_DK, (h + 1) * HG_DK)
        q = q_all[:, sl]
        kk = kk_all[:, sl]
        cum = cum_all[:, sl]
        vb = hi_ref[:, sl].astype(BF16)
        last = cum[rows - 1:rows, :]
        st = st_ref[h]
        o = lax.dot_general((q * ecum_all[:, sl]).astype(BF16), st.astype(BF16), NT,
                            preferred_element_type=F32)
        kt = (kk * jnp.exp2(last - cum)).astype(BF16)
        st_ref[h] = st * jnp.exp2(last) + lax.dot_general(vb, kt, TN, preferred_element_type=F32)
        qb = q.astype(BF16)
        kb = kk.astype(BF16)
        a = masks_ref[nlev] * lax.dot_general(qb, kb, NT, preferred_element_type=F32)
        for l in range(nlev):
            e = jnp.exp2(dall[(l + 1) * rows:(l + 2) * rows, sl]).astype(BF16)
            a = a + masks_ref[l] * lax.dot_general(qb * e, kb * e, NT, preferred_element_type=F32)
        o = o + _dot(a.astype(BF16), vb)
        og_ref[:, sl] = (_rms(o) * nw_ref[...] * _silu(hg_ref[:, sl])).astype(BF16)

    yield
    @pl.when(c == nc - 1)
    def _():
        for h in range(HG_HB):
            sfin_ref[0, h] = st_ref[h].T


def hg_prompt(po, lb, nw, nseq, seq, chunk):
    m = nseq * seq
    nc = seq // chunk
    row = lambda b, c: b * nc + c
    sums_np, masks_np = _hg_tables(chunk)
    sums = jnp.asarray(sums_np, BF16)
    masks = jnp.asarray(masks_np, F32)
    return dict(
        kernel=_hg_prompt_kernel,
        arrays=[po, po, po, po, lb, nw, sums, masks],
        in_specs=[pl.BlockSpec((chunk, D_HG), lambda b, c: (row(b, c), 0)),
                  pl.BlockSpec((chunk, D_HG), lambda b, c: (row(b, c), 1)),
                  pl.BlockSpec((chunk, D_HG), lambda b, c: (row(b, c), 2)),
                  pl.BlockSpec((chunk, D_HG), lambda b, c: (row(b, c), 3)),
                  pl.BlockSpec((1, D_HG), lambda b, c: (0, 0)),
                  pl.BlockSpec((1, HG_DV), lambda b, c: (0, 0)),
                  pl.BlockSpec(sums.shape, lambda b, c: (0, 0)),
                  pl.BlockSpec(masks.shape, lambda b, c: (0, 0, 0))],
        out_specs=[pl.BlockSpec((chunk, D_HG), lambda b, c: (row(b, c), 0)),
                   pl.BlockSpec((1, HG_HEADS, HG_DK, HG_DV), lambda b, c: (b, 0, 0, 0))],
        out_shape=[jax.ShapeDtypeStruct((m, D_HG), BF16),
                   jax.ShapeDtypeStruct((nseq, HG_HEADS, HG_DK, HG_DV), F32)],
        scratch=[pltpu.VMEM((HG_HEADS, HG_DV, HG_DK), F32)])


def _lin_scan(a, b, h0):
    rows, cols = a.shape
    groups = rows // SUBLANES
    a3 = a.reshape(groups, SUBLANES, cols)
    b3 = b.reshape(groups, SUBLANES, cols)
    si = lax.broadcasted_iota(jnp.int32, a3.shape, 1)
    d = 1
    while d < SUBLANES:
        keep = si >= d
        a_s = jnp.where(keep, pltpu.roll(a3, d, 1), 1.0)
        b_s = jnp.where(keep, pltpu.roll(b3, d, 1), 0.0)
        b3 = a3 * b_s + b3
        a3 = a3 * a_s
        d *= 2
    h_in = jnp.broadcast_to(h0, (SUBLANES, cols))
    out = []
    for g in range(groups):
        hg = a3[g] * h_in + b3[g]
        out.append(hg)
        h_in = jnp.broadcast_to(hg[SUBLANES - 1:SUBLANES, :], (SUBLANES, cols))
    return jnp.concatenate(out, axis=0), h_in[0:1, :]


def _lru_gates(xr, wr_ref, br, wi_ref, bi, ap):
    xb = xr.astype(BF16)
    r_parts, i_parts = [], []
    for n in range(LRU_BLOCKS):
        xn = xb[:, n * LRU_BW:(n + 1) * LRU_BW]
        r_parts.append(_dot(xn, wr_ref[n].astype(BF16)))
        i_parts.append(_dot(xn, wi_ref[n].astype(BF16)))
    r = jax.nn.sigmoid(jnp.concatenate(r_parts, axis=1) + br)
    gi = jax.nn.sigmoid(jnp.concatenate(i_parts, axis=1) + bi)
    la = -LRU_C * r * _softplus(-ap)
    a = jnp.exp(la)
    th = jnp.tanh(la)
    mult = jnp.sqrt(-2.0 * th / (1.0 - th))
    return a, mult, gi


def _lru_prompt_kernel(rx_ref, rg_ref, cw_ref, cb_ref, wr_ref, br_ref, wi_ref, bi_ref, ap_ref,
                       yl_ref, hfin_ref, cfin_ref, buf_ref, hc_ref):
    c = pl.program_id(1)
    nc = pl.num_programs(1)
    rows = rx_ref.shape[0]

    @pl.when(c == 0)
    def _():
        buf_ref[0:SUBLANES, :] = jnp.zeros((SUBLANES, D_RNN), F32)
        hc_ref[...] = jnp.zeros_like(hc_ref)

    yield
    xr, tail = _causal_conv(buf_ref, rx_ref[...], cw_ref, cb_ref, 1, LRU_CONV)
    a, mult, gi = _lru_gates(xr, wr_ref, br_ref[...], wi_ref, bi_ref[...], ap_ref[...])
    ri = lax.broadcasted_iota(jnp.int32, (rows, D_RNN), 0)
    mult = jnp.where(jnp.logical_and(c == 0, ri == 0), 1.0, mult)
    hs, hlast = _lin_scan(a, mult * gi * xr, hc_ref[0:1, :])
    hc_ref[0:1, :] = hlast
    yl_ref[...] = (hs * _gelu_tanh(rg_ref[...])).astype(BF16)

    yield
    @pl.when(c == nc - 1)
    def _():
        hfin_ref[0] = hlast
        cfin_ref[0] = tail


def lru_prompt(po, prm, nseq, seq, chunk):
    m = nseq * seq
    nc = seq // chunk
    row = lambda b, c: b * nc + c
    full = lambda shp: pl.BlockSpec(shp, lambda b, c: (0,) * len(shp))
    return dict(
        kernel=_lru_prompt_kernel,
        arrays=[po, po, prm["lru_cw"], prm["lru_cb"], prm["lru_wr"], prm["lru_br"], prm["lru_wi"],
                prm["lru_bi"], prm["lru_ap"]],
        in_specs=[pl.BlockSpec((chunk, D_RNN), lambda b, c: (row(b, c), 4)),
                  pl.BlockSpec((chunk, D_RNN), lambda b, c: (row(b, c), 5)),
                  full((LRU_CONV, D_RNN)), full((1, D_RNN)),
                  full((LRU_BLOCKS, LRU_BW, LRU_BW)), full((1, D_RNN)),
                  full((LRU_BLOCKS, LRU_BW, LRU_BW)), full((1, D_RNN)), full((1, D_RNN))],
        out_specs=[pl.BlockSpec((chunk, D_RNN), lambda b, c: (row(b, c), 0)),
                   pl.BlockSpec((1, 1, D_RNN), lambda b, c: (b, 0, 0)),
                   pl.BlockSpec((1, LRU_CONV - 1, D_RNN), lambda b, c: (b, 0, 0))],
        out_shape=[jax.ShapeDtypeStruct((m, D_RNN), BF16),
                   jax.ShapeDtypeStruct((nseq, 1, D_RNN), F32),
                   jax.ShapeDtypeStruct((nseq, LRU_CONV - 1, D_RNN), F32)],
        scratch=[pltpu.VMEM((SUBLANES + chunk, D_RNN), F32),
                 pltpu.VMEM((SUBLANES, D_RNN), F32)])


TOKP = SUBLANES


def _ssd_state_kernel(q_ref, k_ref, vw_ref, dec_ref, st_ref, u_ref, snew_ref):
    hpg = SSD_HEADS // SSD_GROUPS
    gw = hpg * SSD_HEAD_DIM

    def body(b, carry):
        q = q_ref[b].astype(BF16)
        k = k_ref[b].astype(BF16)
        vw = vw_ref[b].astype(BF16)
        for g in range(SSD_GROUPS):
            ks = slice(g * SSD_STATE, (g + 1) * SSD_STATE)
            stg = st_ref[b, g * hpg:(g + 1) * hpg].reshape(gw, SSD_STATE)
            u_ref[b, :, g * gw:(g + 1) * gw] = lax.dot_general(
                q[:, ks], stg.astype(BF16), NT, preferred_element_type=F32)
            upd = lax.dot_general(vw[:, g * gw:(g + 1) * gw], k[:, ks], TN, preferred_element_type=F32)
            for hl in range(hpg):
                h = g * hpg + hl
                snew_ref[b, h] = (st_ref[b, h] * dec_ref[b, h:h + 1, :]
                                  + upd[hl * SSD_HEAD_DIM:(hl + 1) * SSD_HEAD_DIM, :])
        return carry

    lax.fori_loop(0, st_ref.shape[0], body, 0)


def _ret_state_kernel(q_ref, k_ref, v_ref, s_ref, u_ref, snew_ref, *, ntok):
    def body(b, carry):
        q = q_ref[b].astype(BF16)
        k = k_ref[b].astype(BF16)
        v = v_ref[b].astype(BF16)
        for h in range(RET_HEADS):
            ks = slice(h * RET_DK, (h + 1) * RET_DK)
            vs = slice(h * RET_DV, (h + 1) * RET_DV)
            s0 = s_ref[b, h]
            u_ref[b, :, vs] = _dot(q[:, ks], s0.astype(BF16))
            upd = lax.dot_general(k[:, ks], v[:, vs], TN, preferred_element_type=F32)
            snew_ref[b, h] = s0 * math.exp(RET_LOG_GAMMA[h] * ntok) + upd
        return carry

    lax.fori_loop(0, s_ref.shape[0], body, 0)


def _hg_state_kernel(q_ref, k_ref, v_ref, dcol_ref, s_ref, u_ref, snew_ref):
    def body(b, carry):
        q = q_ref[b].astype(BF16)
        k = k_ref[b].astype(BF16)
        v = v_ref[b].astype(BF16)
        dc = dcol_ref[b]
        for h in range(HG_HEADS):
            ks = slice(h * HG_DK, (h + 1) * HG_DK)
            s0 = s_ref[b, h]
            u_ref[b, :, ks] = _dot(q[:, ks], s0.astype(BF16))
            upd = lax.dot_general(k[:, ks], v[:, ks], TN, preferred_element_type=F32)
            snew_ref[b, h] = s0 * dc[:, h:h + 1] + upd
        return carry

    lax.fori_loop(0, s_ref.shape[0], body, 0)


def _state_part(kern, rows_in, extra, s, u_cols, nseq, nc):
    nb = s.shape[0]
    bb = nb // (nseq * nc)
    assert bb * nseq * nc == nb
    blk = lambda a: pl.BlockSpec((bb,) + a.shape[1:],
                                 lambda b, c: (b * nc + c,) + (0,) * (a.ndim - 1))
    ins = list(rows_in) + ([extra] if extra is not None else []) + [s]

    def phases(*refs):
        yield
        kern(*refs)
        yield

    return dict(kernel=phases, arrays=ins, in_specs=[blk(a) for a in ins],
                out_specs=[pl.BlockSpec((bb, TOKP, u_cols), lambda b, c: (b * nc + c, 0, 0)), blk(s)],
                out_shape=[jax.ShapeDtypeStruct((nb, TOKP, u_cols), F32),
                           jax.ShapeDtypeStruct(s.shape, F32)],
                scratch=[])


def _tok(x, t, nb):
    return x[t * nb:(t + 1) * nb]


def _head_sums(x, width):
    r, n = x.shape
    tiles = []
    for h in range(n // width):
        s = jnp.sum(x[:, h * width:(h + 1) * width], axis=1, keepdims=True)
        tiles.append(jnp.broadcast_to(s, (r, width)))
    return jnp.concatenate(tiles, axis=1)


def _ssd_sample_pre_kernel(xs_ref, bc_ref, dt_ref, prevx_ref, prevb_ref, cwx_ref, cwb_ref,
                           cbx_ref, cbb_ref, dtb_ref, alog_ref,
                           xs_out, bc_out, vw_out, oi_out, ecum_out, elast_out, tailx_out, tailb_out,
                           bufx_ref, bufb_ref, *, nb, ntok):
    n = (SSD_CONV - 1) * nb
    offx = bufx_ref.shape[0] - nb * ntok
    bufx_ref[offx - n:offx, :] = prevx_ref[...]
    bufb_ref[offx - n:offx, :] = prevb_ref[...]
    xs_c, tailx = _causal_conv(bufx_ref, xs_ref[...], cwx_ref, cbx_ref, nb, SSD_CONV)
    bc_c, tailb = _causal_conv(bufb_ref, bc_ref[...], cwb_ref, cbb_ref, nb, SSD_CONV)
    xs, bc, dt, la = _ssd_gates(xs_c, bc_c, dt_ref[...], dtb_ref[...], alog_ref[...])
    tailx_out[...] = tailx
    tailb_out[...] = tailb
    xs_out[...] = xs
    bc_out[...] = bc
    cums = []
    for t in range(ntok):
        lt = _tok(la, t, nb)
        cums.append(lt if t == 0 else cums[-1] + lt)
    cumx = [_expand_heads64(cm, SSD_HEADS) for cm in cums]
    v = xs * _expand_heads64(dt, SSD_HEADS)
    kw = SSD_GROUPS * SSD_STATE
    for t in range(ntok):
        ct = _tok(bc, t, nb)[:, kw:2 * kw]
        acc = None
        for t2 in range(t + 1):
            bt = _tok(bc, t2, nb)[:, 0:kw]
            sc = _head_sums(ct * bt, SSD_STATE)
            scx = jnp.concatenate(
                [jnp.concatenate([sc[:, g * SSD_STATE:(g + 1) * SSD_STATE]] * 4, axis=1)
                 for g in range(SSD_GROUPS)], axis=1)
            term = scx * jnp.exp(cumx[t] - cumx[t2]) * _tok(v, t2, nb)
            acc = term if acc is None else acc + term
        oi_out[t * nb:(t + 1) * nb, :] = acc
        ecum_out[t * nb:(t + 1) * nb, :] = jnp.exp(cumx[t])
        vw_out[t * nb:(t + 1) * nb, :] = _tok(v, t, nb) * jnp.exp(cumx[ntok - 1] - cumx[t])
    elast_out[...] = jnp.exp(cums[ntok - 1])


def _ssd_sample_post_kernel(oi_ref, u_ref, ecum_ref, xs_ref, z_ref, dx_ref, nw_ref, y_ref):
    o = oi_ref[...] + ecum_ref[...] * u_ref[...]
    y_ref[...] = _ssd_finish(o, xs_ref[...], z_ref[...], dx_ref[...], nw_ref[...]).astype(BF16)


def _ret_sample_pre_kernel(q_ref, k_ref, v_ref, cos_ref, sin_ref, qd_out, kd_out, oi_out, *, nb, ntok):
    rows = nb * ntok
    cos = jnp.concatenate([jnp.broadcast_to(cos_ref[t:t + 1, :], (nb, RET_DK)) for t in range(ntok)], axis=0)
    sin = jnp.concatenate([jnp.broadcast_to(sin_ref[t:t + 1, :], (nb, RET_DK)) for t in range(ntok)], axis=0)
    qr, kr = [], []
    for h in range(RET_HEADS):
        sl = slice(h * RET_DK, (h + 1) * RET_DK)
        qr.append(_rotary(q_ref[:, sl], cos, sin))
        kr.append(_rotary(k_ref[:, sl], cos, sin) * (RET_DK ** -0.5))
    qr = jnp.concatenate(qr, axis=1)
    kr = jnp.concatenate(kr, axis=1)
    v = v_ref[...]
    for t in range(ntok):
        qt = _tok(qr, t, nb)
        acc = None
        for t2 in range(t + 1):
            sc = _head_sums(qt * _tok(kr, t2, nb), RET_DK)
            vt = _tok(v, t2, nb)
            tiles = []
            for h in range(RET_HEADS):
                dec = math.exp(RET_LOG_GAMMA[h] * (t - t2))
                s = sc[:, h * RET_DK:(h + 1) * RET_DK] * dec
                tiles.append(jnp.concatenate([s, s], axis=1) * vt[:, h * RET_DV:(h + 1) * RET_DV])
            term = jnp.concatenate(tiles, axis=1)
            acc = term if acc is None else acc + term
        oi_out[t * nb:(t + 1) * nb, :] = acc
        qd = jnp.concatenate([qt[:, h * RET_DK:(h + 1) * RET_DK] * math.exp(RET_LOG_GAMMA[h] * (t + 1))
                              for h in range(RET_HEADS)], axis=1)
        kt = _tok(kr, t, nb)
        kd = jnp.concatenate([kt[:, h * RET_DK:(h + 1) * RET_DK] * math.exp(RET_LOG_GAMMA[h] * (ntok - 1 - t))
                              for h in range(RET_HEADS)], axis=1)
        qd_out[t * nb:(t + 1) * nb, :] = qd
        kd_out[t * nb:(t + 1) * nb, :] = kd


def _ret_sample_post_kernel(oi_ref, u_ref, g_ref, o_ref):
    o = oi_ref[...] + u_ref[...]
    g = g_ref[...]
    for h in range(RET_HEADS):
        sl = slice(h * RET_DV, (h + 1) * RET_DV)
        o_ref[:, sl] = (_rms(o[:, sl]) * _silu(g[:, sl])).astype(BF16)


def _hg_sample_pre_kernel(hq_ref, hf_ref, hi_ref, lb_ref, qd_out, kd_out, dl_out, oi_out, *, nb, ntok):
    q, _, kk, lg = _hg_gates(hq_ref[...], hf_ref[...], lb_ref[...])
    v = hi_ref[...]
    cums = []
    for t in range(ntok):
        lt = _tok(lg, t, nb)
        cums.append(lt if t == 0 else cums[-1] + lt)
    for t in range(ntok):
        qt = _tok(q, t, nb)
        acc = None
        for t2 in range(t + 1):
            w = qt * _tok(kk, t2, nb)
            if t2 < t:
                w = w * jnp.exp(cums[t] - cums[t2])
            term = _head_sums(w, HG_DK) * _tok(v, t2, nb)
            acc = term if acc is None else acc + term
        oi_out[t * nb:(t + 1) * nb, :] = acc
        qd_out[t * nb:(t + 1) * nb, :] = qt * jnp.exp(cums[t])
        kd_out[t * nb:(t + 1) * nb, :] = _tok(kk, t, nb) * jnp.exp(cums[ntok - 1] - cums[t])
    dl_out[...] = jnp.exp(cums[ntok - 1])


def _hg_sample_post_kernel(oi_ref, u_ref, hg_ref, nw_ref, og_ref):
    o = oi_ref[...] + u_ref[...]
    hg = hg_ref[...]
    for h in range(HG_HEADS):
        sl = slice(h * HG_DV, (h + 1) * HG_DV)
        og_ref[:, sl] = (_rms(o[:, sl]) * nw_ref[...] * _silu(hg[:, sl])).astype(BF16)


def _lru_sample_kernel(rx_ref, rg_ref, prev_ref, h0_ref, cw_ref, cb_ref, wr_ref, br_ref, wi_ref,
                       bi_ref, ap_ref, yl_out, hfin_out, tail_out, buf_ref, *, nb, ntok):
    rows = nb * ntok
    n = (LRU_CONV - 1) * nb
    off = buf_ref.shape[0] - rows
    buf_ref[off - n:off, :] = prev_ref[...]
    xr, tail = _causal_conv(buf_ref, rx_ref[...], cw_ref, cb_ref, nb, LRU_CONV)
    tail_out[...] = tail
    a, mult, gi = _lru_gates(xr, wr_ref, br_ref[...], wi_ref, bi_ref[...], ap_ref[...])
    b = mult * gi * xr
    h = h0_ref[...]
    for t in range(ntok):
        h = _tok(a, t, nb) * h + _tok(b, t, nb)
        yl_out[t * nb:(t + 1) * nb, :] = (h * _gelu_tanh(rg_ref[t * nb:(t + 1) * nb, :])).astype(BF16)
    hfin_out[...] = h


def _whole(shape):
    return pl.BlockSpec(shape, lambda i: (0,) * len(shape))


def _colblock(rows, width, idx):
    return pl.BlockSpec((rows, width), functools.partial(lambda i, k: (0, k), k=idx))


def _call_whole(kern, in_arrays, in_specs, out_shapes, scratch, name, vmem):
    return pl.pallas_call(
        kern, grid=(1,), in_specs=in_specs,
        out_specs=[_whole(s.shape) for s in out_shapes],
        out_shape=out_shapes, scratch_shapes=scratch,
        compiler_params=_cparams(("arbitrary",), vmem), name=name,
    )(*in_arrays)


def _to_token_major(s):
    nb, w, c = s.shape
    return jnp.transpose(s, (1, 0, 2)).reshape(w * nb, c)


def _from_token_major(x, nb):
    w = x.shape[0] // nb
    return jnp.transpose(x.reshape(w, nb, x.shape[1]), (1, 0, 2))


def _rows_to_batch(x, nb):
    ntok = x.shape[0] // nb
    x = jnp.transpose(x.reshape(ntok, nb, x.shape[1]), (1, 0, 2))
    return jnp.pad(x, ((0, 0), (0, TOKP - ntok), (0, 0)))


def _batch_to_rows(u, ntok):
    nb = u.shape[0]
    return jnp.transpose(u[:, :ntok], (1, 0, 2)).reshape(ntok * nb, u.shape[2])


def ssd_sample_pre(pe, s_ssm, s_conv, prm, nb, ntok, nseq, nc):
    rows = nb * ntok
    n = (SSD_CONV - 1) * nb
    prev = _to_token_major(s_conv)
    off = _conv_off(nb, SSD_CONV)
    f = lambda *shape: jax.ShapeDtypeStruct(shape, F32)
    outs = _call_whole(
        functools.partial(_ssd_sample_pre_kernel, nb=nb, ntok=ntok),
        [pe, pe, pe, prev, prev, prm["ssd_cw"], prm["ssd_cw"], prm["ssd_cb"], prm["ssd_cb"],
         prm["ssd_dtb"], prm["ssd_alog"]],
        [_colblock(rows, D_SSD, 1), _colblock(rows, SSD_BC, 10), _colblock(rows, LANES, DT_COL // LANES),
         _colblock(n, D_SSD, 0), _colblock(n, SSD_BC, 2),
         _colblock(SSD_CONV, D_SSD, 0), _colblock(SSD_CONV, SSD_BC, 2),
         _colblock(1, D_SSD, 0), _colblock(1, SSD_BC, 2), _whole((1, LANES)), _whole((1, LANES))],
        [f(rows, D_SSD), f(rows, SSD_BC), f(rows, D_SSD), f(rows, D_SSD), f(rows, D_SSD),
         f(nb, LANES), f(n, D_SSD), f(n, SSD_BC)],
        [pltpu.VMEM((off + rows, D_SSD), F32), pltpu.VMEM((off + rows, SSD_BC), F32)],
        "ssd_sample_pre", _mib(56))
    xs, bc, vw, oi, ecum, elast, tailx, tailb = outs
    kw = SSD_GROUPS * SSD_STATE
    dec = jnp.broadcast_to(elast[:, :SSD_HEADS, None], (nb, SSD_HEADS, SSD_STATE))
    st = jnp.swapaxes(s_ssm, -1, -2)
    part = _state_part(_ssd_state_kernel,
                       [_rows_to_batch(bc[:, kw:], nb), _rows_to_batch(bc[:, :kw], nb),
                        _rows_to_batch(vw, nb)], dec, st, D_SSD, nseq, nc)
    return dict(oi=oi, ecum=ecum, xs=xs, tailx=tailx, tailb=tailb), part


def ssd_sample_post(ctx, u, st_new, pe, prm, nb, ntok):
    rows = nb * ntok
    s_new = jnp.swapaxes(st_new, -1, -2)
    u = _batch_to_rows(u, ntok)
    (y,) = _call_whole(
        _ssd_sample_post_kernel, [ctx["oi"], u, ctx["ecum"], ctx["xs"], pe, prm["ssd_dx"], prm["ssd_nw"]],
        [_whole((rows, D_SSD))] * 4 + [_colblock(rows, D_SSD, 0), _whole((1, D_SSD)), _whole((1, D_SSD))],
        [jax.ShapeDtypeStruct((rows, D_SSD), BF16)], [], "ssd_sample_post", _mib(48))
    conv_new = _from_token_major(jnp.concatenate([ctx["tailx"], ctx["tailb"]], axis=1), nb)
    return y, s_new, conv_new


def ret_sample_pre(pe, s_ret, cos, sin, nb, ntok, nseq, nc):
    rows = nb * ntok
    f = lambda *shape: jax.ShapeDtypeStruct(shape, F32)
    qd, kd, oi = _call_whole(
        functools.partial(_ret_sample_pre_kernel, nb=nb, ntok=ntok),
        [pe, pe, pe, cos, sin],
        [_colblock(rows, RET_QK, 8), _colblock(rows, RET_QK, 9), _colblock(rows, D_RET, 2),
         _whole(cos.shape), _whole(sin.shape)],
        [f(rows, RET_QK), f(rows, RET_QK), f(rows, D_RET)], [], "ret_sample_pre", _mib(48))
    v = _rows_to_batch(pe[:, 2 * D_RET:3 * D_RET], nb)
    part = _state_part(functools.partial(_ret_state_kernel, ntok=ntok),
                       [_rows_to_batch(qd, nb), _rows_to_batch(kd, nb), v], None, s_ret,
                       D_RET, nseq, nc)
    return oi, part


def ret_sample_post(oi, u, pe, nb, ntok):
    rows = nb * ntok
    u = _batch_to_rows(u, ntok)
    (o,) = _call_whole(
        _ret_sample_post_kernel, [oi, u, pe],
        [_whole((rows, D_RET)), _whole((rows, D_RET)), _colblock(rows, D_RET, 3)],
        [jax.ShapeDtypeStruct((rows, D_RET), BF16)], [], "ret_sample_post", _mib(40))
    return o


def hg_sample_pre(po, s_hg, lb, nb, ntok, nseq, nc):
    rows = nb * ntok
    f = lambda *shape: jax.ShapeDtypeStruct(shape, F32)
    qd, kd, dl, oi = _call_whole(
        functools.partial(_hg_sample_pre_kernel, nb=nb, ntok=ntok),
        [po, po, po, lb],
        [_colblock(rows, D_HG, 0), _colblock(rows, D_HG, 1), _colblock(rows, D_HG, 2), _whole((1, D_HG))],
        [f(rows, D_HG), f(rows, D_HG), f(nb, D_HG), f(rows, D_HG)], [], "hg_sample_pre", _mib(48))
    dcol = jnp.pad(jnp.transpose(dl.reshape(nb, HG_HEADS, HG_DK), (0, 2, 1)),
                   ((0, 0), (0, 0), (0, LANES - HG_HEADS)))
    v = _rows_to_batch(po[:, 2 * D_HG:3 * D_HG], nb)
    part = _state_part(_hg_state_kernel, [_rows_to_batch(qd, nb), _rows_to_batch(kd, nb), v],
                       dcol, s_hg, D_HG, nseq, nc)
    return oi, part


def hg_sample_post(oi, u, po, nw, nb, ntok):
    rows = nb * ntok
    u = _batch_to_rows(u, ntok)
    (og,) = _call_whole(
        _hg_sample_post_kernel, [oi, u, po, nw],
        [_whole((rows, D_HG)), _whole((rows, D_HG)), _colblock(rows, D_HG, 3), _whole((1, HG_DV))],
        [jax.ShapeDtypeStruct((rows, D_HG), BF16)], [], "hg_sample_post", _mib(40))
    return og


def lru_sample(po, s_lru, s_lconv, prm, nb, ntok):
    rows = nb * ntok
    n = (LRU_CONV - 1) * nb
    off = _conv_off(nb, LRU_CONV)
    prev = _to_token_major(s_lconv)
    yl, hfin, tail = _call_whole(
        functools.partial(_lru_sample_kernel, nb=nb, ntok=ntok),
        [po, po, prev, s_lru, prm["lru_cw"], prm["lru_cb"], prm["lru_wr"], prm["lru_br"], prm["lru_wi"],
         prm["lru_bi"], prm["lru_ap"]],
        [_colblock(rows, D_RNN, 4), _colblock(rows, D_RNN, 5), _whole((n, D_RNN)), _whole((nb, D_RNN)),
         _whole((LRU_CONV, D_RNN)), _whole((1, D_RNN)), _whole((LRU_BLOCKS, LRU_BW, LRU_BW)),
         _whole((1, D_RNN)), _whole((LRU_BLOCKS, LRU_BW, LRU_BW)), _whole((1, D_RNN)), _whole((1, D_RNN))],
        [jax.ShapeDtypeStruct((rows, D_RNN), BF16), jax.ShapeDtypeStruct((nb, D_RNN), F32),
         jax.ShapeDtypeStruct((n, D_RNN), F32)],
        [pltpu.VMEM((off + rows, D_RNN), F32)], "lru_sample", _mib(48))
    return yl, hfin, _from_token_major(tail, nb)


def _rope_tables(pos):
    half = RET_DK // 2
    inv = ROPE_BASE ** (-jnp.arange(half, dtype=F32) / half)
    ang = pos.astype(F32)[:, None] * inv[None, :]
    cos, sin = jnp.cos(ang), jnp.sin(ang)
    return jnp.concatenate([cos, cos], axis=1), jnp.concatenate([-sin, sin], axis=1)


def _pad_lanes(v):
    return jnp.pad(v.astype(F32), (0, LANES - v.shape[0])).reshape(1, LANES)


def _prepare(p):
    lbs = jnp.cumsum(jax.nn.softmax(p["hg_lower_bounds"].astype(F32), axis=0), axis=0)
    lbs = lbs - lbs[0]
    return {
        "even_wt": jnp.swapaxes(p["even_w_in"][0], 0, 1),
        "ssd_cw": p["ssd_conv_w"][0], "ssd_cb": p["ssd_conv_b"][0].reshape(1, -1),
        "ssd_dtb": _pad_lanes(p["ssd_dt_bias"][0]), "ssd_alog": _pad_lanes(p["ssd_A_log"][0]),
        "ssd_dx": jnp.repeat(p["ssd_D"][0], SSD_HEAD_DIM).reshape(1, D_SSD),
        "ssd_nw": p["ssd_norm_w"][0].reshape(1, D_SSD),
        "hg_lb": lbs[1].reshape(1, D_HG), "hg_nw": p["hg_norm_w"][0].reshape(1, HG_DV),
        "lru_cw": p["lru_conv_w"][0], "lru_cb": p["lru_conv_b"][0].reshape(1, D_RNN),
        "lru_wr": p["lru_w_r"][0], "lru_br": p["lru_b_r"][0].reshape(1, D_RNN),
        "lru_wi": p["lru_w_i"][0], "lru_bi": p["lru_b_i"][0].reshape(1, D_RNN),
        "lru_ap": p["lru_a_param"][0].reshape(1, D_RNN),
    }


def _even_row_offsets(bn):
    o_xbc, o_dt = D_SSD, 2 * D_SSD + SSD_BC
    o_q = o_dt + SSD_HEADS
    o_k, o_v = o_q + RET_QK, o_q + 2 * RET_QK
    o_g = o_v + D_RET
    segs = [(0, D_SSD), (o_xbc, D_SSD), (o_v, D_RET), (o_g, D_RET), (o_q, RET_QK), (o_k, RET_QK),
            (o_xbc + D_SSD, SSD_BC), (o_dt, bn)]
    offs = [start + i for start, width in segs for i in range(0, width, bn)]
    assert len(offs) * bn == EVEN_PACKED and offs[-1] + bn <= D_IN_EVEN
    return offs


def _even_proj(x, p, prm, bm):
    return mm_in_t(x, p["norm_mix"][0], prm["even_wt"], _even_row_offsets(512), bm, 512)


def _ffn(x, p, prm, l, prev, shift, group_rows, bm, bf=512):
    act, st = ffn_up(x, p["norm_ffn"][l], p["ffn_w_up"], l, p["ffn_conv_w"][l], p["ffn_conv_b"][l],
                     prev, shift, group_rows, bm, bf)
    return mm_out([act], p["ffn_w_down"], l, x, min(bm, 1024), 256), st


def _trunks(xp, xs, st, p, prm, nseq, seq, nb, ntok):
    m = nseq * seq
    rows = nb * ntok
    bm = min(2048, seq)
    nc = seq // MIX_CHUNK
    grid = (nseq, nc)
    cos_p, sin_p = _rope_tables(jnp.arange(seq, dtype=jnp.int32))
    cos_s, sin_s = _rope_tables(PAST_LEN + jnp.arange(ntok, dtype=jnp.int32))
    zeros_ffn = jnp.zeros((nseq, FFN_CONV - 1, D_FF), F32)

    pe_s = _even_proj(xs, p, prm, rows)
    ssd_ctx, ssd_part = ssd_sample_pre(pe_s, st["ssm"][0], st["ssm_conv"][0], prm, nb, ntok, nseq, nc)
    ret_oi, ret_part = ret_sample_pre(pe_s, st["ret"][0], cos_s, sin_s, nb, ntok, nseq, nc)
    pe_p = _even_proj(xp, p, prm, bm)
    (y, ssm_p, ssm_conv_p), (o, ret_p), (u_ssd, st_ssd), (u_ret, ret_s) = _run_parts(
        [ssd_prompt(pe_p, prm, nseq, seq, MIX_CHUNK), ret_prompt(pe_p, cos_p, sin_p, nseq, seq, MIX_CHUNK),
         ssd_part, ret_part], grid, "even_mix", _mib(58))
    xp = mm_out([y, o], p["even_w_out"], 0, xp, min(bm, 1024), 512)
    xp, ffn0_p = _ffn(xp, p, prm, 0, zeros_ffn, 1, seq, bm, 256)
    y_s, ssm_s, ssm_conv_s = ssd_sample_post(ssd_ctx, u_ssd, st_ssd, pe_s, prm, nb, ntok)
    o_s = ret_sample_post(ret_oi, u_ret, pe_s, nb, ntok)
    xs = mm_out([y_s, o_s], p["even_w_out"], 0, xs, rows, 512)
    xs, ffn0_s = _ffn(xs, p, prm, 0, _to_token_major(st["ffn_conv"][0])[None], nb, rows, rows)

    po_s = mm_in(xs, p["norm_mix"][1], p["odd_w_in"][0], rows, 512)
    hg_oi, hg_part = hg_sample_pre(po_s, st["hgrn"][0], prm["hg_lb"], nb, ntok, nseq, nc)
    yl_s, lru_s, lru_conv_s = lru_sample(po_s, st["lru"][0], st["lru_conv"][0], prm, nb, ntok)
    po_p = mm_in(xp, p["norm_mix"][1], p["odd_w_in"][0], bm, 512)
    (og, hgrn_p), (yl, lru_p, lru_conv_p), (u_hg, hgrn_s) = _run_parts(
        [hg_prompt(po_p, prm["hg_lb"], prm["hg_nw"], nseq, seq, MIX_CHUNK),
         lru_prompt(po_p, prm, nseq, seq, MIX_CHUNK), hg_part], grid, "odd_mix", _mib(58))
    xp = mm_out([og, yl], p["odd_w_out"], 0, xp, min(bm, 1024), 512)
    xp, ffn1_p = _ffn(xp, p, prm, 1, zeros_ffn, 1, seq, bm, 256)
    og_s = hg_sample_post(hg_oi, u_hg, po_s, prm["hg_nw"], nb, ntok)
    xs = mm_out([og_s, yl_s], p["odd_w_out"], 0, xs, rows, 512)
    xs, ffn1_s = _ffn(xs, p, prm, 1, _to_token_major(st["ffn_conv"][1])[None], nb, rows, rows)

    y_p = rmsnorm(xp, p["norm_final"], min(512, m)).reshape(nseq, seq, D_MODEL)
    y_s = jnp.transpose(rmsnorm(xs, p["norm_final"], rows).reshape(ntok, nb, D_MODEL), (1, 0, 2))
    ffn_s = jnp.stack([_from_token_major(ffn0_s[0], nb), _from_token_major(ffn1_s[0], nb)])
    return (y_p, y_s, ssm_p[None], ssm_s[None], ssm_conv_p[None], ssm_conv_s[None],
            ret_p[None], ret_s[None], hgrn_p[None], hgrn_s[None],
            lru_p.reshape(1, nseq, D_RNN), lru_s[None], lru_conv_p[None], lru_conv_s[None],
            jnp.stack([ffn0_p, ffn1_p]), ffn_s)


def kernel(x_prompt, x_sample, state_ssm, state_ssm_conv, state_ret, state_hgrn, state_lru, state_lru_conv, state_ffn_conv, norm_mix, norm_ffn, norm_final, even_w_in, ssd_conv_w, ssd_conv_b, ssd_dt_bias, ssd_A_log, ssd_D, ssd_norm_w, even_w_out, odd_w_in, hg_lower_bounds, hg_norm_w, lru_conv_w, lru_conv_b, lru_w_r, lru_b_r, lru_w_i, lru_b_i, lru_a_param, odd_w_out, ffn_w_up, ffn_conv_w, ffn_conv_b, ffn_w_down):
    p = {
        "norm_mix": norm_mix, "norm_ffn": norm_ffn, "norm_final": norm_final,
        "even_w_in": even_w_in, "ssd_conv_w": ssd_conv_w, "ssd_conv_b": ssd_conv_b,
        "ssd_dt_bias": ssd_dt_bias, "ssd_A_log": ssd_A_log, "ssd_D": ssd_D,
        "ssd_norm_w": ssd_norm_w, "even_w_out": even_w_out, "odd_w_in": odd_w_in,
        "hg_lower_bounds": hg_lower_bounds, "hg_norm_w": hg_norm_w,
        "lru_conv_w": lru_conv_w, "lru_conv_b": lru_conv_b, "lru_w_r": lru_w_r,
        "lru_b_r": lru_b_r, "lru_w_i": lru_w_i, "lru_b_i": lru_b_i,
        "lru_a_param": lru_a_param, "odd_w_out": odd_w_out, "ffn_w_up": ffn_w_up,
        "ffn_conv_w": ffn_conv_w, "ffn_conv_b": ffn_conv_b, "ffn_w_down": ffn_w_down,
    }
    prm = _prepare(p)
    nseq, seq, _ = x_prompt.shape
    nb, ntok, _ = x_sample.shape
    st = {"ssm": state_ssm, "ssm_conv": state_ssm_conv, "ret": state_ret, "hgrn": state_hgrn,
          "lru": state_lru, "lru_conv": state_lru_conv, "ffn_conv": state_ffn_conv}
    xs_tm = jnp.transpose(x_sample, (1, 0, 2)).reshape(ntok * nb, D_MODEL)
    return _trunks(x_prompt.reshape(nseq * seq, D_MODEL), xs_tm, st, p, prm, nseq, seq, nb, ntok)
```

```python
import functools
import math

import numpy as np
import jax
import jax.numpy as jnp
from jax import lax
from jax.experimental import pallas as pl
from jax.experimental.pallas import tpu as pltpu

F32 = jnp.float32
BF16 = jnp.bfloat16
EPS = 1e-6
LOG2E = math.log2(math.e)

D_MODEL = 2048
PAST_LEN = 16384
SSD_HEADS = 32
SSD_HEAD_DIM = 64
D_SSD = SSD_HEADS * SSD_HEAD_DIM
SSD_GROUPS = 4
SSD_STATE = 128
SSD_CONV = 4
SSD_BC = 2 * SSD_GROUPS * SSD_STATE
RET_HEADS = 8
RET_DK = 128
RET_DV = 256
RET_QK = RET_HEADS * RET_DK
D_RET = RET_HEADS * RET_DV
ROPE_BASE = 10000.0
HG_HEADS = 16
HG_DK = 128
HG_DV = 128
D_HG = HG_HEADS * HG_DV
D_RNN = 2048
LRU_BLOCKS = 8
LRU_BW = D_RNN // LRU_BLOCKS
LRU_CONV = 4
LRU_C = 8.0
D_FF = 5632
FFN_CONV = 3
D_IN_EVEN = D_SSD + (D_SSD + SSD_BC) + SSD_HEADS + 2 * RET_QK + 2 * D_RET
EVEN_PACKED = 11776
DT_COL = 11264

V7X_VMEM_BYTES = 64 * 1024 * 1024
V7X_VMEM_CAP = 60 * 1024 * 1024
LANES = 128
SUBLANES = 8

RET_LOG_GAMMA = [float(v) for v in np.log1p(-np.exp(np.linspace(
    math.log(1.0 / 32.0), math.log(1.0 / 512.0), RET_HEADS, dtype=np.float32))).astype(np.float32)]

NT = (((1,), (1,)), ((), ()))
TN = (((0,), (0,)), ((), ()))


def _cparams(sem, vmem_bytes):
    return pltpu.CompilerParams(dimension_semantics=sem,
                                vmem_limit_bytes=int(min(V7X_VMEM_CAP, vmem_bytes)))


def _mib(n):
    return n * 1024 * 1024


def _silu(x):
    return x * jax.nn.sigmoid(x)


def _softplus(x):
    return jnp.maximum(x, 0.0) + jnp.log1p(jnp.exp(-jnp.abs(x)))


def _gelu_tanh(x):
    return 0.5 * x * (1.0 + jnp.tanh(math.sqrt(2.0 / math.pi) * (x + 0.044715 * (x * x * x))))


def _rms(x):
    return x * lax.rsqrt(jnp.mean(x * x, axis=-1, keepdims=True) + EPS)


def _dot(a, b):
    return jnp.dot(a, b, preferred_element_type=F32)


def _tril_ones(n):
    r = lax.broadcasted_iota(jnp.int32, (n, n), 0)
    c = lax.broadcasted_iota(jnp.int32, (n, n), 1)
    return (r >= c).astype(F32)


def _cumsum_rows(x):
    return jnp.dot(_tril_ones(x.shape[0]), x, precision=lax.Precision.HIGHEST,
                   preferred_element_type=F32)


def _shifted_rows(buf_ref, start, rows, back):
    if back % SUBLANES == 0:
        return buf_ref[start - back:start - back + rows, :]
    assert back < SUBLANES and start % SUBLANES == 0
    ext = buf_ref[start - SUBLANES:start + rows, :]
    return pltpu.roll(ext, back, 0)[SUBLANES:, :]


def _causal_conv(buf_ref, x, w_ref, b_ref, shift, width):
    rows = x.shape[0]
    off = buf_ref.shape[0] - rows
    n = (width - 1) * shift
    buf_ref[off:off + rows, :] = x
    acc = None
    for j in range(width):
        term = _shifted_rows(buf_ref, off, rows, (width - 1 - j) * shift) * w_ref[j:j + 1, :]
        acc = term if acc is None else acc + term
    out = b_ref[...] + acc
    tail = buf_ref[off + rows - n:off + rows, :]
    buf_ref[off - n:off, :] = tail
    return out, tail


def _conv_off(shift, width):
    n = (width - 1) * shift
    return -(-n // SUBLANES) * SUBLANES


NORM_ROWS = 256


def _norm_rows_to(x_ref, nw_ref, xn_ref):
    rows = x_ref.shape[0]
    step = min(NORM_ROWS, rows)

    def body(i, carry):
        r = pl.multiple_of(i * step, step)
        xn_ref[pl.ds(r, step), :] = (_rms(x_ref[pl.ds(r, step), :]) * nw_ref[...]).astype(BF16)
        return carry

    lax.fori_loop(0, rows // step, body, 0)


def _mm_in_kernel(x_ref, nw_ref, w_ref, o_ref, xn_ref):
    @pl.when(pl.program_id(1) == 0)
    def _():
        _norm_rows_to(x_ref, nw_ref, xn_ref)
    o_ref[...] = _dot(xn_ref[...], w_ref[...].astype(BF16))


def _mm_in_vmem(bm, d, bn):
    return (bm * d * 4 + bm * d * 2 + 2 * d * bn * 4 + d * bn * 2 + 2 * bm * bn * 4
            + 4 * NORM_ROWS * d * 4 + _mib(6))


def mm_in(x, nw, w, bm, bn):
    m, d = x.shape
    n = w.shape[1]
    vmem = _mm_in_vmem(bm, d, bn)
    return pl.pallas_call(
        _mm_in_kernel,
        grid=(m // bm, n // bn),
        in_specs=[pl.BlockSpec((bm, d), lambda i, j: (i, 0), pipeline_mode=pl.Buffered(1)),
                  pl.BlockSpec((1, d), lambda i, j: (0, 0)),
                  pl.BlockSpec((d, bn), lambda i, j: (0, j))],
        out_specs=pl.BlockSpec((bm, bn), lambda i, j: (i, j)),
        out_shape=jax.ShapeDtypeStruct((m, n), F32),
        scratch_shapes=[pltpu.VMEM((bm, d), BF16)],
        compiler_params=_cparams(("parallel", "arbitrary"), vmem),
        name="mm_in",
    )(x, nw.reshape(1, d), w)


def _mm_in_t_kernel(offs_ref, x_ref, nw_ref, wt_ref, o_ref, xn_ref):
    del offs_ref
    @pl.when(pl.program_id(1) == 0)
    def _():
        _norm_rows_to(x_ref, nw_ref, xn_ref)
    o_ref[...] = lax.dot_general(xn_ref[...], wt_ref[...].astype(BF16), NT, preferred_element_type=F32)


ROW_ALIGN = 32


def mm_in_t(x, nw, wt, row_offsets, bm, bn):
    m, d = x.shape
    nblk = len(row_offsets)
    assert all(o % ROW_ALIGN == 0 for o in row_offsets)
    vmem = _mm_in_vmem(bm, d, bn)
    grid_spec = pltpu.PrefetchScalarGridSpec(
        num_scalar_prefetch=1,
        grid=(m // bm, nblk),
        in_specs=[pl.BlockSpec((bm, d), lambda i, j, offs: (i, 0), pipeline_mode=pl.Buffered(1)),
                  pl.BlockSpec((1, d), lambda i, j, offs: (0, 0)),
                  pl.BlockSpec((pl.Element(bn), pl.Element(d)),
                               lambda i, j, offs: (offs[j] * ROW_ALIGN, 0))],
        out_specs=pl.BlockSpec((bm, bn), lambda i, j, offs: (i, j)),
        scratch_shapes=[pltpu.VMEM((bm, d), BF16)])
    return pl.pallas_call(
        _mm_in_t_kernel,
        grid_spec=grid_spec,
        out_shape=jax.ShapeDtypeStruct((m, nblk * bn), F32),
        compiler_params=_cparams(("parallel", "arbitrary"), vmem),
        name="mm_in_t",
    )(jnp.asarray([o // ROW_ALIGN for o in row_offsets], jnp.int32), x, nw.reshape(1, d), wt)


def _mm_out_kernel(*refs, nparts):
    a_refs = refs[:nparts]
    w_refs = refs[nparts:2 * nparts]
    r_ref = refs[2 * nparts]
    o_ref = refs[2 * nparts + 1]
    acc = r_ref[...]
    for a_ref, w_ref in zip(a_refs, w_refs):
        acc = acc + _dot(a_ref[...], w_ref[...].astype(BF16))
    o_ref[...] = acc


def mm_out(parts, w, layer, resid, bm, bn):
    nparts = len(parts)
    m, kp = parts[0].shape
    n = w.shape[2]
    vmem = nparts * (2 * bm * kp * 2 + 2 * kp * bn * 4 + kp * bn * 2) + 4 * bm * bn * 4 + 2 * bm * bn * 4 + _mib(6)
    in_specs = [pl.BlockSpec((bm, kp), lambda i, j: (i, 0)) for _ in range(nparts)]
    in_specs += [pl.BlockSpec((None, kp, bn), functools.partial(lambda i, j, p: (layer, p, j), p=p))
                 for p in range(nparts)]
    in_specs += [pl.BlockSpec((bm, bn), lambda i, j: (i, j))]
    return pl.pallas_call(
        functools.partial(_mm_out_kernel, nparts=nparts),
        grid=(m // bm, n // bn),
        in_specs=in_specs,
        out_specs=pl.BlockSpec((bm, bn), lambda i, j: (i, j)),
        out_shape=jax.ShapeDtypeStruct((m, n), F32),
        compiler_params=_cparams(("parallel", "parallel"), vmem),
        name="mm_out",
    )(*parts, *([w] * nparts), resid)


def _rmsnorm_kernel(x_ref, nw_ref, o_ref):
    o_ref[...] = _rms(x_ref[...]) * nw_ref[...]


def rmsnorm(x, nw, bm):
    m, d = x.shape
    return pl.pallas_call(
        _rmsnorm_kernel,
        grid=(m // bm,),
        in_specs=[pl.BlockSpec((bm, d), lambda i: (i, 0)), pl.BlockSpec((1, d), lambda i: (0, 0))],
        out_specs=pl.BlockSpec((bm, d), lambda i: (i, 0)),
        out_shape=jax.ShapeDtypeStruct((m, d), F32),
        compiler_params=_cparams(("parallel",), 6 * bm * d * 4 + _mib(4)),
        name="rmsnorm",
    )(x, nw.reshape(1, d))


FFN_SLAB = 512


def _ffn_up_kernel(x_ref, nw_ref, wg_ref, wu_ref, cw_ref, cb_ref, prev_ref,
                   act_ref, st_ref, xn_ref, gbuf_ref, carry_ref, *, shift, blocks_per_group):
    i = pl.program_id(0)
    j = pl.program_id(1)
    bm = x_ref.shape[0]
    n = (FFN_CONV - 1) * shift
    off = gbuf_ref.shape[0] - bm

    @pl.when(j == 0)
    def _():
        _norm_rows_to(x_ref, nw_ref, xn_ref)

    first = (i % blocks_per_group) == 0

    @pl.when(first)
    def _():
        gbuf_ref[off - n:off, :] = prev_ref[0]

    @pl.when(jnp.logical_not(first))
    def _():
        gbuf_ref[off - n:off, :] = carry_ref[j]

    wg = wg_ref[...].astype(BF16)
    wu = wu_ref[...].astype(BF16)
    slab = min(FFN_SLAB, bm)
    for s in range(bm // slab):
        r0 = s * slab
        xs = xn_ref[r0:r0 + slab, :]
        gbuf_ref[off + r0:off + r0 + slab, :] = _dot(xs, wg)
        u = _dot(xs, wu)
        gc = cb_ref[...]
        for t in range(FFN_CONV):
            back = (FFN_CONV - 1 - t) * shift
            gc = gc + _shifted_rows(gbuf_ref, off + r0, slab, back) * cw_ref[t:t + 1, :]
        act_ref[r0:r0 + slab, :] = (_silu(gc) * u).astype(BF16)
    tail = gbuf_ref[off + bm - n:off + bm, :]
    carry_ref[j] = tail
    st_ref[0] = tail


def ffn_up(x, nw, w_up, layer, cw, cb, prev, shift, group_rows, bm, bf):
    m, d = x.shape
    f = cw.shape[1]
    n = (FFN_CONV - 1) * shift
    off = _conv_off(shift, FFN_CONV)
    bpg = group_rows // bm
    nf = f // bf
    vmem = (bm * d * 4 + bm * d * 2 + 4 * d * bf * 4 + 2 * d * bf * 2 + 2 * bm * bf * 2
            + (off + bm) * bf * 4 + nf * max(n, SUBLANES) * bf * 4 + 4 * n * bf * 4 + 5 * bm * bf * 4
            + 4 * NORM_ROWS * d * 4 + _mib(6))
    act, st = pl.pallas_call(
        functools.partial(_ffn_up_kernel, shift=shift, blocks_per_group=bpg),
        grid=(m // bm, nf),
        in_specs=[pl.BlockSpec((bm, d), lambda i, j: (i, 0), pipeline_mode=pl.Buffered(1)),
                  pl.BlockSpec((1, d), lambda i, j: (0, 0)),
                  pl.BlockSpec((None, d, bf), lambda i, j: (layer, 0, j)),
                  pl.BlockSpec((None, d, bf), lambda i, j: (layer, 0, j + nf)),
                  pl.BlockSpec((FFN_CONV, bf), lambda i, j: (0, j)),
                  pl.BlockSpec((1, bf), lambda i, j: (0, j)),
                  pl.BlockSpec((1, n, bf), lambda i, j: (i // bpg, 0, j))],
        out_specs=[pl.BlockSpec((bm, bf), lambda i, j: (i, j)),
                   pl.BlockSpec((1, n, bf), lambda i, j: (i, 0, j))],
        out_shape=[jax.ShapeDtypeStruct((m, f), BF16),
                   jax.ShapeDtypeStruct((m // bm, n, f), F32)],
        scratch_shapes=[pltpu.VMEM((bm, d), BF16),
                        pltpu.VMEM((off + bm, bf), F32),
                        pltpu.VMEM((nf, n, bf), F32)],
        compiler_params=_cparams(("arbitrary", "arbitrary"), vmem),
        name="ffn_up",
    )(x, nw.reshape(1, d), w_up, w_up, cw, cb.reshape(1, f), prev)
    return act, st[bpg - 1::bpg]


def _expand_heads64(x, nheads):
    r = x.shape[0]
    lo = lax.broadcasted_iota(jnp.int32, (r, LANES), 1) < SSD_HEAD_DIM
    tiles = []
    for p in range(nheads // 2):
        a0 = jnp.broadcast_to(x[:, 2 * p:2 * p + 1], (r, LANES))
        a1 = jnp.broadcast_to(x[:, 2 * p + 1:2 * p + 2], (r, LANES))
        tiles.append(jnp.where(lo, a0, a1))
    return jnp.concatenate(tiles, axis=1)


def _ssd_gates(xs_c, bc_c, dt_raw, dtb, alog):
    xs = _silu(xs_c)
    bc = _silu(bc_c)
    dt = _softplus(dt_raw + dtb)
    la = dt * (-jnp.exp(alog))
    return xs, bc, dt, la


def _ssd_finish(o, xs, z, dx, nw):
    y = (o + dx * xs) * _silu(z)
    gw = D_SSD // SSD_GROUPS
    y = jnp.concatenate([_rms(y[:, g * gw:(g + 1) * gw]) for g in range(SSD_GROUPS)], axis=1)
    return y * nw


def _ssd_prompt_kernel(z_ref, xs_ref, bc_ref, dt_ref, cwx_ref, cwb_ref, cbx_ref, cbb_ref,
                       dtb_ref, alog_ref, dx_ref, nw_ref,
                       y_ref, sfin_ref, cfin_ref, s_ref, bufx_ref, bufb_ref):
    c = pl.program_id(1)
    nc = pl.num_programs(1)
    rows = xs_ref.shape[0]
    gw = D_SSD // SSD_GROUPS
    hpg = SSD_HEADS // SSD_GROUPS

    @pl.when(c == 0)
    def _():
        s_ref[...] = jnp.zeros_like(s_ref)
        bufx_ref[0:SUBLANES, :] = jnp.zeros((SUBLANES, D_SSD), F32)
        bufb_ref[0:SUBLANES, :] = jnp.zeros((SUBLANES, SSD_BC), F32)

    yield
    xs_c, tailx = _causal_conv(bufx_ref, xs_ref[...], cwx_ref, cbx_ref, 1, SSD_CONV)
    bc_c, tailb = _causal_conv(bufb_ref, bc_ref[...], cwb_ref, cbb_ref, 1, SSD_CONV)
    xs, bc, dt, la = _ssd_gates(xs_c, bc_c, dt_ref[...], dtb_ref[...], alog_ref[...])
    cum = _cumsum_rows(la * LOG2E)
    cum_t = cum.T
    cumx = _expand_heads64(cum, SSD_HEADS)
    dtx = _expand_heads64(dt, SSD_HEADS)
    lastx = cumx[rows - 1:rows, :]
    ecum = jnp.exp2(cumx)
    wx = jnp.exp2(lastx - cumx)
    elast = jnp.exp2(lastx)
    v_all = xs * dtx
    vw_all = v_all * wx

    ri = lax.broadcasted_iota(jnp.int32, (rows, rows), 0)
    ci = lax.broadcasted_iota(jnp.int32, (rows, rows), 1)
    causal = ri >= ci
    lo = lax.broadcasted_iota(jnp.int32, (rows, LANES), 1) < SSD_HEAD_DIM

    o_groups = []
    for g in range(SSD_GROUPS):
        kb = bc[:, g * SSD_STATE:(g + 1) * SSD_STATE].astype(BF16)
        qb = bc[:, (SSD_GROUPS + g) * SSD_STATE:(SSD_GROUPS + g + 1) * SSD_STATE].astype(BF16)
        qk = lax.dot_general(qb, kb, NT, preferred_element_type=F32)
        o_tiles = []
        for p in range(hpg // 2):
            acc = None
            for q in range(2):
                h = g * hpg + 2 * p + q
                diff = cum[:, h:h + 1] - cum_t[h:h + 1, :]
                dec = jnp.exp2(jnp.where(causal, diff, -1e30))
                pm = (qk * dec).astype(BF16)
                col = g * gw + p * LANES
                vp = v_all[:, col:col + LANES]
                vh = jnp.where(lo, vp, 0.0) if q == 0 else jnp.where(lo, 0.0, vp)
                t = _dot(pm, vh.astype(BF16))
                acc = t if acc is None else acc + t
            o_tiles.append(acc)
        o_intra = jnp.concatenate(o_tiles, axis=1)
        sg = s_ref[g]
        sl = slice(g * gw, (g + 1) * gw)
        o_inter = _dot(qb, sg.astype(BF16)) * ecum[:, sl]
        upd = lax.dot_general(kb, vw_all[:, sl].astype(BF16), TN, preferred_element_type=F32)
        s_ref[g] = sg * elast[:, sl] + upd
        o_groups.append(o_intra + o_inter)
    o = jnp.concatenate(o_groups, axis=1)
    y_ref[...] = _ssd_finish(o, xs, z_ref[...], dx_ref[...], nw_ref[...]).astype(BF16)

    yield
    @pl.when(c == nc - 1)
    def _():
        for h in range(SSD_HEADS):
            g, hl = divmod(h, hpg)
            sfin_ref[0, h] = s_ref[g, :, hl * SSD_HEAD_DIM:(hl + 1) * SSD_HEAD_DIM]
        cfin_ref[0, :, 0:D_SSD] = tailx
        cfin_ref[0, :, D_SSD:D_SSD + SSD_BC] = tailb


def ssd_prompt(pe, prm, nseq, seq, chunk):
    m = nseq * seq
    nc = seq // chunk
    row = lambda b, c: b * nc + c
    full = lambda shp: pl.BlockSpec(shp, lambda b, c: (0,) * len(shp))
    in_specs = [
        pl.BlockSpec((chunk, D_SSD), lambda b, c: (row(b, c), 0)),
        pl.BlockSpec((chunk, D_SSD), lambda b, c: (row(b, c), 1)),
        pl.BlockSpec((chunk, SSD_BC), lambda b, c: (row(b, c), 10)),
        pl.BlockSpec((chunk, LANES), lambda b, c: (row(b, c), DT_COL // LANES)),
        pl.BlockSpec((SSD_CONV, D_SSD), lambda b, c: (0, 0)),
        pl.BlockSpec((SSD_CONV, SSD_BC), lambda b, c: (0, 2)),
        pl.BlockSpec((1, D_SSD), lambda b, c: (0, 0)),
        pl.BlockSpec((1, SSD_BC), lambda b, c: (0, 2)),
        full((1, LANES)), full((1, LANES)), full((1, D_SSD)), full((1, D_SSD)),
    ]
    return dict(
        kernel=_ssd_prompt_kernel,
        arrays=[pe, pe, pe, pe, prm["ssd_cw"], prm["ssd_cw"], prm["ssd_cb"], prm["ssd_cb"],
                prm["ssd_dtb"], prm["ssd_alog"], prm["ssd_dx"], prm["ssd_nw"]],
        in_specs=in_specs,
        out_specs=[pl.BlockSpec((chunk, D_SSD), lambda b, c: (row(b, c), 0)),
                   pl.BlockSpec((1, SSD_HEADS, SSD_STATE, SSD_HEAD_DIM), lambda b, c: (b, 0, 0, 0)),
                   pl.BlockSpec((1, SSD_CONV - 1, D_SSD + SSD_BC), lambda b, c: (b, 0, 0))],
        out_shape=[jax.ShapeDtypeStruct((m, D_SSD), BF16),
                   jax.ShapeDtypeStruct((nseq, SSD_HEADS, SSD_STATE, SSD_HEAD_DIM), F32),
                   jax.ShapeDtypeStruct((nseq, SSD_CONV - 1, D_SSD + SSD_BC), F32)],
        scratch=[pltpu.VMEM((SSD_GROUPS, SSD_STATE, D_SSD // SSD_GROUPS), F32),
                 pltpu.VMEM((SUBLANES + chunk, D_SSD), F32),
                 pltpu.VMEM((SUBLANES + chunk, SSD_BC), F32)])


MIX_CHUNK = 128


def _run_parts(parts, grid, name, vmem):
    counts = [(len(p["arrays"]), len(p["out_shape"]), len(p["scratch"])) for p in parts]
    n_in = sum(c[0] for c in counts)
    n_out = sum(c[1] for c in counts)

    def body(*refs):
        i, o, s = 0, n_in, n_in + n_out
        gens = []
        for p, (a, b, c) in zip(parts, counts):
            gens.append(p["kernel"](*refs[i:i + a], *refs[o:o + b], *refs[s:s + c]))
            i, o, s = i + a, o + b, s + c
        for _ in range(3):
            for g in gens:
                next(g, None)

    outs = pl.pallas_call(
        body,
        grid=grid,
        in_specs=[sp for p in parts for sp in p["in_specs"]],
        out_specs=[sp for p in parts for sp in p["out_specs"]],
        out_shape=[sh for p in parts for sh in p["out_shape"]],
        scratch_shapes=[sc for p in parts for sc in p["scratch"]],
        compiler_params=_cparams(("parallel", "arbitrary"), vmem),
        name=name,
    )(*[a for p in parts for a in p["arrays"]])
    res, k = [], 0
    for _, b, _ in counts:
        res.append(outs[k:k + b])
        k += b
    return res


def _rotary(x, cos, sin_signed):
    return x * cos + pltpu.roll(x, RET_DK // 2, 1) * sin_signed


def _ret_prompt_kernel(q_ref, k_ref, v_ref, g_ref, cos_ref, sin_ref, o_ref, sfin_ref, s_ref):
    c = pl.program_id(1)
    nc = pl.num_programs(1)
    rows = q_ref.shape[0]

    @pl.when(c == 0)
    def _():
        s_ref[...] = jnp.zeros_like(s_ref)

    yield
    cos = cos_ref[...]
    sin = sin_ref[...]
    ri = lax.broadcasted_iota(jnp.int32, (rows, rows), 0)
    ci = lax.broadcasted_iota(jnp.int32, (rows, rows), 1)
    dij = jnp.where(ri >= ci, (ri - ci).astype(F32), 1e30)
    tk = lax.broadcasted_iota(jnp.int32, (rows, RET_DK), 0).astype(F32)
    tv = lax.broadcasted_iota(jnp.int32, (rows, RET_DV), 0).astype(F32)
    for h in range(RET_HEADS):
        lg = RET_LOG_GAMMA[h]
        lg2 = lg * LOG2E
        qr = _rotary(q_ref[:, h * RET_DK:(h + 1) * RET_DK], cos, sin)
        kr = _rotary(k_ref[:, h * RET_DK:(h + 1) * RET_DK], cos, sin) * (RET_DK ** -0.5)
        qb = qr.astype(BF16)
        s = lax.dot_general(qb, kr.astype(BF16), NT, preferred_element_type=F32)
        pm = (s * jnp.exp2(lg2 * dij)).astype(BF16)
        vb = v_ref[:, h * RET_DV:(h + 1) * RET_DV].astype(BF16)
        sh = s_ref[h]
        o = _dot(pm, vb) + _dot(qb, sh.astype(BF16)) * jnp.exp2(lg2 * (tv + 1.0))
        kw = (kr * jnp.exp2(lg2 * ((rows - 1.0) - tk))).astype(BF16)
        s_ref[h] = sh * math.exp(lg * rows) + lax.dot_general(kw, vb, TN, preferred_element_type=F32)
        gh = g_ref[:, h * RET_DV:(h + 1) * RET_DV]
        o_ref[:, h * RET_DV:(h + 1) * RET_DV] = (_rms(o) * _silu(gh)).astype(BF16)

    yield
    @pl.when(c == nc - 1)
    def _():
        sfin_ref[0] = s_ref[...]


def ret_prompt(pe, cos, sin, nseq, seq, chunk):
    m = nseq * seq
    nc = seq // chunk
    row = lambda b, c: b * nc + c
    return dict(
        kernel=_ret_prompt_kernel,
        arrays=[pe, pe, pe, pe, cos, sin],
        in_specs=[pl.BlockSpec((chunk, RET_QK), lambda b, c: (row(b, c), 8)),
                  pl.BlockSpec((chunk, RET_QK), lambda b, c: (row(b, c), 9)),
                  pl.BlockSpec((chunk, D_RET), lambda b, c: (row(b, c), 2)),
                  pl.BlockSpec((chunk, D_RET), lambda b, c: (row(b, c), 3)),
                  pl.BlockSpec((chunk, RET_DK), lambda b, c: (c, 0)),
                  pl.BlockSpec((chunk, RET_DK), lambda b, c: (c, 0))],
        out_specs=[pl.BlockSpec((chunk, D_RET), lambda b, c: (row(b, c), 0)),
                   pl.BlockSpec((1, RET_HEADS, RET_DK, RET_DV), lambda b, c: (b, 0, 0, 0))],
        out_shape=[jax.ShapeDtypeStruct((m, D_RET), BF16),
                   jax.ShapeDtypeStruct((nseq, RET_HEADS, RET_DK, RET_DV), F32)],
        scratch=[pltpu.VMEM((RET_HEADS, RET_DK, RET_DV), F32)])


HG_HB = HG_HEADS


def _hg_gates(hq, hf, lb):
    q = _silu(hq)
    f = lb + (1.0 - lb) * jax.nn.sigmoid(hf)
    return q, f, 1.0 - f, jnp.log(f)


def _hg_tables(rows):
    r = np.arange(rows)[:, None]
    t = np.arange(rows)[None, :]
    sums = [t <= r]
    masks = []
    s = rows // 2
    while s >= 1:
        blk, pos = r // (2 * s), r % (2 * s)
        ref = blk * 2 * s + s - 1
        upper = pos >= s
        sums.append(np.where(upper, (t > ref) & (t <= r), (t > r) & (t <= ref)))
        masks.append((blk == t // (2 * s)) & upper & (t % (2 * s) < s))
        s //= 2
    masks.append(r == t)
    return (np.concatenate(sums, axis=0).astype(np.float32),
            np.stack(masks).astype(np.float32))


def _hg_prompt_kernel(hq_ref, hf_ref, hi_ref, hg_ref, lb_ref, nw_ref, sums_ref, masks_ref,
                      og_ref, sfin_ref, st_ref):
    c = pl.program_id(1)
    nc = pl.num_programs(1)
    rows = hq_ref.shape[0]
    nlev = masks_ref.shape[0] - 1

    @pl.when(c == 0)
    def _():
        st_ref[...] = jnp.zeros_like(st_ref)

    yield
    q_all, _, kk_all, lg_all = _hg_gates(hq_ref[...], hf_ref[...], lb_ref[...])
    lg_all = lg_all * LOG2E
    lg_hi = lg_all.astype(BF16)
    lg_lo = (lg_all - lg_hi.astype(F32)).astype(BF16)
    dall = _dot(sums_ref[...], jnp.concatenate([lg_hi, lg_lo], axis=0))
    cum_all = dall[0:rows]
    ecum_all = jnp.exp2(cum_all)
    for h in range(HG_HB):
        sl = slice(h * HG_DK, (h + 1) * HG_DK)
        q = q_all[:, sl]
        kk = kk_all[:, sl]
        cum = cum_all[:, sl]
        vb = hi_ref[:, sl].astype(BF16)
        last = cum[rows - 1:rows, :]
        st = st_ref[h]
        o = lax.dot_general((q * ecum_all[:, sl]).astype(BF16), st.astype(BF16), NT,
                            preferred_element_type=F32)
        kt = (kk * jnp.exp2(last - cum)).astype(BF16)
        st_ref[h] = st * jnp.exp2(last) + lax.dot_general(vb, kt, TN, preferred_element_type=F32)
        qb = q.astype(BF16)
        kb = kk.astype(BF16)
        a = masks_ref[nlev] * lax.dot_general(qb, kb, NT, preferred_element_type=F32)
        for l in range(nlev):
            e = jnp.exp2(dall[(l + 1) * rows:(l + 2) * rows, sl]).astype(BF16)
            a = a + masks_ref[l] * lax.dot_general(qb * e, kb * e, NT, preferred_element_type=F32)
        o = o + _dot(a.astype(BF16), vb)
        og_ref[:, sl] = (_rms(o) * nw_ref[...] * _silu(hg_ref[:, sl])).astype(BF16)

    yield
    @pl.when(c == nc - 1)
    def _():
        for h in range(HG_HB):
            sfin_ref[0, h] = st_ref[h].T


def hg_prompt(po, lb, nw, nseq, seq, chunk):
    m = nseq * seq
    nc = seq // chunk
    row = lambda b, c: b * nc + c
    sums_np, masks_np = _hg_tables(chunk)
    sums = jnp.asarray(np.concatenate([sums_np, sums_np], axis=1), BF16)
    masks = jnp.asarray(masks_np, F32)
    return dict(
        kernel=_hg_prompt_kernel,
        arrays=[po, po, po, po, lb, nw, sums, masks],
        in_specs=[pl.BlockSpec((chunk, D_HG), lambda b, c: (row(b, c), 0)),
                  pl.BlockSpec((chunk, D_HG), lambda b, c: (row(b, c), 1)),
                  pl.BlockSpec((chunk, D_HG), lambda b, c: (row(b, c), 2)),
                  pl.BlockSpec((chunk, D_HG), lambda b, c: (row(b, c), 3)),
                  pl.BlockSpec((1, D_HG), lambda b, c: (0, 0)),
                  pl.BlockSpec((1, HG_DV), lambda b, c: (0, 0)),
                  pl.BlockSpec(sums.shape, lambda b, c: (0, 0)),
                  pl.BlockSpec(masks.shape, lambda b, c: (0, 0, 0))],
        out_specs=[pl.BlockSpec((chunk, D_HG), lambda b, c: (row(b, c), 0)),
                   pl.BlockSpec((1, HG_HEADS, HG_DK, HG_DV), lambda b, c: (b, 0, 0, 0))],
        out_shape=[jax.ShapeDtypeStruct((m, D_HG), BF16),
                   jax.ShapeDtypeStruct((nseq, HG_HEADS, HG_DK, HG_DV), F32)],
        scratch=[pltpu.VMEM((HG_HEADS, HG_DV, HG_DK), F32)])


def _lin_scan(a, b, h0):
    rows, cols = a.shape
    groups = rows // SUBLANES
    a3 = a.reshape(groups, SUBLANES, cols)
    b3 = b.reshape(groups, SUBLANES, cols)
    si = lax.broadcasted_iota(jnp.int32, a3.shape, 1)
    d = 1
    while d < SUBLANES:
        keep = si >= d
        a_s = jnp.where(keep, pltpu.roll(a3, d, 1), 1.0)
        b_s = jnp.where(keep, pltpu.roll(b3, d, 1), 0.0)
        b3 = a3 * b_s + b3
        a3 = a3 * a_s
        d *= 2
    h_in = jnp.broadcast_to(h0, (SUBLANES, cols))
    out = []
    for g in range(groups):
        hg = a3[g] * h_in + b3[g]
        out.append(hg)
        h_in = jnp.broadcast_to(hg[SUBLANES - 1:SUBLANES, :], (SUBLANES, cols))
    return jnp.concatenate(out, axis=0), h_in[0:1, :]


def _lru_gates(xr, wr_ref, br, wi_ref, bi, ap):
    xb = xr.astype(BF16)
    r_parts, i_parts = [], []
    for n in range(LRU_BLOCKS):
        xn = xb[:, n * LRU_BW:(n + 1) * LRU_BW]
        r_parts.append(_dot(xn, wr_ref[n].astype(BF16)))
        i_parts.append(_dot(xn, wi_ref[n].astype(BF16)))
    r = jax.nn.sigmoid(jnp.concatenate(r_parts, axis=1) + br)
    gi = jax.nn.sigmoid(jnp.concatenate(i_parts, axis=1) + bi)
    la = -LRU_C * r * _softplus(-ap)
    a = jnp.exp(la)
    th = jnp.tanh(la)
    mult = jnp.sqrt(-2.0 * th / (1.0 - th))
    return a, mult, gi


def _lru_prompt_kernel(rx_ref, rg_ref, cw_ref, cb_ref, wr_ref, br_ref, wi_ref, bi_ref, ap_ref,
                       yl_ref, hfin_ref, cfin_ref, buf_ref, hc_ref):
    c = pl.program_id(1)
    nc = pl.num_programs(1)
    rows = rx_ref.shape[0]

    @pl.when(c == 0)
    def _():
        buf_ref[0:SUBLANES, :] = jnp.zeros((SUBLANES, D_RNN), F32)
        hc_ref[...] = jnp.zeros_like(hc_ref)

    yield
    xr, tail = _causal_conv(buf_ref, rx_ref[...], cw_ref, cb_ref, 1, LRU_CONV)
    a, mult, gi = _lru_gates(xr, wr_ref, br_ref[...], wi_ref, bi_ref[...], ap_ref[...])
    ri = lax.broadcasted_iota(jnp.int32, (rows, D_RNN), 0)
    mult = jnp.where(jnp.logical_and(c == 0, ri == 0), 1.0, mult)
    hs, hlast = _lin_scan(a, mult * gi * xr, hc_ref[0:1, :])
    hc_ref[0:1, :] = hlast
    yl_ref[...] = (hs * _gelu_tanh(rg_ref[...])).astype(BF16)

    yield
    @pl.when(c == nc - 1)
    def _():
        hfin_ref[0] = hlast
        cfin_ref[0] = tail


def lru_prompt(po, prm, nseq, seq, chunk):
    m = nseq * seq
    nc = seq // chunk
    row = lambda b, c: b * nc + c
    full = lambda shp: pl.BlockSpec(shp, lambda b, c: (0,) * len(shp))
    return dict(
        kernel=_lru_prompt_kernel,
        arrays=[po, po, prm["lru_cw"], prm["lru_cb"], prm["lru_wr"], prm["lru_br"], prm["lru_wi"],
                prm["lru_bi"], prm["lru_ap"]],
        in_specs=[pl.BlockSpec((chunk, D_RNN), lambda b, c: (row(b, c), 4)),
                  pl.BlockSpec((chunk, D_RNN), lambda b, c: (row(b, c), 5)),
                  full((LRU_CONV, D_RNN)), full((1, D_RNN)),
                  full((LRU_BLOCKS, LRU_BW, LRU_BW)), full((1, D_RNN)),
                  full((LRU_BLOCKS, LRU_BW, LRU_BW)), full((1, D_RNN)), full((1, D_RNN))],
        out_specs=[pl.BlockSpec((chunk, D_RNN), lambda b, c: (row(b, c), 0)),
                   pl.BlockSpec((1, 1, D_RNN), lambda b, c: (b, 0, 0)),
                   pl.BlockSpec((1, LRU_CONV - 1, D_RNN), lambda b, c: (b, 0, 0))],
        out_shape=[jax.ShapeDtypeStruct((m, D_RNN), BF16),
                   jax.ShapeDtypeStruct((nseq, 1, D_RNN), F32),
                   jax.ShapeDtypeStruct((nseq, LRU_CONV - 1, D_RNN), F32)],
        scratch=[pltpu.VMEM((SUBLANES + chunk, D_RNN), F32),
                 pltpu.VMEM((SUBLANES, D_RNN), F32)])


TOKP = SUBLANES


def _ssd_state_kernel(q_ref, k_ref, vw_ref, dec_ref, st_ref, u_ref, snew_ref):
    hpg = SSD_HEADS // SSD_GROUPS
    gw = hpg * SSD_HEAD_DIM

    def body(b, carry):
        q = q_ref[b].astype(BF16)
        k = k_ref[b].astype(BF16)
        vw = vw_ref[b].astype(BF16)
        for g in range(SSD_GROUPS):
            ks = slice(g * SSD_STATE, (g + 1) * SSD_STATE)
            stg = st_ref[b, g * hpg:(g + 1) * hpg].reshape(gw, SSD_STATE)
            u_ref[b, :, g * gw:(g + 1) * gw] = lax.dot_general(
                q[:, ks], stg.astype(BF16), NT, preferred_element_type=F32)
            upd = lax.dot_general(vw[:, g * gw:(g + 1) * gw], k[:, ks], TN, preferred_element_type=F32)
            for hl in range(hpg):
                h = g * hpg + hl
                snew_ref[b, h] = (st_ref[b, h] * dec_ref[b, h:h + 1, :]
                                  + upd[hl * SSD_HEAD_DIM:(hl + 1) * SSD_HEAD_DIM, :])
        return carry

    lax.fori_loop(0, st_ref.shape[0], body, 0)


def _ret_state_kernel(q_ref, k_ref, v_ref, s_ref, u_ref, snew_ref, *, ntok):
    def body(b, carry):
        q = q_ref[b].astype(BF16)
        k = k_ref[b].astype(BF16)
        v = v_ref[b].astype(BF16)
        for h in range(RET_HEADS):
            ks = slice(h * RET_DK, (h + 1) * RET_DK)
            vs = slice(h * RET_DV, (h + 1) * RET_DV)
            s0 = s_ref[b, h]
            u_ref[b, :, vs] = _dot(q[:, ks], s0.astype(BF16))
            upd = lax.dot_general(k[:, ks], v[:, vs], TN, preferred_element_type=F32)
            snew_ref[b, h] = s0 * math.exp(RET_LOG_GAMMA[h] * ntok) + upd
        return carry

    lax.fori_loop(0, s_ref.shape[0], body, 0)


def _hg_state_kernel(q_ref, k_ref, v_ref, dcol_ref, s_ref, u_ref, snew_ref):
    def body(b, carry):
        q = q_ref[b].astype(BF16)
        k = k_ref[b].astype(BF16)
        v = v_ref[b].astype(BF16)
        dc = dcol_ref[b]
        for h in range(HG_HEADS):
            ks = slice(h * HG_DK, (h + 1) * HG_DK)
            s0 = s_ref[b, h]
            u_ref[b, :, ks] = _dot(q[:, ks], s0.astype(BF16))
            upd = lax.dot_general(k[:, ks], v[:, ks], TN, preferred_element_type=F32)
            snew_ref[b, h] = s0 * dc[:, h:h + 1] + upd
        return carry

    lax.fori_loop(0, s_ref.shape[0], body, 0)


def _state_part(kern, rows_in, extra, s, u_cols, nseq, nc):
    nb = s.shape[0]
    bb = nb // (nseq * nc)
    assert bb * nseq * nc == nb
    blk = lambda a: pl.BlockSpec((bb,) + a.shape[1:],
                                 lambda b, c: (b * nc + c,) + (0,) * (a.ndim - 1))
    ins = list(rows_in) + ([extra] if extra is not None else []) + [s]

    def phases(*refs):
        yield
        kern(*refs)
        yield

    return dict(kernel=phases, arrays=ins, in_specs=[blk(a) for a in ins],
                out_specs=[pl.BlockSpec((bb, TOKP, u_cols), lambda b, c: (b * nc + c, 0, 0)), blk(s)],
                out_shape=[jax.ShapeDtypeStruct((nb, TOKP, u_cols), F32),
                           jax.ShapeDtypeStruct(s.shape, F32)],
                scratch=[])


def _tok(x, t, nb):
    return x[t * nb:(t + 1) * nb]


def _head_sums(x, width):
    r, n = x.shape
    tiles = []
    for h in range(n // width):
        s = jnp.sum(x[:, h * width:(h + 1) * width], axis=1, keepdims=True)
        tiles.append(jnp.broadcast_to(s, (r, width)))
    return jnp.concatenate(tiles, axis=1)


def _ssd_sample_pre_kernel(xs_ref, bc_ref, dt_ref, prevx_ref, prevb_ref, cwx_ref, cwb_ref,
                           cbx_ref, cbb_ref, dtb_ref, alog_ref,
                           xs_out, bc_out, vw_out, oi_out, ecum_out, elast_out, tailx_out, tailb_out,
                           bufx_ref, bufb_ref, *, nb, ntok):
    n = (SSD_CONV - 1) * nb
    offx = bufx_ref.shape[0] - nb * ntok
    bufx_ref[offx - n:offx, :] = prevx_ref[...]
    bufb_ref[offx - n:offx, :] = prevb_ref[...]
    xs_c, tailx = _causal_conv(bufx_ref, xs_ref[...], cwx_ref, cbx_ref, nb, SSD_CONV)
    bc_c, tailb = _causal_conv(bufb_ref, bc_ref[...], cwb_ref, cbb_ref, nb, SSD_CONV)
    xs, bc, dt, la = _ssd_gates(xs_c, bc_c, dt_ref[...], dtb_ref[...], alog_ref[...])
    tailx_out[...] = tailx
    tailb_out[...] = tailb
    xs_out[...] = xs
    bc_out[...] = bc
    cums = []
    for t in range(ntok):
        lt = _tok(la, t, nb)
        cums.append(lt if t == 0 else cums[-1] + lt)
    cumx = [_expand_heads64(cm, SSD_HEADS) for cm in cums]
    v = xs * _expand_heads64(dt, SSD_HEADS)
    kw = SSD_GROUPS * SSD_STATE
    for t in range(ntok):
        ct = _tok(bc, t, nb)[:, kw:2 * kw]
        acc = None
        for t2 in range(t + 1):
            bt = _tok(bc, t2, nb)[:, 0:kw]
            sc = _head_sums(ct * bt, SSD_STATE)
            scx = jnp.concatenate(
                [jnp.concatenate([sc[:, g * SSD_STATE:(g + 1) * SSD_STATE]] * 4, axis=1)
                 for g in range(SSD_GROUPS)], axis=1)
            term = scx * jnp.exp(cumx[t] - cumx[t2]) * _tok(v, t2, nb)
            acc = term if acc is None else acc + term
        oi_out[t * nb:(t + 1) * nb, :] = acc
        ecum_out[t * nb:(t + 1) * nb, :] = jnp.exp(cumx[t])
        vw_out[t * nb:(t + 1) * nb, :] = _tok(v, t, nb) * jnp.exp(cumx[ntok - 1] - cumx[t])
    elast_out[...] = jnp.exp(cums[ntok - 1])


def _ssd_sample_post_kernel(oi_ref, u_ref, ecum_ref, xs_ref, z_ref, dx_ref, nw_ref, y_ref):
    o = oi_ref[...] + ecum_ref[...] * u_ref[...]
    y_ref[...] = _ssd_finish(o, xs_ref[...], z_ref[...], dx_ref[...], nw_ref[...]).astype(BF16)


def _ret_sample_pre_kernel(q_ref, k_ref, v_ref, cos_ref, sin_ref, qd_out, kd_out, oi_out, *, nb, ntok):
    rows = nb * ntok
    cos = jnp.concatenate([jnp.broadcast_to(cos_ref[t:t + 1, :], (nb, RET_DK)) for t in range(ntok)], axis=0)
    sin = jnp.concatenate([jnp.broadcast_to(sin_ref[t:t + 1, :], (nb, RET_DK)) for t in range(ntok)], axis=0)
    qr, kr = [], []
    for h in range(RET_HEADS):
        sl = slice(h * RET_DK, (h + 1) * RET_DK)
        qr.append(_rotary(q_ref[:, sl], cos, sin))
        kr.append(_rotary(k_ref[:, sl], cos, sin) * (RET_DK ** -0.5))
    qr = jnp.concatenate(qr, axis=1)
    kr = jnp.concatenate(kr, axis=1)
    v = v_ref[...]
    for t in range(ntok):
        qt = _tok(qr, t, nb)
        acc = None
        for t2 in range(t + 1):
            sc = _head_sums(qt * _tok(kr, t2, nb), RET_DK)
            vt = _tok(v, t2, nb)
            tiles = []
            for h in range(RET_HEADS):
                dec = math.exp(RET_LOG_GAMMA[h] * (t - t2))
                s = sc[:, h * RET_DK:(h + 1) * RET_DK] * dec
                tiles.append(jnp.concatenate([s, s], axis=1) * vt[:, h * RET_DV:(h + 1) * RET_DV])
            term = jnp.concatenate(tiles, axis=1)
            acc = term if acc is None else acc + term
        oi_out[t * nb:(t + 1) * nb, :] = acc
        qd = jnp.concatenate([qt[:, h * RET_DK:(h + 1) * RET_DK] * math.exp(RET_LOG_GAMMA[h] * (t + 1))
                              for h in range(RET_HEADS)], axis=1)
        kt = _tok(kr, t, nb)
        kd = jnp.concatenate([kt[:, h * RET_DK:(h + 1) * RET_DK] * math.exp(RET_LOG_GAMMA[h] * (ntok - 1 - t))
                              for h in range(RET_HEADS)], axis=1)
        qd_out[t * nb:(t + 1) * nb, :] = qd
        kd_out[t * nb:(t + 1) * nb, :] = kd


def _ret_sample_post_kernel(oi_ref, u_ref, g_ref, o_ref):
    o = oi_ref[...] + u_ref[...]
    g = g_ref[...]
    for h in range(RET_HEADS):
        sl = slice(h * RET_DV, (h + 1) * RET_DV)
        o_ref[:, sl] = (_rms(o[:, sl]) * _silu(g[:, sl])).astype(BF16)


def _hg_sample_pre_kernel(hq_ref, hf_ref, hi_ref, lb_ref, qd_out, kd_out, dl_out, oi_out, *, nb, ntok):
    q, _, kk, lg = _hg_gates(hq_ref[...], hf_ref[...], lb_ref[...])
    v = hi_ref[...]
    cums = []
    for t in range(ntok):
        lt = _tok(lg, t, nb)
        cums.append(lt if t == 0 else cums[-1] + lt)
    for t in range(ntok):
        qt = _tok(q, t, nb)
        acc = None
        for t2 in range(t + 1):
            w = qt * _tok(kk, t2, nb)
            if t2 < t:
                w = w * jnp.exp(cums[t] - cums[t2])
            term = _head_sums(w, HG_DK) * _tok(v, t2, nb)
            acc = term if acc is None else acc + term
        oi_out[t * nb:(t + 1) * nb, :] = acc
        qd_out[t * nb:(t + 1) * nb, :] = qt * jnp.exp(cums[t])
        kd_out[t * nb:(t + 1) * nb, :] = _tok(kk, t, nb) * jnp.exp(cums[ntok - 1] - cums[t])
    dl_out[...] = jnp.exp(cums[ntok - 1])


def _hg_sample_post_kernel(oi_ref, u_ref, hg_ref, nw_ref, og_ref):
    o = oi_ref[...] + u_ref[...]
    hg = hg_ref[...]
    for h in range(HG_HEADS):
        sl = slice(h * HG_DV, (h + 1) * HG_DV)
        og_ref[:, sl] = (_rms(o[:, sl]) * nw_ref[...] * _silu(hg[:, sl])).astype(BF16)


def _lru_sample_kernel(rx_ref, rg_ref, prev_ref, h0_ref, cw_ref, cb_ref, wr_ref, br_ref, wi_ref,
                       bi_ref, ap_ref, yl_out, hfin_out, tail_out, buf_ref, *, nb, ntok):
    rows = nb * ntok
    n = (LRU_CONV - 1) * nb
    off = buf_ref.shape[0] - rows
    buf_ref[off - n:off, :] = prev_ref[...]
    xr, tail = _causal_conv(buf_ref, rx_ref[...], cw_ref, cb_ref, nb, LRU_CONV)
    tail_out[...] = tail
    a, mult, gi = _lru_gates(xr, wr_ref, br_ref[...], wi_ref, bi_ref[...], ap_ref[...])
    b = mult * gi * xr
    h = h0_ref[...]
    for t in range(ntok):
        h = _tok(a, t, nb) * h + _tok(b, t, nb)
        yl_out[t * nb:(t + 1) * nb, :] = (h * _gelu_tanh(rg_ref[t * nb:(t + 1) * nb, :])).astype(BF16)
    hfin_out[...] = h


def _whole(shape):
    return pl.BlockSpec(shape, lambda i: (0,) * len(shape))


def _colblock(rows, width, idx):
    return pl.BlockSpec((rows, width), functools.partial(lambda i, k: (0, k), k=idx))


def _call_whole(kern, in_arrays, in_specs, out_shapes, scratch, name, vmem):
    return pl.pallas_call(
        kern, grid=(1,), in_specs=in_specs,
        out_specs=[_whole(s.shape) for s in out_shapes],
        out_shape=out_shapes, scratch_shapes=scratch,
        compiler_params=_cparams(("arbitrary",), vmem), name=name,
    )(*in_arrays)


def _to_token_major(s):
    nb, w, c = s.shape
    return jnp.transpose(s, (1, 0, 2)).reshape(w * nb, c)


def _from_token_major(x, nb):
    w = x.shape[0] // nb
    return jnp.transpose(x.reshape(w, nb, x.shape[1]), (1, 0, 2))


def _rows_to_batch(x, nb):
    ntok = x.shape[0] // nb
    x = jnp.transpose(x.reshape(ntok, nb, x.shape[1]), (1, 0, 2))
    return jnp.pad(x, ((0, 0), (0, TOKP - ntok), (0, 0)))


def _batch_to_rows(u, ntok):
    nb = u.shape[0]
    return jnp.transpose(u[:, :ntok], (1, 0, 2)).reshape(ntok * nb, u.shape[2])


def ssd_sample_pre(pe, s_ssm, s_conv, prm, nb, ntok, nseq, nc):
    rows = nb * ntok
    n = (SSD_CONV - 1) * nb
    prev = _to_token_major(s_conv)
    off = _conv_off(nb, SSD_CONV)
    f = lambda *shape: jax.ShapeDtypeStruct(shape, F32)
    outs = _call_whole(
        functools.partial(_ssd_sample_pre_kernel, nb=nb, ntok=ntok),
        [pe, pe, pe, prev, prev, prm["ssd_cw"], prm["ssd_cw"], prm["ssd_cb"], prm["ssd_cb"],
         prm["ssd_dtb"], prm["ssd_alog"]],
        [_colblock(rows, D_SSD, 1), _colblock(rows, SSD_BC, 10), _colblock(rows, LANES, DT_COL // LANES),
         _colblock(n, D_SSD, 0), _colblock(n, SSD_BC, 2),
         _colblock(SSD_CONV, D_SSD, 0), _colblock(SSD_CONV, SSD_BC, 2),
         _colblock(1, D_SSD, 0), _colblock(1, SSD_BC, 2), _whole((1, LANES)), _whole((1, LANES))],
        [f(rows, D_SSD), f(rows, SSD_BC), f(rows, D_SSD), f(rows, D_SSD), f(rows, D_SSD),
         f(nb, LANES), f(n, D_SSD), f(n, SSD_BC)],
        [pltpu.VMEM((off + rows, D_SSD), F32), pltpu.VMEM((off + rows, SSD_BC), F32)],
        "ssd_sample_pre", _mib(56))
    xs, bc, vw, oi, ecum, elast, tailx, tailb = outs
    kw = SSD_GROUPS * SSD_STATE
    dec = jnp.broadcast_to(elast[:, :SSD_HEADS, None], (nb, SSD_HEADS, SSD_STATE))
    st = jnp.swapaxes(s_ssm, -1, -2)
    part = _state_part(_ssd_state_kernel,
                       [_rows_to_batch(bc[:, kw:], nb), _rows_to_batch(bc[:, :kw], nb),
                        _rows_to_batch(vw, nb)], dec, st, D_SSD, nseq, nc)
    return dict(oi=oi, ecum=ecum, xs=xs, tailx=tailx, tailb=tailb), part


def ssd_sample_post(ctx, u, st_new, pe, prm, nb, ntok):
    rows = nb * ntok
    s_new = jnp.swapaxes(st_new, -1, -2)
    u = _batch_to_rows(u, ntok)
    (y,) = _call_whole(
        _ssd_sample_post_kernel, [ctx["oi"], u, ctx["ecum"], ctx["xs"], pe, prm["ssd_dx"], prm["ssd_nw"]],
        [_whole((rows, D_SSD))] * 4 + [_colblock(rows, D_SSD, 0), _whole((1, D_SSD)), _whole((1, D_SSD))],
        [jax.ShapeDtypeStruct((rows, D_SSD), BF16)], [], "ssd_sample_post", _mib(48))
    conv_new = _from_token_major(jnp.concatenate([ctx["tailx"], ctx["tailb"]], axis=1), nb)
    return y, s_new, conv_new


def ret_sample_pre(pe, s_ret, cos, sin, nb, ntok, nseq, nc):
    rows = nb * ntok
    f = lambda *shape: jax.ShapeDtypeStruct(shape, F32)
    qd, kd, oi = _call_whole(
        functools.partial(_ret_sample_pre_kernel, nb=nb, ntok=ntok),
        [pe, pe, pe, cos, sin],
        [_colblock(rows, RET_QK, 8), _colblock(rows, RET_QK, 9), _colblock(rows, D_RET, 2),
         _whole(cos.shape), _whole(sin.shape)],
        [f(rows, RET_QK), f(rows, RET_QK), f(rows, D_RET)], [], "ret_sample_pre", _mib(48))
    v = _rows_to_batch(pe[:, 2 * D_RET:3 * D_RET], nb)
    part = _state_part(functools.partial(_ret_state_kernel, ntok=ntok),
                       [_rows_to_batch(qd, nb), _rows_to_batch(kd, nb), v], None, s_ret,
                       D_RET, nseq, nc)
    return oi, part


def ret_sample_post(oi, u, pe, nb, ntok):
    rows = nb * ntok
    u = _batch_to_rows(u, ntok)
    (o,) = _call_whole(
        _ret_sample_post_kernel, [oi, u, pe],
        [_whole((rows, D_RET)), _whole((rows, D_RET)), _colblock(rows, D_RET, 3)],
        [jax.ShapeDtypeStruct((rows, D_RET), BF16)], [], "ret_sample_post", _mib(40))
    return o


def hg_sample_pre(po, s_hg, lb, nb, ntok, nseq, nc):
    rows = nb * ntok
    f = lambda *shape: jax.ShapeDtypeStruct(shape, F32)
    qd, kd, dl, oi = _call_whole(
        functools.partial(_hg_sample_pre_kernel, nb=nb, ntok=ntok),
        [po, po, po, lb],
        [_colblock(rows, D_HG, 0), _colblock(rows, D_HG, 1), _colblock(rows, D_HG, 2), _whole((1, D_HG))],
        [f(rows, D_HG), f(rows, D_HG), f(nb, D_HG), f(rows, D_HG)], [], "hg_sample_pre", _mib(48))
    dcol = jnp.pad(jnp.transpose(dl.reshape(nb, HG_HEADS, HG_DK), (0, 2, 1)),
                   ((0, 0), (0, 0), (0, LANES - HG_HEADS)))
    v = _rows_to_batch(po[:, 2 * D_HG:3 * D_HG], nb)
    part = _state_part(_hg_state_kernel, [_rows_to_batch(qd, nb), _rows_to_batch(kd, nb), v],
                       dcol, s_hg, D_HG, nseq, nc)
    return oi, part


def hg_sample_post(oi, u, po, nw, nb, ntok):
    rows = nb * ntok
    u = _batch_to_rows(u, ntok)
    (og,) = _call_whole(
        _hg_sample_post_kernel, [oi, u, po, nw],
        [_whole((rows, D_HG)), _whole((rows, D_HG)), _colblock(rows, D_HG, 3), _whole((1, HG_DV))],
        [jax.ShapeDtypeStruct((rows, D_HG), BF16)], [], "hg_sample_post", _mib(40))
    return og


def lru_sample(po, s_lru, s_lconv, prm, nb, ntok):
    rows = nb * ntok
    n = (LRU_CONV - 1) * nb
    off = _conv_off(nb, LRU_CONV)
    prev = _to_token_major(s_lconv)
    yl, hfin, tail = _call_whole(
        functools.partial(_lru_sample_kernel, nb=nb, ntok=ntok),
        [po, po, prev, s_lru, prm["lru_cw"], prm["lru_cb"], prm["lru_wr"], prm["lru_br"], prm["lru_wi"],
         prm["lru_bi"], prm["lru_ap"]],
        [_colblock(rows, D_RNN, 4), _colblock(rows, D_RNN, 5), _whole((n, D_RNN)), _whole((nb, D_RNN)),
         _whole((LRU_CONV, D_RNN)), _whole((1, D_RNN)), _whole((LRU_BLOCKS, LRU_BW, LRU_BW)),
         _whole((1, D_RNN)), _whole((LRU_BLOCKS, LRU_BW, LRU_BW)), _whole((1, D_RNN)), _whole((1, D_RNN))],
        [jax.ShapeDtypeStruct((rows, D_RNN), BF16), jax.ShapeDtypeStruct((nb, D_RNN), F32),
         jax.ShapeDtypeStruct((n, D_RNN), F32)],
        [pltpu.VMEM((off + rows, D_RNN), F32)], "lru_sample", _mib(48))
    return yl, hfin, _from_token_major(tail, nb)


def _rope_tables(pos):
    half = RET_DK // 2
    inv = ROPE_BASE ** (-jnp.arange(half, dtype=F32) / half)
    ang = pos.astype(F32)[:, None] * inv[None, :]
    cos, sin = jnp.cos(ang), jnp.sin(ang)
    return jnp.concatenate([cos, cos], axis=1), jnp.concatenate([-sin, sin], axis=1)


def _pad_lanes(v):
    return jnp.pad(v.astype(F32), (0, LANES - v.shape[0])).reshape(1, LANES)


def _prepare(p):
    lbs = jnp.cumsum(jax.nn.softmax(p["hg_lower_bounds"].astype(F32), axis=0), axis=0)
    lbs = lbs - lbs[0]
    return {
        "even_wt": jnp.swapaxes(p["even_w_in"][0], 0, 1),
        "ssd_cw": p["ssd_conv_w"][0], "ssd_cb": p["ssd_conv_b"][0].reshape(1, -1),
        "ssd_dtb": _pad_lanes(p["ssd_dt_bias"][0]), "ssd_alog": _pad_lanes(p["ssd_A_log"][0]),
        "ssd_dx": jnp.repeat(p["ssd_D"][0], SSD_HEAD_DIM).reshape(1, D_SSD),
        "ssd_nw": p["ssd_norm_w"][0].reshape(1, D_SSD),
        "hg_lb": lbs[1].reshape(1, D_HG), "hg_nw": p["hg_norm_w"][0].reshape(1, HG_DV),
        "lru_cw": p["lru_conv_w"][0], "lru_cb": p["lru_conv_b"][0].reshape(1, D_RNN),
        "lru_wr": p["lru_w_r"][0], "lru_br": p["lru_b_r"][0].reshape(1, D_RNN),
        "lru_wi": p["lru_w_i"][0], "lru_bi": p["lru_b_i"][0].reshape(1, D_RNN),
        "lru_ap": p["lru_a_param"][0].reshape(1, D_RNN),
    }


def _even_row_offsets(bn):
    o_xbc, o_dt = D_SSD, 2 * D_SSD + SSD_BC
    o_q = o_dt + SSD_HEADS
    o_k, o_v = o_q + RET_QK, o_q + 2 * RET_QK
    o_g = o_v + D_RET
    segs = [(0, D_SSD), (o_xbc, D_SSD), (o_v, D_RET), (o_g, D_RET), (o_q, RET_QK), (o_k, RET_QK),
            (o_xbc + D_SSD, SSD_BC), (o_dt, bn)]
    offs = [start + i for start, width in segs for i in range(0, width, bn)]
    assert len(offs) * bn == EVEN_PACKED and offs[-1] + bn <= D_IN_EVEN
    return offs


def _even_proj(x, p, prm, bm):
    return mm_in_t(x, p["norm_mix"][0], prm["even_wt"], _even_row_offsets(512), bm, 512)


def _ffn(x, p, prm, l, prev, shift, group_rows, bm, bf=512):
    act, st = ffn_up(x, p["norm_ffn"][l], p["ffn_w_up"], l, p["ffn_conv_w"][l], p["ffn_conv_b"][l],
                     prev, shift, group_rows, bm, bf)
    return mm_out([act], p["ffn_w_down"], l, x, min(bm, 1024), 256), st


def _trunks(xp, xs, st, p, prm, nseq, seq, nb, ntok):
    m = nseq * seq
    rows = nb * ntok
    bm = min(2048, seq)
    nc = seq // MIX_CHUNK
    grid = (nseq, nc)
    cos_p, sin_p = _rope_tables(jnp.arange(seq, dtype=jnp.int32))
    cos_s, sin_s = _rope_tables(PAST_LEN + jnp.arange(ntok, dtype=jnp.int32))
    zeros_ffn = jnp.zeros((nseq, FFN_CONV - 1, D_FF), F32)

    pe_s = _even_proj(xs, p, prm, rows)
    ssd_ctx, ssd_part = ssd_sample_pre(pe_s, st["ssm"][0], st["ssm_conv"][0], prm, nb, ntok, nseq, nc)
    ret_oi, ret_part = ret_sample_pre(pe_s, st["ret"][0], cos_s, sin_s, nb, ntok, nseq, nc)
    pe_p = _even_proj(xp, p, prm, bm)
    (y, ssm_p, ssm_conv_p), (o, ret_p), (u_ssd, st_ssd), (u_ret, ret_s) = _run_parts(
        [ssd_prompt(pe_p, prm, nseq, seq, MIX_CHUNK), ret_prompt(pe_p, cos_p, sin_p, nseq, seq, MIX_CHUNK),
         ssd_part, ret_part], grid, "even_mix", _mib(58))
    xp = mm_out([y, o], p["even_w_out"], 0, xp, min(bm, 1024), 512)
    xp, ffn0_p = _ffn(xp, p, prm, 0, zeros_ffn, 1, seq, bm, 256)
    y_s, ssm_s, ssm_conv_s = ssd_sample_post(ssd_ctx, u_ssd, st_ssd, pe_s, prm, nb, ntok)
    o_s = ret_sample_post(ret_oi, u_ret, pe_s, nb, ntok)
    xs = mm_out([y_s, o_s], p["even_w_out"], 0, xs, rows, 512)
    xs, ffn0_s = _ffn(xs, p, prm, 0, _to_token_major(st["ffn_conv"][0])[None], nb, rows, rows)

    po_s = mm_in(xs, p["norm_mix"][1], p["odd_w_in"][0], rows, 512)
    hg_oi, hg_part = hg_sample_pre(po_s, st["hgrn"][0], prm["hg_lb"], nb, ntok, nseq, nc)
    yl_s, lru_s, lru_conv_s = lru_sample(po_s, st["lru"][0], st["lru_conv"][0], prm, nb, ntok)
    po_p = mm_in(xp, p["norm_mix"][1], p["odd_w_in"][0], bm, 512)
    (og, hgrn_p), (yl, lru_p, lru_conv_p), (u_hg, hgrn_s) = _run_parts(
        [hg_prompt(po_p, prm["hg_lb"], prm["hg_nw"], nseq, seq, MIX_CHUNK),
         lru_prompt(po_p, prm, nseq, seq, MIX_CHUNK), hg_part], grid, "odd_mix", _mib(58))
    xp = mm_out([og, yl], p["odd_w_out"], 0, xp, min(bm, 1024), 512)
    xp, ffn1_p = _ffn(xp, p, prm, 1, zeros_ffn, 1, seq, bm, 256)
    og_s = hg_sample_post(hg_oi, u_hg, po_s, prm["hg_nw"], nb, ntok)
    xs = mm_out([og_s, yl_s], p["odd_w_out"], 0, xs, rows, 512)
    xs, ffn1_s = _ffn(xs, p, prm, 1, _to_token_major(st["ffn_conv"][1])[None], nb, rows, rows)

    y_p = rmsnorm(xp, p["norm_final"], min(512, m)).reshape(nseq, seq, D_MODEL)
    y_s = jnp.transpose(rmsnorm(xs, p["norm_final"], rows).reshape(ntok, nb, D_MODEL), (1, 0, 2))
    ffn_s = jnp.stack([_from_token_major(ffn0_s[0], nb), _from_token_major(ffn1_s[0], nb)])
    return (y_p, y_s, ssm_p[None], ssm_s[None], ssm_conv_p[None], ssm_conv_s[None],
            ret_p[None], ret_s[None], hgrn_p[None], hgrn_s[None],
            lru_p.reshape(1, nseq, D_RNN), lru_s[None], lru_conv_p[None], lru_conv_s[None],
            jnp.stack([ffn0_p, ffn1_p]), ffn_s)


def kernel(x_prompt, x_sample, state_ssm, state_ssm_conv, state_ret, state_hgrn, state_lru, state_lru_conv, state_ffn_conv, norm_mix, norm_ffn, norm_final, even_w_in, ssd_conv_w, ssd_conv_b, ssd_dt_bias, ssd_A_log, ssd_D, ssd_norm_w, even_w_out, odd_w_in, hg_lower_bounds, hg_norm_w, lru_conv_w, lru_conv_b, lru_w_r, lru_b_r, lru_w_i, lru_b_i, lru_a_param, odd_w_out, ffn_w_up, ffn_conv_w, ffn_conv_b, ffn_w_down):
    p = {
        "norm_mix": norm_mix, "norm_ffn": norm_ffn, "norm_final": norm_final,
        "even_w_in": even_w_in, "ssd_conv_w": ssd_conv_w, "ssd_conv_b": ssd_conv_b,
        "ssd_dt_bias": ssd_dt_bias, "ssd_A_log": ssd_A_log, "ssd_D": ssd_D,
        "ssd_norm_w": ssd_norm_w, "even_w_out": even_w_out, "odd_w_in": odd_w_in,
        "hg_lower_bounds": hg_lower_bounds, "hg_norm_w": hg_norm_w,
        "lru_conv_w": lru_conv_w, "lru_conv_b": lru_conv_b, "lru_w_r": lru_w_r,
        "lru_b_r": lru_b_r, "lru_w_i": lru_w_i, "lru_b_i": lru_b_i,
        "lru_a_param": lru_a_param, "odd_w_out": odd_w_out, "ffn_w_up": ffn_w_up,
        "ffn_conv_w": ffn_conv_w, "ffn_conv_b": ffn_conv_b, "ffn_w_down": ffn_w_down,
    }
    prm = _prepare(p)
    nseq, seq, _ = x_prompt.shape
    nb, ntok, _ = x_sample.shape
    st = {"ssm": state_ssm, "ssm_conv": state_ssm_conv, "ret": state_ret, "hgrn": state_hgrn,
          "lru": state_lru, "lru_conv": state_lru_conv, "ffn_conv": state_ffn_conv}
    xs_tm = jnp.transpose(x_sample, (1, 0, 2)).reshape(ntok * nb, D_MODEL)
    return _trunks(x_prompt.reshape(nseq * seq, D_MODEL), xs_tm, st, p, prm, nseq, seq, nb, ntok)
```

```python
import functools
import math

import numpy as np
import jax
import jax.numpy as jnp
from jax import lax
from jax.experimental import pallas as pl
from jax.experimental.pallas import tpu as pltpu

F32 = jnp.float32
BF16 = jnp.bfloat16
EPS = 1e-6
LOG2E = math.log2(math.e)

D_MODEL = 2048
PAST_LEN = 16384
SSD_HEADS = 32
SSD_HEAD_DIM = 64
D_SSD = SSD_HEADS * SSD_HEAD_DIM
SSD_GROUPS = 4
SSD_STATE = 128
SSD_CONV = 4
SSD_BC = 2 * SSD_GROUPS * SSD_STATE
RET_HEADS = 8
RET_DK = 128
RET_DV = 256
RET_QK = RET_HEADS * RET_DK
D_RET = RET_HEADS * RET_DV
ROPE_BASE = 10000.0
HG_HEADS = 16
HG_DK = 128
HG_DV = 128
D_HG = HG_HEADS * HG_DV
D_RNN = 2048
LRU_BLOCKS = 8
LRU_BW = D_RNN // LRU_BLOCKS
LRU_CONV = 4
LRU_C = 8.0
D_FF = 5632
FFN_CONV = 3
D_IN_EVEN = D_SSD + (D_SSD + SSD_BC) + SSD_HEADS + 2 * RET_QK + 2 * D_RET
EVEN_PACKED = 11776
DT_COL = 11264

V7X_VMEM_BYTES = 64 * 1024 * 1024
V7X_VMEM_CAP = 60 * 1024 * 1024
LANES = 128
SUBLANES = 8

RET_LOG_GAMMA = [float(v) for v in np.log1p(-np.exp(np.linspace(
    math.log(1.0 / 32.0), math.log(1.0 / 512.0), RET_HEADS, dtype=np.float32))).astype(np.float32)]

NT = (((1,), (1,)), ((), ()))
TN = (((0,), (0,)), ((), ()))


def _cparams(sem, vmem_bytes):
    return pltpu.CompilerParams(dimension_semantics=sem,
                                vmem_limit_bytes=int(min(V7X_VMEM_CAP, vmem_bytes)))


def _mib(n):
    return n * 1024 * 1024


def _silu(x):
    return x * jax.nn.sigmoid(x)


def _softplus(x):
    return jnp.maximum(x, 0.0) + jnp.log1p(jnp.exp(-jnp.abs(x)))


def _gelu_tanh(x):
    return 0.5 * x * (1.0 + jnp.tanh(math.sqrt(2.0 / math.pi) * (x + 0.044715 * (x * x * x))))


def _rms(x):
    return x * lax.rsqrt(jnp.mean(x * x, axis=-1, keepdims=True) + EPS)


def _dot(a, b):
    return jnp.dot(a, b, preferred_element_type=F32)


def _tril_ones(n):
    r = lax.broadcasted_iota(jnp.int32, (n, n), 0)
    c = lax.broadcasted_iota(jnp.int32, (n, n), 1)
    return (r >= c).astype(F32)


def _cumsum_rows(x):
    return jnp.dot(_tril_ones(x.shape[0]), x, precision=lax.Precision.HIGHEST,
                   preferred_element_type=F32)


def _shifted_rows(buf_ref, start, rows, back):
    if back % SUBLANES == 0:
        return buf_ref[start - back:start - back + rows, :]
    assert back < SUBLANES and start % SUBLANES == 0
    ext = buf_ref[start - SUBLANES:start + rows, :]
    return pltpu.roll(ext, back, 0)[SUBLANES:, :]


def _causal_conv(buf_ref, x, w_ref, b_ref, shift, width):
    rows = x.shape[0]
    off = buf_ref.shape[0] - rows
    n = (width - 1) * shift
    buf_ref[off:off + rows, :] = x
    acc = None
    for j in range(width):
        term = _shifted_rows(buf_ref, off, rows, (width - 1 - j) * shift) * w_ref[j:j + 1, :]
        acc = term if acc is None else acc + term
    out = b_ref[...] + acc
    tail = buf_ref[off + rows - n:off + rows, :]
    buf_ref[off - n:off, :] = tail
    return out, tail


def _conv_off(shift, width):
    n = (width - 1) * shift
    return -(-n // SUBLANES) * SUBLANES


NORM_ROWS = 256


def _norm_rows_to(x_ref, nw_ref, xn_ref):
    rows = x_ref.shape[0]
    step = min(NORM_ROWS, rows)

    def body(i, carry):
        r = pl.multiple_of(i * step, step)
        xn_ref[pl.ds(r, step), :] = (_rms(x_ref[pl.ds(r, step), :]) * nw_ref[...]).astype(BF16)
        return carry

    lax.fori_loop(0, rows // step, body, 0)


def _mm_in_kernel(x_ref, nw_ref, w_ref, o_ref, xn_ref):
    @pl.when(pl.program_id(1) == 0)
    def _():
        _norm_rows_to(x_ref, nw_ref, xn_ref)
    o_ref[...] = _dot(xn_ref[...], w_ref[...].astype(BF16))


def _mm_in_vmem(bm, d, bn):
    return (bm * d * 4 + bm * d * 2 + 2 * d * bn * 4 + d * bn * 2 + 2 * bm * bn * 4
            + 4 * NORM_ROWS * d * 4 + _mib(6))


def mm_in(x, nw, w, bm, bn):
    m, d = x.shape
    n = w.shape[1]
    vmem = _mm_in_vmem(bm, d, bn)
    return pl.pallas_call(
        _mm_in_kernel,
        grid=(m // bm, n // bn),
        in_specs=[pl.BlockSpec((bm, d), lambda i, j: (i, 0), pipeline_mode=pl.Buffered(1)),
                  pl.BlockSpec((1, d), lambda i, j: (0, 0)),
                  pl.BlockSpec((d, bn), lambda i, j: (0, j))],
        out_specs=pl.BlockSpec((bm, bn), lambda i, j: (i, j)),
        out_shape=jax.ShapeDtypeStruct((m, n), F32),
        scratch_shapes=[pltpu.VMEM((bm, d), BF16)],
        compiler_params=_cparams(("parallel", "arbitrary"), vmem),
        name="mm_in",
    )(x, nw.reshape(1, d), w)


def _mm_in_t_kernel(offs_ref, x_ref, nw_ref, wt_ref, o_ref, xn_ref):
    del offs_ref
    @pl.when(pl.program_id(1) == 0)
    def _():
        _norm_rows_to(x_ref, nw_ref, xn_ref)
    o_ref[...] = lax.dot_general(xn_ref[...], wt_ref[...].astype(BF16), NT, preferred_element_type=F32)


ROW_ALIGN = 32


def mm_in_t(x, nw, wt, row_offsets, bm, bn):
    m, d = x.shape
    nblk = len(row_offsets)
    assert all(o % ROW_ALIGN == 0 for o in row_offsets)
    vmem = _mm_in_vmem(bm, d, bn)
    grid_spec = pltpu.PrefetchScalarGridSpec(
        num_scalar_prefetch=1,
        grid=(m // bm, nblk),
        in_specs=[pl.BlockSpec((bm, d), lambda i, j, offs: (i, 0), pipeline_mode=pl.Buffered(1)),
                  pl.BlockSpec((1, d), lambda i, j, offs: (0, 0)),
                  pl.BlockSpec((pl.Element(bn), pl.Element(d)),
                               lambda i, j, offs: (offs[j] * ROW_ALIGN, 0))],
        out_specs=pl.BlockSpec((bm, bn), lambda i, j, offs: (i, j)),
        scratch_shapes=[pltpu.VMEM((bm, d), BF16)])
    return pl.pallas_call(
        _mm_in_t_kernel,
        grid_spec=grid_spec,
        out_shape=jax.ShapeDtypeStruct((m, nblk * bn), F32),
        compiler_params=_cparams(("parallel", "arbitrary"), vmem),
        name="mm_in_t",
    )(jnp.asarray([o // ROW_ALIGN for o in row_offsets], jnp.int32), x, nw.reshape(1, d), wt)


def _mm_out_kernel(*refs, nparts):
    a_refs = refs[:nparts]
    w_refs = refs[nparts:2 * nparts]
    r_ref = refs[2 * nparts]
    o_ref = refs[2 * nparts + 1]
    acc = r_ref[...]
    for a_ref, w_ref in zip(a_refs, w_refs):
        acc = acc + _dot(a_ref[...], w_ref[...].astype(BF16))
    o_ref[...] = acc


def mm_out(parts, w, layer, resid, bm, bn):
    nparts = len(parts)
    m, kp = parts[0].shape
    n = w.shape[2]
    vmem = nparts * (2 * bm * kp * 2 + 2 * kp * bn * 4 + kp * bn * 2) + 4 * bm * bn * 4 + 2 * bm * bn * 4 + _mib(6)
    in_specs = [pl.BlockSpec((bm, kp), lambda i, j: (i, 0)) for _ in range(nparts)]
    in_specs += [pl.BlockSpec((None, kp, bn), functools.partial(lambda i, j, p: (layer, p, j), p=p))
                 for p in range(nparts)]
    in_specs += [pl.BlockSpec((bm, bn), lambda i, j: (i, j))]
    return pl.pallas_call(
        functools.partial(_mm_out_kernel, nparts=nparts),
        grid=(m // bm, n // bn),
        in_specs=in_specs,
        out_specs=pl.BlockSpec((bm, bn), lambda i, j: (i, j)),
        out_shape=jax.ShapeDtypeStruct((m, n), F32),
        compiler_params=_cparams(("parallel", "parallel"), vmem),
        name="mm_out",
    )(*parts, *([w] * nparts), resid)


def _rmsnorm_kernel(x_ref, nw_ref, o_ref):
    o_ref[...] = _rms(x_ref[...]) * nw_ref[...]


def rmsnorm(x, nw, bm):
    m, d = x.shape
    return pl.pallas_call(
        _rmsnorm_kernel,
        grid=(m // bm,),
        in_specs=[pl.BlockSpec((bm, d), lambda i: (i, 0)), pl.BlockSpec((1, d), lambda i: (0, 0))],
        out_specs=pl.BlockSpec((bm, d), lambda i: (i, 0)),
        out_shape=jax.ShapeDtypeStruct((m, d), F32),
        compiler_params=_cparams(("parallel",), 6 * bm * d * 4 + _mib(4)),
        name="rmsnorm",
    )(x, nw.reshape(1, d))


FFN_SLAB = 512


def _ffn_up_kernel(x_ref, nw_ref, wg_ref, wu_ref, cw_ref, cb_ref, prev_ref,
                   act_ref, st_ref, xn_ref, gbuf_ref, carry_ref, *, shift, blocks_per_group):
    i = pl.program_id(0)
    j = pl.program_id(1)
    bm = x_ref.shape[0]
    n = (FFN_CONV - 1) * shift
    off = gbuf_ref.shape[0] - bm

    @pl.when(j == 0)
    def _():
        _norm_rows_to(x_ref, nw_ref, xn_ref)

    first = (i % blocks_per_group) == 0

    @pl.when(first)
    def _():
        gbuf_ref[off - n:off, :] = prev_ref[0]

    @pl.when(jnp.logical_not(first))
    def _():
        gbuf_ref[off - n:off, :] = carry_ref[j]

    wg = wg_ref[...].astype(BF16)
    wu = wu_ref[...].astype(BF16)
    slab = min(FFN_SLAB, bm)
    for s in range(bm // slab):
        r0 = s * slab
        xs = xn_ref[r0:r0 + slab, :]
        gbuf_ref[off + r0:off + r0 + slab, :] = _dot(xs, wg)
        u = _dot(xs, wu)
        gc = cb_ref[...]
        for t in range(FFN_CONV):
            back = (FFN_CONV - 1 - t) * shift
            gc = gc + _shifted_rows(gbuf_ref, off + r0, slab, back) * cw_ref[t:t + 1, :]
        act_ref[r0:r0 + slab, :] = (_silu(gc) * u).astype(BF16)
    tail = gbuf_ref[off + bm - n:off + bm, :]
    carry_ref[j] = tail
    st_ref[0] = tail


def ffn_up(x, nw, w_up, layer, cw, cb, prev, shift, group_rows, bm, bf):
    m, d = x.shape
    f = cw.shape[1]
    n = (FFN_CONV - 1) * shift
    off = _conv_off(shift, FFN_CONV)
    bpg = group_rows // bm
    nf = f // bf
    vmem = (bm * d * 4 + bm * d * 2 + 4 * d * bf * 4 + 2 * d * bf * 2 + 2 * bm * bf * 2
            + (off + bm) * bf * 4 + nf * max(n, SUBLANES) * bf * 4 + 4 * n * bf * 4 + 5 * bm * bf * 4
            + 4 * NORM_ROWS * d * 4 + _mib(6))
    act, st = pl.pallas_call(
        functools.partial(_ffn_up_kernel, shift=shift, blocks_per_group=bpg),
        grid=(m // bm, nf),
        in_specs=[pl.BlockSpec((bm, d), lambda i, j: (i, 0), pipeline_mode=pl.Buffered(1)),
                  pl.BlockSpec((1, d), lambda i, j: (0, 0)),
                  pl.BlockSpec((None, d, bf), lambda i, j: (layer, 0, j)),
                  pl.BlockSpec((None, d, bf), lambda i, j: (layer, 0, j + nf)),
                  pl.BlockSpec((FFN_CONV, bf), lambda i, j: (0, j)),
                  pl.BlockSpec((1, bf), lambda i, j: (0, j)),
                  pl.BlockSpec((1, n, bf), lambda i, j: (i // bpg, 0, j))],
        out_specs=[pl.BlockSpec((bm, bf), lambda i, j: (i, j)),
                   pl.BlockSpec((1, n, bf), lambda i, j: (i, 0, j))],
        out_shape=[jax.ShapeDtypeStruct((m, f), BF16),
                   jax.ShapeDtypeStruct((m // bm, n, f), F32)],
        scratch_shapes=[pltpu.VMEM((bm, d), BF16),
                        pltpu.VMEM((off + bm, bf), F32),
                        pltpu.VMEM((nf, n, bf), F32)],
        compiler_params=_cparams(("arbitrary", "arbitrary"), vmem),
        name="ffn_up",
    )(x, nw.reshape(1, d), w_up, w_up, cw, cb.reshape(1, f), prev)
    return act, st[bpg - 1::bpg]


def _expand_heads64(x, nheads):
    r = x.shape[0]
    lo = lax.broadcasted_iota(jnp.int32, (r, LANES), 1) < SSD_HEAD_DIM
    tiles = []
    for p in range(nheads // 2):
        a0 = jnp.broadcast_to(x[:, 2 * p:2 * p + 1], (r, LANES))
        a1 = jnp.broadcast_to(x[:, 2 * p + 1:2 * p + 2], (r, LANES))
        tiles.append(jnp.where(lo, a0, a1))
    return jnp.concatenate(tiles, axis=1)


def _ssd_gates(xs_c, bc_c, dt_raw, dtb, alog):
    xs = _silu(xs_c)
    bc = _silu(bc_c)
    dt = _softplus(dt_raw + dtb)
    la = dt * (-jnp.exp(alog))
    return xs, bc, dt, la


def _ssd_finish(o, xs, z, dx, nw):
    y = (o + dx * xs) * _silu(z)
    gw = D_SSD // SSD_GROUPS
    y = jnp.concatenate([_rms(y[:, g * gw:(g + 1) * gw]) for g in range(SSD_GROUPS)], axis=1)
    return y * nw


def _ssd_prompt_kernel(z_ref, xs_ref, bc_ref, dt_ref, cwx_ref, cwb_ref, cbx_ref, cbb_ref,
                       dtb_ref, alog_ref, dx_ref, nw_ref,
                       y_ref, sfin_ref, cfin_ref, s_ref, bufx_ref, bufb_ref):
    c = pl.program_id(1)
    nc = pl.num_programs(1)
    rows = xs_ref.shape[0]
    gw = D_SSD // SSD_GROUPS
    hpg = SSD_HEADS // SSD_GROUPS

    @pl.when(c == 0)
    def _():
        s_ref[...] = jnp.zeros_like(s_ref)
        bufx_ref[0:SUBLANES, :] = jnp.zeros((SUBLANES, D_SSD), F32)
        bufb_ref[0:SUBLANES, :] = jnp.zeros((SUBLANES, SSD_BC), F32)

    yield
    xs_c, tailx = _causal_conv(bufx_ref, xs_ref[...], cwx_ref, cbx_ref, 1, SSD_CONV)
    bc_c, tailb = _causal_conv(bufb_ref, bc_ref[...], cwb_ref, cbb_ref, 1, SSD_CONV)
    xs, bc, dt, la = _ssd_gates(xs_c, bc_c, dt_ref[...], dtb_ref[...], alog_ref[...])
    cum = _cumsum_rows(la * LOG2E)
    cum_t = cum.T
    cumx = _expand_heads64(cum, SSD_HEADS)
    dtx = _expand_heads64(dt, SSD_HEADS)
    lastx = cumx[rows - 1:rows, :]
    ecum = jnp.exp2(cumx)
    wx = jnp.exp2(lastx - cumx)
    elast = jnp.exp2(lastx)
    v_all = xs * dtx
    vw_all = v_all * wx

    ri = lax.broadcasted_iota(jnp.int32, (rows, rows), 0)
    ci = lax.broadcasted_iota(jnp.int32, (rows, rows), 1)
    causal = ri >= ci
    lo = lax.broadcasted_iota(jnp.int32, (rows, LANES), 1) < SSD_HEAD_DIM

    o_groups = []
    for g in range(SSD_GROUPS):
        kb = bc[:, g * SSD_STATE:(g + 1) * SSD_STATE].astype(BF16)
        qb = bc[:, (SSD_GROUPS + g) * SSD_STATE:(SSD_GROUPS + g + 1) * SSD_STATE].astype(BF16)
        qk = lax.dot_general(qb, kb, NT, preferred_element_type=F32)
        o_tiles = []
        for p in range(hpg // 2):
            pms, vhs = [], []
            for q in range(2):
                h = g * hpg + 2 * p + q
                diff = cum[:, h:h + 1] - cum_t[h:h + 1, :]
                dec = jnp.exp2(jnp.where(causal, diff, -1e30))
                pms.append((qk * dec).astype(BF16))
                col = g * gw + p * LANES
                vp = v_all[:, col:col + LANES]
                vh = jnp.where(lo, vp, 0.0) if q == 0 else jnp.where(lo, 0.0, vp)
                vhs.append(vh.astype(BF16))
            o_tiles.append(_dot(jnp.concatenate(pms, axis=1), jnp.concatenate(vhs, axis=0)))
        o_intra = jnp.concatenate(o_tiles, axis=1)
        sg = s_ref[g]
        sl = slice(g * gw, (g + 1) * gw)
        o_inter = _dot(qb, sg.astype(BF16)) * ecum[:, sl]
        upd = lax.dot_general(kb, vw_all[:, sl].astype(BF16), TN, preferred_element_type=F32)
        s_ref[g] = sg * elast[:, sl] + upd
        o_groups.append(o_intra + o_inter)
    o = jnp.concatenate(o_groups, axis=1)
    y_ref[...] = _ssd_finish(o, xs, z_ref[...], dx_ref[...], nw_ref[...]).astype(BF16)

    yield
    @pl.when(c == nc - 1)
    def _():
        for h in range(SSD_HEADS):
            g, hl = divmod(h, hpg)
            sfin_ref[0, h] = s_ref[g, :, hl * SSD_HEAD_DIM:(hl + 1) * SSD_HEAD_DIM]
        cfin_ref[0, :, 0:D_SSD] = tailx
        cfin_ref[0, :, D_SSD:D_SSD + SSD_BC] = tailb


def ssd_prompt(pe, prm, nseq, seq, chunk):
    m = nseq * seq
    nc = seq // chunk
    row = lambda b, c: b * nc + c
    full = lambda shp: pl.BlockSpec(shp, lambda b, c: (0,) * len(shp))
    in_specs = [
        pl.BlockSpec((chunk, D_SSD), lambda b, c: (row(b, c), 0)),
        pl.BlockSpec((chunk, D_SSD), lambda b, c: (row(b, c), 1)),
        pl.BlockSpec((chunk, SSD_BC), lambda b, c: (row(b, c), 10)),
        pl.BlockSpec((chunk, LANES), lambda b, c: (row(b, c), DT_COL // LANES)),
        pl.BlockSpec((SSD_CONV, D_SSD), lambda b, c: (0, 0)),
        pl.BlockSpec((SSD_CONV, SSD_BC), lambda b, c: (0, 2)),
        pl.BlockSpec((1, D_SSD), lambda b, c: (0, 0)),
        pl.BlockSpec((1, SSD_BC), lambda b, c: (0, 2)),
        full((1, LANES)), full((1, LANES)), full((1, D_SSD)), full((1, D_SSD)),
    ]
    return dict(
        kernel=_ssd_prompt_kernel,
        arrays=[pe, pe, pe, pe, prm["ssd_cw"], prm["ssd_cw"], prm["ssd_cb"], prm["ssd_cb"],
                prm["ssd_dtb"], prm["ssd_alog"], prm["ssd_dx"], prm["ssd_nw"]],
        in_specs=in_specs,
        out_specs=[pl.BlockSpec((chunk, D_SSD), lambda b, c: (row(b, c), 0)),
                   pl.BlockSpec((1, SSD_HEADS, SSD_STATE, SSD_HEAD_DIM), lambda b, c: (b, 0, 0, 0)),
                   pl.BlockSpec((1, SSD_CONV - 1, D_SSD + SSD_BC), lambda b, c: (b, 0, 0))],
        out_shape=[jax.ShapeDtypeStruct((m, D_SSD), BF16),
                   jax.ShapeDtypeStruct((nseq, SSD_HEADS, SSD_STATE, SSD_HEAD_DIM), F32),
                   jax.ShapeDtypeStruct((nseq, SSD_CONV - 1, D_SSD + SSD_BC), F32)],
        scratch=[pltpu.VMEM((SSD_GROUPS, SSD_STATE, D_SSD // SSD_GROUPS), F32),
                 pltpu.VMEM((SUBLANES + chunk, D_SSD), F32),
                 pltpu.VMEM((SUBLANES + chunk, SSD_BC), F32)])


MIX_CHUNK = 128


def _run_parts(parts, grid, name, vmem):
    counts = [(len(p["arrays"]), len(p["out_shape"]), len(p["scratch"])) for p in parts]
    n_in = sum(c[0] for c in counts)
    n_out = sum(c[1] for c in counts)

    def body(*refs):
        i, o, s = 0, n_in, n_in + n_out
        gens = []
        for p, (a, b, c) in zip(parts, counts):
            gens.append(p["kernel"](*refs[i:i + a], *refs[o:o + b], *refs[s:s + c]))
            i, o, s = i + a, o + b, s + c
        for _ in range(3):
            for g in gens:
                next(g, None)

    outs = pl.pallas_call(
        body,
        grid=grid,
        in_specs=[sp for p in parts for sp in p["in_specs"]],
        out_specs=[sp for p in parts for sp in p["out_specs"]],
        out_shape=[sh for p in parts for sh in p["out_shape"]],
        scratch_shapes=[sc for p in parts for sc in p["scratch"]],
        compiler_params=_cparams(("parallel", "arbitrary"), vmem),
        name=name,
    )(*[a for p in parts for a in p["arrays"]])
    res, k = [], 0
    for _, b, _ in counts:
        res.append(outs[k:k + b])
        k += b
    return res


def _rotary(x, cos, sin_signed):
    return x * cos + pltpu.roll(x, RET_DK // 2, 1) * sin_signed


def _ret_prompt_kernel(q_ref, k_ref, v_ref, g_ref, cos_ref, sin_ref, o_ref, sfin_ref, s_ref):
    c = pl.program_id(1)
    nc = pl.num_programs(1)
    rows = q_ref.shape[0]

    @pl.when(c == 0)
    def _():
        s_ref[...] = jnp.zeros_like(s_ref)

    yield
    cos = cos_ref[...]
    sin = sin_ref[...]
    ri = lax.broadcasted_iota(jnp.int32, (rows, rows), 0)
    ci = lax.broadcasted_iota(jnp.int32, (rows, rows), 1)
    dij = jnp.where(ri >= ci, (ri - ci).astype(F32), 1e30)
    tk = lax.broadcasted_iota(jnp.int32, (rows, RET_DK), 0).astype(F32)
    tv = lax.broadcasted_iota(jnp.int32, (rows, RET_DV), 0).astype(F32)
    for h in range(RET_HEADS):
        lg = RET_LOG_GAMMA[h]
        lg2 = lg * LOG2E
        qr = _rotary(q_ref[:, h * RET_DK:(h + 1) * RET_DK], cos, sin)
        kr = _rotary(k_ref[:, h * RET_DK:(h + 1) * RET_DK], cos, sin) * (RET_DK ** -0.5)
        qb = qr.astype(BF16)
        s = lax.dot_general(qb, kr.astype(BF16), NT, preferred_element_type=F32)
        pm = (s * jnp.exp2(lg2 * dij)).astype(BF16)
        vb = v_ref[:, h * RET_DV:(h + 1) * RET_DV].astype(BF16)
        sh = s_ref[h]
        o = _dot(pm, vb) + _dot(qb, sh.astype(BF16)) * jnp.exp2(lg2 * (tv + 1.0))
        kw = (kr * jnp.exp2(lg2 * ((rows - 1.0) - tk))).astype(BF16)
        s_ref[h] = sh * math.exp(lg * rows) + lax.dot_general(kw, vb, TN, preferred_element_type=F32)
        gh = g_ref[:, h * RET_DV:(h + 1) * RET_DV]
        o_ref[:, h * RET_DV:(h + 1) * RET_DV] = (_rms(o) * _silu(gh)).astype(BF16)

    yield
    @pl.when(c == nc - 1)
    def _():
        sfin_ref[0] = s_ref[...]


def ret_prompt(pe, cos, sin, nseq, seq, chunk):
    m = nseq * seq
    nc = seq // chunk
    row = lambda b, c: b * nc + c
    return dict(
        kernel=_ret_prompt_kernel,
        arrays=[pe, pe, pe, pe, cos, sin],
        in_specs=[pl.BlockSpec((chunk, RET_QK), lambda b, c: (row(b, c), 8)),
                  pl.BlockSpec((chunk, RET_QK), lambda b, c: (row(b, c), 9)),
                  pl.BlockSpec((chunk, D_RET), lambda b, c: (row(b, c), 2)),
                  pl.BlockSpec((chunk, D_RET), lambda b, c: (row(b, c), 3)),
                  pl.BlockSpec((chunk, RET_DK), lambda b, c: (c, 0)),
                  pl.BlockSpec((chunk, RET_DK), lambda b, c: (c, 0))],
        out_specs=[pl.BlockSpec((chunk, D_RET), lambda b, c: (row(b, c), 0)),
                   pl.BlockSpec((1, RET_HEADS, RET_DK, RET_DV), lambda b, c: (b, 0, 0, 0))],
        out_shape=[jax.ShapeDtypeStruct((m, D_RET), BF16),
                   jax.ShapeDtypeStruct((nseq, RET_HEADS, RET_DK, RET_DV), F32)],
        scratch=[pltpu.VMEM((RET_HEADS, RET_DK, RET_DV), F32)])


HG_HB = HG_HEADS


def _hg_gates(hq, hf, lb):
    q = _silu(hq)
    f = lb + (1.0 - lb) * jax.nn.sigmoid(hf)
    return q, f, 1.0 - f, jnp.log(f)


def _hg_tables(rows):
    r = np.arange(rows)[:, None]
    t = np.arange(rows)[None, :]
    sums = [t <= r]
    masks = []
    s = rows // 2
    while s >= 1:
        blk, pos = r // (2 * s), r % (2 * s)
        ref = blk * 2 * s + s - 1
        upper = pos >= s
        sums.append(np.where(upper, (t > ref) & (t <= r), (t > r) & (t <= ref)))
        masks.append((blk == t // (2 * s)) & upper & (t % (2 * s) < s))
        s //= 2
    masks.append(r == t)
    return (np.concatenate(sums, axis=0).astype(np.float32),
            np.stack(masks).astype(np.float32))


def _hg_prompt_kernel(hq_ref, hf_ref, hi_ref, hg_ref, lb_ref, nw_ref, sums_ref, masks_ref,
                      og_ref, sfin_ref, st_ref):
    c = pl.program_id(1)
    nc = pl.num_programs(1)
    rows = hq_ref.shape[0]
    nlev = masks_ref.shape[0] - 1

    @pl.when(c == 0)
    def _():
        st_ref[...] = jnp.zeros_like(st_ref)

    yield
    q_all, _, kk_all, lg_all = _hg_gates(hq_ref[...], hf_ref[...], lb_ref[...])
    lg_all = lg_all * LOG2E
    lg_hi = lg_all.astype(BF16)
    lg_lo = (lg_all - lg_hi.astype(F32)).astype(BF16)
    dall = _dot(sums_ref[...], jnp.concatenate([lg_hi, lg_lo], axis=0))
    cum_all = dall[0:rows]
    ecum_all = jnp.exp2(cum_all)
    for h in range(HG_HB):
        sl = slice(h * HG_DK, (h + 1) * HG_DK)
        q = q_all[:, sl]
        kk = kk_all[:, sl]
        cum = cum_all[:, sl]
        vb = hi_ref[:, sl].astype(BF16)
        last = cum[rows - 1:rows, :]
        st = st_ref[h]
        o = lax.dot_general((q * ecum_all[:, sl]).astype(BF16), st.astype(BF16), NT,
                            preferred_element_type=F32)
        kt = (kk * jnp.exp2(last - cum)).astype(BF16)
        st_ref[h] = st * jnp.exp2(last) + lax.dot_general(vb, kt, TN, preferred_element_type=F32)
        qb = q.astype(BF16)
        kb = kk.astype(BF16)
        a = masks_ref[nlev] * lax.dot_general(qb, kb, NT, preferred_element_type=F32)
        for l in range(nlev):
            e = jnp.exp2(dall[(l + 1) * rows:(l + 2) * rows, sl]).astype(BF16)
            a = a + masks_ref[l] * lax.dot_general(qb * e, kb * e, NT, preferred_element_type=F32)
        o = o + _dot(a.astype(BF16), vb)
        og_ref[:, sl] = (_rms(o) * nw_ref[...] * _silu(hg_ref[:, sl])).astype(BF16)

    yield
    @pl.when(c == nc - 1)
    def _():
        for h in range(HG_HB):
            sfin_ref[0, h] = st_ref[h].T


def hg_prompt(po, lb, nw, nseq, seq, chunk):
    m = nseq * seq
    nc = seq // chunk
    row = lambda b, c: b * nc + c
    sums_np, masks_np = _hg_tables(chunk)
    sums = jnp.asarray(np.concatenate([sums_np, sums_np], axis=1), BF16)
    masks = jnp.asarray(masks_np, F32)
    return dict(
        kernel=_hg_prompt_kernel,
        arrays=[po, po, po, po, lb, nw, sums, masks],
        in_specs=[pl.BlockSpec((chunk, D_HG), lambda b, c: (row(b, c), 0)),
                  pl.BlockSpec((chunk, D_HG), lambda b, c: (row(b, c), 1)),
                  pl.BlockSpec((chunk, D_HG), lambda b, c: (row(b, c), 2)),
                  pl.BlockSpec((chunk, D_HG), lambda b, c: (row(b, c), 3)),
                  pl.BlockSpec((1, D_HG), lambda b, c: (0, 0)),
                  pl.BlockSpec((1, HG_DV), lambda b, c: (0, 0)),
                  pl.BlockSpec(sums.shape, lambda b, c: (0, 0)),
                  pl.BlockSpec(masks.shape, lambda b, c: (0, 0, 0))],
        out_specs=[pl.BlockSpec((chunk, D_HG), lambda b, c: (row(b, c), 0)),
                   pl.BlockSpec((1, HG_HEADS, HG_DK, HG_DV), lambda b, c: (b, 0, 0, 0))],
        out_shape=[jax.ShapeDtypeStruct((m, D_HG), BF16),
                   jax.ShapeDtypeStruct((nseq, HG_HEADS, HG_DK, HG_DV), F32)],
        scratch=[pltpu.VMEM((HG_HEADS, HG_DV, HG_DK), F32)])


def _lin_scan(a, b, h0):
    rows, cols = a.shape
    groups = rows // SUBLANES
    a3 = a.reshape(groups, SUBLANES, cols)
    b3 = b.reshape(groups, SUBLANES, cols)
    si = lax.broadcasted_iota(jnp.int32, a3.shape, 1)
    d = 1
    while d < SUBLANES:
        keep = si >= d
        a_s = jnp.where(keep, pltpu.roll(a3, d, 1), 1.0)
        b_s = jnp.where(keep, pltpu.roll(b3, d, 1), 0.0)
        b3 = a3 * b_s + b3
        a3 = a3 * a_s
        d *= 2
    h_in = jnp.broadcast_to(h0, (SUBLANES, cols))
    out = []
    for g in range(groups):
        hg = a3[g] * h_in + b3[g]
        out.append(hg)
        h_in = jnp.broadcast_to(hg[SUBLANES - 1:SUBLANES, :], (SUBLANES, cols))
    return jnp.concatenate(out, axis=0), h_in[0:1, :]


def _lru_gates(xr, wr_ref, br, wi_ref, bi, ap):
    xb = xr.astype(BF16)
    r_parts, i_parts = [], []
    for n in range(LRU_BLOCKS):
        xn = xb[:, n * LRU_BW:(n + 1) * LRU_BW]
        r_parts.append(_dot(xn, wr_ref[n].astype(BF16)))
        i_parts.append(_dot(xn, wi_ref[n].astype(BF16)))
    r = jax.nn.sigmoid(jnp.concatenate(r_parts, axis=1) + br)
    gi = jax.nn.sigmoid(jnp.concatenate(i_parts, axis=1) + bi)
    la = -LRU_C * r * _softplus(-ap)
    a = jnp.exp(la)
    th = jnp.tanh(la)
    mult = jnp.sqrt(-2.0 * th / (1.0 - th))
    return a, mult, gi


def _lru_prompt_kernel(rx_ref, rg_ref, cw_ref, cb_ref, wr_ref, br_ref, wi_ref, bi_ref, ap_ref,
                       yl_ref, hfin_ref, cfin_ref, buf_ref, hc_ref):
    c = pl.program_id(1)
    nc = pl.num_programs(1)
    rows = rx_ref.shape[0]

    @pl.when(c == 0)
    def _():
        buf_ref[0:SUBLANES, :] = jnp.zeros((SUBLANES, D_RNN), F32)
        hc_ref[...] = jnp.zeros_like(hc_ref)

    yield
    xr, tail = _causal_conv(buf_ref, rx_ref[...], cw_ref, cb_ref, 1, LRU_CONV)
    a, mult, gi = _lru_gates(xr, wr_ref, br_ref[...], wi_ref, bi_ref[...], ap_ref[...])
    ri = lax.broadcasted_iota(jnp.int32, (rows, D_RNN), 0)
    mult = jnp.where(jnp.logical_and(c == 0, ri == 0), 1.0, mult)
    hs, hlast = _lin_scan(a, mult * gi * xr, hc_ref[0:1, :])
    hc_ref[0:1, :] = hlast
    yl_ref[...] = (hs * _gelu_tanh(rg_ref[...])).astype(BF16)

    yield
    @pl.when(c == nc - 1)
    def _():
        hfin_ref[0] = hlast
        cfin_ref[0] = tail


def lru_prompt(po, prm, nseq, seq, chunk):
    m = nseq * seq
    nc = seq // chunk
    row = lambda b, c: b * nc + c
    full = lambda shp: pl.BlockSpec(shp, lambda b, c: (0,) * len(shp))
    return dict(
        kernel=_lru_prompt_kernel,
        arrays=[po, po, prm["lru_cw"], prm["lru_cb"], prm["lru_wr"], prm["lru_br"], prm["lru_wi"],
                prm["lru_bi"], prm["lru_ap"]],
        in_specs=[pl.BlockSpec((chunk, D_RNN), lambda b, c: (row(b, c), 4)),
                  pl.BlockSpec((chunk, D_RNN), lambda b, c: (row(b, c), 5)),
                  full((LRU_CONV, D_RNN)), full((1, D_RNN)),
                  full((LRU_BLOCKS, LRU_BW, LRU_BW)), full((1, D_RNN)),
                  full((LRU_BLOCKS, LRU_BW, LRU_BW)), full((1, D_RNN)), full((1, D_RNN))],
        out_specs=[pl.BlockSpec((chunk, D_RNN), lambda b, c: (row(b, c), 0)),
                   pl.BlockSpec((1, 1, D_RNN), lambda b, c: (b, 0, 0)),
                   pl.BlockSpec((1, LRU_CONV - 1, D_RNN), lambda b, c: (b, 0, 0))],
        out_shape=[jax.ShapeDtypeStruct((m, D_RNN), BF16),
                   jax.ShapeDtypeStruct((nseq, 1, D_RNN), F32),
                   jax.ShapeDtypeStruct((nseq, LRU_CONV - 1, D_RNN), F32)],
        scratch=[pltpu.VMEM((SUBLANES + chunk, D_RNN), F32),
                 pltpu.VMEM((SUBLANES, D_RNN), F32)])


TOKP = SUBLANES


def _ssd_state_kernel(q_ref, k_ref, vw_ref, dec_ref, st_ref, u_ref, snew_ref):
    hpg = SSD_HEADS // SSD_GROUPS
    gw = hpg * SSD_HEAD_DIM

    def body(b, carry):
        q = q_ref[b].astype(BF16)
        k = k_ref[b].astype(BF16)
        vw = vw_ref[b].astype(BF16)
        for g in range(SSD_GROUPS):
            ks = slice(g * SSD_STATE, (g + 1) * SSD_STATE)
            stg = st_ref[b, g * hpg:(g + 1) * hpg].reshape(gw, SSD_STATE)
            u_ref[b, :, g * gw:(g + 1) * gw] = lax.dot_general(
                q[:, ks], stg.astype(BF16), NT, preferred_element_type=F32)
            upd = lax.dot_general(vw[:, g * gw:(g + 1) * gw], k[:, ks], TN, preferred_element_type=F32)
            for hl in range(hpg):
                h = g * hpg + hl
                snew_ref[b, h] = (st_ref[b, h] * dec_ref[b, h:h + 1, :]
                                  + upd[hl * SSD_HEAD_DIM:(hl + 1) * SSD_HEAD_DIM, :])
        return carry

    lax.fori_loop(0, st_ref.shape[0], body, 0)


def _ret_state_kernel(q_ref, k_ref, v_ref, s_ref, u_ref, snew_ref, *, ntok):
    def body(b, carry):
        q = q_ref[b].astype(BF16)
        k = k_ref[b].astype(BF16)
        v = v_ref[b].astype(BF16)
        for h in range(RET_HEADS):
            ks = slice(h * RET_DK, (h + 1) * RET_DK)
            vs = slice(h * RET_DV, (h + 1) * RET_DV)
            s0 = s_ref[b, h]
            u_ref[b, :, vs] = _dot(q[:, ks], s0.astype(BF16))
            upd = lax.dot_general(k[:, ks], v[:, vs], TN, preferred_element_type=F32)
            snew_ref[b, h] = s0 * math.exp(RET_LOG_GAMMA[h] * ntok) + upd
        return carry

    lax.fori_loop(0, s_ref.shape[0], body, 0)


def _hg_state_kernel(q_ref, k_ref, v_ref, dcol_ref, s_ref, u_ref, snew_ref):
    def body(b, carry):
        q = q_ref[b].astype(BF16)
        k = k_ref[b].astype(BF16)
        v = v_ref[b].astype(BF16)
        dc = dcol_ref[b]
        for h in range(HG_HEADS):
            ks = slice(h * HG_DK, (h + 1) * HG_DK)
            s0 = s_ref[b, h]
            u_ref[b, :, ks] = _dot(q[:, ks], s0.astype(BF16))
            upd = lax.dot_general(k[:, ks], v[:, ks], TN, preferred_element_type=F32)
            snew_ref[b, h] = s0 * dc[:, h:h + 1] + upd
        return carry

    lax.fori_loop(0, s_ref.shape[0], body, 0)


def _state_part(kern, rows_in, extra, s, u_cols, nseq, nc):
    nb = s.shape[0]
    bb = nb // (nseq * nc)
    assert bb * nseq * nc == nb
    blk = lambda a: pl.BlockSpec((bb,) + a.shape[1:],
                                 lambda b, c: (b * nc + c,) + (0,) * (a.ndim - 1))
    ins = list(rows_in) + ([extra] if extra is not None else []) + [s]

    def phases(*refs):
        yield
        kern(*refs)
        yield

    return dict(kernel=phases, arrays=ins, in_specs=[blk(a) for a in ins],
                out_specs=[pl.BlockSpec((bb, TOKP, u_cols), lambda b, c: (b * nc + c, 0, 0)), blk(s)],
                out_shape=[jax.ShapeDtypeStruct((nb, TOKP, u_cols), F32),
                           jax.ShapeDtypeStruct(s.shape, F32)],
                scratch=[])


def _tok(x, t, nb):
    return x[t * nb:(t + 1) * nb]


def _head_sums(x, width):
    r, n = x.shape
    tiles = []
    for h in range(n // width):
        s = jnp.sum(x[:, h * width:(h + 1) * width], axis=1, keepdims=True)
        tiles.append(jnp.broadcast_to(s, (r, width)))
    return jnp.concatenate(tiles, axis=1)


def _ssd_sample_pre_kernel(xs_ref, bc_ref, dt_ref, prevx_ref, prevb_ref, cwx_ref, cwb_ref,
                           cbx_ref, cbb_ref, dtb_ref, alog_ref,
                           xs_out, bc_out, vw_out, oi_out, ecum_out, elast_out, tailx_out, tailb_out,
                           bufx_ref, bufb_ref, *, nb, ntok):
    n = (SSD_CONV - 1) * nb
    offx = bufx_ref.shape[0] - nb * ntok
    bufx_ref[offx - n:offx, :] = prevx_ref[...]
    bufb_ref[offx - n:offx, :] = prevb_ref[...]
    xs_c, tailx = _causal_conv(bufx_ref, xs_ref[...], cwx_ref, cbx_ref, nb, SSD_CONV)
    bc_c, tailb = _causal_conv(bufb_ref, bc_ref[...], cwb_ref, cbb_ref, nb, SSD_CONV)
    xs, bc, dt, la = _ssd_gates(xs_c, bc_c, dt_ref[...], dtb_ref[...], alog_ref[...])
    tailx_out[...] = tailx
    tailb_out[...] = tailb
    xs_out[...] = xs
    bc_out[...] = bc
    cums = []
    for t in range(ntok):
        lt = _tok(la, t, nb)
        cums.append(lt if t == 0 else cums[-1] + lt)
    cumx = [_expand_heads64(cm, SSD_HEADS) for cm in cums]
    v = xs * _expand_heads64(dt, SSD_HEADS)
    kw = SSD_GROUPS * SSD_STATE
    for t in range(ntok):
        ct = _tok(bc, t, nb)[:, kw:2 * kw]
        acc = None
        for t2 in range(t + 1):
            bt = _tok(bc, t2, nb)[:, 0:kw]
            sc = _head_sums(ct * bt, SSD_STATE)
            scx = jnp.concatenate(
                [jnp.concatenate([sc[:, g * SSD_STATE:(g + 1) * SSD_STATE]] * 4, axis=1)
                 for g in range(SSD_GROUPS)], axis=1)
            term = scx * jnp.exp(cumx[t] - cumx[t2]) * _tok(v, t2, nb)
            acc = term if acc is None else acc + term
        oi_out[t * nb:(t + 1) * nb, :] = acc
        ecum_out[t * nb:(t + 1) * nb, :] = jnp.exp(cumx[t])
        vw_out[t * nb:(t + 1) * nb, :] = _tok(v, t, nb) * jnp.exp(cumx[ntok - 1] - cumx[t])
    elast_out[...] = jnp.exp(cums[ntok - 1])


def _ssd_sample_post_kernel(oi_ref, u_ref, ecum_ref, xs_ref, z_ref, dx_ref, nw_ref, y_ref):
    o = oi_ref[...] + ecum_ref[...] * u_ref[...]
    y_ref[...] = _ssd_finish(o, xs_ref[...], z_ref[...], dx_ref[...], nw_ref[...]).astype(BF16)


def _ret_sample_pre_kernel(q_ref, k_ref, v_ref, cos_ref, sin_ref, qd_out, kd_out, oi_out, *, nb, ntok):
    rows = nb * ntok
    cos = jnp.concatenate([jnp.broadcast_to(cos_ref[t:t + 1, :], (nb, RET_DK)) for t in range(ntok)], axis=0)
    sin = jnp.concatenate([jnp.broadcast_to(sin_ref[t:t + 1, :], (nb, RET_DK)) for t in range(ntok)], axis=0)
    qr, kr = [], []
    for h in range(RET_HEADS):
        sl = slice(h * RET_DK, (h + 1) * RET_DK)
        qr.append(_rotary(q_ref[:, sl], cos, sin))
        kr.append(_rotary(k_ref[:, sl], cos, sin) * (RET_DK ** -0.5))
    qr = jnp.concatenate(qr, axis=1)
    kr = jnp.concatenate(kr, axis=1)
    v = v_ref[...]
    for t in range(ntok):
        qt = _tok(qr, t, nb)
        acc = None
        for t2 in range(t + 1):
            sc = _head_sums(qt * _tok(kr, t2, nb), RET_DK)
            vt = _tok(v, t2, nb)
            tiles = []
            for h in range(RET_HEADS):
                dec = math.exp(RET_LOG_GAMMA[h] * (t - t2))
                s = sc[:, h * RET_DK:(h + 1) * RET_DK] * dec
                tiles.append(jnp.concatenate([s, s], axis=1) * vt[:, h * RET_DV:(h + 1) * RET_DV])
            term = jnp.concatenate(tiles, axis=1)
            acc = term if acc is None else acc + term
        oi_out[t * nb:(t + 1) * nb, :] = acc
        qd = jnp.concatenate([qt[:, h * RET_DK:(h + 1) * RET_DK] * math.exp(RET_LOG_GAMMA[h] * (t + 1))
                              for h in range(RET_HEADS)], axis=1)
        kt = _tok(kr, t, nb)
        kd = jnp.concatenate([kt[:, h * RET_DK:(h + 1) * RET_DK] * math.exp(RET_LOG_GAMMA[h] * (ntok - 1 - t))
                              for h in range(RET_HEADS)], axis=1)
        qd_out[t * nb:(t + 1) * nb, :] = qd
        kd_out[t * nb:(t + 1) * nb, :] = kd


def _ret_sample_post_kernel(oi_ref, u_ref, g_ref, o_ref):
    o = oi_ref[...] + u_ref[...]
    g = g_ref[...]
    for h in range(RET_HEADS):
        sl = slice(h * RET_DV, (h + 1) * RET_DV)
        o_ref[:, sl] = (_rms(o[:, sl]) * _silu(g[:, sl])).astype(BF16)


def _hg_sample_pre_kernel(hq_ref, hf_ref, hi_ref, lb_ref, qd_out, kd_out, dl_out, oi_out, *, nb, ntok):
    q, _, kk, lg = _hg_gates(hq_ref[...], hf_ref[...], lb_ref[...])
    v = hi_ref[...]
    cums = []
    for t in range(ntok):
        lt = _tok(lg, t, nb)
        cums.append(lt if t == 0 else cums[-1] + lt)
    for t in range(ntok):
        qt = _tok(q, t, nb)
        acc = None
        for t2 in range(t + 1):
            w = qt * _tok(kk, t2, nb)
            if t2 < t:
                w = w * jnp.exp(cums[t] - cums[t2])
            term = _head_sums(w, HG_DK) * _tok(v, t2, nb)
            acc = term if acc is None else acc + term
        oi_out[t * nb:(t + 1) * nb, :] = acc
        qd_out[t * nb:(t + 1) * nb, :] = qt * jnp.exp(cums[t])
        kd_out[t * nb:(t + 1) * nb, :] = _tok(kk, t, nb) * jnp.exp(cums[ntok - 1] - cums[t])
    dl_out[...] = jnp.exp(cums[ntok - 1])


def _hg_sample_post_kernel(oi_ref, u_ref, hg_ref, nw_ref, og_ref):
    o = oi_ref[...] + u_ref[...]
    hg = hg_ref[...]
    for h in range(HG_HEADS):
        sl = slice(h * HG_DV, (h + 1) * HG_DV)
        og_ref[:, sl] = (_rms(o[:, sl]) * nw_ref[...] * _silu(hg[:, sl])).astype(BF16)


def _lru_sample_kernel(rx_ref, rg_ref, prev_ref, h0_ref, cw_ref, cb_ref, wr_ref, br_ref, wi_ref,
                       bi_ref, ap_ref, yl_out, hfin_out, tail_out, buf_ref, *, nb, ntok):
    rows = nb * ntok
    n = (LRU_CONV - 1) * nb
    off = buf_ref.shape[0] - rows
    buf_ref[off - n:off, :] = prev_ref[...]
    xr, tail = _causal_conv(buf_ref, rx_ref[...], cw_ref, cb_ref, nb, LRU_CONV)
    tail_out[...] = tail
    a, mult, gi = _lru_gates(xr, wr_ref, br_ref[...], wi_ref, bi_ref[...], ap_ref[...])
    b = mult * gi * xr
    h = h0_ref[...]
    for t in range(ntok):
        h = _tok(a, t, nb) * h + _tok(b, t, nb)
        yl_out[t * nb:(t + 1) * nb, :] = (h * _gelu_tanh(rg_ref[t * nb:(t + 1) * nb, :])).astype(BF16)
    hfin_out[...] = h


def _whole(shape):
    return pl.BlockSpec(shape, lambda i: (0,) * len(shape))


def _colblock(rows, width, idx):
    return pl.BlockSpec((rows, width), functools.partial(lambda i, k: (0, k), k=idx))


def _call_whole(kern, in_arrays, in_specs, out_shapes, scratch, name, vmem):
    return pl.pallas_call(
        kern, grid=(1,), in_specs=in_specs,
        out_specs=[_whole(s.shape) for s in out_shapes],
        out_shape=out_shapes, scratch_shapes=scratch,
        compiler_params=_cparams(("arbitrary",), vmem), name=name,
    )(*in_arrays)


def _to_token_major(s):
    nb, w, c = s.shape
    return jnp.transpose(s, (1, 0, 2)).reshape(w * nb, c)


def _from_token_major(x, nb):
    w = x.shape[0] // nb
    return jnp.transpose(x.reshape(w, nb, x.shape[1]), (1, 0, 2))


def _rows_to_batch(x, nb):
    ntok = x.shape[0] // nb
    x = jnp.transpose(x.reshape(ntok, nb, x.shape[1]), (1, 0, 2))
    return jnp.pad(x, ((0, 0), (0, TOKP - ntok), (0, 0)))


def _batch_to_rows(u, ntok):
    nb = u.shape[0]
    return jnp.transpose(u[:, :ntok], (1, 0, 2)).reshape(ntok * nb, u.shape[2])


def ssd_sample_pre(pe, s_ssm, s_conv, prm, nb, ntok, nseq, nc):
    rows = nb * ntok
    n = (SSD_CONV - 1) * nb
    prev = _to_token_major(s_conv)
    off = _conv_off(nb, SSD_CONV)
    f = lambda *shape: jax.ShapeDtypeStruct(shape, F32)
    outs = _call_whole(
        functools.partial(_ssd_sample_pre_kernel, nb=nb, ntok=ntok),
        [pe, pe, pe, prev, prev, prm["ssd_cw"], prm["ssd_cw"], prm["ssd_cb"], prm["ssd_cb"],
         prm["ssd_dtb"], prm["ssd_alog"]],
        [_colblock(rows, D_SSD, 1), _colblock(rows, SSD_BC, 10), _colblock(rows, LANES, DT_COL // LANES),
         _colblock(n, D_SSD, 0), _colblock(n, SSD_BC, 2),
         _colblock(SSD_CONV, D_SSD, 0), _colblock(SSD_CONV, SSD_BC, 2),
         _colblock(1, D_SSD, 0), _colblock(1, SSD_BC, 2), _whole((1, LANES)), _whole((1, LANES))],
        [f(rows, D_SSD), f(rows, SSD_BC), f(rows, D_SSD), f(rows, D_SSD), f(rows, D_SSD),
         f(nb, LANES), f(n, D_SSD), f(n, SSD_BC)],
        [pltpu.VMEM((off + rows, D_SSD), F32), pltpu.VMEM((off + rows, SSD_BC), F32)],
        "ssd_sample_pre", _mib(56))
    xs, bc, vw, oi, ecum, elast, tailx, tailb = outs
    kw = SSD_GROUPS * SSD_STATE
    dec = jnp.broadcast_to(elast[:, :SSD_HEADS, None], (nb, SSD_HEADS, SSD_STATE))
    st = jnp.swapaxes(s_ssm, -1, -2)
    part = _state_part(_ssd_state_kernel,
                       [_rows_to_batch(bc[:, kw:], nb), _rows_to_batch(bc[:, :kw], nb),
                        _rows_to_batch(vw, nb)], dec, st, D_SSD, nseq, nc)
    return dict(oi=oi, ecum=ecum, xs=xs, tailx=tailx, tailb=tailb), part


def ssd_sample_post(ctx, u, st_new, pe, prm, nb, ntok):
    rows = nb * ntok
    s_new = jnp.swapaxes(st_new, -1, -2)
    u = _batch_to_rows(u, ntok)
    (y,) = _call_whole(
        _ssd_sample_post_kernel, [ctx["oi"], u, ctx["ecum"], ctx["xs"], pe, prm["ssd_dx"], prm["ssd_nw"]],
        [_whole((rows, D_SSD))] * 4 + [_colblock(rows, D_SSD, 0), _whole((1, D_SSD)), _whole((1, D_SSD))],
        [jax.ShapeDtypeStruct((rows, D_SSD), BF16)], [], "ssd_sample_post", _mib(48))
    conv_new = _from_token_major(jnp.concatenate([ctx["tailx"], ctx["tailb"]], axis=1), nb)
    return y, s_new, conv_new


def ret_sample_pre(pe, s_ret, cos, sin, nb, ntok, nseq, nc):
    rows = nb * ntok
    f = lambda *shape: jax.ShapeDtypeStruct(shape, F32)
    qd, kd, oi = _call_whole(
        functools.partial(_ret_sample_pre_kernel, nb=nb, ntok=ntok),
        [pe, pe, pe, cos, sin],
        [_colblock(rows, RET_QK, 8), _colblock(rows, RET_QK, 9), _colblock(rows, D_RET, 2),
         _whole(cos.shape), _whole(sin.shape)],
        [f(rows, RET_QK), f(rows, RET_QK), f(rows, D_RET)], [], "ret_sample_pre", _mib(48))
    v = _rows_to_batch(pe[:, 2 * D_RET:3 * D_RET], nb)
    part = _state_part(functools.partial(_ret_state_kernel, ntok=ntok),
                       [_rows_to_batch(qd, nb), _rows_to_batch(kd, nb), v], None, s_ret,
                       D_RET, nseq, nc)
    return oi, part


def ret_sample_post(oi, u, pe, nb, ntok):
    rows = nb * ntok
    u = _batch_to_rows(u, ntok)
    (o,) = _call_whole(
        _ret_sample_post_kernel, [oi, u, pe],
        [_whole((rows, D_RET)), _whole((rows, D_RET)), _colblock(rows, D_RET, 3)],
        [jax.ShapeDtypeStruct((rows, D_RET), BF16)], [], "ret_sample_post", _mib(40))
    return o


def hg_sample_pre(po, s_hg, lb, nb, ntok, nseq, nc):
    rows = nb * ntok
    f = lambda *shape: jax.ShapeDtypeStruct(shape, F32)
    qd, kd, dl, oi = _call_whole(
        functools.partial(_hg_sample_pre_kernel, nb=nb, ntok=ntok),
        [po, po, po, lb],
        [_colblock(rows, D_HG, 0), _colblock(rows, D_HG, 1), _colblock(rows, D_HG, 2), _whole((1, D_HG))],
        [f(rows, D_HG), f(rows, D_HG), f(nb, D_HG), f(rows, D_HG)], [], "hg_sample_pre", _mib(48))
    dcol = jnp.pad(jnp.transpose(dl.reshape(nb, HG_HEADS, HG_DK), (0, 2, 1)),
                   ((0, 0), (0, 0), (0, LANES - HG_HEADS)))
    v = _rows_to_batch(po[:, 2 * D_HG:3 * D_HG], nb)
    part = _state_part(_hg_state_kernel, [_rows_to_batch(qd, nb), _rows_to_batch(kd, nb), v],
                       dcol, s_hg, D_HG, nseq, nc)
    return oi, part


def hg_sample_post(oi, u, po, nw, nb, ntok):
    rows = nb * ntok
    u = _batch_to_rows(u, ntok)
    (og,) = _call_whole(
        _hg_sample_post_kernel, [oi, u, po, nw],
        [_whole((rows, D_HG)), _whole((rows, D_HG)), _colblock(rows, D_HG, 3), _whole((1, HG_DV))],
        [jax.ShapeDtypeStruct((rows, D_HG), BF16)], [], "hg_sample_post", _mib(40))
    return og


def lru_sample(po, s_lru, s_lconv, prm, nb, ntok):
    rows = nb * ntok
    n = (LRU_CONV - 1) * nb
    off = _conv_off(nb, LRU_CONV)
    prev = _to_token_major(s_lconv)
    yl, hfin, tail = _call_whole(
        functools.partial(_lru_sample_kernel, nb=nb, ntok=ntok),
        [po, po, prev, s_lru, prm["lru_cw"], prm["lru_cb"], prm["lru_wr"], prm["lru_br"], prm["lru_wi"],
         prm["lru_bi"], prm["lru_ap"]],
        [_colblock(rows, D_RNN, 4), _colblock(rows, D_RNN, 5), _whole((n, D_RNN)), _whole((nb, D_RNN)),
         _whole((LRU_CONV, D_RNN)), _whole((1, D_RNN)), _whole((LRU_BLOCKS, LRU_BW, LRU_BW)),
         _whole((1, D_RNN)), _whole((LRU_BLOCKS, LRU_BW, LRU_BW)), _whole((1, D_RNN)), _whole((1, D_RNN))],
        [jax.ShapeDtypeStruct((rows, D_RNN), BF16), jax.ShapeDtypeStruct((nb, D_RNN), F32),
         jax.ShapeDtypeStruct((n, D_RNN), F32)],
        [pltpu.VMEM((off + rows, D_RNN), F32)], "lru_sample", _mib(48))
    return yl, hfin, _from_token_major(tail, nb)


def _rope_tables(pos):
    half = RET_DK // 2
    inv = ROPE_BASE ** (-jnp.arange(half, dtype=F32) / half)
    ang = pos.astype(F32)[:, None] * inv[None, :]
    cos, sin = jnp.cos(ang), jnp.sin(ang)
    return jnp.concatenate([cos, cos], axis=1), jnp.concatenate([-sin, sin], axis=1)


def _pad_lanes(v):
    return jnp.pad(v.astype(F32), (0, LANES - v.shape[0])).reshape(1, LANES)


def _prepare(p):
    lbs = jnp.cumsum(jax.nn.softmax(p["hg_lower_bounds"].astype(F32), axis=0), axis=0)
    lbs = lbs - lbs[0]
    return {
        "even_wt": jnp.swapaxes(p["even_w_in"][0], 0, 1),
        "ssd_cw": p["ssd_conv_w"][0], "ssd_cb": p["ssd_conv_b"][0].reshape(1, -1),
        "ssd_dtb": _pad_lanes(p["ssd_dt_bias"][0]), "ssd_alog": _pad_lanes(p["ssd_A_log"][0]),
        "ssd_dx": jnp.repeat(p["ssd_D"][0], SSD_HEAD_DIM).reshape(1, D_SSD),
        "ssd_nw": p["ssd_norm_w"][0].reshape(1, D_SSD),
        "hg_lb": lbs[1].reshape(1, D_HG), "hg_nw": p["hg_norm_w"][0].reshape(1, HG_DV),
        "lru_cw": p["lru_conv_w"][0], "lru_cb": p["lru_conv_b"][0].reshape(1, D_RNN),
        "lru_wr": p["lru_w_r"][0], "lru_br": p["lru_b_r"][0].reshape(1, D_RNN),
        "lru_wi": p["lru_w_i"][0], "lru_bi": p["lru_b_i"][0].reshape(1, D_RNN),
        "lru_ap": p["lru_a_param"][0].reshape(1, D_RNN),
    }


def _even_row_offsets(bn):
    o_xbc, o_dt = D_SSD, 2 * D_SSD + SSD_BC
    o_q = o_dt + SSD_HEADS
    o_k, o_v = o_q + RET_QK, o_q + 2 * RET_QK
    o_g = o_v + D_RET
    segs = [(0, D_SSD), (o_xbc, D_SSD), (o_v, D_RET), (o_g, D_RET), (o_q, RET_QK), (o_k, RET_QK),
            (o_xbc + D_SSD, SSD_BC), (o_dt, bn)]
    offs = [start + i for start, width in segs for i in range(0, width, bn)]
    assert len(offs) * bn == EVEN_PACKED and offs[-1] + bn <= D_IN_EVEN
    return offs


def _even_proj(x, p, prm, bm):
    return mm_in_t(x, p["norm_mix"][0], prm["even_wt"], _even_row_offsets(512), bm, 512)


def _ffn(x, p, prm, l, prev, shift, group_rows, bm, bf=512):
    act, st = ffn_up(x, p["norm_ffn"][l], p["ffn_w_up"], l, p["ffn_conv_w"][l], p["ffn_conv_b"][l],
                     prev, shift, group_rows, bm, bf)
    return mm_out([act], p["ffn_w_down"], l, x, min(bm, 1024), 256), st


def _trunks(xp, xs, st, p, prm, nseq, seq, nb, ntok):
    m = nseq * seq
    rows = nb * ntok
    bm = min(2048, seq)
    nc = seq // MIX_CHUNK
    grid = (nseq, nc)
    cos_p, sin_p = _rope_tables(jnp.arange(seq, dtype=jnp.int32))
    cos_s, sin_s = _rope_tables(PAST_LEN + jnp.arange(ntok, dtype=jnp.int32))
    zeros_ffn = jnp.zeros((nseq, FFN_CONV - 1, D_FF), F32)

    pe_s = _even_proj(xs, p, prm, rows)
    ssd_ctx, ssd_part = ssd_sample_pre(pe_s, st["ssm"][0], st["ssm_conv"][0], prm, nb, ntok, nseq, nc)
    ret_oi, ret_part = ret_sample_pre(pe_s, st["ret"][0], cos_s, sin_s, nb, ntok, nseq, nc)
    pe_p = _even_proj(xp, p, prm, bm)
    (y, ssm_p, ssm_conv_p), (o, ret_p), (u_ssd, st_ssd), (u_ret, ret_s) = _run_parts(
        [ssd_prompt(pe_p, prm, nseq, seq, MIX_CHUNK), ret_prompt(pe_p, cos_p, sin_p, nseq, seq, MIX_CHUNK),
         ssd_part, ret_part], grid, "even_mix", _mib(58))
    xp = mm_out([y, o], p["even_w_out"], 0, xp, min(bm, 1024), 512)
    xp, ffn0_p = _ffn(xp, p, prm, 0, zeros_ffn, 1, seq, bm, 256)
    y_s, ssm_s, ssm_conv_s = ssd_sample_post(ssd_ctx, u_ssd, st_ssd, pe_s, prm, nb, ntok)
    o_s = ret_sample_post(ret_oi, u_ret, pe_s, nb, ntok)
    xs = mm_out([y_s, o_s], p["even_w_out"], 0, xs, rows, 512)
    xs, ffn0_s = _ffn(xs, p, prm, 0, _to_token_major(st["ffn_conv"][0])[None], nb, rows, rows)

    po_s = mm_in(xs, p["norm_mix"][1], p["odd_w_in"][0], rows, 512)
    hg_oi, hg_part = hg_sample_pre(po_s, st["hgrn"][0], prm["hg_lb"], nb, ntok, nseq, nc)
    yl_s, lru_s, lru_conv_s = lru_sample(po_s, st["lru"][0], st["lru_conv"][0], prm, nb, ntok)
    po_p = mm_in(xp, p["norm_mix"][1], p["odd_w_in"][0], bm, 512)
    (og, hgrn_p), (yl, lru_p, lru_conv_p), (u_hg, hgrn_s) = _run_parts(
        [hg_prompt(po_p, prm["hg_lb"], prm["hg_nw"], nseq, seq, MIX_CHUNK),
         lru_prompt(po_p, prm, nseq, seq, MIX_CHUNK), hg_part], grid, "odd_mix", _mib(58))
    xp = mm_out([og, yl], p["odd_w_out"], 0, xp, min(bm, 1024), 512)
    xp, ffn1_p = _ffn(xp, p, prm, 1, zeros_ffn, 1, seq, bm, 256)
    og_s = hg_sample_post(hg_oi, u_hg, po_s, prm["hg_nw"], nb, ntok)
    xs = mm_out([og_s, yl_s], p["odd_w_out"], 0, xs, rows, 512)
    xs, ffn1_s = _ffn(xs, p, prm, 1, _to_token_major(st["ffn_conv"][1])[None], nb, rows, rows)

    y_p = rmsnorm(xp, p["norm_final"], min(512, m)).reshape(nseq, seq, D_MODEL)
    y_s = jnp.transpose(rmsnorm(xs, p["norm_final"], rows).reshape(ntok, nb, D_MODEL), (1, 0, 2))
    ffn_s = jnp.stack([_from_token_major(ffn0_s[0], nb), _from_token_major(ffn1_s[0], nb)])
    return (y_p, y_s, ssm_p[None], ssm_s[None], ssm_conv_p[None], ssm_conv_s[None],
            ret_p[None], ret_s[None], hgrn_p[None], hgrn_s[None],
            lru_p.reshape(1, nseq, D_RNN), lru_s[None], lru_conv_p[None], lru_conv_s[None],
            jnp.stack([ffn0_p, ffn1_p]), ffn_s)


def kernel(x_prompt, x_sample, state_ssm, state_ssm_conv, state_ret, state_hgrn, state_lru, state_lru_conv, state_ffn_conv, norm_mix, norm_ffn, norm_final, even_w_in, ssd_conv_w, ssd_conv_b, ssd_dt_bias, ssd_A_log, ssd_D, ssd_norm_w, even_w_out, odd_w_in, hg_lower_bounds, hg_norm_w, lru_conv_w, lru_conv_b, lru_w_r, lru_b_r, lru_w_i, lru_b_i, lru_a_param, odd_w_out, ffn_w_up, ffn_conv_w, ffn_conv_b, ffn_w_down):
    p = {
        "norm_mix": norm_mix, "norm_ffn": norm_ffn, "norm_final": norm_final,
        "even_w_in": even_w_in, "ssd_conv_w": ssd_conv_w, "ssd_conv_b": ssd_conv_b,
        "ssd_dt_bias": ssd_dt_bias, "ssd_A_log": ssd_A_log, "ssd_D": ssd_D,
        "ssd_norm_w": ssd_norm_w, "even_w_out": even_w_out, "odd_w_in": odd_w_in,
        "hg_lower_bounds": hg_lower_bounds, "hg_norm_w": hg_norm_w,
        "lru_conv_w": lru_conv_w, "lru_conv_b": lru_conv_b, "lru_w_r": lru_w_r,
        "lru_b_r": lru_b_r, "lru_w_i": lru_w_i, "lru_b_i": lru_b_i,
        "lru_a_param": lru_a_param, "odd_w_out": odd_w_out, "ffn_w_up": ffn_w_up,
        "ffn_conv_w": ffn_conv_w, "ffn_conv_b": ffn_conv_b, "ffn_w_down": ffn_w_down,
    }
    prm = _prepare(p)
    nseq, seq, _ = x_prompt.shape
    nb, ntok, _ = x_sample.shape
    st = {"ssm": state_ssm, "ssm_conv": state_ssm_conv, "ret": state_ret, "hgrn": state_hgrn,
          "lru": state_lru, "lru_conv": state_lru_conv, "ffn_conv": state_ffn_conv}
    xs_tm = jnp.transpose(x_sample, (1, 0, 2)).reshape(ntok * nb, D_MODEL)
    return _trunks(x_prompt.reshape(nseq * seq, D_MODEL), xs_tm, st, p, prm, nseq, seq, nb, ntok)
```
